```python
import math
import jax, jax.numpy as jnp
from jax import lax
import numpy as np

D_MODEL = 1024
BATCH = 32
SEQ = 2048
DEPTH = 2
DEC_BATCH = 16
DEC_SEQ = 32
PAST_LEN = 2048

CHUNK = 64
Q_BLOCK = 128
HEAD_DIM = 64
D_MIX = D_MODEL
A_WIDTH = D_MIX // 2
A_HEADS = A_WIDTH // HEAD_DIM
A_KV_HEADS = A_HEADS // 4
A_GROUP = A_HEADS // A_KV_HEADS
IDX_HEADS = A_HEADS
IDX_DIM = HEAD_DIM
IDX_W_SCALE = (IDX_HEADS * IDX_DIM) ** -0.5
TOPK_MAX = 256
B_WIDTH = D_MIX // 4
B_HEADS = B_WIDTH // HEAD_DIM
C_WIDTH = D_MIX // 4
C_HEADS = C_WIDTH // HEAD_DIM
ROPE_THETA = 10000.0
EPS = 1e-6

_SIZES = (A_WIDTH, A_KV_HEADS * HEAD_DIM, A_KV_HEADS * HEAD_DIM, A_WIDTH,
          IDX_HEADS * IDX_DIM, IDX_DIM, IDX_HEADS,
          B_WIDTH, B_WIDTH, B_WIDTH, B_WIDTH,
          C_WIDTH, C_WIDTH, C_WIDTH, C_WIDTH)
SPLIT_IDX = tuple(int(s) for s in np.cumsum(_SIZES)[:-1])
N_IN = int(sum(_SIZES))

kernel_name = 'hymba_dsa_hgrn2_retention_stream_step'


def rms_norm(x, g):
    xf = x.astype(jnp.float32)
    y = xf * lax.rsqrt(jnp.mean(xf * xf, axis=-1, keepdims=True) + EPS) * g.astype(jnp.float32)
    return y.astype(x.dtype)


def rope(x, pos):
    half = x.shape[-1] // 2
    freqs = ROPE_THETA ** (-jnp.arange(half, dtype=jnp.float32) / half)
    ang = pos.astype(jnp.float32)[:, None] * freqs[None, :]
    cos = jnp.cos(ang)[None, :, None, :]
    sin = jnp.sin(ang)[None, :, None, :]
    xf = x.astype(jnp.float32)
    x1, x2 = xf[..., :half], xf[..., half:]
    return jnp.concatenate([x1 * cos - x2 * sin, x2 * cos + x1 * sin], axis=-1).astype(x.dtype)


def dsa_block(q, qi, w, qpos, k_all, v_all, ki_all, topk):
    f32 = jnp.float32
    kpos = jnp.arange(k_all.shape[1])
    sc = jnp.einsum('bthd,bsd->bths', qi.astype(f32), ki_all.astype(f32))
    sc = jnp.einsum('bths,bth->bts', jax.nn.relu(sc), w.astype(f32))
    admissible = (kpos[None, :] // CHUNK) <= (qpos[:, None] // CHUNK)
    sc = jnp.where(admissible[None], sc, -jnp.inf)
    vals, idx = lax.top_k(sc, topk)
    valid = jnp.isfinite(vals)
    gather = jax.vmap(lambda rows, ii: rows[ii])
    kg = gather(k_all, idx)
    vg = gather(v_all, idx)
    b, t = q.shape[:2]
    qg = q.reshape(b, t, A_KV_HEADS, A_GROUP, HEAD_DIM)
    logits = jnp.einsum('bthgd,btshd->bthgs', qg, kg).astype(f32) * (HEAD_DIM ** -0.5)
    logits = jnp.where(valid[:, :, None, None, :], logits, -jnp.inf)
    p = jax.nn.softmax(logits, axis=-1).astype(vg.dtype)
    o = jnp.einsum('bthgs,btshd->bthgd', p, vg)
    return o.reshape(b, t, A_WIDTH)


def sparse_attention(q, qi, w, qpos, k_all, v_all, ki_all):
    b, t = q.shape[:2]
    topk = min(TOPK_MAX, k_all.shape[1] // 4)
    if t > Q_BLOCK and t % Q_BLOCK == 0:
        nb = t // Q_BLOCK
        blk = lambda a: a.reshape((b, nb, Q_BLOCK) + a.shape[2:]).swapaxes(0, 1)
        out = lax.map(lambda xs: dsa_block(xs[0], xs[1], xs[2], xs[3], k_all, v_all, ki_all, topk),
                      (blk(q), blk(qi), blk(w), qpos.reshape(nb, Q_BLOCK)))
        return out.swapaxes(0, 1).reshape(b, t, A_WIDTH)
    return dsa_block(q, qi, w, qpos, k_all, v_all, ki_all, topk)


def gla_chunk(S, xs):
    q, k, v, logf = xs
    c = q.shape[1]
    bcum = jnp.cumsum(logf, axis=1)
    causal = jnp.tril(jnp.ones((c, c), dtype=bool))
    diff = bcum[:, :, None] - bcum[:, None]
    dec = jnp.exp(jnp.where(causal[None, :, :, None, None], diff, -jnp.inf))
    att = jnp.einsum('bthk,bshk,btshk->bths', q, k, dec)
    o = jnp.einsum('bths,bshv->bthv', att, v) + jnp.einsum('bthk,bhkv->bthv', q * jnp.exp(bcum), S)
    b_last = bcum[:, -1]
    S_new = jnp.exp(b_last)[..., None] * S + jnp.einsum('bshk,bshv->bhkv', k * jnp.exp(b_last[:, None] - bcum), v)
    return S_new, o


def retention_log_decay():
    return jnp.log(1.0 - jnp.exp(jnp.linspace(math.log(1.0 / 32), math.log(1.0 / 512), C_HEADS, dtype=jnp.float32)))


def ret_chunk(R, xs):
    q, k, v = xs
    lg = retention_log_decay()
    c = q.shape[1]
    n = jnp.arange(c, dtype=jnp.float32)
    diff = n[:, None] - n[None, :]
    causal = diff >= 0
    dmat = jnp.where(causal[None], jnp.exp(jnp.where(causal, diff, 0.0)[None] * lg[:, None, None]), 0.0)
    att = jnp.einsum('bthk,bshk->bhts', q, k) * dmat[None]
    inner_scale = jnp.exp((n[:, None] + 1.0) * lg[None, :])
    o = jnp.einsum('bhts,bshv->bthv', att, v) + jnp.einsum('bthk,bhkv->bthv', q, R) * inner_scale[None, :, :, None]
    k_scale = jnp.exp((c - 1.0 - n)[:, None] * lg[None, :])
    R_new = jnp.exp(c * lg)[None, :, None, None] * R + jnp.einsum('bshk,bshv->bhkv', k * k_scale[None, :, :, None], v)
    return R_new, o


def chunked_scan(step, S0, xs):
    t = xs[0].shape[1]
    if t <= CHUNK:
        return step(S0, xs)
    nc = t // CHUNK
    xs_c = tuple(a.reshape((a.shape[0], nc, CHUNK) + a.shape[2:]).swapaxes(0, 1) for a in xs)
    S, o = lax.scan(step, S0, xs_c)
    o = o.swapaxes(0, 1)
    return S, o.reshape((o.shape[0], t) + o.shape[3:])


def mixer_layer(x, pos, k_past, v_past, ik_past, s_hgrn, s_ret, lb,
                norm_g, w_in, q_norm_g, k_norm_g, hgrn_norm_g, ret_norm_g, w_out):
    f32 = jnp.float32
    b, t, _ = x.shape
    h = rms_norm(x, norm_g)
    proj = jnp.einsum('btd,dn->btn', h, w_in)
    (aq, ak, av, ag, iq, ik, iw, bq, bf, bi, bg, cq, ck, cv, cg) = jnp.split(proj, SPLIT_IDX, axis=-1)

    aq = rope(rms_norm(aq.reshape(b, t, A_HEADS, HEAD_DIM), q_norm_g), pos)
    ak = rope(rms_norm(ak.reshape(b, t, A_KV_HEADS, HEAD_DIM), k_norm_g), pos)
    av = av.reshape(b, t, A_KV_HEADS, HEAD_DIM)
    iq = rope(iq.reshape(b, t, IDX_HEADS, IDX_DIM), pos)
    ik = rope(ik.reshape(b, t, 1, IDX_DIM), pos)[:, :, 0]
    k_all = jnp.concatenate([k_past.astype(ak.dtype), ak], axis=1)
    v_all = jnp.concatenate([v_past.astype(av.dtype), av], axis=1)
    ik_all = jnp.concatenate([ik_past.astype(ik.dtype), ik], axis=1)
    ya = sparse_attention(aq, iq, iw * IDX_W_SCALE, pos, k_all, v_all, ik_all) * jax.nn.silu(ag)

    f = lb + (1.0 - lb) * jax.nn.sigmoid(bf.astype(f32))
    heads_b = lambda a: a.reshape(b, t, B_HEADS, HEAD_DIM)
    s_hgrn_new, ob = chunked_scan(gla_chunk, s_hgrn.astype(f32),
                                  (heads_b(bq.astype(f32)), heads_b(1.0 - f), heads_b(bi.astype(f32)), heads_b(jnp.log(f))))
    yb = rms_norm(ob, hgrn_norm_g).reshape(b, t, B_WIDTH).astype(x.dtype) * jax.nn.silu(bg)

    cq = rope(cq.reshape(b, t, C_HEADS, HEAD_DIM), pos).astype(f32)
    ck = rope(ck.reshape(b, t, C_HEADS, HEAD_DIM), pos).astype(f32) * (HEAD_DIM ** -0.5)
    cv = cv.reshape(b, t, C_HEADS, HEAD_DIM).astype(f32)
    s_ret_new, oc = chunked_scan(ret_chunk, s_ret.astype(f32), (cq, ck, cv))
    yc = rms_norm(oc, ret_norm_g).reshape(b, t, C_WIDTH).astype(x.dtype) * jax.nn.silu(cg)

    y = jnp.concatenate([ya, yb.astype(ya.dtype), yc.astype(ya.dtype)], axis=-1)
    out = x + jnp.einsum('btm,md->btd', y, w_out)
    return out, (ak, av, ik, s_hgrn_new, s_ret_new)


def setup_inputs(seed: int = 0) -> dict:
    key = jax.random.key(seed)
    ks = jax.random.split(key, 16)
    nrm = jax.random.normal
    f32 = jnp.float32
    return {
        'x_prompt': nrm(ks[0], (BATCH, SEQ, D_MODEL), f32),
        'x_sample': nrm(ks[1], (DEC_BATCH, DEC_SEQ, D_MODEL), f32),
        'cache_k': nrm(ks[2], (DEPTH, DEC_BATCH, PAST_LEN, A_KV_HEADS, HEAD_DIM), f32),
        'cache_v': nrm(ks[3], (DEPTH, DEC_BATCH, PAST_LEN, A_KV_HEADS, HEAD_DIM), f32),
        'cache_idx_k': nrm(ks[4], (DEPTH, DEC_BATCH, PAST_LEN, IDX_DIM), f32),
        'state_hgrn': 0.5 * nrm(ks[5], (DEPTH, DEC_BATCH, B_HEADS, HEAD_DIM, HEAD_DIM), f32),
        'state_ret': 4.0 * nrm(ks[6], (DEPTH, DEC_BATCH, C_HEADS, HEAD_DIM, HEAD_DIM), f32),
        'norm_g': 1.0 + 0.02 * nrm(ks[7], (DEPTH, D_MODEL), f32),
        'w_in': nrm(ks[8], (DEPTH, D_MODEL, N_IN), f32) * (D_MODEL ** -0.5),
        'q_norm_g': 1.0 + 0.02 * nrm(ks[9], (DEPTH, HEAD_DIM), f32),
        'k_norm_g': 1.0 + 0.02 * nrm(ks[10], (DEPTH, HEAD_DIM), f32),
        'hgrn_lb_logits': 0.5 * nrm(ks[11], (DEPTH, B_WIDTH), f32),
        'hgrn_norm_g': 1.0 + 0.02 * nrm(ks[12], (DEPTH, HEAD_DIM), f32),
        'ret_norm_g': 1.0 + 0.02 * nrm(ks[13], (DEPTH, HEAD_DIM), f32),
        'w_out': nrm(ks[14], (DEPTH, D_MIX, D_MODEL), f32) * (D_MIX ** -0.5),
    }


def reference(x_prompt, x_sample, cache_k, cache_v, cache_idx_k, state_hgrn, state_ret,
              norm_g, w_in, q_norm_g, k_norm_g, hgrn_lb_logits, hgrn_norm_g, ret_norm_g, w_out):
    f32 = jnp.float32
    sm = jax.nn.softmax(hgrn_lb_logits.astype(f32), axis=0)
    lbs = jnp.cumsum(sm, axis=0) - sm[0:1]

    bp, tp = x_prompt.shape[0], x_prompt.shape[1]
    bs, ts = x_sample.shape[0], x_sample.shape[1]
    past = cache_k.shape[2]
    pos_p = jnp.arange(tp)
    pos_s = past + jnp.arange(ts)
    empty_k = jnp.zeros((bp, 0, A_KV_HEADS, HEAD_DIM), x_prompt.dtype)
    empty_ik = jnp.zeros((bp, 0, IDX_DIM), x_prompt.dtype)
    zero_hg = jnp.zeros((bp, B_HEADS, HEAD_DIM, HEAD_DIM), f32)
    zero_rt = jnp.zeros((bp, C_HEADS, HEAD_DIM, HEAD_DIM), f32)

    yp, ys = x_prompt, x_sample
    pk, pv, pik, phg, prt = [], [], [], [], []
    sk, sv, sik, shg, srt = [], [], [], [], []
    for l in range(DEPTH):
        w_l = (norm_g[l], w_in[l], q_norm_g[l], k_norm_g[l], hgrn_norm_g[l], ret_norm_g[l], w_out[l])
        yp, (k_, v_, ik_, hg_, rt_) = mixer_layer(yp, pos_p, empty_k, empty_k, empty_ik, zero_hg, zero_rt, lbs[l], *w_l)
        pk.append(k_); pv.append(v_); pik.append(ik_); phg.append(hg_); prt.append(rt_)
        ys, (k_, v_, ik_, hg_, rt_) = mixer_layer(ys, pos_s, cache_k[l], cache_v[l], cache_idx_k[l],
                                                   state_hgrn[l], state_ret[l], lbs[l], *w_l)
        sk.append(k_); sv.append(v_); sik.append(ik_); shg.append(hg_); srt.append(rt_)

    return (yp, ys,
            jnp.stack(pk), jnp.stack(pv), jnp.stack(pik), jnp.stack(phg), jnp.stack(prt),
            jnp.stack(sk), jnp.stack(sv), jnp.stack(sik), jnp.stack(shg), jnp.stack(srt))
```

```python
import functools
import math

import numpy as np
import jax
import jax.numpy as jnp
from jax import lax
from jax.experimental import pallas as pl
from jax.experimental.pallas import tpu as pltpu

F32 = jnp.float32
BF16 = jnp.bfloat16
I32 = jnp.int32

D_MODEL = 1024
HEAD_DIM = 64
CHUNK = 64
A_WIDTH = 512
A_HEADS = 8
A_KV_HEADS = 2
KV_WIDTH = A_KV_HEADS * HEAD_DIM
IDX_HEADS = 8
IDX_W_SCALE = (IDX_HEADS * HEAD_DIM) ** -0.5
TOPK_MAX = 256
B_WIDTH = 256
C_WIDTH = 256
REC_HEADS = 4
ROPE_THETA = 10000.0
EPS = 1e-6
LANES = 128
VMEM_LIMIT = 48 * 1024 * 1024

_SIZES = (A_WIDTH, KV_WIDTH, KV_WIDTH, A_WIDTH, IDX_HEADS * HEAD_DIM, HEAD_DIM, IDX_HEADS,
          B_WIDTH, B_WIDTH, B_WIDTH, B_WIDTH, C_WIDTH, C_WIDTH, C_WIDTH, C_WIDTH)
_SPLIT_IDX = tuple(int(s) for s in np.cumsum(_SIZES)[:-1])

_AQ, _AK, _AV, _AG, _IQ, _IKW = 0, 512, 640, 768, 1280, 1792
_BQ, _BF, _BI, _BG, _CQ, _CK, _CV, _CG = 1920, 2176, 2432, 2688, 2944, 3200, 3456, 3712
_N_PROJ = 3968

INT_MIN = -(2 ** 31)
NEG_BIG = -1e30
POS_BIG = 1 << 20


def _prep_w_in(w):
    (aq, ak, av, ag, iq, ik, iw, bq, bf, bi, bg, cq, ck, cv, cg) = jnp.split(w, _SPLIT_IDX, axis=1)
    pad = jnp.zeros((w.shape[0], LANES - HEAD_DIM - IDX_HEADS), w.dtype)
    return jnp.concatenate([aq, ak, av, ag, iq, ik, iw, pad, bq, bf, bi, bg, cq, ck, cv, cg],
                           axis=1).astype(BF16)


def _rope_tables(pos):
    half = HEAD_DIM // 2
    freqs = ROPE_THETA ** (-jnp.arange(half, dtype=F32) / half)
    ang = pos.astype(F32)[:, None] * freqs[None, :]
    cos, sin = jnp.cos(ang), jnp.sin(ang)
    cos64 = jnp.concatenate([cos, cos], axis=1)
    sin64 = jnp.concatenate([-sin, sin], axis=1)
    return jnp.concatenate([cos64, cos64], axis=1), jnp.concatenate([sin64, sin64], axis=1)


def _group_mean_matrix(width):
    idx = np.arange(width) // HEAD_DIM
    return jnp.asarray((idx[:, None] == idx[None, :]).astype(np.float32) / HEAD_DIM, BF16)


def _split2(x):
    hi = x.astype(BF16)
    lo = (x - hi.astype(F32)).astype(BF16)
    return hi, lo


def _group_mean(x2, gmat):
    hi, lo = _split2(x2)
    return (jnp.dot(hi, gmat, preferred_element_type=F32)
            + jnp.dot(lo, gmat, preferred_element_type=F32))


def _nt_dot(a, b):
    return lax.dot_general(a, b, (((1,), (1,)), ((), ())), preferred_element_type=F32)


def _proj_kernel(x_ref, g_ref, w_ref, cos_ref, sin_ref, qg_ref, kg_ref, lb_ref, gm_ref, ex_ref,
                 q_out, k_out, v_out, kd_out, vd_out, iq_out, sg_out, ik_out, ikd_out, gate_out,
                 bq_out, bk_out, bv_out, blf_out, cq_out, ck_out, cv_out, *, layer):
    x = x_ref[...]
    tm = x.shape[0]
    ms = jnp.mean(x * x, axis=-1, keepdims=True)
    h = (x * lax.rsqrt(ms + EPS) * g_ref[...]).astype(BF16)
    cos = cos_ref[...]
    sin = sin_ref[...]
    lane = lax.broadcasted_iota(I32, (tm, LANES), 1)
    first_half = (lane % HEAD_DIM) < (HEAD_DIM // 2)
    low_head = lane < HEAD_DIM
    gmat = gm_ref[...]

    def proj(c0, width):
        return jnp.dot(h, w_ref[:, c0:c0 + width], preferred_element_type=F32)

    def rope(xb):
        swapped = jnp.where(first_half, pltpu.roll(xb, LANES - HEAD_DIM // 2, 1),
                            pltpu.roll(xb, HEAD_DIM // 2, 1))
        return xb * cos + swapped * sin

    def head_norm(xb, gain):
        return xb * lax.rsqrt(_group_mean(xb * xb, gmat) + EPS) * gain

    def dup_heads(xb):
        other = pltpu.roll(xb, HEAD_DIM, 1)
        return jnp.where(low_head, xb, other), jnp.where(low_head, other, xb)

    aq = proj(_AQ, A_WIDTH)
    for c in range(A_WIDTH // LANES):
        blk = rope(head_norm(aq[:, c * LANES:(c + 1) * LANES], qg_ref[...]))
        q_out[:, c * LANES:(c + 1) * LANES] = (blk * (HEAD_DIM ** -0.5)).astype(BF16)

    akv = proj(_AK, 2 * KV_WIDTH)
    k = rope(head_norm(akv[:, :KV_WIDTH], kg_ref[...]))
    v = akv[:, KV_WIDTH:]
    k_out[...] = k
    v_out[...] = v
    k0, k1 = dup_heads(k)
    kd_out[:, :LANES] = k0.astype(BF16)
    kd_out[:, LANES:] = k1.astype(BF16)
    v0, v1 = dup_heads(v)
    vd_out[:, :LANES] = v0.astype(BF16)
    vd_out[:, LANES:] = v1.astype(BF16)

    ikw = proj(_IKW, LANES)
    ikr = rope(ikw)
    ik_out[...] = ikr[:, :HEAD_DIM]
    ikd_out[...] = dup_heads(ikr)[0].astype(BF16)
    sg_out[...] = jnp.where(ikw >= 0, 1.0, -1.0).astype(F32)
    whi, wlo = _split2(jnp.abs(ikw) * IDX_W_SCALE)
    wexp = (jnp.dot(whi, ex_ref[...], preferred_element_type=F32)
            + jnp.dot(wlo, ex_ref[...], preferred_element_type=F32))
    iq = proj(_IQ, IDX_HEADS * HEAD_DIM)
    for c in range(IDX_HEADS * HEAD_DIM // LANES):
        sl = slice(c * LANES, (c + 1) * LANES)
        iq_out[:, sl] = (rope(iq[:, sl]) * wexp[:, sl]).astype(BF16)

    def silu(z):
        return z / (1.0 + jnp.exp(-z))

    gate_out[:, :A_WIDTH] = silu(proj(_AG, A_WIDTH)).astype(BF16)
    gate_out[:, A_WIDTH:A_WIDTH + B_WIDTH] = silu(proj(_BG, B_WIDTH)).astype(BF16)
    gate_out[:, A_WIDTH + B_WIDTH:] = silu(proj(_CG, C_WIDTH)).astype(BF16)

    logits = lb_ref[...]
    e = jnp.exp(logits - jnp.max(logits, axis=0, keepdims=True))
    sm = e / jnp.sum(e, axis=0, keepdims=True)
    lb = jnp.zeros((1, B_WIDTH), F32)
    for j in range(1, layer + 1):
        lb = lb + sm[j:j + 1, :]
    bf = proj(_BF, B_WIDTH)
    f = lb + (1.0 - lb) / (1.0 + jnp.exp(-bf))
    bq_out[...] = proj(_BQ, B_WIDTH)
    bk_out[...] = (1.0 - lb) / (1.0 + jnp.exp(bf))
    bv_out[...] = proj(_BI, B_WIDTH)
    blf_out[...] = jnp.log(f)

    cq = proj(_CQ, C_WIDTH)
    ck = proj(_CK, C_WIDTH)
    for c in range(C_WIDTH // LANES):
        sl = slice(c * LANES, (c + 1) * LANES)
        cq_out[:, sl] = rope(cq[:, sl])
        ck_out[:, sl] = rope(ck[:, sl]) * (HEAD_DIM ** -0.5)
    cv_out[...] = proj(_CV, C_WIDTH)


def _proj_call(x2d, seq, pos, norm_g, w_p, q_gain, k_gain, lb_logits, layer, tm):
    n = x2d.shape[0]
    tm = min(tm, n)
    cos, sin = _rope_tables(pos)
    if seq % tm == 0:
        per_seq = seq // tm
        tab_map = lambda r: (r % per_seq, 0)
    else:
        cos = jnp.tile(cos, (n // seq, 1))
        sin = jnp.tile(sin, (n // seq, 1))
        tab_map = lambda r: (r, 0)
    expand = np.zeros((LANES, IDX_HEADS * HEAD_DIM), np.float32)
    for hd in range(IDX_HEADS):
        expand[HEAD_DIM + hd, hd * HEAD_DIM:(hd + 1) * HEAD_DIM] = 1.0
    const = lambda r: (0, 0)
    row = lambda r: (r, 0)
    widths = [(A_WIDTH, BF16), (KV_WIDTH, F32), (KV_WIDTH, F32), (2 * KV_WIDTH, BF16), (2 * KV_WIDTH, BF16),
              (IDX_HEADS * HEAD_DIM, BF16), (LANES, F32), (HEAD_DIM, F32), (LANES, BF16), (D_MODEL, BF16),
              (B_WIDTH, F32), (B_WIDTH, F32), (B_WIDTH, F32), (B_WIDTH, F32),
              (C_WIDTH, F32), (C_WIDTH, F32), (C_WIDTH, F32)]
    return pl.pallas_call(
        functools.partial(_proj_kernel, layer=layer),
        grid=(n // tm,),
        in_specs=[pl.BlockSpec((tm, D_MODEL), row),
                  pl.BlockSpec((1, D_MODEL), const),
                  pl.BlockSpec((D_MODEL, _N_PROJ), const),
                  pl.BlockSpec((tm, LANES), tab_map),
                  pl.BlockSpec((tm, LANES), tab_map),
                  pl.BlockSpec((1, LANES), const),
                  pl.BlockSpec((1, LANES), const),
                  pl.BlockSpec(lb_logits.shape, const),
                  pl.BlockSpec((LANES, LANES), const),
                  pl.BlockSpec((LANES, IDX_HEADS * HEAD_DIM), const)],
        out_specs=[pl.BlockSpec((tm, w), row) for w, _ in widths],
        out_shape=[jax.ShapeDtypeStruct((n, w), dt) for w, dt in widths],
        compiler_params=pltpu.CompilerParams(dimension_semantics=("parallel",),
                                             vmem_limit_bytes=VMEM_LIMIT),
        name=f"proj_l{layer}",
    )(x2d, norm_g.reshape(1, D_MODEL), w_p, cos, sin,
      jnp.tile(q_gain, 2).reshape(1, LANES), jnp.tile(k_gain, 2).reshape(1, LANES),
      lb_logits, _group_mean_matrix(LANES), jnp.asarray(expand, BF16))


def _attn_kernel(q_ref, iq_ref, sg_ref, kd_ref, vd_ref, ikd_ref, o_ref, key_scr, lim_scr,
                 *, topk, qpos0, n_keys):
    tq = q_ref.shape[1]
    s = kd_ref.shape[1]
    i = pl.program_id(1)
    lane = lax.broadcasted_iota(I32, (tq, LANES), 1)
    halves = (lane < HEAD_DIM, lane >= HEAD_DIM)

    ikd = ikd_ref[0]
    sg = sg_ref[0]
    sc = jnp.zeros((tq, s), F32)
    for hd in range(IDX_HEADS):
        qb = iq_ref[0, :, (hd // 2) * LANES:(hd // 2 + 1) * LANES]
        qm = jnp.where(halves[hd % 2], qb, jnp.zeros_like(qb))
        y = _nt_dot(qm, ikd)
        sc = sc + sg[:, HEAD_DIM + hd:HEAD_DIM + hd + 1] * jnp.maximum(y, 0.0)

    tpos = qpos0 + i * tq + lax.broadcasted_iota(I32, (tq, 1), 0)
    lim = jnp.minimum((tpos // CHUNK + 1) * CHUNK, n_keys)
    kpos = lax.broadcasted_iota(I32, (tq, s), 1)
    adm = kpos < lim
    bits = lax.bitcast_convert_type(sc, I32)
    key = bits ^ ((bits >> 31) & 0x7FFFFFFF)
    key_scr[...] = jnp.where(adm, key, INT_MIN)

    def bit_step(j, tu):
        cand = tu | lax.shift_left(jnp.int32(1), 31 - j)
        cnt = jnp.sum(jnp.where(key_scr[...] >= (cand ^ INT_MIN), 1.0, 0.0), axis=1, keepdims=True)
        return jnp.where(cnt >= topk, cand, tu)

    thr = lax.fori_loop(0, 32, bit_step, jnp.zeros((tq, 1), I32)) ^ INT_MIN

    key = key_scr[...]
    gt = key > thr
    eq = (key == thr) & adm
    cnt_gt = jnp.sum(jnp.where(gt, 1.0, 0.0), axis=1, keepdims=True)
    cnt_eq = jnp.sum(jnp.where(eq, 1.0, 0.0), axis=1, keepdims=True)
    want_eq = topk - cnt_gt

    lim_scr[...] = jnp.full((tq, LANES), POS_BIG, I32)

    @pl.when(jnp.max(cnt_gt + cnt_eq) > topk)
    def _():
        eqpos = jnp.where(eq, kpos, POS_BIG)
        nbits = max(1, int(s).bit_length())

        def pos_step(j, xlim):
            cand = xlim | lax.shift_left(jnp.int32(1), nbits - 1 - j)
            cnt = jnp.sum(jnp.where(eqpos < cand, 1.0, 0.0), axis=1, keepdims=True)
            return jnp.where(cnt <= want_eq, cand, xlim)

        xlim = lax.fori_loop(0, nbits, pos_step, jnp.zeros((tq, 1), I32))
        lim_scr[...] = jnp.broadcast_to(xlim, (tq, LANES))

    sel = (gt | (eq & (kpos < lim_scr[:, :1]))) & adm

    for c in range(A_WIDTH // LANES):
        g = (2 * c) // (A_HEADS // A_KV_HEADS)
        qb = q_ref[0, :, c * LANES:(c + 1) * LANES]
        kd = kd_ref[0, :, g * LANES:(g + 1) * LANES]
        vd = vd_ref[0, :, g * LANES:(g + 1) * LANES]
        outs = []
        for e in range(2):
            qm = jnp.where(halves[e], qb, jnp.zeros_like(qb))
            lg = jnp.where(sel, _nt_dot(qm, kd), NEG_BIG)
            m = jnp.max(lg, axis=1, keepdims=True)
            p = jnp.exp(lg - m)
            denom = jnp.sum(p, axis=1, keepdims=True)
            o = jnp.dot(p.astype(BF16), vd, preferred_element_type=F32)
            outs.append(o / denom)
        o_ref[0, :, c * LANES:(c + 1) * LANES] = jnp.where(halves[0], outs[0], outs[1])


def _attn_call(q, iq, sg, kd, vd, ikd, *, topk, qpos0, n_keys, tq, name):
    b, t, _ = q.shape
    s = kd.shape[1]
    qmap = lambda bi, i: (bi, i, 0)
    kmap = lambda bi, i: (bi, 0, 0)
    return pl.pallas_call(
        functools.partial(_attn_kernel, topk=topk, qpos0=qpos0, n_keys=n_keys),
        grid=(b, t // tq),
        in_specs=[pl.BlockSpec((1, tq, A_WIDTH), qmap),
                  pl.BlockSpec((1, tq, IDX_HEADS * HEAD_DIM), qmap),
                  pl.BlockSpec((1, tq, LANES), qmap),
                  pl.BlockSpec((1, s, 2 * KV_WIDTH), kmap),
                  pl.BlockSpec((1, s, 2 * KV_WIDTH), kmap),
                  pl.BlockSpec((1, s, LANES), kmap)],
        out_specs=pl.BlockSpec((1, tq, A_WIDTH), qmap),
        out_shape=jax.ShapeDtypeStruct((b, t, A_WIDTH), F32),
        scratch_shapes=[pltpu.VMEM((tq, s), I32), pltpu.VMEM((tq, LANES), I32)],
        compiler_params=pltpu.CompilerParams(dimension_semantics=("parallel", "arbitrary"),
                                             vmem_limit_bytes=VMEM_LIMIT),
        name=name,
    )(q, iq, sg, kd, vd, ikd)


def _retention_log_decay():
    return jnp.log(1.0 - jnp.exp(jnp.linspace(math.log(1.0 / 32), math.log(1.0 / 512), REC_HEADS, dtype=F32)))


def _rec_kernel(bq_ref, bk_ref, bv_ref, blf_ref, cq_ref, ck_ref, cv_ref, s0b_ref, s0c_ref,
                hg_ref, rg_ref, tri_ref, dmat_ref, inner_ref, kscale_ref, rdec_ref, gm_ref,
                yb_ref, yc_ref, sb_out, sc_out, sb_scr, sc_scr, *, chunk, n_chunks):
    j = pl.program_id(1)
    width = B_WIDTH

    @pl.when(j == 0)
    def _():
        sb_scr[...] = s0b_ref[0]
        sc_scr[...] = s0c_ref[0]

    rowh = lax.broadcasted_iota(I32, (width, width), 0) // HEAD_DIM
    colh = lax.broadcasted_iota(I32, (width, width), 1) // HEAD_DIM
    block_diag = rowh == colh
    lane_h = lax.broadcasted_iota(I32, (chunk, width), 1) // HEAD_DIM
    causal = (lax.broadcasted_iota(I32, (chunk, chunk), 0)
              >= lax.broadcasted_iota(I32, (chunk, chunk), 1))
    gmat = gm_ref[...]
    tri = tri_ref[...]
    tn = (((0,), (0,)), ((), ()))

    def head_rms(o, gain):
        return o * lax.rsqrt(_group_mean(o * o, gmat) + EPS) * gain

    def intra(qx, kb16, vb16, weight_fn):
        acc = jnp.zeros((chunk, width), F32)
        for hd in range(REC_HEADS):
            qh = jnp.where(lane_h == hd, qx, 0.0).astype(BF16)
            att = weight_fn(hd, _nt_dot(qh, kb16))
            oh = jnp.dot(att.astype(BF16), vb16, preferred_element_type=F32)
            acc = acc + jnp.where(lane_h == hd, oh, 0.0)
        return acc

    def body(c, carry):
        r0 = pl.multiple_of(c * chunk, chunk)
        rows = pl.ds(r0, chunk)

        q = bq_ref[0, rows, :]
        k = bk_ref[0, rows, :]
        v = bv_ref[0, rows, :]
        lf = blf_ref[0, rows, :]
        a1 = lf.astype(BF16)
        r1 = lf - a1.astype(F32)
        a2 = r1.astype(BF16)
        a3 = (r1 - a2.astype(F32)).astype(BF16)
        bc = jnp.dot(tri, jnp.concatenate([a1, a2, a3], axis=0), preferred_element_type=F32)
        mid = bc[chunk // 2 - 1:chunk // 2, :]
        last = bc[chunk - 1:chunk, :]
        vb16 = v.astype(BF16)
        st = sb_scr[...]
        o = _nt_dot((q * jnp.exp(bc)).astype(BF16), st.astype(BF16))
        o = o + intra(q * jnp.exp(bc - mid), (k * jnp.exp(mid - bc)).astype(BF16), vb16,
                      lambda hd, a: jnp.where(causal, a, 0.0))
        upd = lax.dot_general(vb16, (k * jnp.exp(last - bc)).astype(BF16), tn, preferred_element_type=F32)
        sb_scr[...] = st * jnp.exp(last) + jnp.where(block_diag, upd, 0.0)
        yb_ref[0, rows, :] = head_rms(o, hg_ref[...])

        q = cq_ref[0, rows, :]
        k = ck_ref[0, rows, :]
        vb16 = cv_ref[0, rows, :].astype(BF16)
        rt = sc_scr[...]
        o = _nt_dot((q * inner_ref[...]).astype(BF16), rt.astype(BF16))
        o = o + intra(q, k.astype(BF16), vb16,
                      lambda hd, a: a * dmat_ref[hd * chunk:(hd + 1) * chunk, :])
        upd = lax.dot_general(vb16, (k * kscale_ref[...]).astype(BF16), tn, preferred_element_type=F32)
        sc_scr[...] = rt * rdec_ref[...] + jnp.where(block_diag, upd, 0.0)
        yc_ref[0, rows, :] = head_rms(o, rg_ref[...])
        return carry

    lax.fori_loop(0, n_chunks, body, 0)

    @pl.when(j == pl.num_programs(1) - 1)
    def _():
        sb_out[0] = sb_scr[...]
        sc_out[0] = sc_scr[...]


def _state_to_blockdiag(s):
    b = s.shape[0]
    eye = jnp.eye(REC_HEADS, dtype=s.dtype)
    return jnp.einsum('bhkv,hg->bhvgk', s, eye).reshape(b, B_WIDTH, B_WIDTH)


def _state_from_blockdiag(sbd):
    b = sbd.shape[0]
    s5 = sbd.reshape(b, REC_HEADS, HEAD_DIM, REC_HEADS, HEAD_DIM)
    return jnp.einsum('bhvhk->bhkv', s5)


def _rec_call(bq, bk, bv, blf, cq, ck, cv, s0b, s0c, hgain, rgain, name):
    b, t, _ = bq.shape
    chunk = min(CHUNK, t)
    tb = min(t, 8 * chunk)
    n_chunks = tb // chunk
    lg = _retention_log_decay()
    n = jnp.arange(chunk, dtype=F32)
    diff = n[:, None] - n[None, :]
    dmat = jnp.where(diff >= 0, jnp.exp(jnp.where(diff >= 0, diff, 0.0)[None] * lg[:, None, None]), 0.0)
    dmat = dmat.reshape(REC_HEADS * chunk, chunk)
    per_lane = lambda a: jnp.repeat(a, HEAD_DIM, axis=-1)
    inner = per_lane(jnp.exp((n[:, None] + 1.0) * lg[None, :]))
    kscale = per_lane(jnp.exp((chunk - 1.0 - n)[:, None] * lg[None, :]))
    rdec = per_lane(jnp.exp(chunk * lg)[None, :])
    tri = jnp.tile(jnp.tril(jnp.ones((chunk, chunk), F32)), (1, 3)).astype(BF16)
    seq = lambda bi, j: (bi, j, 0)
    per_b = lambda bi, j: (bi, 0, 0)
    const = lambda bi, j: (0, 0)
    stream = pl.BlockSpec((1, tb, B_WIDTH), seq)
    state = pl.BlockSpec((1, B_WIDTH, B_WIDTH), per_b)
    return pl.pallas_call(
        functools.partial(_rec_kernel, chunk=chunk, n_chunks=n_chunks),
        grid=(b, t // tb),
        in_specs=[stream] * 7 + [state, state,
                                 pl.BlockSpec((1, B_WIDTH), const), pl.BlockSpec((1, B_WIDTH), const),
                                 pl.BlockSpec((chunk, 3 * chunk), const),
                                 pl.BlockSpec((REC_HEADS * chunk, chunk), const),
                                 pl.BlockSpec((chunk, B_WIDTH), const), pl.BlockSpec((chunk, B_WIDTH), const),
                                 pl.BlockSpec((1, B_WIDTH), const),
                                 pl.BlockSpec((B_WIDTH, B_WIDTH), const)],
        out_specs=[stream, stream, state, state],
        out_shape=[jax.ShapeDtypeStruct((b, t, B_WIDTH), F32), jax.ShapeDtypeStruct((b, t, C_WIDTH), F32),
                   jax.ShapeDtypeStruct((b, B_WIDTH, B_WIDTH), F32),
                   jax.ShapeDtypeStruct((b, C_WIDTH, C_WIDTH), F32)],
        scratch_shapes=[pltpu.VMEM((B_WIDTH, B_WIDTH), F32), pltpu.VMEM((C_WIDTH, C_WIDTH), F32)],
        compiler_params=pltpu.CompilerParams(dimension_semantics=("parallel", "arbitrary"),
                                             vmem_limit_bytes=VMEM_LIMIT),
        name=name,
    )(bq, bk, bv, blf, cq, ck, cv, s0b, s0c,
      jnp.tile(hgain, REC_HEADS).reshape(1, B_WIDTH), jnp.tile(rgain, REC_HEADS).reshape(1, C_WIDTH),
      tri, dmat, inner, kscale, rdec, _group_mean_matrix(B_WIDTH))


def _out_kernel(x_ref, oa_ref, yb_ref, yc_ref, gate_ref, w_ref, y_ref):
    gate = gate_ref[...].astype(F32)
    ya = (oa_ref[...] * gate[:, :A_WIDTH]).astype(BF16)
    yb = (yb_ref[...] * gate[:, A_WIDTH:A_WIDTH + B_WIDTH]).astype(BF16)
    yc = (yc_ref[...] * gate[:, A_WIDTH + B_WIDTH:]).astype(BF16)
    acc = jnp.dot(ya, w_ref[:A_WIDTH, :], preferred_element_type=F32)
    acc = acc + jnp.dot(yb, w_ref[A_WIDTH:A_WIDTH + B_WIDTH, :], preferred_element_type=F32)
    acc = acc + jnp.dot(yc, w_ref[A_WIDTH + B_WIDTH:, :], preferred_element_type=F32)
    y_ref[...] = x_ref[...] + acc


def _out_call(x2d, oa, yb, yc, gate, w_out16, tm, name):
    n = x2d.shape[0]
    tm = min(tm, n)
    row = lambda r: (r, 0)
    return pl.pallas_call(
        _out_kernel,
        grid=(n // tm,),
        in_specs=[pl.BlockSpec((tm, D_MODEL), row), pl.BlockSpec((tm, A_WIDTH), row),
                  pl.BlockSpec((tm, B_WIDTH), row), pl.BlockSpec((tm, C_WIDTH), row),
                  pl.BlockSpec((tm, D_MODEL), row), pl.BlockSpec((D_MODEL, D_MODEL), lambda r: (0, 0))],
        out_specs=pl.BlockSpec((tm, D_MODEL), row),
        out_shape=jax.ShapeDtypeStruct((n, D_MODEL), F32),
        compiler_params=pltpu.CompilerParams(dimension_semantics=("parallel",),
                                             vmem_limit_bytes=VMEM_LIMIT),
        name=name,
    )(x2d, oa, yb, yc, gate, w_out16)


def _dup_kv(x):
    b, s = x.shape[:2]
    return jnp.repeat(x.reshape(b, s, A_KV_HEADS, 1, HEAD_DIM), 2, axis=3).reshape(b, s, 2 * KV_WIDTH).astype(BF16)


def _mixer_layer(x, pos0, past, s_hgrn, s_ret, layer, w_p, w_out16, norm_g, q_gain, k_gain,
                 lb_logits, hgain, rgain, tag):
    b, t, _ = x.shape
    n = b * t
    x2d = x.reshape(n, D_MODEL)
    pos = pos0 + jnp.arange(t)
    (q, k, v, kd, vd, iq, sg, ik, ikd, gate, bq, bk, bv, blf, cq, ck, cv) = _proj_call(
        x2d, t, pos, norm_g, w_p, q_gain, k_gain, lb_logits, layer, 256)

    three = lambda a: a.reshape(b, t, a.shape[-1])
    kd3, vd3, ikd3 = three(kd), three(vd), three(ikd)
    if past is not None:
        pk, pv, pik = past
        kd3 = jnp.concatenate([_dup_kv(pk), kd3], axis=1)
        vd3 = jnp.concatenate([_dup_kv(pv), vd3], axis=1)
        ikd3 = jnp.concatenate([jnp.concatenate([pik, pik], axis=-1).astype(BF16), ikd3], axis=1)
    n_keys = kd3.shape[1]
    s_pad = -(-n_keys // LANES) * LANES
    if s_pad != n_keys:
        padw = ((0, 0), (0, s_pad - n_keys), (0, 0))
        kd3, vd3, ikd3 = jnp.pad(kd3, padw), jnp.pad(vd3, padw), jnp.pad(ikd3, padw)
    topk = min(TOPK_MAX, n_keys // 4)
    oa = _attn_call(three(q), three(iq), three(sg), kd3, vd3, ikd3, topk=topk, qpos0=pos0,
                    n_keys=n_keys, tq=min(t, 128), name=f"attn_{tag}")

    yb, yc, sb, sc = _rec_call(three(bq), three(bk), three(bv), three(blf), three(cq), three(ck), three(cv),
                               _state_to_blockdiag(s_hgrn), _state_to_blockdiag(s_ret), hgain, rgain,
                               name=f"rec_{tag}")
    out = _out_call(x2d, oa.reshape(n, A_WIDTH), yb.reshape(n, B_WIDTH), yc.reshape(n, C_WIDTH), gate,
                    w_out16, 512, name=f"out_{tag}")
    return (out.reshape(b, t, D_MODEL),
            (k.reshape(b, t, A_KV_HEADS, HEAD_DIM), v.reshape(b, t, A_KV_HEADS, HEAD_DIM), ik,
             _state_from_blockdiag(sb), _state_from_blockdiag(sc)))


def kernel(x_prompt, x_sample, cache_k, cache_v, cache_idx_k, state_hgrn, state_ret, norm_g, w_in,
           q_norm_g, k_norm_g, hgrn_lb_logits, hgrn_norm_g, ret_norm_g, w_out):
    depth = w_in.shape[0]
    bp, tp = x_prompt.shape[:2]
    bs, ts = x_sample.shape[:2]
    past = cache_k.shape[2]
    zero_state = jnp.zeros((bp, REC_HEADS, HEAD_DIM, HEAD_DIM), F32)

    yp, ys = x_prompt, x_sample
    outs_p, outs_s = [], []
    for l in range(depth):
        w_p = _prep_w_in(w_in[l])
        w_o = w_out[l].astype(BF16)
        args = (l, w_p, w_o, norm_g[l], q_norm_g[l], k_norm_g[l], hgrn_lb_logits, hgrn_norm_g[l], ret_norm_g[l])
        yp, st = _mixer_layer(yp, 0, None, zero_state, zero_state, *args, tag=f"p{l}")
        outs_p.append(st)
        ys, st = _mixer_layer(ys, past, (cache_k[l], cache_v[l], cache_idx_k[l]),
                              state_hgrn[l], state_ret[l], *args, tag=f"s{l}")
        outs_s.append(st)

    def stack(outs, i, shape):
        return jnp.stack([o[i] for o in outs]).reshape(shape)

    return (yp, ys,
            stack(outs_p, 0, (depth, bp, tp, A_KV_HEADS, HEAD_DIM)),
            stack(outs_p, 1, (depth, bp, tp, A_KV_HEADS, HEAD_DIM)),
            stack(outs_p, 2, (depth, bp, tp, HEAD_DIM)),
            stack(outs_p, 3, (depth, bp, REC_HEADS, HEAD_DIM, HEAD_DIM)),
            stack(outs_p, 4, (depth, bp, REC_HEADS, HEAD_DIM, HEAD_DIM)),
            stack(outs_s, 0, (depth, bs, ts, A_KV_HEADS, HEAD_DIM)),
            stack(outs_s, 1, (depth, bs, ts, A_KV_HEADS, HEAD_DIM)),
            stack(outs_s, 2, (depth, bs, ts, HEAD_DIM)),
            stack(outs_s, 3, (depth, bs, REC_HEADS, HEAD_DIM, HEAD_DIM)),
            stack(outs_s, 4, (depth, bs, REC_HEADS, HEAD_DIM, HEAD_DIM)))
```

```python
import functools
import math

import numpy as np
import jax
import jax.numpy as jnp
from jax import lax
from jax.experimental import pallas as pl
from jax.experimental.pallas import tpu as pltpu

F32 = jnp.float32
BF16 = jnp.bfloat16
I32 = jnp.int32

D_MODEL = 1024
HEAD_DIM = 64
CHUNK = 64
A_WIDTH = 512
A_HEADS = 8
A_KV_HEADS = 2
KV_WIDTH = A_KV_HEADS * HEAD_DIM
IDX_HEADS = 8
IDX_W_SCALE = (IDX_HEADS * HEAD_DIM) ** -0.5
TOPK_MAX = 256
B_WIDTH = 256
C_WIDTH = 256
REC_HEADS = 4
ROPE_THETA = 10000.0
EPS = 1e-6
LANES = 128
VMEM_LIMIT = 48 * 1024 * 1024

_SIZES = (A_WIDTH, KV_WIDTH, KV_WIDTH, A_WIDTH, IDX_HEADS * HEAD_DIM, HEAD_DIM, IDX_HEADS,
          B_WIDTH, B_WIDTH, B_WIDTH, B_WIDTH, C_WIDTH, C_WIDTH, C_WIDTH, C_WIDTH)
_SPLIT_IDX = tuple(int(s) for s in np.cumsum(_SIZES)[:-1])

_AQ, _AK, _AV, _AG, _IQ, _IKW = 0, 512, 640, 768, 1280, 1792
_BQ, _BF, _BI, _BG, _CQ, _CK, _CV, _CG = 1920, 2176, 2432, 2688, 2944, 3200, 3456, 3712
_N_PROJ = 3968

INT_MIN = -(2 ** 31)
NEG_BIG = -1e30
POS_BIG = 1 << 20
LOG2_E = math.log2(math.e)


def _prep_w_in(w):
    (aq, ak, av, ag, iq, ik, iw, bq, bf, bi, bg, cq, ck, cv, cg) = jnp.split(w, _SPLIT_IDX, axis=1)
    pad = jnp.zeros((w.shape[0], LANES - HEAD_DIM - IDX_HEADS), w.dtype)
    return jnp.concatenate([aq, ak, av, ag, iq, ik, iw, pad, bq, bf, bi, bg, cq, ck, cv, cg],
                           axis=1).astype(BF16)


def _rope_tables(pos):
    half = HEAD_DIM // 2
    freqs = ROPE_THETA ** (-jnp.arange(half, dtype=F32) / half)
    ang = pos.astype(F32)[:, None] * freqs[None, :]
    cos, sin = jnp.cos(ang), jnp.sin(ang)
    cos64 = jnp.concatenate([cos, cos], axis=1)
    sin64 = jnp.concatenate([-sin, sin], axis=1)
    return jnp.concatenate([cos64, cos64], axis=1), jnp.concatenate([sin64, sin64], axis=1)


def _group_mean_matrix(width):
    idx = np.arange(width) // HEAD_DIM
    return jnp.asarray((idx[:, None] == idx[None, :]).astype(np.float32) / HEAD_DIM, BF16)


def _split2(x):
    hi = x.astype(BF16)
    lo = (x - hi.astype(F32)).astype(BF16)
    return hi, lo


def _group_mean(x2, gmat):
    hi, lo = _split2(x2)
    return (jnp.dot(hi, gmat, preferred_element_type=F32)
            + jnp.dot(lo, gmat, preferred_element_type=F32))


def _nt_dot(a, b):
    return lax.dot_general(a, b, (((1,), (1,)), ((), ())), preferred_element_type=F32)


def _proj_kernel(x_ref, g_ref, w_ref, cos_ref, sin_ref, qg_ref, kg_ref, lb_ref, gm_ref, ex_ref,
                 q_out, k_out, v_out, kd_out, vd_out, iq_out, sg_out, ik_out, ikd_out, gate_out,
                 bq_out, bk_out, bv_out, blf_out, cq_out, ck_out, cv_out, *, layer):
    x = x_ref[...]
    tm = x.shape[0]
    ms = jnp.mean(x * x, axis=-1, keepdims=True)
    h = (x * lax.rsqrt(ms + EPS) * g_ref[...]).astype(BF16)
    cos = cos_ref[...]
    sin = sin_ref[...]
    lane = lax.broadcasted_iota(I32, (tm, LANES), 1)
    first_half = (lane % HEAD_DIM) < (HEAD_DIM // 2)
    low_head = lane < HEAD_DIM
    gmat = gm_ref[...]

    def proj(c0, width):
        return jnp.dot(h, w_ref[:, c0:c0 + width], preferred_element_type=F32)

    def rope(xb):
        swapped = jnp.where(first_half, pltpu.roll(xb, LANES - HEAD_DIM // 2, 1),
                            pltpu.roll(xb, HEAD_DIM // 2, 1))
        return xb * cos + swapped * sin

    def head_norm(xb, gain):
        return xb * lax.rsqrt(_group_mean(xb * xb, gmat) + EPS) * gain

    def dup_heads(xb):
        other = pltpu.roll(xb, HEAD_DIM, 1)
        return jnp.where(low_head, xb, other), jnp.where(low_head, other, xb)

    aq = proj(_AQ, A_WIDTH)
    for c in range(A_WIDTH // LANES):
        blk = rope(head_norm(aq[:, c * LANES:(c + 1) * LANES], qg_ref[...]))
        q_out[:, c * LANES:(c + 1) * LANES] = (blk * (HEAD_DIM ** -0.5 * LOG2_E)).astype(BF16)

    akv = proj(_AK, 2 * KV_WIDTH)
    k = rope(head_norm(akv[:, :KV_WIDTH], kg_ref[...]))
    v = akv[:, KV_WIDTH:]
    k_out[...] = k
    v_out[...] = v
    k0, k1 = dup_heads(k)
    kd_out[:, :LANES] = k0.astype(BF16)
    kd_out[:, LANES:] = k1.astype(BF16)
    v0, v1 = dup_heads(v)
    vd_out[:, :LANES] = v0.astype(BF16)
    vd_out[:, LANES:] = v1.astype(BF16)

    ikw = proj(_IKW, LANES)
    ikr = rope(ikw)
    ik_out[...] = ikr[:, :HEAD_DIM]
    ikd_out[...] = dup_heads(ikr)[0].astype(BF16)
    sg_out[...] = jnp.where(ikw >= 0, 1.0, -1.0).astype(F32)
    whi, wlo = _split2(jnp.abs(ikw) * IDX_W_SCALE)
    wexp = (jnp.dot(whi, ex_ref[...], preferred_element_type=F32)
            + jnp.dot(wlo, ex_ref[...], preferred_element_type=F32))
    iq = proj(_IQ, IDX_HEADS * HEAD_DIM)
    for c in range(IDX_HEADS * HEAD_DIM // LANES):
        sl = slice(c * LANES, (c + 1) * LANES)
        iq_out[:, sl] = (rope(iq[:, sl]) * wexp[:, sl]).astype(BF16)

    def silu(z):
        return z / (1.0 + jnp.exp(-z))

    gate_out[:, :A_WIDTH] = silu(proj(_AG, A_WIDTH)).astype(BF16)
    gate_out[:, A_WIDTH:A_WIDTH + B_WIDTH] = silu(proj(_BG, B_WIDTH)).astype(BF16)
    gate_out[:, A_WIDTH + B_WIDTH:] = silu(proj(_CG, C_WIDTH)).astype(BF16)

    logits = lb_ref[...]
    e = jnp.exp(logits - jnp.max(logits, axis=0, keepdims=True))
    sm = e / jnp.sum(e, axis=0, keepdims=True)
    lb = jnp.zeros((1, B_WIDTH), F32)
    for j in range(1, layer + 1):
        lb = lb + sm[j:j + 1, :]
    bf = proj(_BF, B_WIDTH)
    f = lb + (1.0 - lb) / (1.0 + jnp.exp(-bf))
    bq_out[...] = proj(_BQ, B_WIDTH)
    bk_out[...] = (1.0 - lb) / (1.0 + jnp.exp(bf))
    bv_out[...] = proj(_BI, B_WIDTH)
    blf_out[...] = jnp.log(f)

    cq = proj(_CQ, C_WIDTH)
    ck = proj(_CK, C_WIDTH)
    for c in range(C_WIDTH // LANES):
        sl = slice(c * LANES, (c + 1) * LANES)
        cq_out[:, sl] = rope(cq[:, sl])
        ck_out[:, sl] = rope(ck[:, sl]) * (HEAD_DIM ** -0.5)
    cv_out[...] = proj(_CV, C_WIDTH)


def _proj_call(x2d, seq, pos, norm_g, w_p, q_gain, k_gain, lb_logits, layer, tm):
    n = x2d.shape[0]
    tm = min(tm, n)
    cos, sin = _rope_tables(pos)
    if seq % tm == 0:
        per_seq = seq // tm
        tab_map = lambda r: (r % per_seq, 0)
    else:
        cos = jnp.tile(cos, (n // seq, 1))
        sin = jnp.tile(sin, (n // seq, 1))
        tab_map = lambda r: (r, 0)
    expand = np.zeros((LANES, IDX_HEADS * HEAD_DIM), np.float32)
    for hd in range(IDX_HEADS):
        expand[HEAD_DIM + hd, hd * HEAD_DIM:(hd + 1) * HEAD_DIM] = 1.0
    const = lambda r: (0, 0)
    row = lambda r: (r, 0)
    widths = [(A_WIDTH, BF16), (KV_WIDTH, F32), (KV_WIDTH, F32), (2 * KV_WIDTH, BF16), (2 * KV_WIDTH, BF16),
              (IDX_HEADS * HEAD_DIM, BF16), (LANES, F32), (HEAD_DIM, F32), (LANES, BF16), (D_MODEL, BF16),
              (B_WIDTH, F32), (B_WIDTH, F32), (B_WIDTH, F32), (B_WIDTH, F32),
              (C_WIDTH, F32), (C_WIDTH, F32), (C_WIDTH, F32)]
    return pl.pallas_call(
        functools.partial(_proj_kernel, layer=layer),
        grid=(n // tm,),
        in_specs=[pl.BlockSpec((tm, D_MODEL), row),
                  pl.BlockSpec((1, D_MODEL), const),
                  pl.BlockSpec((D_MODEL, _N_PROJ), const),
                  pl.BlockSpec((tm, LANES), tab_map),
                  pl.BlockSpec((tm, LANES), tab_map),
                  pl.BlockSpec((1, LANES), const),
                  pl.BlockSpec((1, LANES), const),
                  pl.BlockSpec(lb_logits.shape, const),
                  pl.BlockSpec((LANES, LANES), const),
                  pl.BlockSpec((LANES, IDX_HEADS * HEAD_DIM), const)],
        out_specs=[pl.BlockSpec((tm, w), row) for w, _ in widths],
        out_shape=[jax.ShapeDtypeStruct((n, w), dt) for w, dt in widths],
        compiler_params=pltpu.CompilerParams(dimension_semantics=("parallel",),
                                             vmem_limit_bytes=VMEM_LIMIT),
        name=f"proj_l{layer}",
    )(x2d, norm_g.reshape(1, D_MODEL), w_p, cos, sin,
      jnp.tile(q_gain, 2).reshape(1, LANES), jnp.tile(k_gain, 2).reshape(1, LANES),
      lb_logits, _group_mean_matrix(LANES), jnp.asarray(expand, BF16))


def _attn_kernel(q_ref, iq_ref, sg_ref, kd_ref, vd_ref, ikd_ref, prev_ref, o_ref, key_scr, bias_scr, lim_scr,
                 *, topk, tpos0, n_keys):
    del prev_ref
    tq = q_ref.shape[1]
    s = kd_ref.shape[1]
    lane = lax.broadcasted_iota(I32, (tq, LANES), 1)
    halves = (lane < HEAD_DIM, lane >= HEAD_DIM)

    ikd = ikd_ref[0]
    sg = sg_ref[0]
    sc = jnp.zeros((tq, s), F32)
    for hd in range(IDX_HEADS):
        qb = iq_ref[0, :, (hd // 2) * LANES:(hd // 2 + 1) * LANES]
        qm = jnp.where(halves[hd % 2], qb, jnp.zeros_like(qb))
        y = _nt_dot(qm, ikd)
        sc = sc + sg[:, HEAD_DIM + hd:HEAD_DIM + hd + 1] * jnp.maximum(y, 0.0)

    tpos = tpos0 + lax.broadcasted_iota(I32, (tq, 1), 0)
    lim = jnp.minimum((tpos // CHUNK + 1) * CHUNK, n_keys)
    kpos = lax.broadcasted_iota(I32, (tq, s), 1)
    adm = kpos < lim

    if min(s, n_keys) <= topk:
        bias_scr[...] = jnp.where(adm, 0.0, NEG_BIG)
    else:
        bits = lax.bitcast_convert_type(sc, I32)
        key_scr[...] = jnp.where(adm, bits ^ ((bits >> 31) & 0x7FFFFFFF), INT_MIN)
        tu = jnp.zeros((tq, 1), I32)
        for j in range(32):
            cand = tu | (INT_MIN if j == 0 else (1 << (31 - j)))
            cnt = jnp.sum(jnp.where(key_scr[...] >= (cand ^ INT_MIN), 1.0, 0.0), axis=1, keepdims=True)
            tu = jnp.where(cnt >= topk, cand, tu)
        thr = tu ^ INT_MIN

        key = key_scr[...]
        gt = key > thr
        eq = (key == thr) & adm
        cnt_gt = jnp.sum(jnp.where(gt, 1.0, 0.0), axis=1, keepdims=True)
        cnt_eq = jnp.sum(jnp.where(eq, 1.0, 0.0), axis=1, keepdims=True)
        want_eq = topk - cnt_gt

        lim_scr[...] = jnp.full((tq, LANES), POS_BIG, I32)

        @pl.when(jnp.max(cnt_gt + cnt_eq) > topk)
        def _():
            eqpos = jnp.where(eq, kpos, POS_BIG)
            nbits = max(1, int(s).bit_length())

            def pos_step(j, xlim):
                cand = xlim | lax.shift_left(jnp.int32(1), nbits - 1 - j)
                cnt = jnp.sum(jnp.where(eqpos < cand, 1.0, 0.0), axis=1, keepdims=True)
                return jnp.where(cnt <= want_eq, cand, xlim)

            xlim = lax.fori_loop(0, nbits, pos_step, jnp.zeros((tq, 1), I32))
            lim_scr[...] = jnp.broadcast_to(xlim, (tq, LANES))

        sel = (gt | (eq & (kpos < lim_scr[:, :1]))) & adm
        bias_scr[...] = jnp.where(sel, 0.0, NEG_BIG)

    ones = jnp.ones((s, LANES), BF16)
    for g in range(A_KV_HEADS):
        kd = kd_ref[0, :, g * LANES:(g + 1) * LANES]
        vd1 = jnp.concatenate([vd_ref[0, :, g * LANES:(g + 1) * LANES], ones], axis=1)
        for c in range(g * 2, g * 2 + 2):
            qb = q_ref[0, :, c * LANES:(c + 1) * LANES]
            outs = []
            for e in range(2):
                qm = jnp.where(halves[e], qb, jnp.zeros_like(qb))
                lg = _nt_dot(qm, kd) + bias_scr[...]
                p = jnp.exp2(lg - jnp.max(lg, axis=1, keepdims=True))
                ol = jnp.dot(p.astype(BF16), vd1, preferred_element_type=F32)
                outs.append(ol[:, :LANES] / ol[:, LANES:])
            o_ref[0, :, c * LANES:(c + 1) * LANES] = jnp.where(halves[0], outs[0], outs[1])


def _attn_call(q, iq, sg, kd, vd, ikd, prev, *, topk, qpos0, n_keys, tq, qblock, s_blk, name):
    b, t, _ = q.shape
    qmap = lambda bi: (bi, qblock, 0)
    kmap = lambda bi: (bi, 0, 0)
    return pl.pallas_call(
        functools.partial(_attn_kernel, topk=topk, tpos0=qpos0 + qblock * tq, n_keys=n_keys),
        grid=(b,),
        in_specs=[pl.BlockSpec((1, tq, A_WIDTH), qmap),
                  pl.BlockSpec((1, tq, IDX_HEADS * HEAD_DIM), qmap),
                  pl.BlockSpec((1, tq, LANES), qmap),
                  pl.BlockSpec((1, s_blk, 2 * KV_WIDTH), kmap),
                  pl.BlockSpec((1, s_blk, 2 * KV_WIDTH), kmap),
                  pl.BlockSpec((1, s_blk, LANES), kmap),
                  pl.BlockSpec(memory_space=pl.ANY)],
        out_specs=pl.BlockSpec((1, tq, A_WIDTH), qmap),
        out_shape=jax.ShapeDtypeStruct((b, t, A_WIDTH), F32),
        input_output_aliases={6: 0},
        scratch_shapes=[pltpu.VMEM((tq, s_blk), I32), pltpu.VMEM((tq, s_blk), F32), pltpu.VMEM((tq, LANES), I32)],
        compiler_params=pltpu.CompilerParams(dimension_semantics=("parallel",),
                                             vmem_limit_bytes=VMEM_LIMIT),
        name=name,
    )(q, iq, sg, kd, vd, ikd, prev)


def _attention(q, iq, sg, kd, vd, ikd, *, topk, qpos0, n_keys, tag):
    b, t, _ = q.shape
    tq = min(t, 128)
    out = jnp.zeros((b, t, A_WIDTH), F32)
    for qblock in range(t // tq):
        last_pos = qpos0 + (qblock + 1) * tq - 1
        visible = min((last_pos // CHUNK + 1) * CHUNK, n_keys)
        s_blk = min(-(-visible // LANES) * LANES, kd.shape[1])
        out = _attn_call(q, iq, sg, kd, vd, ikd, out, topk=topk, qpos0=qpos0, n_keys=n_keys, tq=tq,
                         qblock=qblock, s_blk=s_blk, name=f"attn_{tag}_q{qblock}")
    return out


def _retention_log_decay():
    return jnp.log(1.0 - jnp.exp(jnp.linspace(math.log(1.0 / 32), math.log(1.0 / 512), REC_HEADS, dtype=F32)))


def _rec_kernel(bq_ref, bk_ref, bv_ref, blf_ref, cq_ref, ck_ref, cv_ref, s0b_ref, s0c_ref,
                hg_ref, rg_ref, tri_ref, dmat_ref, inner_ref, kscale_ref, rdec_ref, gm_ref,
                yb_ref, yc_ref, sb_out, sc_out, sb_scr, sc_scr, *, chunk, n_chunks):
    j = pl.program_id(1)
    width = B_WIDTH

    @pl.when(j == 0)
    def _():
        sb_scr[...] = s0b_ref[0]
        sc_scr[...] = s0c_ref[0]

    rowh = lax.broadcasted_iota(I32, (width, width), 0) // HEAD_DIM
    colh = lax.broadcasted_iota(I32, (width, width), 1) // HEAD_DIM
    block_diag = rowh == colh
    lane_h = lax.broadcasted_iota(I32, (chunk, width), 1) // HEAD_DIM
    causal = (lax.broadcasted_iota(I32, (chunk, chunk), 0)
              >= lax.broadcasted_iota(I32, (chunk, chunk), 1))
    gmat = gm_ref[...]
    tri = tri_ref[...]
    tn = (((0,), (0,)), ((), ()))

    def head_rms(o, gain):
        return o * lax.rsqrt(_group_mean(o * o, gmat) + EPS) * gain

    def intra(qx, kb16, vb16, weight_fn):
        acc = jnp.zeros((chunk, width), F32)
        for hd in range(REC_HEADS):
            qh = jnp.where(lane_h == hd, qx, 0.0).astype(BF16)
            att = weight_fn(hd, _nt_dot(qh, kb16))
            oh = jnp.dot(att.astype(BF16), vb16, preferred_element_type=F32)
            acc = acc + jnp.where(lane_h == hd, oh, 0.0)
        return acc

    def body(c, carry):
        r0 = pl.multiple_of(c * chunk, chunk)
        rows = pl.ds(r0, chunk)

        q = bq_ref[0, rows, :]
        k = bk_ref[0, rows, :]
        v = bv_ref[0, rows, :]
        lf = blf_ref[0, rows, :]
        a1 = lf.astype(BF16)
        r1 = lf - a1.astype(F32)
        a2 = r1.astype(BF16)
        a3 = (r1 - a2.astype(F32)).astype(BF16)
        bc = jnp.dot(tri, jnp.concatenate([a1, a2, a3], axis=0), preferred_element_type=F32)
        mid = bc[chunk // 2 - 1:chunk // 2, :]
        last = bc[chunk - 1:chunk, :]
        vb16 = v.astype(BF16)
        st = sb_scr[...]
        o = _nt_dot((q * jnp.exp(bc)).astype(BF16), st.astype(BF16))
        o = o + intra(q * jnp.exp(bc - mid), (k * jnp.exp(mid - bc)).astype(BF16), vb16,
                      lambda hd, a: jnp.where(causal, a, 0.0))
        upd = lax.dot_general(vb16, (k * jnp.exp(last - bc)).astype(BF16), tn, preferred_element_type=F32)
        sb_scr[...] = st * jnp.exp(last) + jnp.where(block_diag, upd, 0.0)
        yb_ref[0, rows, :] = head_rms(o, hg_ref[...])

        q = cq_ref[0, rows, :]
        k = ck_ref[0, rows, :]
        vb16 = cv_ref[0, rows, :].astype(BF16)
        rt = sc_scr[...]
        o = _nt_dot((q * inner_ref[...]).astype(BF16), rt.astype(BF16))
        o = o + intra(q, k.astype(BF16), vb16,
                      lambda hd, a: a * dmat_ref[hd * chunk:(hd + 1) * chunk, :])
        upd = lax.dot_general(vb16, (k * kscale_ref[...]).astype(BF16), tn, preferred_element_type=F32)
        sc_scr[...] = rt * rdec_ref[...] + jnp.where(block_diag, upd, 0.0)
        yc_ref[0, rows, :] = head_rms(o, rg_ref[...])
        return carry

    lax.fori_loop(0, n_chunks, body, 0)

    @pl.when(j == pl.num_programs(1) - 1)
    def _():
        sb_out[0] = sb_scr[...]
        sc_out[0] = sc_scr[...]


def _state_to_blockdiag(s):
    b = s.shape[0]
    eye = jnp.eye(REC_HEADS, dtype=s.dtype)
    return jnp.einsum('bhkv,hg->bhvgk', s, eye).reshape(b, B_WIDTH, B_WIDTH)


def _state_from_blockdiag(sbd):
    b = sbd.shape[0]
    s5 = sbd.reshape(b, REC_HEADS, HEAD_DIM, REC_HEADS, HEAD_DIM)
    return jnp.einsum('bhvhk->bhkv', s5)


def _rec_call(bq, bk, bv, blf, cq, ck, cv, s0b, s0c, hgain, rgain, name):
    b, t, _ = bq.shape
    chunk = min(CHUNK, t)
    tb = min(t, 8 * chunk)
    n_chunks = tb // chunk
    lg = _retention_log_decay()
    n = jnp.arange(chunk, dtype=F32)
    diff = n[:, None] - n[None, :]
    dmat = jnp.where(diff >= 0, jnp.exp(jnp.where(diff >= 0, diff, 0.0)[None] * lg[:, None, None]), 0.0)
    dmat = dmat.reshape(REC_HEADS * chunk, chunk)
    per_lane = lambda a: jnp.repeat(a, HEAD_DIM, axis=-1)
    inner = per_lane(jnp.exp((n[:, None] + 1.0) * lg[None, :]))
    kscale = per_lane(jnp.exp((chunk - 1.0 - n)[:, None] * lg[None, :]))
    rdec = per_lane(jnp.exp(chunk * lg)[None, :])
    tri = jnp.tile(jnp.tril(jnp.ones((chunk, chunk), F32)), (1, 3)).astype(BF16)
    seq = lambda bi, j: (bi, j, 0)
    per_b = lambda bi, j: (bi, 0, 0)
    const = lambda bi, j: (0, 0)
    stream = pl.BlockSpec((1, tb, B_WIDTH), seq)
    state = pl.BlockSpec((1, B_WIDTH, B_WIDTH), per_b)
    return pl.pallas_call(
        functools.partial(_rec_kernel, chunk=chunk, n_chunks=n_chunks),
        grid=(b, t // tb),
        in_specs=[stream] * 7 + [state, state,
                                 pl.BlockSpec((1, B_WIDTH), const), pl.BlockSpec((1, B_WIDTH), const),
                                 pl.BlockSpec((chunk, 3 * chunk), const),
                                 pl.BlockSpec((REC_HEADS * chunk, chunk), const),
                                 pl.BlockSpec((chunk, B_WIDTH), const), pl.BlockSpec((chunk, B_WIDTH), const),
                                 pl.BlockSpec((1, B_WIDTH), const),
                                 pl.BlockSpec((B_WIDTH, B_WIDTH), const)],
        out_specs=[stream, stream, state, state],
        out_shape=[jax.ShapeDtypeStruct((b, t, B_WIDTH), F32), jax.ShapeDtypeStruct((b, t, C_WIDTH), F32),
                   jax.ShapeDtypeStruct((b, B_WIDTH, B_WIDTH), F32),
                   jax.ShapeDtypeStruct((b, C_WIDTH, C_WIDTH), F32)],
        scratch_shapes=[pltpu.VMEM((B_WIDTH, B_WIDTH), F32), pltpu.VMEM((C_WIDTH, C_WIDTH), F32)],
        compiler_params=pltpu.CompilerParams(dimension_semantics=("parallel", "arbitrary"),
                                             vmem_limit_bytes=VMEM_LIMIT),
        name=name,
    )(bq, bk, bv, blf, cq, ck, cv, s0b, s0c,
      jnp.tile(hgain, REC_HEADS).reshape(1, B_WIDTH), jnp.tile(rgain, REC_HEADS).reshape(1, C_WIDTH),
      tri, dmat, inner, kscale, rdec, _group_mean_matrix(B_WIDTH))


def _out_kernel(x_ref, oa_ref, yb_ref, yc_ref, gate_ref, w_ref, y_ref):
    gate = gate_ref[...].astype(F32)
    ya = (oa_ref[...] * gate[:, :A_WIDTH]).astype(BF16)
    yb = (yb_ref[...] * gate[:, A_WIDTH:A_WIDTH + B_WIDTH]).astype(BF16)
    yc = (yc_ref[...] * gate[:, A_WIDTH + B_WIDTH:]).astype(BF16)
    acc = jnp.dot(ya, w_ref[:A_WIDTH, :], preferred_element_type=F32)
    acc = acc + jnp.dot(yb, w_ref[A_WIDTH:A_WIDTH + B_WIDTH, :], preferred_element_type=F32)
    acc = acc + jnp.dot(yc, w_ref[A_WIDTH + B_WIDTH:, :], preferred_element_type=F32)
    y_ref[...] = x_ref[...] + acc


def _out_call(x2d, oa, yb, yc, gate, w_out16, tm, name):
    n = x2d.shape[0]
    tm = min(tm, n)
    row = lambda r: (r, 0)
    return pl.pallas_call(
        _out_kernel,
        grid=(n // tm,),
        in_specs=[pl.BlockSpec((tm, D_MODEL), row), pl.BlockSpec((tm, A_WIDTH), row),
                  pl.BlockSpec((tm, B_WIDTH), row), pl.BlockSpec((tm, C_WIDTH), row),
                  pl.BlockSpec((tm, D_MODEL), row), pl.BlockSpec((D_MODEL, D_MODEL), lambda r: (0, 0))],
        out_specs=pl.BlockSpec((tm, D_MODEL), row),
        out_shape=jax.ShapeDtypeStruct((n, D_MODEL), F32),
        compiler_params=pltpu.CompilerParams(dimension_semantics=("parallel",),
                                             vmem_limit_bytes=VMEM_LIMIT),
        name=name,
    )(x2d, oa, yb, yc, gate, w_out16)


def _dup_kv(x):
    b, s = x.shape[:2]
    return jnp.repeat(x.reshape(b, s, A_KV_HEADS, 1, HEAD_DIM), 2, axis=3).reshape(b, s, 2 * KV_WIDTH).astype(BF16)


def _mixer_layer(x, pos0, past, s_hgrn, s_ret, layer, w_p, w_out16, norm_g, q_gain, k_gain,
                 lb_logits, hgain, rgain, tag):
    b, t, _ = x.shape
    n = b * t
    x2d = x.reshape(n, D_MODEL)
    pos = pos0 + jnp.arange(t)
    (q, k, v, kd, vd, iq, sg, ik, ikd, gate, bq, bk, bv, blf, cq, ck, cv) = _proj_call(
        x2d, t, pos, norm_g, w_p, q_gain, k_gain, lb_logits, layer, 256)

    three = lambda a: a.reshape(b, t, a.shape[-1])
    kd3, vd3, ikd3 = three(kd), three(vd), three(ikd)
    if past is not None:
        pk, pv, pik = past
        kd3 = jnp.concatenate([_dup_kv(pk), kd3], axis=1)
        vd3 = jnp.concatenate([_dup_kv(pv), vd3], axis=1)
        ikd3 = jnp.concatenate([jnp.concatenate([pik, pik], axis=-1).astype(BF16), ikd3], axis=1)
    n_keys = kd3.shape[1]
    s_pad = -(-n_keys // LANES) * LANES
    if s_pad != n_keys:
        padw = ((0, 0), (0, s_pad - n_keys), (0, 0))
        kd3, vd3, ikd3 = jnp.pad(kd3, padw), jnp.pad(vd3, padw), jnp.pad(ikd3, padw)
    topk = min(TOPK_MAX, n_keys // 4)
    oa = _attention(three(q), three(iq), three(sg), kd3, vd3, ikd3, topk=topk, qpos0=pos0,
                    n_keys=n_keys, tag=tag)

    yb, yc, sb, sc = _rec_call(three(bq), three(bk), three(bv), three(blf), three(cq), three(ck), three(cv),
                               _state_to_blockdiag(s_hgrn), _state_to_blockdiag(s_ret), hgain, rgain,
                               name=f"rec_{tag}")
    out = _out_call(x2d, oa.reshape(n, A_WIDTH), yb.reshape(n, B_WIDTH), yc.reshape(n, C_WIDTH), gate,
                    w_out16, 512, name=f"out_{tag}")
    return (out.reshape(b, t, D_MODEL),
            (k.reshape(b, t, A_KV_HEADS, HEAD_DIM), v.reshape(b, t, A_KV_HEADS, HEAD_DIM), ik,
             _state_from_blockdiag(sb), _state_from_blockdiag(sc)))


def kernel(x_prompt, x_sample, cache_k, cache_v, cache_idx_k, state_hgrn, state_ret, norm_g, w_in,
           q_norm_g, k_norm_g, hgrn_lb_logits, hgrn_norm_g, ret_norm_g, w_out):
    depth = w_in.shape[0]
    bp, tp = x_prompt.shape[:2]
    bs, ts = x_sample.shape[:2]
    past = cache_k.shape[2]
    zero_state = jnp.zeros((bp, REC_HEADS, HEAD_DIM, HEAD_DIM), F32)

    yp, ys = x_prompt, x_sample
    outs_p, outs_s = [], []
    for l in range(depth):
        w_p = _prep_w_in(w_in[l])
        w_o = w_out[l].astype(BF16)
        args = (l, w_p, w_o, norm_g[l], q_norm_g[l], k_norm_g[l], hgrn_lb_logits, hgrn_norm_g[l], ret_norm_g[l])
        yp, st = _mixer_layer(yp, 0, None, zero_state, zero_state, *args, tag=f"p{l}")
        outs_p.append(st)
        ys, st = _mixer_layer(ys, past, (cache_k[l], cache_v[l], cache_idx_k[l]),
                              state_hgrn[l], state_ret[l], *args, tag=f"s{l}")
        outs_s.append(st)

    def stack(outs, i, shape):
        return jnp.stack([o[i] for o in outs]).reshape(shape)

    return (yp, ys,
            stack(outs_p, 0, (depth, bp, tp, A_KV_HEADS, HEAD_DIM)),
            stack(outs_p, 1, (depth, bp, tp, A_KV_HEADS, HEAD_DIM)),
            stack(outs_p, 2, (depth, bp, tp, HEAD_DIM)),
            stack(outs_p, 3, (depth, bp, REC_HEADS, HEAD_DIM, HEAD_DIM)),
            stack(outs_p, 4, (depth, bp, REC_HEADS, HEAD_DIM, HEAD_DIM)),
            stack(outs_s, 0, (depth, bs, ts, A_KV_HEADS, HEAD_DIM)),
            stack(outs_s, 1, (depth, bs, ts, A_KV_HEADS, HEAD_DIM)),
            stack(outs_s, 2, (depth, bs, ts, HEAD_DIM)),
            stack(outs_s, 3, (depth, bs, REC_HEADS, HEAD_DIM, HEAD_DIM)),
            stack(outs_s, 4, (depth, bs, REC_HEADS, HEAD_DIM, HEAD_DIM)))
```

```python
import functools
import math

import numpy as np
import jax
import jax.numpy as jnp
from jax import lax
from jax.experimental import pallas as pl
from jax.experimental.pallas import tpu as pltpu

F32 = jnp.float32
BF16 = jnp.bfloat16
I32 = jnp.int32

D_MODEL = 1024
HEAD_DIM = 64
CHUNK = 64
A_WIDTH = 512
A_HEADS = 8
A_KV_HEADS = 2
KV_WIDTH = A_KV_HEADS * HEAD_DIM
IDX_HEADS = 8
IDX_W_SCALE = (IDX_HEADS * HEAD_DIM) ** -0.5
TOPK_MAX = 256
B_WIDTH = 256
C_WIDTH = 256
REC_HEADS = 4
ROPE_THETA = 10000.0
EPS = 1e-6
LANES = 128
VMEM_LIMIT = 48 * 1024 * 1024

_SIZES = (A_WIDTH, KV_WIDTH, KV_WIDTH, A_WIDTH, IDX_HEADS * HEAD_DIM, HEAD_DIM, IDX_HEADS,
          B_WIDTH, B_WIDTH, B_WIDTH, B_WIDTH, C_WIDTH, C_WIDTH, C_WIDTH, C_WIDTH)
_SPLIT_IDX = tuple(int(s) for s in np.cumsum(_SIZES)[:-1])

_AQ, _AK, _AV, _AG, _IQ, _IKW = 0, 512, 640, 768, 1280, 1792
_BQ, _BF, _BI, _BG, _CQ, _CK, _CV, _CG = 1920, 2176, 2432, 2688, 2944, 3200, 3456, 3712
_N_PROJ = 3968

INT_MIN = -(2 ** 31)
NEG_BIG = -1e30
POS_BIG = 1 << 20
LOG2_E = math.log2(math.e)


def _prep_w_in(w):
    (aq, ak, av, ag, iq, ik, iw, bq, bf, bi, bg, cq, ck, cv, cg) = jnp.split(w, _SPLIT_IDX, axis=1)
    pad = jnp.zeros((w.shape[0], LANES - HEAD_DIM - IDX_HEADS), w.dtype)
    return jnp.concatenate([aq, ak, av, ag, iq, ik, iw, pad, bq, bf, bi, bg, cq, ck, cv, cg],
                           axis=1).astype(BF16)


def _rope_tables(pos):
    half = HEAD_DIM // 2
    freqs = ROPE_THETA ** (-jnp.arange(half, dtype=F32) / half)
    ang = pos.astype(F32)[:, None] * freqs[None, :]
    cos, sin = jnp.cos(ang), jnp.sin(ang)
    cos64 = jnp.concatenate([cos, cos], axis=1)
    sin64 = jnp.concatenate([-sin, sin], axis=1)
    return jnp.concatenate([cos64, cos64], axis=1), jnp.concatenate([sin64, sin64], axis=1)


def _group_mean_matrix(width):
    idx = np.arange(width) // HEAD_DIM
    return jnp.asarray((idx[:, None] == idx[None, :]).astype(np.float32) / HEAD_DIM, BF16)


def _split2(x):
    hi = x.astype(BF16)
    lo = (x - hi.astype(F32)).astype(BF16)
    return hi, lo


def _group_mean(x2, gmat):
    hi, lo = _split2(x2)
    return (jnp.dot(hi, gmat, preferred_element_type=F32)
            + jnp.dot(lo, gmat, preferred_element_type=F32))


def _nt_dot(a, b):
    return lax.dot_general(a, b, (((1,), (1,)), ((), ())), preferred_element_type=F32)


def _proj_kernel(x_ref, g_ref, w_ref, cos_ref, sin_ref, qg_ref, kg_ref, lb_ref, gm_ref, ex_ref,
                 q_out, k_out, v_out, kd_out, vd_out, iq_out, sg_out, ik_out, ikd_out, gate_out,
                 bq_out, bk_out, bv_out, blf_out, cq_out, ck_out, cv_out, *, layer):
    x = x_ref[...]
    tm = x.shape[0]
    ms = jnp.mean(x * x, axis=-1, keepdims=True)
    h = (x * lax.rsqrt(ms + EPS) * g_ref[...]).astype(BF16)
    cos = cos_ref[...]
    sin = sin_ref[...]
    lane = lax.broadcasted_iota(I32, (tm, LANES), 1)
    first_half = (lane % HEAD_DIM) < (HEAD_DIM // 2)
    low_head = lane < HEAD_DIM
    gmat = gm_ref[...]

    def proj(c0, width):
        return jnp.dot(h, w_ref[:, c0:c0 + width], preferred_element_type=F32)

    def rope(xb):
        swapped = jnp.where(first_half, pltpu.roll(xb, LANES - HEAD_DIM // 2, 1),
                            pltpu.roll(xb, HEAD_DIM // 2, 1))
        return xb * cos + swapped * sin

    def head_norm(xb, gain):
        return xb * lax.rsqrt(_group_mean(xb * xb, gmat) + EPS) * gain

    def dup_heads(xb):
        other = pltpu.roll(xb, HEAD_DIM, 1)
        return jnp.where(low_head, xb, other), jnp.where(low_head, other, xb)

    aq = proj(_AQ, A_WIDTH)
    for c in range(A_WIDTH // LANES):
        blk = rope(head_norm(aq[:, c * LANES:(c + 1) * LANES], qg_ref[...]))
        q_out[:, c * LANES:(c + 1) * LANES] = (blk * (HEAD_DIM ** -0.5 * LOG2_E)).astype(BF16)

    akv = proj(_AK, 2 * KV_WIDTH)
    k = rope(head_norm(akv[:, :KV_WIDTH], kg_ref[...]))
    v = akv[:, KV_WIDTH:]
    k_out[...] = k
    v_out[...] = v
    k0, k1 = dup_heads(k)
    kd_out[:, :LANES] = k0.astype(BF16)
    kd_out[:, LANES:] = k1.astype(BF16)
    v0, v1 = dup_heads(v)
    vd_out[:, :LANES] = v0.astype(BF16)
    vd_out[:, LANES:] = v1.astype(BF16)

    ikw = proj(_IKW, LANES)
    ikr = rope(ikw)
    ik_out[...] = ikr[:, :HEAD_DIM]
    ikd_out[...] = dup_heads(ikr)[0].astype(BF16)
    sg_out[...] = jnp.where(ikw >= 0, 1.0, -1.0).astype(F32)
    whi, wlo = _split2(jnp.abs(ikw) * IDX_W_SCALE)
    wexp = (jnp.dot(whi, ex_ref[...], preferred_element_type=F32)
            + jnp.dot(wlo, ex_ref[...], preferred_element_type=F32))
    iq = proj(_IQ, IDX_HEADS * HEAD_DIM)
    for c in range(IDX_HEADS * HEAD_DIM // LANES):
        sl = slice(c * LANES, (c + 1) * LANES)
        iq_out[:, sl] = (rope(iq[:, sl]) * wexp[:, sl]).astype(BF16)

    def silu(z):
        return z / (1.0 + jnp.exp(-z))

    gate_out[:, :A_WIDTH] = silu(proj(_AG, A_WIDTH)).astype(BF16)
    gate_out[:, A_WIDTH:A_WIDTH + B_WIDTH] = silu(proj(_BG, B_WIDTH)).astype(BF16)
    gate_out[:, A_WIDTH + B_WIDTH:] = silu(proj(_CG, C_WIDTH)).astype(BF16)

    logits = lb_ref[...]
    e = jnp.exp(logits - jnp.max(logits, axis=0, keepdims=True))
    sm = e / jnp.sum(e, axis=0, keepdims=True)
    lb = jnp.zeros((1, B_WIDTH), F32)
    for j in range(1, layer + 1):
        lb = lb + sm[j:j + 1, :]
    bf = proj(_BF, B_WIDTH)
    f = lb + (1.0 - lb) / (1.0 + jnp.exp(-bf))
    bq_out[...] = proj(_BQ, B_WIDTH)
    bk_out[...] = (1.0 - lb) / (1.0 + jnp.exp(bf))
    bv_out[...] = proj(_BI, B_WIDTH)
    blf_out[...] = jnp.log(f)

    cq = proj(_CQ, C_WIDTH)
    ck = proj(_CK, C_WIDTH)
    for c in range(C_WIDTH // LANES):
        sl = slice(c * LANES, (c + 1) * LANES)
        cq_out[:, sl] = rope(cq[:, sl])
        ck_out[:, sl] = rope(ck[:, sl]) * (HEAD_DIM ** -0.5)
    cv_out[...] = proj(_CV, C_WIDTH)


def _proj_call(x2d, seq, pos, norm_g, w_p, q_gain, k_gain, lb_logits, layer, tm):
    n = x2d.shape[0]
    tm = min(tm, n)
    cos, sin = _rope_tables(pos)
    if seq % tm == 0:
        per_seq = seq // tm
        tab_map = lambda r: (r % per_seq, 0)
    else:
        cos = jnp.tile(cos, (n // seq, 1))
        sin = jnp.tile(sin, (n // seq, 1))
        tab_map = lambda r: (r, 0)
    expand = np.zeros((LANES, IDX_HEADS * HEAD_DIM), np.float32)
    for hd in range(IDX_HEADS):
        expand[HEAD_DIM + hd, hd * HEAD_DIM:(hd + 1) * HEAD_DIM] = 1.0
    const = lambda r: (0, 0)
    row = lambda r: (r, 0)
    widths = [(A_WIDTH, BF16), (KV_WIDTH, F32), (KV_WIDTH, F32), (2 * KV_WIDTH, BF16), (2 * KV_WIDTH, BF16),
              (IDX_HEADS * HEAD_DIM, BF16), (LANES, F32), (HEAD_DIM, F32), (LANES, BF16), (D_MODEL, BF16),
              (B_WIDTH, F32), (B_WIDTH, F32), (B_WIDTH, F32), (B_WIDTH, F32),
              (C_WIDTH, F32), (C_WIDTH, F32), (C_WIDTH, F32)]
    return pl.pallas_call(
        functools.partial(_proj_kernel, layer=layer),
        grid=(n // tm,),
        in_specs=[pl.BlockSpec((tm, D_MODEL), row),
                  pl.BlockSpec((1, D_MODEL), const),
                  pl.BlockSpec((D_MODEL, _N_PROJ), const),
                  pl.BlockSpec((tm, LANES), tab_map),
                  pl.BlockSpec((tm, LANES), tab_map),
                  pl.BlockSpec((1, LANES), const),
                  pl.BlockSpec((1, LANES), const),
                  pl.BlockSpec(lb_logits.shape, const),
                  pl.BlockSpec((LANES, LANES), const),
                  pl.BlockSpec((LANES, IDX_HEADS * HEAD_DIM), const)],
        out_specs=[pl.BlockSpec((tm, w), row) for w, _ in widths],
        out_shape=[jax.ShapeDtypeStruct((n, w), dt) for w, dt in widths],
        compiler_params=pltpu.CompilerParams(dimension_semantics=("parallel",),
                                             vmem_limit_bytes=VMEM_LIMIT),
        name=f"proj_l{layer}",
    )(x2d, norm_g.reshape(1, D_MODEL), w_p, cos, sin,
      jnp.tile(q_gain, 2).reshape(1, LANES), jnp.tile(k_gain, 2).reshape(1, LANES),
      lb_logits, _group_mean_matrix(LANES), jnp.asarray(expand, BF16))


SEARCH_INTERP_STEPS = 8
SEARCH_WALK_STEPS = 4
SEARCH_RANK_BIAS = 1.5
SEARCH_EDGE = 0.05


def _normal_upper_quantile(r):
    rr = jnp.minimum(r, 1.0 - r)
    t = jnp.sqrt(-2.0 * jnp.log(rr))
    z = t - ((0.010328 * t + 0.802853) * t + 2.515517) / (((0.001308 * t + 0.189269) * t + 1.432788) * t + 1.0)
    return jnp.where(r <= 0.5, z, -z)


def _select_topk(xm_scr, sum1, sum2, lim, topk, key_scr, thr_scr, lim_scr, bias_scr):
    tq, s = xm_scr.shape
    kf = float(topk)
    nf = lim.astype(F32)
    trivial = lim <= topk

    def count_gt(p):
        return jnp.sum(jnp.where(xm_scr[...] > p, 1.0, 0.0), axis=1, keepdims=True)

    mu = sum1 / nf
    var = sum2 / nf - mu * mu
    sd = jnp.sqrt(jnp.maximum(var, 1e-30))
    target = kf + SEARCH_RANK_BIAS
    lo, hi = mu - 8.0 * sd, mu + 8.0 * sd
    clo, chi = nf, jnp.zeros_like(nf)
    lo_counted = jnp.zeros((tq, 1), jnp.bool_)
    p = mu + _normal_upper_quantile(jnp.clip(target / nf, 1e-6, 1.0 - 1e-6)) * sd
    found = trivial
    thr = jnp.full((tq, 1), 0.1 * NEG_BIG, F32)
    for it in range(SEARCH_INTERP_STEPS):
        c = count_gt(p)
        hit = (c == kf) & jnp.logical_not(found)
        thr = jnp.where(hit, p, thr)
        found = found | hit
        above = c > kf
        lo = jnp.where(above, p, lo)
        clo = jnp.where(above, c, clo)
        lo_counted = lo_counted | above
        hi = jnp.where(above, hi, p)
        chi = jnp.where(above, chi, c)
        frac = jnp.clip((clo - target) / (clo - chi), SEARCH_EDGE, 1.0 - SEARCH_EDGE)
        p = lo + (hi - lo) * frac

    cur = lo
    for it in range(SEARCH_WALK_STEPS):
        xv = xm_scr[...]
        nxt = jnp.min(jnp.where(xv > cur, xv, -NEG_BIG), axis=1, keepdims=True)
        c = count_gt(nxt)
        done = (c <= kf) & lo_counted & jnp.logical_not(found)
        thr = jnp.where(done, nxt, thr)
        found = found | done
        cur = nxt
    thr_scr[...] = jnp.broadcast_to(thr, (tq, LANES))

    @pl.when(jnp.min(jnp.where(found, 1.0, 0.0)) < 0.5)
    def _():
        bits = lax.bitcast_convert_type(xm_scr[...], I32)
        key_scr[...] = bits ^ ((bits >> 31) & 0x7FFFFFFF)

        def bit_step(j, tu):
            cand = tu | lax.shift_left(jnp.int32(1), 31 - j)
            cnt = jnp.sum(jnp.where(key_scr[...] >= (cand ^ INT_MIN), 1.0, 0.0), axis=1, keepdims=True)
            return jnp.where(cnt >= kf, cand, tu)

        tkey = lax.fori_loop(0, 32, bit_step, jnp.zeros((tq, 1), I32)) ^ INT_MIN
        tval = lax.bitcast_convert_type(tkey ^ ((tkey >> 31) & 0x7FFFFFFF), F32)
        thr_scr[...] = jnp.broadcast_to(jnp.where(trivial, 0.1 * NEG_BIG, tval), (tq, LANES))

    thr = thr_scr[:, :1]
    xv = xm_scr[...]
    gt = xv > thr
    eq = xv == thr
    cnt_gt = jnp.sum(jnp.where(gt, 1.0, 0.0), axis=1, keepdims=True)
    cnt_eq = jnp.sum(jnp.where(eq, 1.0, 0.0), axis=1, keepdims=True)
    want_eq = kf - cnt_gt

    lim_scr[...] = jnp.broadcast_to(jnp.where(want_eq >= cnt_eq, POS_BIG, 0), (tq, LANES))

    kpos = lax.broadcasted_iota(I32, (tq, s), 1)

    @pl.when(jnp.max(jnp.where((want_eq > 0.0) & (want_eq < cnt_eq), 1.0, 0.0)) > 0.5)
    def _():
        eqpos = jnp.where(eq, kpos, POS_BIG)
        nbits = max(1, int(s).bit_length())

        def pos_step(j, xlim):
            cand = xlim | lax.shift_left(jnp.int32(1), nbits - 1 - j)
            cnt = jnp.sum(jnp.where(eqpos < cand, 1.0, 0.0), axis=1, keepdims=True)
            return jnp.where(cnt <= want_eq, cand, xlim)

        xlim = lax.fori_loop(0, nbits, pos_step, jnp.zeros((tq, 1), I32))
        lim_scr[...] = jnp.broadcast_to(xlim, (tq, LANES))

    sel = gt | (eq & (kpos < lim_scr[:, :1]))
    bias_scr[...] = jnp.where(sel, 0.0, NEG_BIG)


def _attn_kernel(q_ref, iq_ref, sg_ref, kd_ref, vd_ref, ikd_ref, prev_ref, o_ref,
                 xm_scr, key_scr, bias_scr, thr_scr, lim_scr, sum1_scr, sum2_scr, *, topk, tpos0, n_keys):
    del prev_ref
    tq = q_ref.shape[1]
    s = kd_ref.shape[1]
    lane = lax.broadcasted_iota(I32, (tq, LANES), 1)
    halves = (lane < HEAD_DIM, lane >= HEAD_DIM)

    tpos = tpos0 + lax.broadcasted_iota(I32, (tq, 1), 0)
    lim = jnp.minimum((tpos // CHUNK + 1) * CHUNK, n_keys)
    search = min(s, n_keys) > topk

    rb = min(tq, LANES)
    key_tile = 2 * LANES
    lane_rb = lax.broadcasted_iota(I32, (rb, LANES), 1)
    halves_rb = (lane_rb < HEAD_DIM, lane_rb >= HEAD_DIM)
    for r in range(tq // rb):
        rs = slice(r * rb, (r + 1) * rb)
        sg = sg_ref[0, rs, :]
        qms, sgs = [], []
        for hd in range(IDX_HEADS):
            qb = iq_ref[0, rs, (hd // 2) * LANES:(hd // 2 + 1) * LANES]
            qms.append(jnp.where(halves_rb[hd % 2], qb, jnp.zeros_like(qb)))
            sgs.append(sg[:, HEAD_DIM + hd:HEAD_DIM + hd + 1])
        acc1 = jnp.zeros((rb, LANES), F32)
        acc2 = jnp.zeros((rb, LANES), F32)
        for c0 in range(0, s, key_tile):
            w = min(key_tile, s - c0)
            ikd_t = ikd_ref[0, c0:c0 + w, :]
            acc = jnp.zeros((rb, w), F32)
            for hd in range(IDX_HEADS):
                acc = acc + sgs[hd] * jnp.maximum(_nt_dot(qms[hd], ikd_t), 0.0)
            adm_t = (c0 + lax.broadcasted_iota(I32, (rb, w), 1)) < lim[rs]
            xm_scr[rs, c0:c0 + w] = jnp.where(adm_t, acc, NEG_BIG)
            if search:
                sc0 = jnp.where(adm_t, acc, 0.0)
                for l0 in range(0, w, LANES):
                    part = sc0[:, l0:l0 + LANES]
                    acc1 = acc1 + part
                    acc2 = acc2 + part * part
        if search:
            sum1_scr[rs, :] = acc1
            sum2_scr[rs, :] = acc2

    logits = []
    for c in range(A_WIDTH // LANES):
        g = (2 * c) // (A_HEADS // A_KV_HEADS)
        qb = q_ref[0, :, c * LANES:(c + 1) * LANES]
        for e in range(2):
            qm = jnp.where(halves[e], qb, jnp.zeros_like(qb))
            logits.append(_nt_dot(qm, kd_ref[0, :, g * LANES:(g + 1) * LANES]))

    if search:
        sum1 = jnp.sum(sum1_scr[...], axis=1, keepdims=True)
        sum2 = jnp.sum(sum2_scr[...], axis=1, keepdims=True)
        _select_topk(xm_scr, sum1, sum2, lim, topk, key_scr, thr_scr, lim_scr, bias_scr)
    else:
        bias_scr[...] = jnp.where(lax.broadcasted_iota(I32, (tq, s), 1) < lim, 0.0, NEG_BIG)

    ones = jnp.ones((s, LANES), BF16)
    for g in range(A_KV_HEADS):
        vd1 = jnp.concatenate([vd_ref[0, :, g * LANES:(g + 1) * LANES], ones], axis=1)
        for c in range(g * 2, g * 2 + 2):
            outs = []
            for e in range(2):
                lg = logits[2 * c + e] + bias_scr[...]
                p = jnp.exp2(lg - jnp.max(lg, axis=1, keepdims=True))
                ol = jnp.dot(p.astype(BF16), vd1, preferred_element_type=F32)
                outs.append(ol[:, :LANES] / ol[:, LANES:])
            o_ref[0, :, c * LANES:(c + 1) * LANES] = jnp.where(halves[0], outs[0], outs[1])


def _attn_call(q, iq, sg, kd, vd, ikd, prev, *, topk, qpos0, n_keys, tq, qblock, s_blk, name):
    b, t, _ = q.shape
    qmap = lambda bi: (bi, qblock, 0)
    kmap = lambda bi: (bi, 0, 0)
    return pl.pallas_call(
        functools.partial(_attn_kernel, topk=topk, tpos0=qpos0 + qblock * tq, n_keys=n_keys),
        grid=(b,),
        in_specs=[pl.BlockSpec((1, tq, A_WIDTH), qmap),
                  pl.BlockSpec((1, tq, IDX_HEADS * HEAD_DIM), qmap),
                  pl.BlockSpec((1, tq, LANES), qmap),
                  pl.BlockSpec((1, s_blk, 2 * KV_WIDTH), kmap),
                  pl.BlockSpec((1, s_blk, 2 * KV_WIDTH), kmap),
                  pl.BlockSpec((1, s_blk, LANES), kmap),
                  pl.BlockSpec(memory_space=pl.ANY)],
        out_specs=pl.BlockSpec((1, tq, A_WIDTH), qmap),
        out_shape=jax.ShapeDtypeStruct((b, t, A_WIDTH), F32),
        input_output_aliases={6: 0},
        scratch_shapes=[pltpu.VMEM((tq, s_blk), F32), pltpu.VMEM((tq, s_blk), I32), pltpu.VMEM((tq, s_blk), F32),
                        pltpu.VMEM((tq, LANES), F32), pltpu.VMEM((tq, LANES), I32),
                        pltpu.VMEM((tq, LANES), F32), pltpu.VMEM((tq, LANES), F32)],
        compiler_params=pltpu.CompilerParams(dimension_semantics=("parallel",),
                                             vmem_limit_bytes=VMEM_LIMIT),
        name=name,
    )(q, iq, sg, kd, vd, ikd, prev)


def _attention(q, iq, sg, kd, vd, ikd, *, topk, qpos0, n_keys, tag):
    b, t, _ = q.shape
    tq = min(t, 256)
    out = jnp.zeros((b, t, A_WIDTH), F32)
    for qblock in range(t // tq):
        last_pos = qpos0 + (qblock + 1) * tq - 1
        visible = min((last_pos // CHUNK + 1) * CHUNK, n_keys)
        s_blk = min(-(-visible // LANES) * LANES, kd.shape[1])
        out = _attn_call(q, iq, sg, kd, vd, ikd, out, topk=topk, qpos0=qpos0, n_keys=n_keys, tq=tq,
                         qblock=qblock, s_blk=s_blk, name=f"attn_{tag}_q{qblock}")
    return out


def _retention_log_decay():
    return jnp.log(1.0 - jnp.exp(jnp.linspace(math.log(1.0 / 32), math.log(1.0 / 512), REC_HEADS, dtype=F32)))


def _rec_kernel(bq_ref, bk_ref, bv_ref, blf_ref, cq_ref, ck_ref, cv_ref, s0b_ref, s0c_ref,
                hg_ref, rg_ref, tri_ref, dmat_ref, inner_ref, kscale_ref, rdec_ref, gm_ref,
                yb_ref, yc_ref, sb_out, sc_out, sb_scr, sc_scr, *, chunk, n_chunks):
    j = pl.program_id(1)
    width = B_WIDTH

    @pl.when(j == 0)
    def _():
        sb_scr[...] = s0b_ref[0]
        sc_scr[...] = s0c_ref[0]

    rowh = lax.broadcasted_iota(I32, (width, width), 0) // HEAD_DIM
    colh = lax.broadcasted_iota(I32, (width, width), 1) // HEAD_DIM
    block_diag = rowh == colh
    lane_h = lax.broadcasted_iota(I32, (chunk, width), 1) // HEAD_DIM
    causal = (lax.broadcasted_iota(I32, (chunk, chunk), 0)
              >= lax.broadcasted_iota(I32, (chunk, chunk), 1))
    gmat = gm_ref[...]
    tri = tri_ref[...]
    tn = (((0,), (0,)), ((), ()))

    def masked_heads(x):
        return [jnp.where(lane_h == hd, x, 0.0).astype(BF16) for hd in range(REC_HEADS)]

    heads = range(REC_HEADS)
    chunks = range(n_chunks)
    rows = [pl.ds(c * chunk, chunk) for c in chunks]

    bcs = []
    for c in chunks:
        lf = blf_ref[0, rows[c], :]
        a1 = lf.astype(BF16)
        r1 = lf - a1.astype(F32)
        a2 = r1.astype(BF16)
        a3 = (r1 - a2.astype(F32)).astype(BF16)
        bcs.append(jnp.dot(tri, jnp.concatenate([a1, a2, a3], axis=0), preferred_element_type=F32))

    att_b, att_c, upd_b, upd_c, qs_b, qs_c, dec_b, vb_b, vb_c = ([] for _ in range(9))
    for c in chunks:
        bc = bcs[c]
        mid = bc[chunk // 2 - 1:chunk // 2, :]
        last = bc[chunk - 1:chunk, :]
        q = bq_ref[0, rows[c], :]
        k = bk_ref[0, rows[c], :]
        vb = bv_ref[0, rows[c], :].astype(BF16)
        kt = (k * jnp.exp(mid - bc)).astype(BF16)
        att_b.append([_nt_dot(qh, kt) for qh in masked_heads(q * jnp.exp(bc - mid))])
        upd_b.append(lax.dot_general(vb, (k * jnp.exp(last - bc)).astype(BF16), tn, preferred_element_type=F32))
        qs_b.append((q * jnp.exp(bc)).astype(BF16))
        dec_b.append(jnp.exp(last))
        vb_b.append(vb)
        q = cq_ref[0, rows[c], :]
        k = ck_ref[0, rows[c], :]
        vb = cv_ref[0, rows[c], :].astype(BF16)
        kb = k.astype(BF16)
        att_c.append([_nt_dot(qh, kb) for qh in masked_heads(q)])
        upd_c.append(lax.dot_general(vb, (k * kscale_ref[...]).astype(BF16), tn, preferred_element_type=F32))
        qs_c.append((q * inner_ref[...]).astype(BF16))
        vb_c.append(vb)

    st = sb_scr[...]
    rt = sc_scr[...]
    o_b, o_c = [], []
    for c in chunks:
        o_b.append(_nt_dot(qs_b[c], st.astype(BF16)))
        st = st * dec_b[c] + jnp.where(block_diag, upd_b[c], 0.0)
        o_c.append(_nt_dot(qs_c[c], rt.astype(BF16)))
        rt = rt * rdec_ref[...] + jnp.where(block_diag, upd_c[c], 0.0)
    sb_scr[...] = st
    sc_scr[...] = rt

    prod_b = [[jnp.dot(jnp.where(causal, att_b[c][hd], 0.0).astype(BF16), vb_b[c], preferred_element_type=F32)
               for hd in heads] for c in chunks]
    prod_c = [[jnp.dot((att_c[c][hd] * dmat_ref[hd * chunk:(hd + 1) * chunk, :]).astype(BF16), vb_c[c],
                       preferred_element_type=F32) for hd in heads] for c in chunks]
    outs = []
    for c in chunks:
        for o, prod in ((o_b[c], prod_b[c]), (o_c[c], prod_c[c])):
            for hd in heads:
                o = o + jnp.where(lane_h == hd, prod[hd], 0.0)
            outs.append(o)

    means = []
    for o in outs:
        hi, lo = _split2(o * o)
        means.append(jnp.dot(jnp.concatenate([hi, lo], axis=1), gmat, preferred_element_type=F32))
    for c in chunks:
        yb_ref[0, rows[c], :] = outs[2 * c] * lax.rsqrt(means[2 * c] + EPS) * hg_ref[...]
        yc_ref[0, rows[c], :] = outs[2 * c + 1] * lax.rsqrt(means[2 * c + 1] + EPS) * rg_ref[...]

    @pl.when(j == pl.num_programs(1) - 1)
    def _():
        sb_out[0] = sb_scr[...]
        sc_out[0] = sc_scr[...]


def _state_to_blockdiag(s):
    b = s.shape[0]
    eye = jnp.eye(REC_HEADS, dtype=s.dtype)
    return jnp.einsum('bhkv,hg->bhvgk', s, eye).reshape(b, B_WIDTH, B_WIDTH)


def _state_from_blockdiag(sbd):
    b = sbd.shape[0]
    s5 = sbd.reshape(b, REC_HEADS, HEAD_DIM, REC_HEADS, HEAD_DIM)
    return jnp.einsum('bhvhk->bhkv', s5)


def _rec_call(bq, bk, bv, blf, cq, ck, cv, s0b, s0c, hgain, rgain, name):
    b, t, _ = bq.shape
    chunk = min(CHUNK, t)
    tb = min(t, 8 * chunk)
    n_chunks = tb // chunk
    lg = _retention_log_decay()
    n = jnp.arange(chunk, dtype=F32)
    diff = n[:, None] - n[None, :]
    dmat = jnp.where(diff >= 0, jnp.exp(jnp.where(diff >= 0, diff, 0.0)[None] * lg[:, None, None]), 0.0)
    dmat = dmat.reshape(REC_HEADS * chunk, chunk)
    per_lane = lambda a: jnp.repeat(a, HEAD_DIM, axis=-1)
    inner = per_lane(jnp.exp((n[:, None] + 1.0) * lg[None, :]))
    kscale = per_lane(jnp.exp((chunk - 1.0 - n)[:, None] * lg[None, :]))
    rdec = per_lane(jnp.exp(chunk * lg)[None, :])
    tri = jnp.tile(jnp.tril(jnp.ones((chunk, chunk), F32)), (1, 3)).astype(BF16)
    seq = lambda bi, j: (bi, j, 0)
    per_b = lambda bi, j: (bi, 0, 0)
    const = lambda bi, j: (0, 0)
    stream = pl.BlockSpec((1, tb, B_WIDTH), seq)
    state = pl.BlockSpec((1, B_WIDTH, B_WIDTH), per_b)
    return pl.pallas_call(
        functools.partial(_rec_kernel, chunk=chunk, n_chunks=n_chunks),
        grid=(b, t // tb),
        in_specs=[stream] * 7 + [state, state,
                                 pl.BlockSpec((1, B_WIDTH), const), pl.BlockSpec((1, B_WIDTH), const),
                                 pl.BlockSpec((chunk, 3 * chunk), const),
                                 pl.BlockSpec((REC_HEADS * chunk, chunk), const),
                                 pl.BlockSpec((chunk, B_WIDTH), const), pl.BlockSpec((chunk, B_WIDTH), const),
                                 pl.BlockSpec((1, B_WIDTH), const),
                                 pl.BlockSpec((2 * B_WIDTH, B_WIDTH), const)],
        out_specs=[stream, stream, state, state],
        out_shape=[jax.ShapeDtypeStruct((b, t, B_WIDTH), F32), jax.ShapeDtypeStruct((b, t, C_WIDTH), F32),
                   jax.ShapeDtypeStruct((b, B_WIDTH, B_WIDTH), F32),
                   jax.ShapeDtypeStruct((b, C_WIDTH, C_WIDTH), F32)],
        scratch_shapes=[pltpu.VMEM((B_WIDTH, B_WIDTH), F32), pltpu.VMEM((C_WIDTH, C_WIDTH), F32)],
        compiler_params=pltpu.CompilerParams(dimension_semantics=("parallel", "arbitrary"),
                                             vmem_limit_bytes=VMEM_LIMIT),
        name=name,
    )(bq, bk, bv, blf, cq, ck, cv, s0b, s0c,
      jnp.tile(hgain, REC_HEADS).reshape(1, B_WIDTH), jnp.tile(rgain, REC_HEADS).reshape(1, C_WIDTH),
      tri, dmat, inner, kscale, rdec, jnp.tile(_group_mean_matrix(B_WIDTH), (2, 1)))


def _out_kernel(x_ref, oa_ref, yb_ref, yc_ref, gate_ref, w_ref, y_ref):
    gate = gate_ref[...].astype(F32)
    ya = (oa_ref[...] * gate[:, :A_WIDTH]).astype(BF16)
    yb = (yb_ref[...] * gate[:, A_WIDTH:A_WIDTH + B_WIDTH]).astype(BF16)
    yc = (yc_ref[...] * gate[:, A_WIDTH + B_WIDTH:]).astype(BF16)
    acc = jnp.dot(ya, w_ref[:A_WIDTH, :], preferred_element_type=F32)
    acc = acc + jnp.dot(yb, w_ref[A_WIDTH:A_WIDTH + B_WIDTH, :], preferred_element_type=F32)
    acc = acc + jnp.dot(yc, w_ref[A_WIDTH + B_WIDTH:, :], preferred_element_type=F32)
    y_ref[...] = x_ref[...] + acc


def _out_call(x2d, oa, yb, yc, gate, w_out16, tm, name):
    n = x2d.shape[0]
    tm = min(tm, n)
    row = lambda r: (r, 0)
    return pl.pallas_call(
        _out_kernel,
        grid=(n // tm,),
        in_specs=[pl.BlockSpec((tm, D_MODEL), row), pl.BlockSpec((tm, A_WIDTH), row),
                  pl.BlockSpec((tm, B_WIDTH), row), pl.BlockSpec((tm, C_WIDTH), row),
                  pl.BlockSpec((tm, D_MODEL), row), pl.BlockSpec((D_MODEL, D_MODEL), lambda r: (0, 0))],
        out_specs=pl.BlockSpec((tm, D_MODEL), row),
        out_shape=jax.ShapeDtypeStruct((n, D_MODEL), F32),
        compiler_params=pltpu.CompilerParams(dimension_semantics=("parallel",),
                                             vmem_limit_bytes=VMEM_LIMIT),
        name=name,
    )(x2d, oa, yb, yc, gate, w_out16)


def _dup_kv(x):
    b, s = x.shape[:2]
    return jnp.repeat(x.reshape(b, s, A_KV_HEADS, 1, HEAD_DIM), 2, axis=3).reshape(b, s, 2 * KV_WIDTH).astype(BF16)


def _mixer_layer(x, pos0, past, s_hgrn, s_ret, layer, w_p, w_out16, norm_g, q_gain, k_gain,
                 lb_logits, hgain, rgain, tag):
    b, t, _ = x.shape
    n = b * t
    x2d = x.reshape(n, D_MODEL)
    pos = pos0 + jnp.arange(t)
    (q, k, v, kd, vd, iq, sg, ik, ikd, gate, bq, bk, bv, blf, cq, ck, cv) = _proj_call(
        x2d, t, pos, norm_g, w_p, q_gain, k_gain, lb_logits, layer, 256)

    three = lambda a: a.reshape(b, t, a.shape[-1])
    kd3, vd3, ikd3 = three(kd), three(vd), three(ikd)
    if past is not None:
        pk, pv, pik = past
        kd3 = jnp.concatenate([_dup_kv(pk), kd3], axis=1)
        vd3 = jnp.concatenate([_dup_kv(pv), vd3], axis=1)
        ikd3 = jnp.concatenate([jnp.concatenate([pik, pik], axis=-1).astype(BF16), ikd3], axis=1)
    n_keys = kd3.shape[1]
    s_pad = -(-n_keys // LANES) * LANES
    if s_pad != n_keys:
        padw = ((0, 0), (0, s_pad - n_keys), (0, 0))
        kd3, vd3, ikd3 = jnp.pad(kd3, padw), jnp.pad(vd3, padw), jnp.pad(ikd3, padw)
    topk = min(TOPK_MAX, n_keys // 4)
    oa = _attention(three(q), three(iq), three(sg), kd3, vd3, ikd3, topk=topk, qpos0=pos0,
                    n_keys=n_keys, tag=tag)

    yb, yc, sb, sc = _rec_call(three(bq), three(bk), three(bv), three(blf), three(cq), three(ck), three(cv),
                               _state_to_blockdiag(s_hgrn), _state_to_blockdiag(s_ret), hgain, rgain,
                               name=f"rec_{tag}")
    out = _out_call(x2d, oa.reshape(n, A_WIDTH), yb.reshape(n, B_WIDTH), yc.reshape(n, C_WIDTH), gate,
                    w_out16, 512, name=f"out_{tag}")
    return (out.reshape(b, t, D_MODEL),
            (k.reshape(b, t, A_KV_HEADS, HEAD_DIM), v.reshape(b, t, A_KV_HEADS, HEAD_DIM), ik,
             _state_from_blockdiag(sb), _state_from_blockdiag(sc)))


def kernel(x_prompt, x_sample, cache_k, cache_v, cache_idx_k, state_hgrn, state_ret, norm_g, w_in,
           q_norm_g, k_norm_g, hgrn_lb_logits, hgrn_norm_g, ret_norm_g, w_out):
    depth = w_in.shape[0]
    bp, tp = x_prompt.shape[:2]
    bs, ts = x_sample.shape[:2]
    past = cache_k.shape[2]
    zero_state = jnp.zeros((bp, REC_HEADS, HEAD_DIM, HEAD_DIM), F32)

    yp, ys = x_prompt, x_sample
    outs_p, outs_s = [], []
    for l in range(depth):
        w_p = _prep_w_in(w_in[l])
        w_o = w_out[l].astype(BF16)
        args = (l, w_p, w_o, norm_g[l], q_norm_g[l], k_norm_g[l], hgrn_lb_logits, hgrn_norm_g[l], ret_norm_g[l])
        yp, st = _mixer_layer(yp, 0, None, zero_state, zero_state, *args, tag=f"p{l}")
        outs_p.append(st)
        ys, st = _mixer_layer(ys, past, (cache_k[l], cache_v[l], cache_idx_k[l]),
                              state_hgrn[l], state_ret[l], *args, tag=f"s{l}")
        outs_s.append(st)

    def stack(outs, i, shape):
        return jnp.stack([o[i] for o in outs]).reshape(shape)

    return (yp, ys,
            stack(outs_p, 0, (depth, bp, tp, A_KV_HEADS, HEAD_DIM)),
            stack(outs_p, 1, (depth, bp, tp, A_KV_HEADS, HEAD_DIM)),
            stack(outs_p, 2, (depth, bp, tp, HEAD_DIM)),
            stack(outs_p, 3, (depth, bp, REC_HEADS, HEAD_DIM, HEAD_DIM)),
            stack(outs_p, 4, (depth, bp, REC_HEADS, HEAD_DIM, HEAD_DIM)),
            stack(outs_s, 0, (depth, bs, ts, A_KV_HEADS, HEAD_DIM)),
            stack(outs_s, 1, (depth, bs, ts, A_KV_HEADS, HEAD_DIM)),
            stack(outs_s, 2, (depth, bs, ts, HEAD_DIM)),
            stack(outs_s, 3, (depth, bs, REC_HEADS, HEAD_DIM, HEAD_DIM)),
            stack(outs_s, 4, (depth, bs, REC_HEADS, HEAD_DIM, HEAD_DIM)))
```

```python
import functools
import math

import numpy as np
import jax
import jax.numpy as jnp
from jax import lax
from jax.experimental import pallas as pl
from jax.experimental.pallas import tpu as pltpu

F32 = jnp.float32
BF16 = jnp.bfloat16
I32 = jnp.int32

D_MODEL = 1024
HEAD_DIM = 64
CHUNK = 64
A_WIDTH = 512
A_HEADS = 8
A_KV_HEADS = 2
KV_WIDTH = A_KV_HEADS * HEAD_DIM
IDX_HEADS = 8
IDX_W_SCALE = (IDX_HEADS * HEAD_DIM) ** -0.5
TOPK_MAX = 256
B_WIDTH = 256
C_WIDTH = 256
REC_HEADS = 4
ROPE_THETA = 10000.0
EPS = 1e-6
LANES = 128
VMEM_LIMIT = 48 * 1024 * 1024

_SIZES = (A_WIDTH, KV_WIDTH, KV_WIDTH, A_WIDTH, IDX_HEADS * HEAD_DIM, HEAD_DIM, IDX_HEADS,
          B_WIDTH, B_WIDTH, B_WIDTH, B_WIDTH, C_WIDTH, C_WIDTH, C_WIDTH, C_WIDTH)
_SPLIT_IDX = tuple(int(s) for s in np.cumsum(_SIZES)[:-1])

_AQ, _AK, _AV, _AG, _IQ, _IKW = 0, 512, 640, 768, 1280, 1792
_BQ, _BF, _BI, _BG, _CQ, _CK, _CV, _CG = 1920, 2176, 2432, 2688, 2944, 3200, 3456, 3712
_N_PROJ = 3968

INT_MIN = -(2 ** 31)
NEG_BIG = -1e30
POS_BIG = 1 << 20
LOG2_E = math.log2(math.e)


def _prep_w_in(w):
    (aq, ak, av, ag, iq, ik, iw, bq, bf, bi, bg, cq, ck, cv, cg) = jnp.split(w, _SPLIT_IDX, axis=1)
    pad = jnp.zeros((w.shape[0], LANES - HEAD_DIM - IDX_HEADS), w.dtype)
    return jnp.concatenate([aq, ak, av, ag, iq, ik, iw, pad, bq, bf, bi, bg, cq, ck, cv, cg],
                           axis=1).astype(BF16)


def _rope_tables(pos):
    half = HEAD_DIM // 2
    freqs = ROPE_THETA ** (-jnp.arange(half, dtype=F32) / half)
    ang = pos.astype(F32)[:, None] * freqs[None, :]
    cos, sin = jnp.cos(ang), jnp.sin(ang)
    cos64 = jnp.concatenate([cos, cos], axis=1)
    sin64 = jnp.concatenate([-sin, sin], axis=1)
    return jnp.concatenate([cos64, cos64], axis=1), jnp.concatenate([sin64, sin64], axis=1)


def _group_mean_matrix(width):
    idx = np.arange(width) // HEAD_DIM
    return jnp.asarray((idx[:, None] == idx[None, :]).astype(np.float32) / HEAD_DIM, BF16)


def _split2(x):
    hi = x.astype(BF16)
    lo = (x - hi.astype(F32)).astype(BF16)
    return hi, lo


def _group_mean(x2, gmat):
    hi, lo = _split2(x2)
    return (jnp.dot(hi, gmat, preferred_element_type=F32)
            + jnp.dot(lo, gmat, preferred_element_type=F32))


def _nt_dot(a, b):
    return lax.dot_general(a, b, (((1,), (1,)), ((), ())), preferred_element_type=F32)


def _proj_kernel(x_ref, g_ref, w_ref, cos_ref, sin_ref, qg_ref, kg_ref, lb_ref, gm_ref, ex_ref,
                 q_out, k_out, v_out, kd_out, vd_out, iq_out, sg_out, ik_out, ikd_out, gate_out,
                 bq_out, bk_out, bv_out, blf_out, cq_out, ck_out, cv_out, *, layer):
    x = x_ref[...]
    tm = x.shape[0]
    ms = jnp.mean(x * x, axis=-1, keepdims=True)
    h = (x * lax.rsqrt(ms + EPS) * g_ref[...]).astype(BF16)
    cos = cos_ref[...]
    sin = sin_ref[...]
    lane = lax.broadcasted_iota(I32, (tm, LANES), 1)
    first_half = (lane % HEAD_DIM) < (HEAD_DIM // 2)
    low_head = lane < HEAD_DIM
    gmat = gm_ref[...]

    def proj(c0, width):
        return jnp.dot(h, w_ref[:, c0:c0 + width], preferred_element_type=F32)

    def rope(xb):
        swapped = jnp.where(first_half, pltpu.roll(xb, LANES - HEAD_DIM // 2, 1),
                            pltpu.roll(xb, HEAD_DIM // 2, 1))
        return xb * cos + swapped * sin

    def head_norm(xb, gain):
        return xb * lax.rsqrt(_group_mean(xb * xb, gmat) + EPS) * gain

    def dup_heads(xb):
        other = pltpu.roll(xb, HEAD_DIM, 1)
        return jnp.where(low_head, xb, other), jnp.where(low_head, other, xb)

    ag = proj(_AG, A_WIDTH)
    bg = proj(_BG, B_WIDTH)
    cg = proj(_CG, C_WIDTH)
    bq = proj(_BQ, B_WIDTH)
    bf = proj(_BF, B_WIDTH)
    bi = proj(_BI, B_WIDTH)
    cq = proj(_CQ, C_WIDTH)
    ck = proj(_CK, C_WIDTH)
    cv = proj(_CV, C_WIDTH)
    akv = proj(_AK, 2 * KV_WIDTH)
    ikw = proj(_IKW, LANES)
    aq = proj(_AQ, A_WIDTH)
    iq = proj(_IQ, IDX_HEADS * HEAD_DIM)

    def silu(z):
        return z / (1.0 + jnp.exp(-z))

    gate_out[:, :A_WIDTH] = silu(ag).astype(BF16)
    gate_out[:, A_WIDTH:A_WIDTH + B_WIDTH] = silu(bg).astype(BF16)
    gate_out[:, A_WIDTH + B_WIDTH:] = silu(cg).astype(BF16)

    logits = lb_ref[...]
    e = jnp.exp(logits - jnp.max(logits, axis=0, keepdims=True))
    sm = e / jnp.sum(e, axis=0, keepdims=True)
    lb = jnp.zeros((1, B_WIDTH), F32)
    for j in range(1, layer + 1):
        lb = lb + sm[j:j + 1, :]
    f = lb + (1.0 - lb) / (1.0 + jnp.exp(-bf))
    bq_out[...] = bq
    bk_out[...] = (1.0 - lb) / (1.0 + jnp.exp(bf))
    bv_out[...] = bi
    blf_out[...] = jnp.log(f)

    for c in range(C_WIDTH // LANES):
        sl = slice(c * LANES, (c + 1) * LANES)
        cq_out[:, sl] = rope(cq[:, sl])
        ck_out[:, sl] = rope(ck[:, sl]) * (HEAD_DIM ** -0.5)
    cv_out[...] = cv

    v = akv[:, KV_WIDTH:]
    v_out[...] = v
    v0, v1 = dup_heads(v)
    vd_out[:, :LANES] = v0.astype(BF16)
    vd_out[:, LANES:] = v1.astype(BF16)
    ikr = rope(ikw)
    ik_out[...] = ikr[:, :HEAD_DIM]
    ikd_out[...] = dup_heads(ikr)[0].astype(BF16)
    sg_out[...] = jnp.where(ikw >= 0, 1.0, -1.0).astype(F32)

    for c in range(A_WIDTH // LANES):
        blk = rope(head_norm(aq[:, c * LANES:(c + 1) * LANES], qg_ref[...]))
        q_out[:, c * LANES:(c + 1) * LANES] = (blk * (HEAD_DIM ** -0.5 * LOG2_E)).astype(BF16)
    k = rope(head_norm(akv[:, :KV_WIDTH], kg_ref[...]))
    k_out[...] = k
    k0, k1 = dup_heads(k)
    kd_out[:, :LANES] = k0.astype(BF16)
    kd_out[:, LANES:] = k1.astype(BF16)

    whi, wlo = _split2(jnp.abs(ikw) * IDX_W_SCALE)
    wexp = (jnp.dot(whi, ex_ref[...], preferred_element_type=F32)
            + jnp.dot(wlo, ex_ref[...], preferred_element_type=F32))
    for c in range(IDX_HEADS * HEAD_DIM // LANES):
        sl = slice(c * LANES, (c + 1) * LANES)
        iq_out[:, sl] = (rope(iq[:, sl]) * wexp[:, sl]).astype(BF16)


def _proj_call(x2d, seq, pos, norm_g, w_p, q_gain, k_gain, lb_logits, layer, tm):
    n = x2d.shape[0]
    tm = min(tm, n)
    cos, sin = _rope_tables(pos)
    if seq % tm == 0:
        per_seq = seq // tm
        tab_map = lambda r: (r % per_seq, 0)
    else:
        cos = jnp.tile(cos, (n // seq, 1))
        sin = jnp.tile(sin, (n // seq, 1))
        tab_map = lambda r: (r, 0)
    expand = np.zeros((LANES, IDX_HEADS * HEAD_DIM), np.float32)
    for hd in range(IDX_HEADS):
        expand[HEAD_DIM + hd, hd * HEAD_DIM:(hd + 1) * HEAD_DIM] = 1.0
    const = lambda r: (0, 0)
    row = lambda r: (r, 0)
    widths = [(A_WIDTH, BF16), (KV_WIDTH, F32), (KV_WIDTH, F32), (2 * KV_WIDTH, BF16), (2 * KV_WIDTH, BF16),
              (IDX_HEADS * HEAD_DIM, BF16), (LANES, F32), (HEAD_DIM, F32), (LANES, BF16), (D_MODEL, BF16),
              (B_WIDTH, F32), (B_WIDTH, F32), (B_WIDTH, F32), (B_WIDTH, F32),
              (C_WIDTH, F32), (C_WIDTH, F32), (C_WIDTH, F32)]
    return pl.pallas_call(
        functools.partial(_proj_kernel, layer=layer),
        grid=(n // tm,),
        in_specs=[pl.BlockSpec((tm, D_MODEL), row),
                  pl.BlockSpec((1, D_MODEL), const),
                  pl.BlockSpec((D_MODEL, _N_PROJ), const),
                  pl.BlockSpec((tm, LANES), tab_map),
                  pl.BlockSpec((tm, LANES), tab_map),
                  pl.BlockSpec((1, LANES), const),
                  pl.BlockSpec((1, LANES), const),
                  pl.BlockSpec(lb_logits.shape, const),
                  pl.BlockSpec((LANES, LANES), const),
                  pl.BlockSpec((LANES, IDX_HEADS * HEAD_DIM), const)],
        out_specs=[pl.BlockSpec((tm, w), row) for w, _ in widths],
        out_shape=[jax.ShapeDtypeStruct((n, w), dt) for w, dt in widths],
        compiler_params=pltpu.CompilerParams(dimension_semantics=("parallel",),
                                             vmem_limit_bytes=VMEM_LIMIT),
        name=f"proj_l{layer}",
    )(x2d, norm_g.reshape(1, D_MODEL), w_p, cos, sin,
      jnp.tile(q_gain, 2).reshape(1, LANES), jnp.tile(k_gain, 2).reshape(1, LANES),
      lb_logits, _group_mean_matrix(LANES), jnp.asarray(expand, BF16))


SEARCH_INTERP_STEPS = 8
SEARCH_WALK_STEPS = 4
SEARCH_RANK_BIAS = 1.5
SEARCH_EDGE = 0.05


def _normal_upper_quantile(r):
    rr = jnp.minimum(r, 1.0 - r)
    t = jnp.sqrt(-2.0 * jnp.log(rr))
    z = t - ((0.010328 * t + 0.802853) * t + 2.515517) / (((0.001308 * t + 0.189269) * t + 1.432788) * t + 1.0)
    return jnp.where(r <= 0.5, z, -z)


def _select_topk(xm_scr, sum1_scr, sum2_scr, lim, topk, key_scr, thr_scr, lim_scr, bias_scr):
    tq, s = xm_scr.shape
    kf = float(topk)
    nf = lim.astype(F32)
    trivial = lim <= topk

    def count_gt(p):
        return jnp.sum(jnp.where(xm_scr[...] > p, 1.0, 0.0), axis=1, keepdims=True)

    def tally(t):
        xv = xm_scr[...]
        return (jnp.sum(jnp.where(xv > t, 1.0, 0.0), axis=1, keepdims=True),
                jnp.sum(jnp.where(xv == t, 1.0, 0.0), axis=1, keepdims=True))

    mu = jnp.sum(sum1_scr[...], axis=1, keepdims=True) / nf
    var = jnp.sum(sum2_scr[...], axis=1, keepdims=True) / nf - mu * mu
    sd = jnp.sqrt(jnp.maximum(var, 1e-30))
    target = kf + SEARCH_RANK_BIAS
    lo, hi = mu - 8.0 * sd, mu + 8.0 * sd
    clo, chi = nf, jnp.zeros_like(nf)
    p = mu + _normal_upper_quantile(jnp.clip(target / nf, 1e-6, 1.0 - 1e-6)) * sd
    for it in range(SEARCH_INTERP_STEPS):
        c = count_gt(p)
        above = c >= kf
        lo = jnp.where(above, p, lo)
        clo = jnp.where(above, c, clo)
        hi = jnp.where(above, hi, p)
        chi = jnp.where(above, chi, c)
        frac = jnp.clip((clo - target) / (clo - chi), SEARCH_EDGE, 1.0 - SEARCH_EDGE)
        p = lo + (hi - lo) * frac

    need = clo - kf
    cur = lo
    thr = lo
    for it in range(SEARCH_WALK_STEPS):
        xv = xm_scr[...]
        cur = jnp.min(jnp.where(xv > cur, xv, -NEG_BIG), axis=1, keepdims=True)
        thr = jnp.where(need == float(it + 1), cur, thr)
    thr = jnp.where(trivial, 0.1 * NEG_BIG, thr)
    thr_scr[...] = jnp.broadcast_to(thr, (tq, LANES))

    cnt_gt, cnt_eq = tally(thr)
    sum1_scr[...] = jnp.broadcast_to(cnt_gt, (tq, LANES))
    sum2_scr[...] = jnp.broadcast_to(cnt_eq, (tq, LANES))
    proven = trivial | ((cnt_gt <= kf) & (cnt_gt + cnt_eq >= kf))

    @pl.when(jnp.min(jnp.where(proven, 1.0, 0.0)) < 0.5)
    def _():
        bits = lax.bitcast_convert_type(xm_scr[...], I32)
        key_scr[...] = bits ^ ((bits >> 31) & 0x7FFFFFFF)

        def bit_step(j, tu):
            cand = tu | lax.shift_left(jnp.int32(1), 31 - j)
            cnt = jnp.sum(jnp.where(key_scr[...] >= (cand ^ INT_MIN), 1.0, 0.0), axis=1, keepdims=True)
            return jnp.where(cnt >= kf, cand, tu)

        tkey = lax.fori_loop(0, 32, bit_step, jnp.zeros((tq, 1), I32)) ^ INT_MIN
        tval = lax.bitcast_convert_type(tkey ^ ((tkey >> 31) & 0x7FFFFFFF), F32)
        texact = jnp.where(trivial, 0.1 * NEG_BIG, tval)
        thr_scr[...] = jnp.broadcast_to(texact, (tq, LANES))
        cg, ce = tally(texact)
        sum1_scr[...] = jnp.broadcast_to(cg, (tq, LANES))
        sum2_scr[...] = jnp.broadcast_to(ce, (tq, LANES))

    thr = thr_scr[:, :1]
    cnt_gt = sum1_scr[:, :1]
    cnt_eq = sum2_scr[:, :1]
    xv = xm_scr[...]
    gt = xv > thr
    eq = xv == thr
    want_eq = kf - cnt_gt

    lim_scr[...] = jnp.broadcast_to(jnp.where(want_eq >= cnt_eq, POS_BIG, 0), (tq, LANES))

    kpos = lax.broadcasted_iota(I32, (tq, s), 1)

    @pl.when(jnp.max(jnp.where((want_eq > 0.0) & (want_eq < cnt_eq), 1.0, 0.0)) > 0.5)
    def _():
        eqpos = jnp.where(eq, kpos, POS_BIG)
        nbits = max(1, int(s).bit_length())

        def pos_step(j, xlim):
            cand = xlim | lax.shift_left(jnp.int32(1), nbits - 1 - j)
            cnt = jnp.sum(jnp.where(eqpos < cand, 1.0, 0.0), axis=1, keepdims=True)
            return jnp.where(cnt <= want_eq, cand, xlim)

        xlim = lax.fori_loop(0, nbits, pos_step, jnp.zeros((tq, 1), I32))
        lim_scr[...] = jnp.broadcast_to(xlim, (tq, LANES))

    sel = gt | (eq & (kpos < lim_scr[:, :1]))
    bias_scr[...] = jnp.where(sel, 0.0, NEG_BIG)


def _attn_kernel(*refs, topk, tpos0, n_keys):
    q_ref, iq_ref, sg_ref, kd_ref, vd_ref, ikd_ref = refs[:6]
    o_ref, xm_scr, key_scr, bias_scr, thr_scr, lim_scr, sum1_scr, sum2_scr = refs[-8:]
    tq = q_ref.shape[1]
    s = kd_ref.shape[1]
    lane = lax.broadcasted_iota(I32, (tq, LANES), 1)
    halves = (lane < HEAD_DIM, lane >= HEAD_DIM)

    tpos = tpos0 + lax.broadcasted_iota(I32, (tq, 1), 0)
    lim = jnp.minimum((tpos // CHUNK + 1) * CHUNK, n_keys)
    search = min(s, n_keys) > topk

    rb = min(tq, LANES)
    key_tile = 2 * LANES
    lane_rb = lax.broadcasted_iota(I32, (rb, LANES), 1)
    halves_rb = (lane_rb < HEAD_DIM, lane_rb >= HEAD_DIM)
    for r in range(tq // rb):
        rs = slice(r * rb, (r + 1) * rb)
        sg = sg_ref[0, rs, :]
        qms, sgs = [], []
        for hd in range(IDX_HEADS):
            qb = iq_ref[0, rs, (hd // 2) * LANES:(hd // 2 + 1) * LANES]
            qms.append(jnp.where(halves_rb[hd % 2], qb, jnp.zeros_like(qb)))
            sgs.append(sg[:, HEAD_DIM + hd:HEAD_DIM + hd + 1])
        acc1 = jnp.zeros((rb, LANES), F32)
        acc2 = jnp.zeros((rb, LANES), F32)
        for c0 in range(0, s, key_tile):
            w = min(key_tile, s - c0)
            ikd_t = ikd_ref[0, c0:c0 + w, :]
            acc = jnp.zeros((rb, w), F32)
            for hd in range(IDX_HEADS):
                acc = acc + sgs[hd] * jnp.maximum(_nt_dot(qms[hd], ikd_t), 0.0)
            adm_t = (c0 + lax.broadcasted_iota(I32, (rb, w), 1)) < lim[rs]
            xm_scr[rs, c0:c0 + w] = jnp.where(adm_t, acc, NEG_BIG)
            if search:
                sc0 = jnp.where(adm_t, acc, 0.0)
                for l0 in range(0, w, LANES):
                    part = sc0[:, l0:l0 + LANES]
                    acc1 = acc1 + part
                    acc2 = acc2 + part * part
        if search:
            sum1_scr[rs, :] = acc1
            sum2_scr[rs, :] = acc2

    logits = []
    for c in range(A_WIDTH // LANES):
        g = (2 * c) // (A_HEADS // A_KV_HEADS)
        qb = q_ref[0, :, c * LANES:(c + 1) * LANES]
        for e in range(2):
            qm = jnp.where(halves[e], qb, jnp.zeros_like(qb))
            logits.append(_nt_dot(qm, kd_ref[0, :, g * LANES:(g + 1) * LANES]))

    if search:
        _select_topk(xm_scr, sum1_scr, sum2_scr, lim, topk, key_scr, thr_scr, lim_scr, bias_scr)
    else:
        bias_scr[...] = jnp.where(lax.broadcasted_iota(I32, (tq, s), 1) < lim, 0.0, NEG_BIG)

    ones = jnp.ones((s, LANES), BF16)
    for g in range(A_KV_HEADS):
        vd1 = jnp.concatenate([vd_ref[0, :, g * LANES:(g + 1) * LANES], ones], axis=1)
        for c in range(g * 2, g * 2 + 2):
            outs = []
            for e in range(2):
                lg = logits[2 * c + e] + bias_scr[...]
                p = jnp.exp2(lg - jnp.max(lg, axis=1, keepdims=True))
                ol = jnp.dot(p.astype(BF16), vd1, preferred_element_type=F32)
                outs.append(ol[:, :LANES] / ol[:, LANES:])
            o_ref[0, :, c * LANES:(c + 1) * LANES] = jnp.where(halves[0], outs[0], outs[1])


def _attn_call(q, iq, sg, kd, vd, ikd, prev, *, topk, qpos0, n_keys, tq, qblock, s_blk, name):
    b, t, _ = q.shape
    qmap = lambda bi: (bi, qblock, 0)
    kmap = lambda bi: (bi, 0, 0)
    chained = prev is not None
    return pl.pallas_call(
        functools.partial(_attn_kernel, topk=topk, tpos0=qpos0 + qblock * tq, n_keys=n_keys),
        grid=(b,),
        in_specs=[pl.BlockSpec((1, tq, A_WIDTH), qmap),
                  pl.BlockSpec((1, tq, IDX_HEADS * HEAD_DIM), qmap),
                  pl.BlockSpec((1, tq, LANES), qmap),
                  pl.BlockSpec((1, s_blk, 2 * KV_WIDTH), kmap),
                  pl.BlockSpec((1, s_blk, 2 * KV_WIDTH), kmap),
                  pl.BlockSpec((1, s_blk, LANES), kmap)] + ([pl.BlockSpec(memory_space=pl.ANY)] if chained else []),
        out_specs=pl.BlockSpec((1, tq, A_WIDTH), qmap),
        out_shape=jax.ShapeDtypeStruct((b, t, A_WIDTH), F32),
        input_output_aliases={6: 0} if chained else {},
        scratch_shapes=[pltpu.VMEM((tq, s_blk), F32), pltpu.VMEM((tq, s_blk), I32), pltpu.VMEM((tq, s_blk), F32),
                        pltpu.VMEM((tq, LANES), F32), pltpu.VMEM((tq, LANES), I32),
                        pltpu.VMEM((tq, LANES), F32), pltpu.VMEM((tq, LANES), F32)],
        compiler_params=pltpu.CompilerParams(dimension_semantics=("parallel",),
                                             vmem_limit_bytes=VMEM_LIMIT),
        name=name,
    )(*((q, iq, sg, kd, vd, ikd) + ((prev,) if chained else ())))


def _attention(q, iq, sg, kd, vd, ikd, *, topk, qpos0, n_keys, tag):
    b, t, _ = q.shape
    tq = min(t, 256)
    out = None
    for qblock in range(t // tq):
        last_pos = qpos0 + (qblock + 1) * tq - 1
        visible = min((last_pos // CHUNK + 1) * CHUNK, n_keys)
        s_blk = min(-(-visible // LANES) * LANES, kd.shape[1])
        out = _attn_call(q, iq, sg, kd, vd, ikd, out, topk=topk, qpos0=qpos0, n_keys=n_keys, tq=tq,
                         qblock=qblock, s_blk=s_blk, name=f"attn_{tag}_q{qblock}")
    return out


def _retention_log_decay():
    return jnp.log(1.0 - jnp.exp(jnp.linspace(math.log(1.0 / 32), math.log(1.0 / 512), REC_HEADS, dtype=F32)))


def _rec_kernel(bq_ref, bk_ref, bv_ref, blf_ref, cq_ref, ck_ref, cv_ref, s0b_ref, s0c_ref,
                hg_ref, rg_ref, tri_ref, dmat_ref, inner_ref, kscale_ref, rdec_ref, gm_ref,
                yb_ref, yc_ref, sb_out, sc_out, sb_scr, sc_scr, *, chunk, n_chunks):
    j = pl.program_id(1)
    width = B_WIDTH

    @pl.when(j == 0)
    def _():
        sb_scr[...] = s0b_ref[0]
        sc_scr[...] = s0c_ref[0]

    rowh = lax.broadcasted_iota(I32, (width, width), 0) // HEAD_DIM
    colh = lax.broadcasted_iota(I32, (width, width), 1) // HEAD_DIM
    block_diag = rowh == colh
    lane_h = lax.broadcasted_iota(I32, (chunk, width), 1) // HEAD_DIM
    causal = (lax.broadcasted_iota(I32, (chunk, chunk), 0)
              >= lax.broadcasted_iota(I32, (chunk, chunk), 1))
    gmat = gm_ref[...]
    tri = tri_ref[...]
    tn = (((0,), (0,)), ((), ()))

    def masked_heads(x):
        return [jnp.where(lane_h == hd, x, 0.0).astype(BF16) for hd in range(REC_HEADS)]

    heads = range(REC_HEADS)
    chunks = range(n_chunks)
    rows = [pl.ds(c * chunk, chunk) for c in chunks]

    bcs = []
    for c in chunks:
        lf = blf_ref[0, rows[c], :]
        a1 = lf.astype(BF16)
        r1 = lf - a1.astype(F32)
        a2 = r1.astype(BF16)
        a3 = (r1 - a2.astype(F32)).astype(BF16)
        bcs.append(jnp.dot(tri, jnp.concatenate([a1, a2, a3], axis=0), preferred_element_type=F32))

    att_b, att_c, upd_b, upd_c, qs_b, qs_c, dec_b, vb_b, vb_c = ([] for _ in range(9))
    for c in chunks:
        bc = bcs[c]
        mid = bc[chunk // 2 - 1:chunk // 2, :]
        last = bc[chunk - 1:chunk, :]
        q = bq_ref[0, rows[c], :]
        k = bk_ref[0, rows[c], :]
        vb = bv_ref[0, rows[c], :].astype(BF16)
        kt = (k * jnp.exp(mid - bc)).astype(BF16)
        att_b.append([_nt_dot(qh, kt) for qh in masked_heads(q * jnp.exp(bc - mid))])
        upd_b.append(lax.dot_general(vb, (k * jnp.exp(last - bc)).astype(BF16), tn, preferred_element_type=F32))
        qs_b.append((q * jnp.exp(bc)).astype(BF16))
        dec_b.append(jnp.exp(last))
        vb_b.append(vb)
        q = cq_ref[0, rows[c], :]
        k = ck_ref[0, rows[c], :]
        vb = cv_ref[0, rows[c], :].astype(BF16)
        kb = k.astype(BF16)
        att_c.append([_nt_dot(qh, kb) for qh in masked_heads(q)])
        upd_c.append(lax.dot_general(vb, (k * kscale_ref[...]).astype(BF16), tn, preferred_element_type=F32))
        qs_c.append((q * inner_ref[...]).astype(BF16))
        vb_c.append(vb)

    st = sb_scr[...]
    rt = sc_scr[...]
    o_b, o_c = [], []
    for c in chunks:
        o_b.append(_nt_dot(qs_b[c], st.astype(BF16)))
        st = st * dec_b[c] + jnp.where(block_diag, upd_b[c], 0.0)
        o_c.append(_nt_dot(qs_c[c], rt.astype(BF16)))
        rt = rt * rdec_ref[...] + jnp.where(block_diag, upd_c[c], 0.0)
    sb_scr[...] = st
    sc_scr[...] = rt

    prod_b = [[jnp.dot(jnp.where(causal, att_b[c][hd], 0.0).astype(BF16), vb_b[c], preferred_element_type=F32)
               for hd in heads] for c in chunks]
    prod_c = [[jnp.dot((att_c[c][hd] * dmat_ref[hd * chunk:(hd + 1) * chunk, :]).astype(BF16), vb_c[c],
                       preferred_element_type=F32) for hd in heads] for c in chunks]
    outs = []
    for c in chunks:
        for o, prod in ((o_b[c], prod_b[c]), (o_c[c], prod_c[c])):
            for hd in heads:
                o = o + jnp.where(lane_h == hd, prod[hd], 0.0)
            outs.append(o)

    means = []
    for o in outs:
        hi, lo = _split2(o * o)
        means.append(jnp.dot(jnp.concatenate([hi, lo], axis=1), gmat, preferred_element_type=F32))
    for c in chunks:
        yb_ref[0, rows[c], :] = outs[2 * c] * lax.rsqrt(means[2 * c] + EPS) * hg_ref[...]
        yc_ref[0, rows[c], :] = outs[2 * c + 1] * lax.rsqrt(means[2 * c + 1] + EPS) * rg_ref[...]

    @pl.when(j == pl.num_programs(1) - 1)
    def _():
        sb_out[0] = sb_scr[...]
        sc_out[0] = sc_scr[...]


def _state_to_blockdiag(s):
    b = s.shape[0]
    eye = jnp.eye(REC_HEADS, dtype=s.dtype)
    return jnp.einsum('bhkv,hg->bhvgk', s, eye).reshape(b, B_WIDTH, B_WIDTH)


def _state_from_blockdiag(sbd):
    b = sbd.shape[0]
    s5 = sbd.reshape(b, REC_HEADS, HEAD_DIM, REC_HEADS, HEAD_DIM)
    return jnp.einsum('bhvhk->bhkv', s5)


def _rec_call(bq, bk, bv, blf, cq, ck, cv, s0b, s0c, hgain, rgain, name):
    b, t, _ = bq.shape
    chunk = min(CHUNK, t)
    tb = min(t, 8 * chunk)
    n_chunks = tb // chunk
    lg = _retention_log_decay()
    n = jnp.arange(chunk, dtype=F32)
    diff = n[:, None] - n[None, :]
    dmat = jnp.where(diff >= 0, jnp.exp(jnp.where(diff >= 0, diff, 0.0)[None] * lg[:, None, None]), 0.0)
    dmat = dmat.reshape(REC_HEADS * chunk, chunk)
    per_lane = lambda a: jnp.repeat(a, HEAD_DIM, axis=-1)
    inner = per_lane(jnp.exp((n[:, None] + 1.0) * lg[None, :]))
    kscale = per_lane(jnp.exp((chunk - 1.0 - n)[:, None] * lg[None, :]))
    rdec = per_lane(jnp.exp(chunk * lg)[None, :])
    tri = jnp.tile(jnp.tril(jnp.ones((chunk, chunk), F32)), (1, 3)).astype(BF16)
    seq = lambda bi, j: (bi, j, 0)
    per_b = lambda bi, j: (bi, 0, 0)
    const = lambda bi, j: (0, 0)
    stream = pl.BlockSpec((1, tb, B_WIDTH), seq)
    state = pl.BlockSpec((1, B_WIDTH, B_WIDTH), per_b)
    return pl.pallas_call(
        functools.partial(_rec_kernel, chunk=chunk, n_chunks=n_chunks),
        grid=(b, t // tb),
        in_specs=[stream] * 7 + [state, state,
                                 pl.BlockSpec((1, B_WIDTH), const), pl.BlockSpec((1, B_WIDTH), const),
                                 pl.BlockSpec((chunk, 3 * chunk), const),
                                 pl.BlockSpec((REC_HEADS * chunk, chunk), const),
                                 pl.BlockSpec((chunk, B_WIDTH), const), pl.BlockSpec((chunk, B_WIDTH), const),
                                 pl.BlockSpec((1, B_WIDTH), const),
                                 pl.BlockSpec((2 * B_WIDTH, B_WIDTH), const)],
        out_specs=[stream, stream, state, state],
        out_shape=[jax.ShapeDtypeStruct((b, t, B_WIDTH), F32), jax.ShapeDtypeStruct((b, t, C_WIDTH), F32),
                   jax.ShapeDtypeStruct((b, B_WIDTH, B_WIDTH), F32),
                   jax.ShapeDtypeStruct((b, C_WIDTH, C_WIDTH), F32)],
        scratch_shapes=[pltpu.VMEM((B_WIDTH, B_WIDTH), F32), pltpu.VMEM((C_WIDTH, C_WIDTH), F32)],
        compiler_params=pltpu.CompilerParams(dimension_semantics=("parallel", "arbitrary"),
                                             vmem_limit_bytes=VMEM_LIMIT),
        name=name,
    )(bq, bk, bv, blf, cq, ck, cv, s0b, s0c,
      jnp.tile(hgain, REC_HEADS).reshape(1, B_WIDTH), jnp.tile(rgain, REC_HEADS).reshape(1, C_WIDTH),
      tri, dmat, inner, kscale, rdec, jnp.tile(_group_mean_matrix(B_WIDTH), (2, 1)))


def _out_kernel(x_ref, oa_ref, yb_ref, yc_ref, gate_ref, w_ref, y_ref):
    gate = gate_ref[...].astype(F32)
    ya = (oa_ref[...] * gate[:, :A_WIDTH]).astype(BF16)
    yb = (yb_ref[...] * gate[:, A_WIDTH:A_WIDTH + B_WIDTH]).astype(BF16)
    yc = (yc_ref[...] * gate[:, A_WIDTH + B_WIDTH:]).astype(BF16)
    acc = jnp.dot(ya, w_ref[:A_WIDTH, :], preferred_element_type=F32)
    acc = acc + jnp.dot(yb, w_ref[A_WIDTH:A_WIDTH + B_WIDTH, :], preferred_element_type=F32)
    acc = acc + jnp.dot(yc, w_ref[A_WIDTH + B_WIDTH:, :], preferred_element_type=F32)
    y_ref[...] = x_ref[...] + acc


def _out_call(x2d, oa, yb, yc, gate, w_out16, tm, name):
    n = x2d.shape[0]
    tm = min(tm, n)
    row = lambda r: (r, 0)
    return pl.pallas_call(
        _out_kernel,
        grid=(n // tm,),
        in_specs=[pl.BlockSpec((tm, D_MODEL), row), pl.BlockSpec((tm, A_WIDTH), row),
                  pl.BlockSpec((tm, B_WIDTH), row), pl.BlockSpec((tm, C_WIDTH), row),
                  pl.BlockSpec((tm, D_MODEL), row), pl.BlockSpec((D_MODEL, D_MODEL), lambda r: (0, 0))],
        out_specs=pl.BlockSpec((tm, D_MODEL), row),
        out_shape=jax.ShapeDtypeStruct((n, D_MODEL), F32),
        compiler_params=pltpu.CompilerParams(dimension_semantics=("parallel",),
                                             vmem_limit_bytes=VMEM_LIMIT),
        name=name,
    )(x2d, oa, yb, yc, gate, w_out16)


def _dup_kv(x):
    b, s = x.shape[:2]
    return jnp.repeat(x.reshape(b, s, A_KV_HEADS, 1, HEAD_DIM), 2, axis=3).reshape(b, s, 2 * KV_WIDTH).astype(BF16)


def _mixer_layer(x, pos0, past, s_hgrn, s_ret, layer, w_p, w_out16, norm_g, q_gain, k_gain,
                 lb_logits, hgain, rgain, tag):
    b, t, _ = x.shape
    n = b * t
    x2d = x.reshape(n, D_MODEL)
    pos = pos0 + jnp.arange(t)
    (q, k, v, kd, vd, iq, sg, ik, ikd, gate, bq, bk, bv, blf, cq, ck, cv) = _proj_call(
        x2d, t, pos, norm_g, w_p, q_gain, k_gain, lb_logits, layer, 256)

    three = lambda a: a.reshape(b, t, a.shape[-1])
    kd3, vd3, ikd3 = three(kd), three(vd), three(ikd)
    if past is not None:
        pk, pv, pik = past
        kd3 = jnp.concatenate([_dup_kv(pk), kd3], axis=1)
        vd3 = jnp.concatenate([_dup_kv(pv), vd3], axis=1)
        ikd3 = jnp.concatenate([jnp.concatenate([pik, pik], axis=-1).astype(BF16), ikd3], axis=1)
    n_keys = kd3.shape[1]
    s_pad = -(-n_keys // LANES) * LANES
    if s_pad != n_keys:
        padw = ((0, 0), (0, s_pad - n_keys), (0, 0))
        kd3, vd3, ikd3 = jnp.pad(kd3, padw), jnp.pad(vd3, padw), jnp.pad(ikd3, padw)
    topk = min(TOPK_MAX, n_keys // 4)
    oa = _attention(three(q), three(iq), three(sg), kd3, vd3, ikd3, topk=topk, qpos0=pos0,
                    n_keys=n_keys, tag=tag)

    yb, yc, sb, sc = _rec_call(three(bq), three(bk), three(bv), three(blf), three(cq), three(ck), three(cv),
                               _state_to_blockdiag(s_hgrn), _state_to_blockdiag(s_ret), hgain, rgain,
                               name=f"rec_{tag}")
    out = _out_call(x2d, oa.reshape(n, A_WIDTH), yb.reshape(n, B_WIDTH), yc.reshape(n, C_WIDTH), gate,
                    w_out16, 512, name=f"out_{tag}")
    return (out.reshape(b, t, D_MODEL),
            (k.reshape(b, t, A_KV_HEADS, HEAD_DIM), v.reshape(b, t, A_KV_HEADS, HEAD_DIM), ik,
             _state_from_blockdiag(sb), _state_from_blockdiag(sc)))


def kernel(x_prompt, x_sample, cache_k, cache_v, cache_idx_k, state_hgrn, state_ret, norm_g, w_in,
           q_norm_g, k_norm_g, hgrn_lb_logits, hgrn_norm_g, ret_norm_g, w_out):
    depth = w_in.shape[0]
    bp, tp = x_prompt.shape[:2]
    bs, ts = x_sample.shape[:2]
    past = cache_k.shape[2]
    zero_state = jnp.zeros((bp, REC_HEADS, HEAD_DIM, HEAD_DIM), F32)

    yp, ys = x_prompt, x_sample
    outs_p, outs_s = [], []
    for l in range(depth):
        w_p = _prep_w_in(w_in[l])
        w_o = w_out[l].astype(BF16)
        args = (l, w_p, w_o, norm_g[l], q_norm_g[l], k_norm_g[l], hgrn_lb_logits, hgrn_norm_g[l], ret_norm_g[l])
        yp, st = _mixer_layer(yp, 0, None, zero_state, zero_state, *args, tag=f"p{l}")
        outs_p.append(st)
        ys, st = _mixer_layer(ys, past, (cache_k[l], cache_v[l], cache_idx_k[l]),
                              state_hgrn[l], state_ret[l], *args, tag=f"s{l}")
        outs_s.append(st)

    def stack(outs, i, shape):
        return jnp.stack([o[i] for o in outs]).reshape(shape)

    return (yp, ys,
            stack(outs_p, 0, (depth, bp, tp, A_KV_HEADS, HEAD_DIM)),
            stack(outs_p, 1, (depth, bp, tp, A_KV_HEADS, HEAD_DIM)),
            stack(outs_p, 2, (depth, bp, tp, HEAD_DIM)),
            stack(outs_p, 3, (depth, bp, REC_HEADS, HEAD_DIM, HEAD_DIM)),
            stack(outs_p, 4, (depth, bp, REC_HEADS, HEAD_DIM, HEAD_DIM)),
            stack(outs_s, 0, (depth, bs, ts, A_KV_HEADS, HEAD_DIM)),
            stack(outs_s, 1, (depth, bs, ts, A_KV_HEADS, HEAD_DIM)),
            stack(outs_s, 2, (depth, bs, ts, HEAD_DIM)),
            stack(outs_s, 3, (depth, bs, REC_HEADS, HEAD_DIM, HEAD_DIM)),
            stack(outs_s, 4, (depth, bs, REC_HEADS, HEAD_DIM, HEAD_DIM)))
```

```python
import functools
import math

import numpy as np
import jax
import jax.numpy as jnp
from jax import lax
from jax.experimental import pallas as pl
from jax.experimental.pallas import tpu as pltpu

F32 = jnp.float32
BF16 = jnp.bfloat16
I32 = jnp.int32

D_MODEL = 1024
HEAD_DIM = 64
CHUNK = 64
A_WIDTH = 512
A_HEADS = 8
A_KV_HEADS = 2
KV_WIDTH = A_KV_HEADS * HEAD_DIM
IDX_HEADS = 8
IDX_W_SCALE = (IDX_HEADS * HEAD_DIM) ** -0.5
TOPK_MAX = 256
B_WIDTH = 256
C_WIDTH = 256
REC_HEADS = 4
ROPE_THETA = 10000.0
EPS = 1e-6
LANES = 128
VMEM_LIMIT = 48 * 1024 * 1024

_SIZES = (A_WIDTH, KV_WIDTH, KV_WIDTH, A_WIDTH, IDX_HEADS * HEAD_DIM, HEAD_DIM, IDX_HEADS,
          B_WIDTH, B_WIDTH, B_WIDTH, B_WIDTH, C_WIDTH, C_WIDTH, C_WIDTH, C_WIDTH)
_SPLIT_IDX = tuple(int(s) for s in np.cumsum(_SIZES)[:-1])

_AQ, _AK, _AV, _AG, _IQ, _IKW = 0, 512, 640, 768, 1280, 1792
_BQ, _BF, _BI, _BG, _CQ, _CK, _CV, _CG = 1920, 2176, 2432, 2688, 2944, 3200, 3456, 3712
_N_PROJ = 3968

INT_MIN = -(2 ** 31)
NEG_BIG = -1e30
POS_BIG = 1 << 20
LOG2_E = math.log2(math.e)


def _prep_w_in(w):
    (aq, ak, av, ag, iq, ik, iw, bq, bf, bi, bg, cq, ck, cv, cg) = jnp.split(w, _SPLIT_IDX, axis=1)
    pad = jnp.zeros((w.shape[0], LANES - HEAD_DIM - IDX_HEADS), w.dtype)
    return jnp.concatenate([aq, ak, av, ag, iq, ik, iw, pad, bq, bf, bi, bg, cq, ck, cv, cg],
                           axis=1).astype(BF16)


def _rope_tables(pos):
    half = HEAD_DIM // 2
    freqs = ROPE_THETA ** (-jnp.arange(half, dtype=F32) / half)
    ang = pos.astype(F32)[:, None] * freqs[None, :]
    cos, sin = jnp.cos(ang), jnp.sin(ang)
    cos64 = jnp.concatenate([cos, cos], axis=1)
    sin64 = jnp.concatenate([-sin, sin], axis=1)
    return jnp.concatenate([cos64, cos64], axis=1), jnp.concatenate([sin64, sin64], axis=1)


def _group_mean_matrix(width):
    idx = np.arange(width) // HEAD_DIM
    return jnp.asarray((idx[:, None] == idx[None, :]).astype(np.float32) / HEAD_DIM, BF16)


def _split2(x):
    hi = x.astype(BF16)
    lo = (x - hi.astype(F32)).astype(BF16)
    return hi, lo


def _group_mean(x2, gmat):
    hi, lo = _split2(x2)
    return (jnp.dot(hi, gmat, preferred_element_type=F32)
            + jnp.dot(lo, gmat, preferred_element_type=F32))


def _nt_dot(a, b):
    return lax.dot_general(a, b, (((1,), (1,)), ((), ())), preferred_element_type=F32)


def _proj_kernel(x_ref, g_ref, w_ref, cos_ref, sin_ref, qg_ref, kg_ref, lb_ref, gm_ref, ex_ref,
                 q_out, k_out, v_out, kd_out, vd_out, iq_out, sg_out, ik_out, ikd_out, gate_out,
                 bq_out, bk_out, bv_out, blf_out, cq_out, ck_out, cv_out, *, layer):
    x = x_ref[...]
    tm = x.shape[0]
    ms = jnp.mean(x * x, axis=-1, keepdims=True)
    h = (x * lax.rsqrt(ms + EPS) * g_ref[...]).astype(BF16)
    cos = cos_ref[...]
    sin = sin_ref[...]
    lane = lax.broadcasted_iota(I32, (tm, LANES), 1)
    first_half = (lane % HEAD_DIM) < (HEAD_DIM // 2)
    low_head = lane < HEAD_DIM
    gmat = gm_ref[...]

    def proj(c0, width):
        return jnp.dot(h, w_ref[:, c0:c0 + width], preferred_element_type=F32)

    def rope(xb):
        swapped = jnp.where(first_half, pltpu.roll(xb, LANES - HEAD_DIM // 2, 1),
                            pltpu.roll(xb, HEAD_DIM // 2, 1))
        return xb * cos + swapped * sin

    def head_norm(xb, gain):
        return xb * lax.rsqrt(_group_mean(xb * xb, gmat) + EPS) * gain

    def dup_heads(xb):
        other = pltpu.roll(xb, HEAD_DIM, 1)
        return jnp.where(low_head, xb, other), jnp.where(low_head, other, xb)

    ag = proj(_AG, A_WIDTH)
    bg = proj(_BG, B_WIDTH)
    cg = proj(_CG, C_WIDTH)
    bq = proj(_BQ, B_WIDTH)
    bf = proj(_BF, B_WIDTH)
    bi = proj(_BI, B_WIDTH)
    cq = proj(_CQ, C_WIDTH)
    ck = proj(_CK, C_WIDTH)
    cv = proj(_CV, C_WIDTH)
    akv = proj(_AK, 2 * KV_WIDTH)
    ikw = proj(_IKW, LANES)
    aq = proj(_AQ, A_WIDTH)
    iq = proj(_IQ, IDX_HEADS * HEAD_DIM)

    def silu(z):
        return z / (1.0 + jnp.exp(-z))

    gate_out[:, :A_WIDTH] = silu(ag).astype(BF16)
    gate_out[:, A_WIDTH:A_WIDTH + B_WIDTH] = silu(bg).astype(BF16)
    gate_out[:, A_WIDTH + B_WIDTH:] = silu(cg).astype(BF16)

    logits = lb_ref[...]
    e = jnp.exp(logits - jnp.max(logits, axis=0, keepdims=True))
    sm = e / jnp.sum(e, axis=0, keepdims=True)
    lb = jnp.zeros((1, B_WIDTH), F32)
    for j in range(1, layer + 1):
        lb = lb + sm[j:j + 1, :]
    f = lb + (1.0 - lb) / (1.0 + jnp.exp(-bf))
    bq_out[...] = bq
    bk_out[...] = (1.0 - lb) / (1.0 + jnp.exp(bf))
    bv_out[...] = bi
    blf_out[...] = jnp.log(f)

    for c in range(C_WIDTH // LANES):
        sl = slice(c * LANES, (c + 1) * LANES)
        cq_out[:, sl] = rope(cq[:, sl])
        ck_out[:, sl] = rope(ck[:, sl]) * (HEAD_DIM ** -0.5)
    cv_out[...] = cv

    v = akv[:, KV_WIDTH:]
    v_out[...] = v
    v0, v1 = dup_heads(v)
    vd_out[:, :LANES] = v0.astype(BF16)
    vd_out[:, LANES:] = v1.astype(BF16)
    ikr = rope(ikw)
    ik_out[...] = ikr[:, :HEAD_DIM]
    ikd_out[...] = dup_heads(ikr)[0].astype(BF16)
    sg_out[...] = jnp.where(ikw >= 0, 1.0, -1.0).astype(F32)

    for c in range(A_WIDTH // LANES):
        blk = rope(head_norm(aq[:, c * LANES:(c + 1) * LANES], qg_ref[...]))
        q_out[:, c * LANES:(c + 1) * LANES] = (blk * (HEAD_DIM ** -0.5 * LOG2_E)).astype(BF16)
    k = rope(head_norm(akv[:, :KV_WIDTH], kg_ref[...]))
    k_out[...] = k
    k0, k1 = dup_heads(k)
    kd_out[:, :LANES] = k0.astype(BF16)
    kd_out[:, LANES:] = k1.astype(BF16)

    whi, wlo = _split2(jnp.abs(ikw) * IDX_W_SCALE)
    wexp = (jnp.dot(whi, ex_ref[...], preferred_element_type=F32)
            + jnp.dot(wlo, ex_ref[...], preferred_element_type=F32))
    for c in range(IDX_HEADS * HEAD_DIM // LANES):
        sl = slice(c * LANES, (c + 1) * LANES)
        iq_out[:, sl] = (rope(iq[:, sl]) * wexp[:, sl]).astype(BF16)


def _proj_call(x2d, seq, pos, norm_g, w_p, q_gain, k_gain, lb_logits, layer, tm):
    n = x2d.shape[0]
    tm = min(tm, n)
    cos, sin = _rope_tables(pos)
    if seq % tm == 0:
        per_seq = seq // tm
        tab_map = lambda r: (r % per_seq, 0)
    else:
        cos = jnp.tile(cos, (n // seq, 1))
        sin = jnp.tile(sin, (n // seq, 1))
        tab_map = lambda r: (r, 0)
    expand = np.zeros((LANES, IDX_HEADS * HEAD_DIM), np.float32)
    for hd in range(IDX_HEADS):
        expand[HEAD_DIM + hd, hd * HEAD_DIM:(hd + 1) * HEAD_DIM] = 1.0
    const = lambda r: (0, 0)
    row = lambda r: (r, 0)
    widths = [(A_WIDTH, BF16), (KV_WIDTH, F32), (KV_WIDTH, F32), (2 * KV_WIDTH, BF16), (2 * KV_WIDTH, BF16),
              (IDX_HEADS * HEAD_DIM, BF16), (LANES, F32), (HEAD_DIM, F32), (LANES, BF16), (D_MODEL, BF16),
              (B_WIDTH, F32), (B_WIDTH, F32), (B_WIDTH, F32), (B_WIDTH, F32),
              (C_WIDTH, F32), (C_WIDTH, F32), (C_WIDTH, F32)]
    return pl.pallas_call(
        functools.partial(_proj_kernel, layer=layer),
        grid=(n // tm,),
        in_specs=[pl.BlockSpec((tm, D_MODEL), row),
                  pl.BlockSpec((1, D_MODEL), const),
                  pl.BlockSpec((D_MODEL, _N_PROJ), const),
                  pl.BlockSpec((tm, LANES), tab_map),
                  pl.BlockSpec((tm, LANES), tab_map),
                  pl.BlockSpec((1, LANES), const),
                  pl.BlockSpec((1, LANES), const),
                  pl.BlockSpec(lb_logits.shape, const),
                  pl.BlockSpec((LANES, LANES), const),
                  pl.BlockSpec((LANES, IDX_HEADS * HEAD_DIM), const)],
        out_specs=[pl.BlockSpec((tm, w), row) for w, _ in widths],
        out_shape=[jax.ShapeDtypeStruct((n, w), dt) for w, dt in widths],
        compiler_params=pltpu.CompilerParams(dimension_semantics=("parallel",),
                                             vmem_limit_bytes=VMEM_LIMIT),
        name=f"proj_l{layer}",
    )(x2d, norm_g.reshape(1, D_MODEL), w_p, cos, sin,
      jnp.tile(q_gain, 2).reshape(1, LANES), jnp.tile(k_gain, 2).reshape(1, LANES),
      lb_logits, _group_mean_matrix(LANES), jnp.asarray(expand, BF16))


SEARCH_INTERP_STEPS = 8
SEARCH_WALK_STEPS = 4
SEARCH_RANK_BIAS = 1.5
SEARCH_EDGE = 0.05
TIE_WALK_STEPS = 8


def _normal_upper_quantile(r):
    rr = jnp.minimum(r, 1.0 - r)
    t = jnp.sqrt(-2.0 * jnp.log(rr))
    z = t - ((0.010328 * t + 0.802853) * t + 2.515517) / (((0.001308 * t + 0.189269) * t + 1.432788) * t + 1.0)
    return jnp.where(r <= 0.5, z, -z)


def _select_topk(xm_scr, sum1_scr, sum2_scr, lim, topk, key_scr, thr_scr, lim_scr, bias_scr):
    tq, s = xm_scr.shape
    kf = float(topk)
    nf = lim.astype(F32)
    trivial = lim <= topk

    def count_gt(p):
        return jnp.sum(jnp.where(xm_scr[...] > p, 1.0, 0.0), axis=1, keepdims=True)

    def tally(t):
        xv = xm_scr[...]
        return (jnp.sum(jnp.where(xv > t, 1.0, 0.0), axis=1, keepdims=True),
                jnp.sum(jnp.where(xv == t, 1.0, 0.0), axis=1, keepdims=True))

    mu = jnp.sum(sum1_scr[...], axis=1, keepdims=True) / nf
    var = jnp.sum(sum2_scr[...], axis=1, keepdims=True) / nf - mu * mu
    sd = jnp.sqrt(jnp.maximum(var, 1e-30))
    target = kf + SEARCH_RANK_BIAS
    lo, hi = mu - 8.0 * sd, mu + 8.0 * sd
    clo, chi = nf, jnp.zeros_like(nf)
    p = mu + _normal_upper_quantile(jnp.clip(target / nf, 1e-6, 1.0 - 1e-6)) * sd
    for it in range(SEARCH_INTERP_STEPS):
        c = count_gt(p)
        above = c >= kf
        lo = jnp.where(above, p, lo)
        clo = jnp.where(above, c, clo)
        hi = jnp.where(above, hi, p)
        chi = jnp.where(above, chi, c)
        frac = jnp.clip((clo - target) / (clo - chi), SEARCH_EDGE, 1.0 - SEARCH_EDGE)
        p = lo + (hi - lo) * frac

    cur = lo
    thr = lo
    done = clo == kf
    for it in range(SEARCH_WALK_STEPS):
        xv = xm_scr[...]
        cur = jnp.min(jnp.where(xv > cur, xv, -NEG_BIG), axis=1, keepdims=True)
        hit = (count_gt(cur) <= kf) & jnp.logical_not(done)
        thr = jnp.where(hit, cur, thr)
        done = done | hit
    thr = jnp.where(trivial, 0.1 * NEG_BIG, thr)
    thr_scr[...] = jnp.broadcast_to(thr, (tq, LANES))

    cnt_gt, cnt_eq = tally(thr)
    sum1_scr[...] = jnp.broadcast_to(cnt_gt, (tq, LANES))
    sum2_scr[...] = jnp.broadcast_to(cnt_eq, (tq, LANES))
    proven = trivial | ((cnt_gt <= kf) & (cnt_gt + cnt_eq >= kf))

    @pl.when(jnp.min(jnp.where(proven, 1.0, 0.0)) < 0.5)
    def _():
        bits = lax.bitcast_convert_type(xm_scr[...], I32)
        key_scr[...] = bits ^ ((bits >> 31) & 0x7FFFFFFF)

        def bit_step(j, tu):
            cand = tu | lax.shift_left(jnp.int32(1), 31 - j)
            cnt = jnp.sum(jnp.where(key_scr[...] >= (cand ^ INT_MIN), 1.0, 0.0), axis=1, keepdims=True)
            return jnp.where(cnt >= kf, cand, tu)

        tkey = lax.fori_loop(0, 32, bit_step, jnp.zeros((tq, 1), I32)) ^ INT_MIN
        tval = lax.bitcast_convert_type(tkey ^ ((tkey >> 31) & 0x7FFFFFFF), F32)
        texact = jnp.where(trivial, 0.1 * NEG_BIG, tval)
        thr_scr[...] = jnp.broadcast_to(texact, (tq, LANES))
        cg, ce = tally(texact)
        sum1_scr[...] = jnp.broadcast_to(cg, (tq, LANES))
        sum2_scr[...] = jnp.broadcast_to(ce, (tq, LANES))

    thr = thr_scr[:, :1]
    cnt_gt = sum1_scr[:, :1]
    cnt_eq = sum2_scr[:, :1]
    xv = xm_scr[...]
    gt = xv > thr
    eq = xv == thr
    want_eq = kf - cnt_gt

    all_or_none = jnp.where(want_eq >= cnt_eq, POS_BIG, 0)
    lim_scr[...] = jnp.broadcast_to(all_or_none, (tq, LANES))
    partial = (want_eq > 0.0) & (want_eq < cnt_eq)
    kpos = lax.broadcasted_iota(I32, (tq, s), 1)

    @pl.when(jnp.max(jnp.where(partial, 1.0, 0.0)) > 0.5)
    def _():
        eqpos = jnp.where(eq, kpos, POS_BIG)
        eqposf = eqpos.astype(F32)
        cur = jnp.full((tq, 1), -1.0, F32)
        xlim = all_or_none
        for it in range(TIE_WALK_STEPS):
            cur = jnp.min(jnp.where(eqposf > cur, eqposf, float(POS_BIG)), axis=1, keepdims=True)
            xlim = jnp.where(partial & (want_eq == float(it)), cur.astype(I32), xlim)
        lim_scr[...] = jnp.broadcast_to(xlim, (tq, LANES))

        @pl.when(jnp.max(jnp.where(partial & (want_eq >= float(TIE_WALK_STEPS)), 1.0, 0.0)) > 0.5)
        def _():
            nbits = max(1, int(s).bit_length())

            def pos_step(j, xl):
                cand = xl | lax.shift_left(jnp.int32(1), nbits - 1 - j)
                cnt = jnp.sum(jnp.where(eqpos < cand, 1.0, 0.0), axis=1, keepdims=True)
                return jnp.where(cnt <= want_eq, cand, xl)

            lim_scr[...] = jnp.broadcast_to(lax.fori_loop(0, nbits, pos_step, jnp.zeros((tq, 1), I32)),
                                            (tq, LANES))

    sel = gt | (eq & (kpos < lim_scr[:, :1]))
    bias_scr[...] = jnp.where(sel, 0.0, NEG_BIG)


def _attn_kernel(*refs, topk, tpos0, n_keys):
    q_ref, iq_ref, sg_ref, kd_ref, vd_ref, ikd_ref = refs[:6]
    o_ref, xm_scr, key_scr, bias_scr, thr_scr, lim_scr, sum1_scr, sum2_scr = refs[-8:]
    tq = q_ref.shape[1]
    s = kd_ref.shape[1]
    lane = lax.broadcasted_iota(I32, (tq, LANES), 1)
    halves = (lane < HEAD_DIM, lane >= HEAD_DIM)

    tpos = tpos0 + lax.broadcasted_iota(I32, (tq, 1), 0)
    lim = jnp.minimum((tpos // CHUNK + 1) * CHUNK, n_keys)
    search = min(s, n_keys) > topk

    rb = min(tq, LANES)
    key_tile = 2 * LANES
    lane_rb = lax.broadcasted_iota(I32, (rb, LANES), 1)
    halves_rb = (lane_rb < HEAD_DIM, lane_rb >= HEAD_DIM)
    for r in range(tq // rb):
        rs = slice(r * rb, (r + 1) * rb)
        sg = sg_ref[0, rs, :]
        qms, sgs = [], []
        for hd in range(IDX_HEADS):
            qb = iq_ref[0, rs, (hd // 2) * LANES:(hd // 2 + 1) * LANES]
            qms.append(jnp.where(halves_rb[hd % 2], qb, jnp.zeros_like(qb)))
            sgs.append(sg[:, HEAD_DIM + hd:HEAD_DIM + hd + 1])
        acc1 = jnp.zeros((rb, LANES), F32)
        acc2 = jnp.zeros((rb, LANES), F32)
        for c0 in range(0, s, key_tile):
            w = min(key_tile, s - c0)
            ikd_t = ikd_ref[0, c0:c0 + w, :]
            acc = jnp.zeros((rb, w), F32)
            for hd in range(IDX_HEADS):
                acc = acc + sgs[hd] * jnp.maximum(_nt_dot(qms[hd], ikd_t), 0.0)
            adm_t = (c0 + lax.broadcasted_iota(I32, (rb, w), 1)) < lim[rs]
            xm_scr[rs, c0:c0 + w] = jnp.where(adm_t, acc, NEG_BIG)
            if search:
                sc0 = jnp.where(adm_t, acc, 0.0)
                for l0 in range(0, w, LANES):
                    part = sc0[:, l0:l0 + LANES]
                    acc1 = acc1 + part
                    acc2 = acc2 + part * part
        if search:
            sum1_scr[rs, :] = acc1
            sum2_scr[rs, :] = acc2

    logits = []
    for c in range(A_WIDTH // LANES):
        g = (2 * c) // (A_HEADS // A_KV_HEADS)
        qb = q_ref[0, :, c * LANES:(c + 1) * LANES]
        for e in range(2):
            qm = jnp.where(halves[e], qb, jnp.zeros_like(qb))
            logits.append(_nt_dot(qm, kd_ref[0, :, g * LANES:(g + 1) * LANES]))

    if search:
        _select_topk(xm_scr, sum1_scr, sum2_scr, lim, topk, key_scr, thr_scr, lim_scr, bias_scr)
    else:
        bias_scr[...] = jnp.where(lax.broadcasted_iota(I32, (tq, s), 1) < lim, 0.0, NEG_BIG)

    ones = jnp.ones((s, LANES), BF16)
    for g in range(A_KV_HEADS):
        vd1 = jnp.concatenate([vd_ref[0, :, g * LANES:(g + 1) * LANES], ones], axis=1)
        for c in range(g * 2, g * 2 + 2):
            outs = []
            for e in range(2):
                lg = logits[2 * c + e] + bias_scr[...]
                p = jnp.exp2(lg - jnp.max(lg, axis=1, keepdims=True))
                ol = jnp.dot(p.astype(BF16), vd1, preferred_element_type=F32)
                outs.append(ol[:, :LANES] / ol[:, LANES:])
            o_ref[0, :, c * LANES:(c + 1) * LANES] = jnp.where(halves[0], outs[0], outs[1])


def _attn_call(q, iq, sg, kd, vd, ikd, prev, *, topk, qpos0, n_keys, tq, qblock, s_blk, name):
    b, t, _ = q.shape
    qmap = lambda bi: (bi, qblock, 0)
    kmap = lambda bi: (bi, 0, 0)
    chained = prev is not None
    return pl.pallas_call(
        functools.partial(_attn_kernel, topk=topk, tpos0=qpos0 + qblock * tq, n_keys=n_keys),
        grid=(b,),
        in_specs=[pl.BlockSpec((1, tq, A_WIDTH), qmap),
                  pl.BlockSpec((1, tq, IDX_HEADS * HEAD_DIM), qmap),
                  pl.BlockSpec((1, tq, LANES), qmap),
                  pl.BlockSpec((1, s_blk, 2 * KV_WIDTH), kmap),
                  pl.BlockSpec((1, s_blk, 2 * KV_WIDTH), kmap),
                  pl.BlockSpec((1, s_blk, LANES), kmap)] + ([pl.BlockSpec(memory_space=pl.ANY)] if chained else []),
        out_specs=pl.BlockSpec((1, tq, A_WIDTH), qmap),
        out_shape=jax.ShapeDtypeStruct((b, t, A_WIDTH), F32),
        input_output_aliases={6: 0} if chained else {},
        scratch_shapes=[pltpu.VMEM((tq, s_blk), F32), pltpu.VMEM((tq, s_blk), I32), pltpu.VMEM((tq, s_blk), F32),
                        pltpu.VMEM((tq, LANES), F32), pltpu.VMEM((tq, LANES), I32),
                        pltpu.VMEM((tq, LANES), F32), pltpu.VMEM((tq, LANES), F32)],
        compiler_params=pltpu.CompilerParams(dimension_semantics=("parallel",),
                                             vmem_limit_bytes=VMEM_LIMIT),
        name=name,
    )(*((q, iq, sg, kd, vd, ikd) + ((prev,) if chained else ())))


def _attention(q, iq, sg, kd, vd, ikd, *, topk, qpos0, n_keys, tag):
    b, t, _ = q.shape
    tq = min(t, 256)
    out = None
    for qblock in range(t // tq):
        last_pos = qpos0 + (qblock + 1) * tq - 1
        visible = min((last_pos // CHUNK + 1) * CHUNK, n_keys)
        s_blk = min(-(-visible // LANES) * LANES, kd.shape[1])
        out = _attn_call(q, iq, sg, kd, vd, ikd, out, topk=topk, qpos0=qpos0, n_keys=n_keys, tq=tq,
                         qblock=qblock, s_blk=s_blk, name=f"attn_{tag}_q{qblock}")
    return out


def _retention_log_decay():
    return jnp.log(1.0 - jnp.exp(jnp.linspace(math.log(1.0 / 32), math.log(1.0 / 512), REC_HEADS, dtype=F32)))


def _rec_kernel(bq_ref, bk_ref, bv_ref, blf_ref, cq_ref, ck_ref, cv_ref, s0b_ref, s0c_ref,
                hg_ref, rg_ref, tri_ref, dmat_ref, inner_ref, kscale_ref, rdec_ref, gm_ref,
                yb_ref, yc_ref, sb_out, sc_out, sb_scr, sc_scr, *, chunk, n_chunks):
    j = pl.program_id(1)
    width = B_WIDTH

    @pl.when(j == 0)
    def _():
        sb_scr[...] = s0b_ref[0]
        sc_scr[...] = s0c_ref[0]

    rowh = lax.broadcasted_iota(I32, (width, width), 0) // HEAD_DIM
    colh = lax.broadcasted_iota(I32, (width, width), 1) // HEAD_DIM
    block_diag = rowh == colh
    lane_h = lax.broadcasted_iota(I32, (chunk, width), 1) // HEAD_DIM
    causal = (lax.broadcasted_iota(I32, (chunk, chunk), 0)
              >= lax.broadcasted_iota(I32, (chunk, chunk), 1))
    gmat = gm_ref[...]
    tri = tri_ref[...]
    tn = (((0,), (0,)), ((), ()))

    def masked_heads(x):
        return [jnp.where(lane_h == hd, x, 0.0).astype(BF16) for hd in range(REC_HEADS)]

    heads = range(REC_HEADS)
    chunks = range(n_chunks)
    rows = [pl.ds(c * chunk, chunk) for c in chunks]

    bcs = []
    for c in chunks:
        lf = blf_ref[0, rows[c], :]
        a1 = lf.astype(BF16)
        r1 = lf - a1.astype(F32)
        a2 = r1.astype(BF16)
        a3 = (r1 - a2.astype(F32)).astype(BF16)
        bcs.append(jnp.dot(tri, jnp.concatenate([a1, a2, a3], axis=0), preferred_element_type=F32))

    att_b, att_c, upd_b, upd_c, qs_b, qs_c, dec_b, vb_b, vb_c = ([] for _ in range(9))
    for c in chunks:
        bc = bcs[c]
        mid = bc[chunk // 2 - 1:chunk // 2, :]
        last = bc[chunk - 1:chunk, :]
        q = bq_ref[0, rows[c], :]
        k = bk_ref[0, rows[c], :]
        vb = bv_ref[0, rows[c], :].astype(BF16)
        kt = (k * jnp.exp(mid - bc)).astype(BF16)
        att_b.append([_nt_dot(qh, kt) for qh in masked_heads(q * jnp.exp(bc - mid))])
        upd_b.append(lax.dot_general(vb, (k * jnp.exp(last - bc)).astype(BF16), tn, preferred_element_type=F32))
        qs_b.append((q * jnp.exp(bc)).astype(BF16))
        dec_b.append(jnp.exp(last))
        vb_b.append(vb)
        q = cq_ref[0, rows[c], :]
        k = ck_ref[0, rows[c], :]
        vb = cv_ref[0, rows[c], :].astype(BF16)
        kb = k.astype(BF16)
        att_c.append([_nt_dot(qh, kb) for qh in masked_heads(q)])
        upd_c.append(lax.dot_general(vb, (k * kscale_ref[...]).astype(BF16), tn, preferred_element_type=F32))
        qs_c.append((q * inner_ref[...]).astype(BF16))
        vb_c.append(vb)

    st = sb_scr[...]
    rt = sc_scr[...]
    o_b, o_c = [], []
    for c in chunks:
        o_b.append(_nt_dot(qs_b[c], st.astype(BF16)))
        st = st * dec_b[c] + jnp.where(block_diag, upd_b[c], 0.0)
        o_c.append(_nt_dot(qs_c[c], rt.astype(BF16)))
        rt = rt * rdec_ref[...] + jnp.where(block_diag, upd_c[c], 0.0)
    sb_scr[...] = st
    sc_scr[...] = rt

    prod_b = [[jnp.dot(jnp.where(causal, att_b[c][hd], 0.0).astype(BF16), vb_b[c], preferred_element_type=F32)
               for hd in heads] for c in chunks]
    prod_c = [[jnp.dot((att_c[c][hd] * dmat_ref[hd * chunk:(hd + 1) * chunk, :]).astype(BF16), vb_c[c],
                       preferred_element_type=F32) for hd in heads] for c in chunks]
    outs = []
    for c in chunks:
        for o, prod in ((o_b[c], prod_b[c]), (o_c[c], prod_c[c])):
            for hd in heads:
                o = o + jnp.where(lane_h == hd, prod[hd], 0.0)
            outs.append(o)

    means = []
    for o in outs:
        hi, lo = _split2(o * o)
        means.append(jnp.dot(jnp.concatenate([hi, lo], axis=1), gmat, preferred_element_type=F32))
    for c in chunks:
        yb_ref[0, rows[c], :] = outs[2 * c] * lax.rsqrt(means[2 * c] + EPS) * hg_ref[...]
        yc_ref[0, rows[c], :] = outs[2 * c + 1] * lax.rsqrt(means[2 * c + 1] + EPS) * rg_ref[...]

    @pl.when(j == pl.num_programs(1) - 1)
    def _():
        sb_out[0] = sb_scr[...]
        sc_out[0] = sc_scr[...]


def _state_to_blockdiag(s):
    b = s.shape[0]
    eye = jnp.eye(REC_HEADS, dtype=s.dtype)
    return jnp.einsum('bhkv,hg->bhvgk', s, eye).reshape(b, B_WIDTH, B_WIDTH)


def _state_from_blockdiag(sbd):
    b = sbd.shape[0]
    s5 = sbd.reshape(b, REC_HEADS, HEAD_DIM, REC_HEADS, HEAD_DIM)
    return jnp.einsum('bhvhk->bhkv', s5)


def _rec_call(bq, bk, bv, blf, cq, ck, cv, s0b, s0c, hgain, rgain, name):
    b, t, _ = bq.shape
    chunk = min(CHUNK, t)
    tb = min(t, 8 * chunk)
    n_chunks = tb // chunk
    lg = _retention_log_decay()
    n = jnp.arange(chunk, dtype=F32)
    diff = n[:, None] - n[None, :]
    dmat = jnp.where(diff >= 0, jnp.exp(jnp.where(diff >= 0, diff, 0.0)[None] * lg[:, None, None]), 0.0)
    dmat = dmat.reshape(REC_HEADS * chunk, chunk)
    per_lane = lambda a: jnp.repeat(a, HEAD_DIM, axis=-1)
    inner = per_lane(jnp.exp((n[:, None] + 1.0) * lg[None, :]))
    kscale = per_lane(jnp.exp((chunk - 1.0 - n)[:, None] * lg[None, :]))
    rdec = per_lane(jnp.exp(chunk * lg)[None, :])
    tri = jnp.tile(jnp.tril(jnp.ones((chunk, chunk), F32)), (1, 3)).astype(BF16)
    seq = lambda bi, j: (bi, j, 0)
    per_b = lambda bi, j: (bi, 0, 0)
    const = lambda bi, j: (0, 0)
    stream = pl.BlockSpec((1, tb, B_WIDTH), seq)
    state = pl.BlockSpec((1, B_WIDTH, B_WIDTH), per_b)
    return pl.pallas_call(
        functools.partial(_rec_kernel, chunk=chunk, n_chunks=n_chunks),
        grid=(b, t // tb),
        in_specs=[stream] * 7 + [state, state,
                                 pl.BlockSpec((1, B_WIDTH), const), pl.BlockSpec((1, B_WIDTH), const),
                                 pl.BlockSpec((chunk, 3 * chunk), const),
                                 pl.BlockSpec((REC_HEADS * chunk, chunk), const),
                                 pl.BlockSpec((chunk, B_WIDTH), const), pl.BlockSpec((chunk, B_WIDTH), const),
                                 pl.BlockSpec((1, B_WIDTH), const),
                                 pl.BlockSpec((2 * B_WIDTH, B_WIDTH), const)],
        out_specs=[stream, stream, state, state],
        out_shape=[jax.ShapeDtypeStruct((b, t, B_WIDTH), F32), jax.ShapeDtypeStruct((b, t, C_WIDTH), F32),
                   jax.ShapeDtypeStruct((b, B_WIDTH, B_WIDTH), F32),
                   jax.ShapeDtypeStruct((b, C_WIDTH, C_WIDTH), F32)],
        scratch_shapes=[pltpu.VMEM((B_WIDTH, B_WIDTH), F32), pltpu.VMEM((C_WIDTH, C_WIDTH), F32)],
        compiler_params=pltpu.CompilerParams(dimension_semantics=("parallel", "arbitrary"),
                                             vmem_limit_bytes=VMEM_LIMIT),
        name=name,
    )(bq, bk, bv, blf, cq, ck, cv, s0b, s0c,
      jnp.tile(hgain, REC_HEADS).reshape(1, B_WIDTH), jnp.tile(rgain, REC_HEADS).reshape(1, C_WIDTH),
      tri, dmat, inner, kscale, rdec, jnp.tile(_group_mean_matrix(B_WIDTH), (2, 1)))


def _out_kernel(x_ref, oa_ref, yb_ref, yc_ref, gate_ref, w_ref, y_ref):
    gate = gate_ref[...].astype(F32)
    ya = (oa_ref[...] * gate[:, :A_WIDTH]).astype(BF16)
    yb = (yb_ref[...] * gate[:, A_WIDTH:A_WIDTH + B_WIDTH]).astype(BF16)
    yc = (yc_ref[...] * gate[:, A_WIDTH + B_WIDTH:]).astype(BF16)
    acc = jnp.dot(ya, w_ref[:A_WIDTH, :], preferred_element_type=F32)
    acc = acc + jnp.dot(yb, w_ref[A_WIDTH:A_WIDTH + B_WIDTH, :], preferred_element_type=F32)
    acc = acc + jnp.dot(yc, w_ref[A_WIDTH + B_WIDTH:, :], preferred_element_type=F32)
    y_ref[...] = x_ref[...] + acc


def _out_call(x2d, oa, yb, yc, gate, w_out16, tm, name):
    n = x2d.shape[0]
    tm = min(tm, n)
    row = lambda r: (r, 0)
    return pl.pallas_call(
        _out_kernel,
        grid=(n // tm,),
        in_specs=[pl.BlockSpec((tm, D_MODEL), row), pl.BlockSpec((tm, A_WIDTH), row),
                  pl.BlockSpec((tm, B_WIDTH), row), pl.BlockSpec((tm, C_WIDTH), row),
                  pl.BlockSpec((tm, D_MODEL), row), pl.BlockSpec((D_MODEL, D_MODEL), lambda r: (0, 0))],
        out_specs=pl.BlockSpec((tm, D_MODEL), row),
        out_shape=jax.ShapeDtypeStruct((n, D_MODEL), F32),
        compiler_params=pltpu.CompilerParams(dimension_semantics=("parallel",),
                                             vmem_limit_bytes=VMEM_LIMIT),
        name=name,
    )(x2d, oa, yb, yc, gate, w_out16)


def _dup_kv(x):
    b, s = x.shape[:2]
    return jnp.repeat(x.reshape(b, s, A_KV_HEADS, 1, HEAD_DIM), 2, axis=3).reshape(b, s, 2 * KV_WIDTH).astype(BF16)


def _mixer_layer(x, pos0, past, s_hgrn, s_ret, layer, w_p, w_out16, norm_g, q_gain, k_gain,
                 lb_logits, hgain, rgain, tag):
    b, t, _ = x.shape
    n = b * t
    x2d = x.reshape(n, D_MODEL)
    pos = pos0 + jnp.arange(t)
    (q, k, v, kd, vd, iq, sg, ik, ikd, gate, bq, bk, bv, blf, cq, ck, cv) = _proj_call(
        x2d, t, pos, norm_g, w_p, q_gain, k_gain, lb_logits, layer, 256)

    three = lambda a: a.reshape(b, t, a.shape[-1])
    kd3, vd3, ikd3 = three(kd), three(vd), three(ikd)
    if past is not None:
        pk, pv, pik = past
        kd3 = jnp.concatenate([_dup_kv(pk), kd3], axis=1)
        vd3 = jnp.concatenate([_dup_kv(pv), vd3], axis=1)
        ikd3 = jnp.concatenate([jnp.concatenate([pik, pik], axis=-1).astype(BF16), ikd3], axis=1)
    n_keys = kd3.shape[1]
    s_pad = -(-n_keys // LANES) * LANES
    if s_pad != n_keys:
        padw = ((0, 0), (0, s_pad - n_keys), (0, 0))
        kd3, vd3, ikd3 = jnp.pad(kd3, padw), jnp.pad(vd3, padw), jnp.pad(ikd3, padw)
    topk = min(TOPK_MAX, n_keys // 4)
    oa = _attention(three(q), three(iq), three(sg), kd3, vd3, ikd3, topk=topk, qpos0=pos0,
                    n_keys=n_keys, tag=tag)

    yb, yc, sb, sc = _rec_call(three(bq), three(bk), three(bv), three(blf), three(cq), three(ck), three(cv),
                               _state_to_blockdiag(s_hgrn), _state_to_blockdiag(s_ret), hgain, rgain,
                               name=f"rec_{tag}")
    out = _out_call(x2d, oa.reshape(n, A_WIDTH), yb.reshape(n, B_WIDTH), yc.reshape(n, C_WIDTH), gate,
                    w_out16, 512, name=f"out_{tag}")
    return (out.reshape(b, t, D_MODEL),
            (k.reshape(b, t, A_KV_HEADS, HEAD_DIM), v.reshape(b, t, A_KV_HEADS, HEAD_DIM), ik,
             _state_from_blockdiag(sb), _state_from_blockdiag(sc)))


def kernel(x_prompt, x_sample, cache_k, cache_v, cache_idx_k, state_hgrn, state_ret, norm_g, w_in,
           q_norm_g, k_norm_g, hgrn_lb_logits, hgrn_norm_g, ret_norm_g, w_out):
    depth = w_in.shape[0]
    bp, tp = x_prompt.shape[:2]
    bs, ts = x_sample.shape[:2]
    past = cache_k.shape[2]
    zero_state = jnp.zeros((bp, REC_HEADS, HEAD_DIM, HEAD_DIM), F32)

    yp, ys = x_prompt, x_sample
    outs_p, outs_s = [], []
    for l in range(depth):
        w_p = _prep_w_in(w_in[l])
        w_o = w_out[l].astype(BF16)
        args = (l, w_p, w_o, norm_g[l], q_norm_g[l], k_norm_g[l], hgrn_lb_logits, hgrn_norm_g[l], ret_norm_g[l])
        yp, st = _mixer_layer(yp, 0, None, zero_state, zero_state, *args, tag=f"p{l}")
        outs_p.append(st)
        ys, st = _mixer_layer(ys, past, (cache_k[l], cache_v[l], cache_idx_k[l]),
                              state_hgrn[l], state_ret[l], *args, tag=f"s{l}")
        outs_s.append(st)

    def stack(outs, i, shape):
        return jnp.stack([o[i] for o in outs]).reshape(shape)

    return (yp, ys,
            stack(outs_p, 0, (depth, bp, tp, A_KV_HEADS, HEAD_DIM)),
            stack(outs_p, 1, (depth, bp, tp, A_KV_HEADS, HEAD_DIM)),
            stack(outs_p, 2, (depth, bp, tp, HEAD_DIM)),
            stack(outs_p, 3, (depth, bp, REC_HEADS, HEAD_DIM, HEAD_DIM)),
            stack(outs_p, 4, (depth, bp, REC_HEADS, HEAD_DIM, HEAD_DIM)),
            stack(outs_s, 0, (depth, bs, ts, A_KV_HEADS, HEAD_DIM)),
            stack(outs_s, 1, (depth, bs, ts, A_KV_HEADS, HEAD_DIM)),
            stack(outs_s, 2, (depth, bs, ts, HEAD_DIM)),
            stack(outs_s, 3, (depth, bs, REC_HEADS, HEAD_DIM, HEAD_DIM)),
            stack(outs_s, 4, (depth, bs, REC_HEADS, HEAD_DIM, HEAD_DIM)))
```

```python
import functools
import math

import numpy as np
import jax
import jax.numpy as jnp
from jax import lax
from jax.experimental import pallas as pl
from jax.experimental.pallas import tpu as pltpu

F32 = jnp.float32
BF16 = jnp.bfloat16
I32 = jnp.int32

D_MODEL = 1024
HEAD_DIM = 64
CHUNK = 64
A_WIDTH = 512
A_HEADS = 8
A_KV_HEADS = 2
KV_WIDTH = A_KV_HEADS * HEAD_DIM
IDX_HEADS = 8
IDX_W_SCALE = (IDX_HEADS * HEAD_DIM) ** -0.5
TOPK_MAX = 256
B_WIDTH = 256
C_WIDTH = 256
REC_HEADS = 4
ROPE_THETA = 10000.0
EPS = 1e-6
LANES = 128
VMEM_LIMIT = 48 * 1024 * 1024

_SIZES = (A_WIDTH, KV_WIDTH, KV_WIDTH, A_WIDTH, IDX_HEADS * HEAD_DIM, HEAD_DIM, IDX_HEADS,
          B_WIDTH, B_WIDTH, B_WIDTH, B_WIDTH, C_WIDTH, C_WIDTH, C_WIDTH, C_WIDTH)
_SPLIT_IDX = tuple(int(s) for s in np.cumsum(_SIZES)[:-1])

_AQ, _AK, _AV, _AG, _IQ, _IKW = 0, 512, 640, 768, 1280, 1792
_BQ, _BF, _BI, _BG, _CQ, _CK, _CV, _CG = 1920, 2176, 2432, 2688, 2944, 3200, 3456, 3712
_N_PROJ = 3968

INT_MIN = -(2 ** 31)
NEG_BIG = -1e30
POS_BIG = 1 << 20
LOG2_E = math.log2(math.e)


def _prep_w_in(w):
    (aq, ak, av, ag, iq, ik, iw, bq, bf, bi, bg, cq, ck, cv, cg) = jnp.split(w, _SPLIT_IDX, axis=1)
    pad = jnp.zeros((w.shape[0], LANES - HEAD_DIM - IDX_HEADS), w.dtype)
    return jnp.concatenate([aq, ak, av, ag, iq, ik, iw, pad, bq, bf, bi, bg, cq, ck, cv, cg],
                           axis=1).astype(BF16)


def _rope_tables(pos):
    half = HEAD_DIM // 2
    freqs = ROPE_THETA ** (-jnp.arange(half, dtype=F32) / half)
    ang = pos.astype(F32)[:, None] * freqs[None, :]
    cos, sin = jnp.cos(ang), jnp.sin(ang)
    cos64 = jnp.concatenate([cos, cos], axis=1)
    sin64 = jnp.concatenate([-sin, sin], axis=1)
    return jnp.concatenate([cos64, cos64], axis=1), jnp.concatenate([sin64, sin64], axis=1)


def _group_mean_matrix(width):
    idx = np.arange(width) // HEAD_DIM
    return jnp.asarray((idx[:, None] == idx[None, :]).astype(np.float32) / HEAD_DIM, BF16)


def _split2(x):
    hi = x.astype(BF16)
    lo = (x - hi.astype(F32)).astype(BF16)
    return hi, lo


def _group_mean(x2, gmat):
    hi, lo = _split2(x2)
    return (jnp.dot(hi, gmat, preferred_element_type=F32)
            + jnp.dot(lo, gmat, preferred_element_type=F32))


def _nt_dot(a, b):
    return lax.dot_general(a, b, (((1,), (1,)), ((), ())), preferred_element_type=F32)


def _proj_kernel(x_ref, g_ref, w_ref, cos_ref, sin_ref, qg_ref, kg_ref, lb_ref, gm_ref, ex_ref,
                 q_out, k_out, v_out, kd_out, vd_out, iq_out, sg_out, ik_out, ikd_out, gate_out,
                 bq_out, bk_out, bv_out, blf_out, cq_out, ck_out, cv_out, *, layer):
    x = x_ref[...]
    tm = x.shape[0]
    ms = jnp.mean(x * x, axis=-1, keepdims=True)
    h = (x * lax.rsqrt(ms + EPS) * g_ref[...]).astype(BF16)
    cos = cos_ref[...]
    sin = sin_ref[...]
    lane = lax.broadcasted_iota(I32, (tm, LANES), 1)
    first_half = (lane % HEAD_DIM) < (HEAD_DIM // 2)
    low_head = lane < HEAD_DIM
    gmat = gm_ref[...]

    def proj(c0, width):
        return jnp.dot(h, w_ref[:, c0:c0 + width], preferred_element_type=F32)

    def rope(xb):
        swapped = jnp.where(first_half, pltpu.roll(xb, LANES - HEAD_DIM // 2, 1),
                            pltpu.roll(xb, HEAD_DIM // 2, 1))
        return xb * cos + swapped * sin

    def head_norm(xb, gain):
        return xb * lax.rsqrt(_group_mean(xb * xb, gmat) + EPS) * gain

    def dup_heads(xb):
        other = pltpu.roll(xb, HEAD_DIM, 1)
        return jnp.where(low_head, xb, other), jnp.where(low_head, other, xb)

    ag = proj(_AG, A_WIDTH)
    bg = proj(_BG, B_WIDTH)
    cg = proj(_CG, C_WIDTH)
    bq = proj(_BQ, B_WIDTH)
    bf = proj(_BF, B_WIDTH)
    bi = proj(_BI, B_WIDTH)
    cq = proj(_CQ, C_WIDTH)
    ck = proj(_CK, C_WIDTH)
    cv = proj(_CV, C_WIDTH)
    akv = proj(_AK, 2 * KV_WIDTH)
    ikw = proj(_IKW, LANES)
    aq = proj(_AQ, A_WIDTH)
    iq = proj(_IQ, IDX_HEADS * HEAD_DIM)

    def silu(z):
        return z / (1.0 + jnp.exp(-z))

    gate_out[:, :A_WIDTH] = silu(ag).astype(BF16)
    gate_out[:, A_WIDTH:A_WIDTH + B_WIDTH] = silu(bg).astype(BF16)
    gate_out[:, A_WIDTH + B_WIDTH:] = silu(cg).astype(BF16)

    logits = lb_ref[...]
    e = jnp.exp(logits - jnp.max(logits, axis=0, keepdims=True))
    sm = e / jnp.sum(e, axis=0, keepdims=True)
    lb = jnp.zeros((1, B_WIDTH), F32)
    for j in range(1, layer + 1):
        lb = lb + sm[j:j + 1, :]
    f = lb + (1.0 - lb) / (1.0 + jnp.exp(-bf))
    bq_out[...] = bq
    bk_out[...] = (1.0 - lb) / (1.0 + jnp.exp(bf))
    bv_out[...] = bi
    blf_out[...] = jnp.log(f)

    for c in range(C_WIDTH // LANES):
        sl = slice(c * LANES, (c + 1) * LANES)
        cq_out[:, sl] = rope(cq[:, sl])
        ck_out[:, sl] = rope(ck[:, sl]) * (HEAD_DIM ** -0.5)
    cv_out[...] = cv

    v = akv[:, KV_WIDTH:]
    v_out[...] = v
    v0, v1 = dup_heads(v)
    vd_out[:, :LANES] = v0.astype(BF16)
    vd_out[:, LANES:] = v1.astype(BF16)
    ikr = rope(ikw)
    ik_out[...] = ikr[:, :HEAD_DIM]
    ikd_out[...] = dup_heads(ikr)[0].astype(BF16)
    sg_out[...] = jnp.where(ikw >= 0, 1.0, -1.0).astype(F32)

    for c in range(A_WIDTH // LANES):
        blk = rope(head_norm(aq[:, c * LANES:(c + 1) * LANES], qg_ref[...]))
        q_out[:, c * LANES:(c + 1) * LANES] = (blk * (HEAD_DIM ** -0.5 * LOG2_E)).astype(BF16)
    k = rope(head_norm(akv[:, :KV_WIDTH], kg_ref[...]))
    k_out[...] = k
    k0, k1 = dup_heads(k)
    kd_out[:, :LANES] = k0.astype(BF16)
    kd_out[:, LANES:] = k1.astype(BF16)

    whi, wlo = _split2(jnp.abs(ikw) * IDX_W_SCALE)
    wexp = (jnp.dot(whi, ex_ref[...], preferred_element_type=F32)
            + jnp.dot(wlo, ex_ref[...], preferred_element_type=F32))
    for c in range(IDX_HEADS * HEAD_DIM // LANES):
        sl = slice(c * LANES, (c + 1) * LANES)
        iq_out[:, sl] = (rope(iq[:, sl]) * wexp[:, sl]).astype(BF16)


def _proj_call(x2d, seq, pos, norm_g, w_p, q_gain, k_gain, lb_logits, layer, tm):
    n = x2d.shape[0]
    tm = min(tm, n)
    cos, sin = _rope_tables(pos)
    if seq % tm == 0:
        per_seq = seq // tm
        tab_map = lambda r: (r % per_seq, 0)
    else:
        cos = jnp.tile(cos, (n // seq, 1))
        sin = jnp.tile(sin, (n // seq, 1))
        tab_map = lambda r: (r, 0)
    expand = np.zeros((LANES, IDX_HEADS * HEAD_DIM), np.float32)
    for hd in range(IDX_HEADS):
        expand[HEAD_DIM + hd, hd * HEAD_DIM:(hd + 1) * HEAD_DIM] = 1.0
    const = lambda r: (0, 0)
    row = lambda r: (r, 0)
    widths = [(A_WIDTH, BF16), (KV_WIDTH, F32), (KV_WIDTH, F32), (2 * KV_WIDTH, BF16), (2 * KV_WIDTH, BF16),
              (IDX_HEADS * HEAD_DIM, BF16), (LANES, F32), (HEAD_DIM, F32), (LANES, BF16), (D_MODEL, BF16),
              (B_WIDTH, F32), (B_WIDTH, F32), (B_WIDTH, F32), (B_WIDTH, F32),
              (C_WIDTH, F32), (C_WIDTH, F32), (C_WIDTH, F32)]
    return pl.pallas_call(
        functools.partial(_proj_kernel, layer=layer),
        grid=(n // tm,),
        in_specs=[pl.BlockSpec((tm, D_MODEL), row),
                  pl.BlockSpec((1, D_MODEL), const),
                  pl.BlockSpec((D_MODEL, _N_PROJ), const),
                  pl.BlockSpec((tm, LANES), tab_map),
                  pl.BlockSpec((tm, LANES), tab_map),
                  pl.BlockSpec((1, LANES), const),
                  pl.BlockSpec((1, LANES), const),
                  pl.BlockSpec(lb_logits.shape, const),
                  pl.BlockSpec((LANES, LANES), const),
                  pl.BlockSpec((LANES, IDX_HEADS * HEAD_DIM), const)],
        out_specs=[pl.BlockSpec((tm, w), row) for w, _ in widths],
        out_shape=[jax.ShapeDtypeStruct((n, w), dt) for w, dt in widths],
        compiler_params=pltpu.CompilerParams(dimension_semantics=("parallel",),
                                             vmem_limit_bytes=VMEM_LIMIT),
        name=f"proj_l{layer}",
    )(x2d, norm_g.reshape(1, D_MODEL), w_p, cos, sin,
      jnp.tile(q_gain, 2).reshape(1, LANES), jnp.tile(k_gain, 2).reshape(1, LANES),
      lb_logits, _group_mean_matrix(LANES), jnp.asarray(expand, BF16))


SEARCH_INTERP_STEPS = 8
SEARCH_WALK_STEPS = 4
SEARCH_RANK_BIAS = 1.5
SEARCH_EDGE = 0.05
TIE_WALK_STEPS = 8


def _normal_upper_quantile(r):
    rr = jnp.minimum(r, 1.0 - r)
    t = jnp.sqrt(-2.0 * jnp.log(rr))
    z = t - ((0.010328 * t + 0.802853) * t + 2.515517) / (((0.001308 * t + 0.189269) * t + 1.432788) * t + 1.0)
    return jnp.where(r <= 0.5, z, -z)


def _select_topk(xm_scr, sum1_scr, sum2_scr, lim, topk, key_scr, thr_scr, lim_scr, bias_scr):
    tq, s = xm_scr.shape
    kf = float(topk)
    nf = lim.astype(F32)
    trivial = lim <= topk

    def count_gt(p):
        return jnp.sum(jnp.where(xm_scr[...] > p, 1.0, 0.0), axis=1, keepdims=True)

    def tally(t):
        xv = xm_scr[...]
        return (jnp.sum(jnp.where(xv > t, 1.0, 0.0), axis=1, keepdims=True),
                jnp.sum(jnp.where(xv == t, 1.0, 0.0), axis=1, keepdims=True))

    mu = jnp.sum(sum1_scr[...], axis=1, keepdims=True) / nf
    var = jnp.sum(sum2_scr[...], axis=1, keepdims=True) / nf - mu * mu
    sd = jnp.sqrt(jnp.maximum(var, 1e-30))
    target = kf + SEARCH_RANK_BIAS
    lo, hi = mu - 8.0 * sd, mu + 8.0 * sd
    clo, chi = nf, jnp.zeros_like(nf)
    p = mu + _normal_upper_quantile(jnp.clip(target / nf, 1e-6, 1.0 - 1e-6)) * sd
    for it in range(SEARCH_INTERP_STEPS):
        c = count_gt(p)
        above = c >= kf
        lo = jnp.where(above, p, lo)
        clo = jnp.where(above, c, clo)
        hi = jnp.where(above, hi, p)
        chi = jnp.where(above, chi, c)
        frac = jnp.clip((clo - target) / (clo - chi), SEARCH_EDGE, 1.0 - SEARCH_EDGE)
        p = lo + (hi - lo) * frac

    cur = lo
    thr = lo
    done = clo == kf
    for it in range(SEARCH_WALK_STEPS):
        xv = xm_scr[...]
        cur = jnp.min(jnp.where(xv > cur, xv, -NEG_BIG), axis=1, keepdims=True)
        hit = (count_gt(cur) <= kf) & jnp.logical_not(done)
        thr = jnp.where(hit, cur, thr)
        done = done | hit
    thr = jnp.where(trivial, 0.1 * NEG_BIG, thr)
    thr_scr[...] = jnp.broadcast_to(thr, (tq, LANES))

    cnt_gt, cnt_eq = tally(thr)
    sum1_scr[...] = jnp.broadcast_to(cnt_gt, (tq, LANES))
    sum2_scr[...] = jnp.broadcast_to(cnt_eq, (tq, LANES))
    proven = trivial | ((cnt_gt <= kf) & (cnt_gt + cnt_eq >= kf))

    @pl.when(jnp.min(jnp.where(proven, 1.0, 0.0)) < 0.5)
    def _():
        bits = lax.bitcast_convert_type(xm_scr[...], I32)
        key_scr[...] = bits ^ ((bits >> 31) & 0x7FFFFFFF)

        def bit_step(j, tu):
            cand = tu | lax.shift_left(jnp.int32(1), 31 - j)
            cnt = jnp.sum(jnp.where(key_scr[...] >= (cand ^ INT_MIN), 1.0, 0.0), axis=1, keepdims=True)
            return jnp.where(cnt >= kf, cand, tu)

        tkey = lax.fori_loop(0, 32, bit_step, jnp.zeros((tq, 1), I32)) ^ INT_MIN
        tval = lax.bitcast_convert_type(tkey ^ ((tkey >> 31) & 0x7FFFFFFF), F32)
        texact = jnp.where(trivial, 0.1 * NEG_BIG, tval)
        thr_scr[...] = jnp.broadcast_to(texact, (tq, LANES))
        cg, ce = tally(texact)
        sum1_scr[...] = jnp.broadcast_to(cg, (tq, LANES))
        sum2_scr[...] = jnp.broadcast_to(ce, (tq, LANES))

    thr = thr_scr[:, :1]
    cnt_gt = sum1_scr[:, :1]
    cnt_eq = sum2_scr[:, :1]
    xv = xm_scr[...]
    gt = xv > thr
    eq = xv == thr
    want_eq = kf - cnt_gt

    all_or_none = jnp.where(want_eq >= cnt_eq, POS_BIG, 0)
    lim_scr[...] = jnp.broadcast_to(all_or_none, (tq, LANES))
    partial = (want_eq > 0.0) & (want_eq < cnt_eq)
    kpos = lax.broadcasted_iota(I32, (tq, s), 1)

    @pl.when(jnp.max(jnp.where(partial, 1.0, 0.0)) > 0.5)
    def _():
        eqpos = jnp.where(eq, kpos, POS_BIG)
        eqposf = eqpos.astype(F32)
        cur = jnp.full((tq, 1), -1.0, F32)
        xlim = all_or_none
        for it in range(TIE_WALK_STEPS):
            cur = jnp.min(jnp.where(eqposf > cur, eqposf, float(POS_BIG)), axis=1, keepdims=True)
            xlim = jnp.where(partial & (want_eq == float(it)), cur.astype(I32), xlim)
        lim_scr[...] = jnp.broadcast_to(xlim, (tq, LANES))

        @pl.when(jnp.max(jnp.where(partial & (want_eq >= float(TIE_WALK_STEPS)), 1.0, 0.0)) > 0.5)
        def _():
            nbits = max(1, int(s).bit_length())

            def pos_step(j, xl):
                cand = xl | lax.shift_left(jnp.int32(1), nbits - 1 - j)
                cnt = jnp.sum(jnp.where(eqpos < cand, 1.0, 0.0), axis=1, keepdims=True)
                return jnp.where(cnt <= want_eq, cand, xl)

            lim_scr[...] = jnp.broadcast_to(lax.fori_loop(0, nbits, pos_step, jnp.zeros((tq, 1), I32)),
                                            (tq, LANES))

    sel = gt | (eq & (kpos < lim_scr[:, :1]))
    bias_scr[...] = jnp.where(sel, 0.0, NEG_BIG)


def _attn_kernel(*refs, topk, tpos0, n_keys):
    q_ref, iq_ref, sg_ref, kd_ref, vd_ref, ikd_ref = refs[:6]
    o_ref, xm_scr, key_scr, bias_scr, thr_scr, lim_scr, sum1_scr, sum2_scr = refs[-8:]
    nb, tq = q_ref.shape[0], q_ref.shape[1]
    s = kd_ref.shape[1]
    rows = nb * tq
    lane = lax.broadcasted_iota(I32, (tq, LANES), 1)
    halves = (lane < HEAD_DIM, lane >= HEAD_DIM)

    tpos = tpos0 + lax.rem(lax.broadcasted_iota(I32, (rows, 1), 0), tq)
    lim = jnp.minimum((tpos // CHUNK + 1) * CHUNK, n_keys)
    search = min(s, n_keys) > topk

    rb = min(tq, LANES)
    key_tile = 2 * LANES
    lane_rb = lax.broadcasted_iota(I32, (rb, LANES), 1)
    halves_rb = (lane_rb < HEAD_DIM, lane_rb >= HEAD_DIM)
    for bi in range(nb):
        for r in range(tq // rb):
            rs = slice(r * rb, (r + 1) * rb)
            gs = slice(bi * tq + r * rb, bi * tq + (r + 1) * rb)
            sg = sg_ref[bi, rs, :]
            qms, sgs = [], []
            for hd in range(IDX_HEADS):
                qb = iq_ref[bi, rs, (hd // 2) * LANES:(hd // 2 + 1) * LANES]
                qms.append(jnp.where(halves_rb[hd % 2], qb, jnp.zeros_like(qb)))
                sgs.append(sg[:, HEAD_DIM + hd:HEAD_DIM + hd + 1])
            acc1 = jnp.zeros((rb, LANES), F32)
            acc2 = jnp.zeros((rb, LANES), F32)
            for c0 in range(0, s, key_tile):
                w = min(key_tile, s - c0)
                ikd_t = ikd_ref[bi, c0:c0 + w, :]
                acc = jnp.zeros((rb, w), F32)
                for hd in range(IDX_HEADS):
                    acc = acc + sgs[hd] * jnp.maximum(_nt_dot(qms[hd], ikd_t), 0.0)
                adm_t = (c0 + lax.broadcasted_iota(I32, (rb, w), 1)) < lim[gs]
                xm_scr[gs, c0:c0 + w] = jnp.where(adm_t, acc, NEG_BIG)
                if search:
                    sc0 = jnp.where(adm_t, acc, 0.0)
                    for l0 in range(0, w, LANES):
                        part = sc0[:, l0:l0 + LANES]
                        acc1 = acc1 + part
                        acc2 = acc2 + part * part
            if search:
                sum1_scr[gs, :] = acc1
                sum2_scr[gs, :] = acc2

    if search:
        _select_topk(xm_scr, sum1_scr, sum2_scr, lim, topk, key_scr, thr_scr, lim_scr, bias_scr)
    else:
        bias_scr[...] = jnp.where(lax.broadcasted_iota(I32, (rows, s), 1) < lim, 0.0, NEG_BIG)

    heads = [(bi, c, e) for bi in range(nb) for c in range(A_WIDTH // LANES) for e in range(2)]

    def logits_of(head):
        bi, c, e = head
        g = (2 * c) // (A_HEADS // A_KV_HEADS)
        qb = q_ref[bi, :, c * LANES:(c + 1) * LANES]
        qm = jnp.where(halves[e], qb, jnp.zeros_like(qb))
        return _nt_dot(qm, kd_ref[bi, :, g * LANES:(g + 1) * LANES])

    ones = jnp.ones((s, LANES), BF16)
    nxt = logits_of(heads[0])
    vd1 = None
    first = None
    for i, (bi, c, e) in enumerate(heads):
        lg = nxt
        if i + 1 < len(heads):
            nxt = logits_of(heads[i + 1])
        if c % 2 == 0 and e == 0:
            g = (2 * c) // (A_HEADS // A_KV_HEADS)
            vd1 = jnp.concatenate([vd_ref[bi, :, g * LANES:(g + 1) * LANES], ones], axis=1)
        lg = lg + bias_scr[bi * tq:(bi + 1) * tq, :]
        p = jnp.exp2(lg - jnp.max(lg, axis=1, keepdims=True))
        ol = jnp.dot(p.astype(BF16), vd1, preferred_element_type=F32)
        out = ol[:, :LANES] / ol[:, LANES:]
        if e == 0:
            first = out
        else:
            o_ref[bi, :, c * LANES:(c + 1) * LANES] = jnp.where(halves[0], first, out).astype(BF16)


def _attn_call(q, iq, sg, kd, vd, ikd, prev, *, topk, qpos0, n_keys, tq, nb, qblock, s_blk, name):
    b, t, _ = q.shape
    qmap = lambda bi: (bi, qblock, 0)
    kmap = lambda bi: (bi, 0, 0)
    chained = prev is not None
    rows = nb * tq
    return pl.pallas_call(
        functools.partial(_attn_kernel, topk=topk, tpos0=qpos0 + qblock * tq, n_keys=n_keys),
        grid=(b // nb,),
        in_specs=[pl.BlockSpec((nb, tq, A_WIDTH), qmap),
                  pl.BlockSpec((nb, tq, IDX_HEADS * HEAD_DIM), qmap),
                  pl.BlockSpec((nb, tq, LANES), qmap),
                  pl.BlockSpec((nb, s_blk, 2 * KV_WIDTH), kmap),
                  pl.BlockSpec((nb, s_blk, 2 * KV_WIDTH), kmap),
                  pl.BlockSpec((nb, s_blk, LANES), kmap)] + ([pl.BlockSpec(memory_space=pl.ANY)] if chained else []),
        out_specs=pl.BlockSpec((nb, tq, A_WIDTH), qmap),
        out_shape=jax.ShapeDtypeStruct((b, t, A_WIDTH), BF16),
        input_output_aliases={6: 0} if chained else {},
        scratch_shapes=[pltpu.VMEM((rows, s_blk), F32), pltpu.VMEM((rows, s_blk), I32),
                        pltpu.VMEM((rows, s_blk), F32),
                        pltpu.VMEM((rows, LANES), F32), pltpu.VMEM((rows, LANES), I32),
                        pltpu.VMEM((rows, LANES), F32), pltpu.VMEM((rows, LANES), F32)],
        compiler_params=pltpu.CompilerParams(dimension_semantics=("parallel",),
                                             vmem_limit_bytes=VMEM_LIMIT),
        name=name,
    )(*((q, iq, sg, kd, vd, ikd) + ((prev,) if chained else ())))


ATTN_STACK_ELEMS = 1 << 20


def _attention(q, iq, sg, kd, vd, ikd, *, topk, qpos0, n_keys, tag):
    b, t, _ = q.shape
    tq = min(t, 256)
    out = None
    for qblock in range(t // tq):
        last_pos = qpos0 + (qblock + 1) * tq - 1
        visible = min((last_pos // CHUNK + 1) * CHUNK, n_keys)
        s_blk = min(-(-visible // LANES) * LANES, kd.shape[1])
        nb = 1
        while nb < 4 and b % (2 * nb) == 0 and 2 * nb * tq * s_blk <= ATTN_STACK_ELEMS:
            nb *= 2
        out = _attn_call(q, iq, sg, kd, vd, ikd, out, topk=topk, qpos0=qpos0, n_keys=n_keys, tq=tq, nb=nb,
                         qblock=qblock, s_blk=s_blk, name=f"attn_{tag}_q{qblock}")
    return out


def _retention_log_decay():
    return jnp.log(1.0 - jnp.exp(jnp.linspace(math.log(1.0 / 32), math.log(1.0 / 512), REC_HEADS, dtype=F32)))


def _rec_kernel(bq_ref, bk_ref, bv_ref, blf_ref, cq_ref, ck_ref, cv_ref, s0b_ref, s0c_ref,
                hg_ref, rg_ref, tri_ref, dmat_ref, inner_ref, kscale_ref, rdec_ref, gm_ref,
                yb_ref, yc_ref, sb_out, sc_out, sb_scr, sc_scr, *, chunk, n_chunks):
    j = pl.program_id(1)
    width = B_WIDTH

    @pl.when(j == 0)
    def _():
        sb_scr[...] = s0b_ref[0]
        sc_scr[...] = s0c_ref[0]

    rowh = lax.broadcasted_iota(I32, (width, width), 0) // HEAD_DIM
    colh = lax.broadcasted_iota(I32, (width, width), 1) // HEAD_DIM
    block_diag = rowh == colh
    lane_h = lax.broadcasted_iota(I32, (chunk, width), 1) // HEAD_DIM
    causal = (lax.broadcasted_iota(I32, (chunk, chunk), 0)
              >= lax.broadcasted_iota(I32, (chunk, chunk), 1))
    gmat = gm_ref[...]
    tri = tri_ref[...]
    tn = (((0,), (0,)), ((), ()))

    def masked_heads(x):
        return [jnp.where(lane_h == hd, x, 0.0).astype(BF16) for hd in range(REC_HEADS)]

    heads = range(REC_HEADS)
    chunks = range(n_chunks)
    rows = [pl.ds(c * chunk, chunk) for c in chunks]

    bcs = []
    for c in chunks:
        lf = blf_ref[0, rows[c], :]
        a1 = lf.astype(BF16)
        r1 = lf - a1.astype(F32)
        a2 = r1.astype(BF16)
        a3 = (r1 - a2.astype(F32)).astype(BF16)
        bcs.append(jnp.dot(tri, jnp.concatenate([a1, a2, a3], axis=0), preferred_element_type=F32))

    att_b, att_c, upd_b, upd_c, qs_b, qs_c, dec_b, vb_b, vb_c = ([] for _ in range(9))
    for c in chunks:
        bc = bcs[c]
        mid = bc[chunk // 2 - 1:chunk // 2, :]
        last = bc[chunk - 1:chunk, :]
        q = bq_ref[0, rows[c], :]
        k = bk_ref[0, rows[c], :]
        vb = bv_ref[0, rows[c], :].astype(BF16)
        kt = (k * jnp.exp(mid - bc)).astype(BF16)
        att_b.append([_nt_dot(qh, kt) for qh in masked_heads(q * jnp.exp(bc - mid))])
        upd_b.append(lax.dot_general(vb, (k * jnp.exp(last - bc)).astype(BF16), tn, preferred_element_type=F32))
        qs_b.append((q * jnp.exp(bc)).astype(BF16))
        dec_b.append(jnp.exp(last))
        vb_b.append(vb)
        q = cq_ref[0, rows[c], :]
        k = ck_ref[0, rows[c], :]
        vb = cv_ref[0, rows[c], :].astype(BF16)
        kb = k.astype(BF16)
        att_c.append([_nt_dot(qh, kb) for qh in masked_heads(q)])
        upd_c.append(lax.dot_general(vb, (k * kscale_ref[...]).astype(BF16), tn, preferred_element_type=F32))
        qs_c.append((q * inner_ref[...]).astype(BF16))
        vb_c.append(vb)

    st = sb_scr[...]
    rt = sc_scr[...]
    o_b, o_c = [], []
    for c in chunks:
        o_b.append(_nt_dot(qs_b[c], st.astype(BF16)))
        st = st * dec_b[c] + jnp.where(block_diag, upd_b[c], 0.0)
        o_c.append(_nt_dot(qs_c[c], rt.astype(BF16)))
        rt = rt * rdec_ref[...] + jnp.where(block_diag, upd_c[c], 0.0)
    sb_scr[...] = st
    sc_scr[...] = rt

    prod_b = [[jnp.dot(jnp.where(causal, att_b[c][hd], 0.0).astype(BF16), vb_b[c], preferred_element_type=F32)
               for hd in heads] for c in chunks]
    prod_c = [[jnp.dot((att_c[c][hd] * dmat_ref[hd * chunk:(hd + 1) * chunk, :]).astype(BF16), vb_c[c],
                       preferred_element_type=F32) for hd in heads] for c in chunks]
    outs = []
    for c in chunks:
        for o, prod in ((o_b[c], prod_b[c]), (o_c[c], prod_c[c])):
            for hd in heads:
                o = o + jnp.where(lane_h == hd, prod[hd], 0.0)
            outs.append(o)

    means = []
    for o in outs:
        hi, lo = _split2(o * o)
        means.append(jnp.dot(jnp.concatenate([hi, lo], axis=1), gmat, preferred_element_type=F32))
    for c in chunks:
        yb_ref[0, rows[c], :] = (outs[2 * c] * lax.rsqrt(means[2 * c] + EPS) * hg_ref[...]).astype(BF16)
        yc_ref[0, rows[c], :] = (outs[2 * c + 1] * lax.rsqrt(means[2 * c + 1] + EPS) * rg_ref[...]).astype(BF16)

    @pl.when(j == pl.num_programs(1) - 1)
    def _():
        sb_out[0] = sb_scr[...]
        sc_out[0] = sc_scr[...]


def _state_to_blockdiag(s):
    b = s.shape[0]
    eye = jnp.eye(REC_HEADS, dtype=s.dtype)
    return jnp.einsum('bhkv,hg->bhvgk', s, eye).reshape(b, B_WIDTH, B_WIDTH)


def _state_from_blockdiag(sbd):
    b = sbd.shape[0]
    s5 = sbd.reshape(b, REC_HEADS, HEAD_DIM, REC_HEADS, HEAD_DIM)
    return jnp.einsum('bhvhk->bhkv', s5)


def _rec_call(bq, bk, bv, blf, cq, ck, cv, s0b, s0c, hgain, rgain, name):
    b, t, _ = bq.shape
    chunk = min(CHUNK, t)
    tb = min(t, 8 * chunk)
    n_chunks = tb // chunk
    lg = _retention_log_decay()
    n = jnp.arange(chunk, dtype=F32)
    diff = n[:, None] - n[None, :]
    dmat = jnp.where(diff >= 0, jnp.exp(jnp.where(diff >= 0, diff, 0.0)[None] * lg[:, None, None]), 0.0)
    dmat = dmat.reshape(REC_HEADS * chunk, chunk)
    per_lane = lambda a: jnp.repeat(a, HEAD_DIM, axis=-1)
    inner = per_lane(jnp.exp((n[:, None] + 1.0) * lg[None, :]))
    kscale = per_lane(jnp.exp((chunk - 1.0 - n)[:, None] * lg[None, :]))
    rdec = per_lane(jnp.exp(chunk * lg)[None, :])
    tri = jnp.tile(jnp.tril(jnp.ones((chunk, chunk), F32)), (1, 3)).astype(BF16)
    seq = lambda bi, j: (bi, j, 0)
    per_b = lambda bi, j: (bi, 0, 0)
    const = lambda bi, j: (0, 0)
    stream = pl.BlockSpec((1, tb, B_WIDTH), seq)
    state = pl.BlockSpec((1, B_WIDTH, B_WIDTH), per_b)
    return pl.pallas_call(
        functools.partial(_rec_kernel, chunk=chunk, n_chunks=n_chunks),
        grid=(b, t // tb),
        in_specs=[stream] * 7 + [state, state,
                                 pl.BlockSpec((1, B_WIDTH), const), pl.BlockSpec((1, B_WIDTH), const),
                                 pl.BlockSpec((chunk, 3 * chunk), const),
                                 pl.BlockSpec((REC_HEADS * chunk, chunk), const),
                                 pl.BlockSpec((chunk, B_WIDTH), const), pl.BlockSpec((chunk, B_WIDTH), const),
                                 pl.BlockSpec((1, B_WIDTH), const),
                                 pl.BlockSpec((2 * B_WIDTH, B_WIDTH), const)],
        out_specs=[stream, stream, state, state],
        out_shape=[jax.ShapeDtypeStruct((b, t, B_WIDTH), BF16), jax.ShapeDtypeStruct((b, t, C_WIDTH), BF16),
                   jax.ShapeDtypeStruct((b, B_WIDTH, B_WIDTH), F32),
                   jax.ShapeDtypeStruct((b, C_WIDTH, C_WIDTH), F32)],
        scratch_shapes=[pltpu.VMEM((B_WIDTH, B_WIDTH), F32), pltpu.VMEM((C_WIDTH, C_WIDTH), F32)],
        compiler_params=pltpu.CompilerParams(dimension_semantics=("parallel", "arbitrary"),
                                             vmem_limit_bytes=VMEM_LIMIT),
        name=name,
    )(bq, bk, bv, blf, cq, ck, cv, s0b, s0c,
      jnp.tile(hgain, REC_HEADS).reshape(1, B_WIDTH), jnp.tile(rgain, REC_HEADS).reshape(1, C_WIDTH),
      tri, dmat, inner, kscale, rdec, jnp.tile(_group_mean_matrix(B_WIDTH), (2, 1)))


def _out_kernel(x_ref, oa_ref, yb_ref, yc_ref, gate_ref, w_ref, y_ref):
    gate = gate_ref[...].astype(F32)
    ya = (oa_ref[...] * gate[:, :A_WIDTH]).astype(BF16)
    yb = (yb_ref[...] * gate[:, A_WIDTH:A_WIDTH + B_WIDTH]).astype(BF16)
    yc = (yc_ref[...] * gate[:, A_WIDTH + B_WIDTH:]).astype(BF16)
    acc = jnp.dot(ya, w_ref[:A_WIDTH, :], preferred_element_type=F32)
    acc = acc + jnp.dot(yb, w_ref[A_WIDTH:A_WIDTH + B_WIDTH, :], preferred_element_type=F32)
    acc = acc + jnp.dot(yc, w_ref[A_WIDTH + B_WIDTH:, :], preferred_element_type=F32)
    y_ref[...] = x_ref[...] + acc


def _out_call(x2d, oa, yb, yc, gate, w_out16, tm, name):
    n = x2d.shape[0]
    tm = min(tm, n)
    row = lambda r: (r, 0)
    return pl.pallas_call(
        _out_kernel,
        grid=(n // tm,),
        in_specs=[pl.BlockSpec((tm, D_MODEL), row), pl.BlockSpec((tm, A_WIDTH), row),
                  pl.BlockSpec((tm, B_WIDTH), row), pl.BlockSpec((tm, C_WIDTH), row),
                  pl.BlockSpec((tm, D_MODEL), row), pl.BlockSpec((D_MODEL, D_MODEL), lambda r: (0, 0))],
        out_specs=pl.BlockSpec((tm, D_MODEL), row),
        out_shape=jax.ShapeDtypeStruct((n, D_MODEL), F32),
        compiler_params=pltpu.CompilerParams(dimension_semantics=("parallel",),
                                             vmem_limit_bytes=VMEM_LIMIT),
        name=name,
    )(x2d, oa, yb, yc, gate, w_out16)


def _dup_kv(x):
    b, s = x.shape[:2]
    return jnp.repeat(x.reshape(b, s, A_KV_HEADS, 1, HEAD_DIM), 2, axis=3).reshape(b, s, 2 * KV_WIDTH).astype(BF16)


def _mixer_layer(x, pos0, past, s_hgrn, s_ret, layer, w_p, w_out16, norm_g, q_gain, k_gain,
                 lb_logits, hgain, rgain, tag):
    b, t, _ = x.shape
    n = b * t
    x2d = x.reshape(n, D_MODEL)
    pos = pos0 + jnp.arange(t)
    (q, k, v, kd, vd, iq, sg, ik, ikd, gate, bq, bk, bv, blf, cq, ck, cv) = _proj_call(
        x2d, t, pos, norm_g, w_p, q_gain, k_gain, lb_logits, layer, 512)

    three = lambda a: a.reshape(b, t, a.shape[-1])
    kd3, vd3, ikd3 = three(kd), three(vd), three(ikd)
    if past is not None:
        pk, pv, pik = past
        kd3 = jnp.concatenate([_dup_kv(pk), kd3], axis=1)
        vd3 = jnp.concatenate([_dup_kv(pv), vd3], axis=1)
        ikd3 = jnp.concatenate([jnp.concatenate([pik, pik], axis=-1).astype(BF16), ikd3], axis=1)
    n_keys = kd3.shape[1]
    s_pad = -(-n_keys // LANES) * LANES
    if s_pad != n_keys:
        padw = ((0, 0), (0, s_pad - n_keys), (0, 0))
        kd3, vd3, ikd3 = jnp.pad(kd3, padw), jnp.pad(vd3, padw), jnp.pad(ikd3, padw)
    topk = min(TOPK_MAX, n_keys // 4)
    oa = _attention(three(q), three(iq), three(sg), kd3, vd3, ikd3, topk=topk, qpos0=pos0,
                    n_keys=n_keys, tag=tag)

    yb, yc, sb, sc = _rec_call(three(bq), three(bk), three(bv), three(blf), three(cq), three(ck), three(cv),
                               _state_to_blockdiag(s_hgrn), _state_to_blockdiag(s_ret), hgain, rgain,
                               name=f"rec_{tag}")
    out = _out_call(x2d, oa.reshape(n, A_WIDTH), yb.reshape(n, B_WIDTH), yc.reshape(n, C_WIDTH), gate,
                    w_out16, 512, name=f"out_{tag}")
    return (out.reshape(b, t, D_MODEL),
            (k.reshape(b, t, A_KV_HEADS, HEAD_DIM), v.reshape(b, t, A_KV_HEADS, HEAD_DIM), ik,
             _state_from_blockdiag(sb), _state_from_blockdiag(sc)))


def kernel(x_prompt, x_sample, cache_k, cache_v, cache_idx_k, state_hgrn, state_ret, norm_g, w_in,
           q_norm_g, k_norm_g, hgrn_lb_logits, hgrn_norm_g, ret_norm_g, w_out):
    depth = w_in.shape[0]
    bp, tp = x_prompt.shape[:2]
    bs, ts = x_sample.shape[:2]
    past = cache_k.shape[2]
    zero_state = jnp.zeros((bp, REC_HEADS, HEAD_DIM, HEAD_DIM), F32)

    yp, ys = x_prompt, x_sample
    outs_p, outs_s = [], []
    for l in range(depth):
        w_p = _prep_w_in(w_in[l])
        w_o = w_out[l].astype(BF16)
        args = (l, w_p, w_o, norm_g[l], q_norm_g[l], k_norm_g[l], hgrn_lb_logits, hgrn_norm_g[l], ret_norm_g[l])
        yp, st = _mixer_layer(yp, 0, None, zero_state, zero_state, *args, tag=f"p{l}")
        outs_p.append(st)
        ys, st = _mixer_layer(ys, past, (cache_k[l], cache_v[l], cache_idx_k[l]),
                              state_hgrn[l], state_ret[l], *args, tag=f"s{l}")
        outs_s.append(st)

    def stack(outs, i, shape):
        return jnp.stack([o[i] for o in outs]).reshape(shape)

    return (yp, ys,
            stack(outs_p, 0, (depth, bp, tp, A_KV_HEADS, HEAD_DIM)),
            stack(outs_p, 1, (depth, bp, tp, A_KV_HEADS, HEAD_DIM)),
            stack(outs_p, 2, (depth, bp, tp, HEAD_DIM)),
            stack(outs_p, 3, (depth, bp, REC_HEADS, HEAD_DIM, HEAD_DIM)),
            stack(outs_p, 4, (depth, bp, REC_HEADS, HEAD_DIM, HEAD_DIM)),
            stack(outs_s, 0, (depth, bs, ts, A_KV_HEADS, HEAD_DIM)),
            stack(outs_s, 1, (depth, bs, ts, A_KV_HEADS, HEAD_DIM)),
            stack(outs_s, 2, (depth, bs, ts, HEAD_DIM)),
            stack(outs_s, 3, (depth, bs, REC_HEADS, HEAD_DIM, HEAD_DIM)),
            stack(outs_s, 4, (depth, bs, REC_HEADS, HEAD_DIM, HEAD_DIM)))
```

```python
import functools
import math

import numpy as np
import jax
import jax.numpy as jnp
from jax import lax
from jax.experimental import pallas as pl
from jax.experimental.pallas import tpu as pltpu

F32 = jnp.float32
BF16 = jnp.bfloat16
I32 = jnp.int32

D_MODEL = 1024
HEAD_DIM = 64
CHUNK = 64
A_WIDTH = 512
A_HEADS = 8
A_KV_HEADS = 2
KV_WIDTH = A_KV_HEADS * HEAD_DIM
IDX_HEADS = 8
IDX_W_SCALE = (IDX_HEADS * HEAD_DIM) ** -0.5
TOPK_MAX = 256
B_WIDTH = 256
C_WIDTH = 256
REC_HEADS = 4
ROPE_THETA = 10000.0
EPS = 1e-6
LANES = 128
VMEM_LIMIT = 48 * 1024 * 1024

_SIZES = (A_WIDTH, KV_WIDTH, KV_WIDTH, A_WIDTH, IDX_HEADS * HEAD_DIM, HEAD_DIM, IDX_HEADS,
          B_WIDTH, B_WIDTH, B_WIDTH, B_WIDTH, C_WIDTH, C_WIDTH, C_WIDTH, C_WIDTH)
_SPLIT_IDX = tuple(int(s) for s in np.cumsum(_SIZES)[:-1])

_AQ, _AK, _AV, _AG, _IQ, _IKW = 0, 512, 640, 768, 1280, 1792
_BQ, _BF, _BI, _BG, _CQ, _CK, _CV, _CG = 1920, 2176, 2432, 2688, 2944, 3200, 3456, 3712
_N_PROJ = 3968

INT_MIN = -(2 ** 31)
NEG_BIG = -1e30
POS_BIG = 1 << 20
LOG2_E = math.log2(math.e)


def _prep_w_in(w):
    (aq, ak, av, ag, iq, ik, iw, bq, bf, bi, bg, cq, ck, cv, cg) = jnp.split(w, _SPLIT_IDX, axis=1)
    pad = jnp.zeros((w.shape[0], LANES - HEAD_DIM - IDX_HEADS), w.dtype)
    return jnp.concatenate([aq, ak, av, ag, iq, ik, iw, pad, bq, bf, bi, bg, cq, ck, cv, cg],
                           axis=1).astype(BF16)


def _rope_tables(pos):
    half = HEAD_DIM // 2
    freqs = ROPE_THETA ** (-jnp.arange(half, dtype=F32) / half)
    ang = pos.astype(F32)[:, None] * freqs[None, :]
    cos, sin = jnp.cos(ang), jnp.sin(ang)
    cos64 = jnp.concatenate([cos, cos], axis=1)
    sin64 = jnp.concatenate([-sin, sin], axis=1)
    return jnp.concatenate([cos64, cos64], axis=1), jnp.concatenate([sin64, sin64], axis=1)


def _group_mean_matrix(width):
    idx = np.arange(width) // HEAD_DIM
    return jnp.asarray((idx[:, None] == idx[None, :]).astype(np.float32) / HEAD_DIM, BF16)


def _split2(x):
    hi = x.astype(BF16)
    lo = (x - hi.astype(F32)).astype(BF16)
    return hi, lo


def _group_mean(x2, gmat):
    hi, lo = _split2(x2)
    return (jnp.dot(hi, gmat, preferred_element_type=F32)
            + jnp.dot(lo, gmat, preferred_element_type=F32))


def _nt_dot(a, b):
    return lax.dot_general(a, b, (((1,), (1,)), ((), ())), preferred_element_type=F32)


def _proj_kernel(x_ref, g_ref, w_ref, cos_ref, sin_ref, qg_ref, kg_ref, lb_ref, gm_ref, ex_ref,
                 q_out, k_out, v_out, kd_out, vd_out, iq_out, sg_out, ik_out, ikd_out, gate_out,
                 bq_out, bk_out, bv_out, blf_out, cq_out, ck_out, cv_out, *, layer):
    x = x_ref[...]
    tm = x.shape[0]
    ms = jnp.mean(x * x, axis=-1, keepdims=True)
    h = (x * lax.rsqrt(ms + EPS) * g_ref[...]).astype(BF16)
    cos = cos_ref[...]
    sin = sin_ref[...]
    lane = lax.broadcasted_iota(I32, (tm, LANES), 1)
    first_half = (lane % HEAD_DIM) < (HEAD_DIM // 2)
    low_head = lane < HEAD_DIM
    gmat = gm_ref[...]

    def proj(c0, width):
        return jnp.dot(h, w_ref[:, c0:c0 + width], preferred_element_type=F32)

    def rope(xb):
        swapped = jnp.where(first_half, pltpu.roll(xb, LANES - HEAD_DIM // 2, 1),
                            pltpu.roll(xb, HEAD_DIM // 2, 1))
        return xb * cos + swapped * sin

    def head_norm(xb, gain):
        return xb * lax.rsqrt(_group_mean(xb * xb, gmat) + EPS) * gain

    def dup_heads(xb):
        other = pltpu.roll(xb, HEAD_DIM, 1)
        return jnp.where(low_head, xb, other), jnp.where(low_head, other, xb)

    ag = proj(_AG, A_WIDTH)
    bg = proj(_BG, B_WIDTH)
    cg = proj(_CG, C_WIDTH)
    bq = proj(_BQ, B_WIDTH)
    bf = proj(_BF, B_WIDTH)
    bi = proj(_BI, B_WIDTH)
    cq = proj(_CQ, C_WIDTH)
    ck = proj(_CK, C_WIDTH)
    cv = proj(_CV, C_WIDTH)
    akv = proj(_AK, 2 * KV_WIDTH)
    ikw = proj(_IKW, LANES)
    aq = proj(_AQ, A_WIDTH)
    iq = proj(_IQ, IDX_HEADS * HEAD_DIM)

    def silu(z):
        return z / (1.0 + jnp.exp(-z))

    gate_out[:, :A_WIDTH] = silu(ag).astype(BF16)
    gate_out[:, A_WIDTH:A_WIDTH + B_WIDTH] = silu(bg).astype(BF16)
    gate_out[:, A_WIDTH + B_WIDTH:] = silu(cg).astype(BF16)

    logits = lb_ref[...]
    e = jnp.exp(logits - jnp.max(logits, axis=0, keepdims=True))
    sm = e / jnp.sum(e, axis=0, keepdims=True)
    lb = jnp.zeros((1, B_WIDTH), F32)
    for j in range(1, layer + 1):
        lb = lb + sm[j:j + 1, :]
    f = lb + (1.0 - lb) / (1.0 + jnp.exp(-bf))
    bq_out[...] = bq
    bk_out[...] = (1.0 - lb) / (1.0 + jnp.exp(bf))
    bv_out[...] = bi
    blf_out[...] = jnp.log(f)

    for c in range(C_WIDTH // LANES):
        sl = slice(c * LANES, (c + 1) * LANES)
        cq_out[:, sl] = rope(cq[:, sl])
        ck_out[:, sl] = rope(ck[:, sl]) * (HEAD_DIM ** -0.5)
    cv_out[...] = cv

    v = akv[:, KV_WIDTH:]
    v_out[...] = v
    v0, v1 = dup_heads(v)
    vd_out[:, :LANES] = v0.astype(BF16)
    vd_out[:, LANES:] = v1.astype(BF16)
    ikr = rope(ikw)
    ik_out[...] = ikr[:, :HEAD_DIM]
    ikd_out[...] = dup_heads(ikr)[0].astype(BF16)
    sg_out[...] = jnp.where(ikw >= 0, 1.0, -1.0).astype(F32)

    for c in range(A_WIDTH // LANES):
        blk = rope(head_norm(aq[:, c * LANES:(c + 1) * LANES], qg_ref[...]))
        q_out[:, c * LANES:(c + 1) * LANES] = (blk * (HEAD_DIM ** -0.5 * LOG2_E)).astype(BF16)
    k = rope(head_norm(akv[:, :KV_WIDTH], kg_ref[...]))
    k_out[...] = k
    k0, k1 = dup_heads(k)
    kd_out[:, :LANES] = k0.astype(BF16)
    kd_out[:, LANES:] = k1.astype(BF16)

    whi, wlo = _split2(jnp.abs(ikw) * IDX_W_SCALE)
    wexp = (jnp.dot(whi, ex_ref[...], preferred_element_type=F32)
            + jnp.dot(wlo, ex_ref[...], preferred_element_type=F32))
    for c in range(IDX_HEADS * HEAD_DIM // LANES):
        sl = slice(c * LANES, (c + 1) * LANES)
        iq_out[:, sl] = (rope(iq[:, sl]) * wexp[:, sl]).astype(BF16)


def _proj_call(x2d, seq, pos, norm_g, w_p, q_gain, k_gain, lb_logits, layer, tm):
    n = x2d.shape[0]
    tm = min(tm, n)
    cos, sin = _rope_tables(pos)
    if seq % tm == 0:
        per_seq = seq // tm
        tab_map = lambda r: (r % per_seq, 0)
    else:
        cos = jnp.tile(cos, (n // seq, 1))
        sin = jnp.tile(sin, (n // seq, 1))
        tab_map = lambda r: (r, 0)
    expand = np.zeros((LANES, IDX_HEADS * HEAD_DIM), np.float32)
    for hd in range(IDX_HEADS):
        expand[HEAD_DIM + hd, hd * HEAD_DIM:(hd + 1) * HEAD_DIM] = 1.0
    const = lambda r: (0, 0)
    row = lambda r: (r, 0)
    widths = [(A_WIDTH, BF16), (KV_WIDTH, F32), (KV_WIDTH, F32), (2 * KV_WIDTH, BF16), (2 * KV_WIDTH, BF16),
              (IDX_HEADS * HEAD_DIM, BF16), (LANES, F32), (HEAD_DIM, F32), (LANES, BF16), (D_MODEL, BF16),
              (B_WIDTH, F32), (B_WIDTH, F32), (B_WIDTH, F32), (B_WIDTH, F32),
              (C_WIDTH, F32), (C_WIDTH, F32), (C_WIDTH, F32)]
    return pl.pallas_call(
        functools.partial(_proj_kernel, layer=layer),
        grid=(n // tm,),
        in_specs=[pl.BlockSpec((tm, D_MODEL), row),
                  pl.BlockSpec((1, D_MODEL), const),
                  pl.BlockSpec((D_MODEL, _N_PROJ), const),
                  pl.BlockSpec((tm, LANES), tab_map),
                  pl.BlockSpec((tm, LANES), tab_map),
                  pl.BlockSpec((1, LANES), const),
                  pl.BlockSpec((1, LANES), const),
                  pl.BlockSpec(lb_logits.shape, const),
                  pl.BlockSpec((LANES, LANES), const),
                  pl.BlockSpec((LANES, IDX_HEADS * HEAD_DIM), const)],
        out_specs=[pl.BlockSpec((tm, w), row) for w, _ in widths],
        out_shape=[jax.ShapeDtypeStruct((n, w), dt) for w, dt in widths],
        compiler_params=pltpu.CompilerParams(dimension_semantics=("parallel",),
                                             vmem_limit_bytes=VMEM_LIMIT),
        name=f"proj_l{layer}",
    )(x2d, norm_g.reshape(1, D_MODEL), w_p, cos, sin,
      jnp.tile(q_gain, 2).reshape(1, LANES), jnp.tile(k_gain, 2).reshape(1, LANES),
      lb_logits, _group_mean_matrix(LANES), jnp.asarray(expand, BF16))


SEARCH_INTERP_STEPS = 8
SEARCH_WALK_STEPS = 4
SEARCH_RANK_BIAS = 1.5
SEARCH_EDGE = 0.05
TIE_WALK_STEPS = (4, 12)


def _normal_upper_quantile(r):
    rr = jnp.minimum(r, 1.0 - r)
    t = jnp.sqrt(-2.0 * jnp.log(rr))
    z = t - ((0.010328 * t + 0.802853) * t + 2.515517) / (((0.001308 * t + 0.189269) * t + 1.432788) * t + 1.0)
    return jnp.where(r <= 0.5, z, -z)


def _select_topk(xm_scr, sum1_scr, sum2_scr, lim, topk, key_scr, thr_scr, lim_scr, bias_scr):
    tq, s = xm_scr.shape
    kf = float(topk)
    nf = lim.astype(F32)
    trivial = lim <= topk

    def count_gt(p):
        return jnp.sum(jnp.where(xm_scr[...] > p, 1.0, 0.0), axis=1, keepdims=True)

    def tally(t):
        xv = xm_scr[...]
        return (jnp.sum(jnp.where(xv > t, 1.0, 0.0), axis=1, keepdims=True),
                jnp.sum(jnp.where(xv == t, 1.0, 0.0), axis=1, keepdims=True))

    mu = jnp.sum(sum1_scr[...], axis=1, keepdims=True) / nf
    var = jnp.sum(sum2_scr[...], axis=1, keepdims=True) / nf - mu * mu
    sd = jnp.sqrt(jnp.maximum(var, 1e-30))
    target = kf + SEARCH_RANK_BIAS
    lo, hi = mu - 8.0 * sd, mu + 8.0 * sd
    clo, chi = nf, jnp.zeros_like(nf)
    p = mu + _normal_upper_quantile(jnp.clip(target / nf, 1e-6, 1.0 - 1e-6)) * sd
    for it in range(SEARCH_INTERP_STEPS):
        c = count_gt(p)
        above = c >= kf
        lo = jnp.where(above, p, lo)
        clo = jnp.where(above, c, clo)
        hi = jnp.where(above, hi, p)
        chi = jnp.where(above, chi, c)
        frac = jnp.clip((clo - target) / (clo - chi), SEARCH_EDGE, 1.0 - SEARCH_EDGE)
        p = lo + (hi - lo) * frac

    cur = lo
    thr = lo
    done = clo == kf
    for it in range(SEARCH_WALK_STEPS):
        xv = xm_scr[...]
        cur = jnp.min(jnp.where(xv > cur, xv, -NEG_BIG), axis=1, keepdims=True)
        hit = (count_gt(cur) <= kf) & jnp.logical_not(done)
        thr = jnp.where(hit, cur, thr)
        done = done | hit
    thr = jnp.where(trivial, 0.1 * NEG_BIG, thr)
    thr_scr[...] = jnp.broadcast_to(thr, (tq, LANES))

    cnt_gt, cnt_eq = tally(thr)
    sum1_scr[...] = jnp.broadcast_to(cnt_gt, (tq, LANES))
    sum2_scr[...] = jnp.broadcast_to(cnt_eq, (tq, LANES))
    proven = trivial | ((cnt_gt <= kf) & (cnt_gt + cnt_eq >= kf))

    @pl.when(jnp.min(jnp.where(proven, 1.0, 0.0)) < 0.5)
    def _():
        bits = lax.bitcast_convert_type(xm_scr[...], I32)
        key_scr[...] = bits ^ ((bits >> 31) & 0x7FFFFFFF)

        def bit_step(j, tu):
            cand = tu | lax.shift_left(jnp.int32(1), 31 - j)
            cnt = jnp.sum(jnp.where(key_scr[...] >= (cand ^ INT_MIN), 1.0, 0.0), axis=1, keepdims=True)
            return jnp.where(cnt >= kf, cand, tu)

        tkey = lax.fori_loop(0, 32, bit_step, jnp.zeros((tq, 1), I32)) ^ INT_MIN
        tval = lax.bitcast_convert_type(tkey ^ ((tkey >> 31) & 0x7FFFFFFF), F32)
        texact = jnp.where(trivial, 0.1 * NEG_BIG, tval)
        thr_scr[...] = jnp.broadcast_to(texact, (tq, LANES))
        cg, ce = tally(texact)
        sum1_scr[...] = jnp.broadcast_to(cg, (tq, LANES))
        sum2_scr[...] = jnp.broadcast_to(ce, (tq, LANES))

    thr = thr_scr[:, :1]
    cnt_gt = sum1_scr[:, :1]
    cnt_eq = sum2_scr[:, :1]
    xv = xm_scr[...]
    gt = xv > thr
    eq = xv == thr
    want_eq = kf - cnt_gt

    all_or_none = jnp.where(want_eq >= cnt_eq, POS_BIG, 0)
    lim_scr[...] = jnp.broadcast_to(all_or_none, (tq, LANES))
    partial = (want_eq > 0.0) & (want_eq < cnt_eq)
    kpos = lax.broadcasted_iota(I32, (tq, s), 1)

    @pl.when(jnp.max(jnp.where(partial, 1.0, 0.0)) > 0.5)
    def _():
        eqpos = jnp.where(eq, kpos, POS_BIG)
        eqposf = eqpos.astype(F32)

        def walk(first, last, cur, xlim):
            for it in range(first, last):
                cur = jnp.min(jnp.where(eqposf > cur, eqposf, float(POS_BIG)), axis=1, keepdims=True)
                xlim = jnp.where(partial & (want_eq == float(it)), cur.astype(I32), xlim)
            return cur, xlim

        cur, xlim = walk(0, TIE_WALK_STEPS[0], jnp.full((tq, 1), -1.0, F32), all_or_none)
        lim_scr[...] = jnp.broadcast_to(xlim, (tq, LANES))

        @pl.when(jnp.max(jnp.where(partial & (want_eq >= float(TIE_WALK_STEPS[0])), 1.0, 0.0)) > 0.5)
        def _():
            _, xlim2 = walk(TIE_WALK_STEPS[0], TIE_WALK_STEPS[1], cur, xlim)
            lim_scr[...] = jnp.broadcast_to(xlim2, (tq, LANES))

            @pl.when(jnp.max(jnp.where(partial & (want_eq >= float(TIE_WALK_STEPS[1])), 1.0, 0.0)) > 0.5)
            def _():
                nbits = max(1, int(s).bit_length())

                def pos_step(j, xl):
                    cand = xl | lax.shift_left(jnp.int32(1), nbits - 1 - j)
                    cnt = jnp.sum(jnp.where(eqpos < cand, 1.0, 0.0), axis=1, keepdims=True)
                    return jnp.where(cnt <= want_eq, cand, xl)

                lim_scr[...] = jnp.broadcast_to(lax.fori_loop(0, nbits, pos_step, jnp.zeros((tq, 1), I32)),
                                                (tq, LANES))

    sel = gt | (eq & (kpos < lim_scr[:, :1]))
    bias_scr[...] = jnp.where(sel, 0.0, NEG_BIG)


def _attn_kernel(*refs, topk, tpos0, n_keys):
    q_ref, iq_ref, sg_ref, kd_ref, vd_ref, ikd_ref = refs[:6]
    o_ref, xm_scr, key_scr, bias_scr, thr_scr, lim_scr, sum1_scr, sum2_scr = refs[-8:]
    tq = q_ref.shape[1]
    s = kd_ref.shape[1]
    lane = lax.broadcasted_iota(I32, (tq, LANES), 1)
    halves = (lane < HEAD_DIM, lane >= HEAD_DIM)

    tpos = tpos0 + lax.broadcasted_iota(I32, (tq, 1), 0)
    lim = jnp.minimum((tpos // CHUNK + 1) * CHUNK, n_keys)
    search = min(s, n_keys) > topk

    rb = min(tq, LANES)
    key_tile = 2 * LANES
    lane_rb = lax.broadcasted_iota(I32, (rb, LANES), 1)
    halves_rb = (lane_rb < HEAD_DIM, lane_rb >= HEAD_DIM)
    for r in range(tq // rb):
        rs = slice(r * rb, (r + 1) * rb)
        sg = sg_ref[0, rs, :]
        qms, sgs = [], []
        for hd in range(IDX_HEADS):
            qb = iq_ref[0, rs, (hd // 2) * LANES:(hd // 2 + 1) * LANES]
            qms.append(jnp.where(halves_rb[hd % 2], qb, jnp.zeros_like(qb)))
            sgs.append(sg[:, HEAD_DIM + hd:HEAD_DIM + hd + 1])
        acc1 = jnp.zeros((rb, LANES), F32)
        acc2 = jnp.zeros((rb, LANES), F32)
        for c0 in range(0, s, key_tile):
            w = min(key_tile, s - c0)
            ikd_t = ikd_ref[0, c0:c0 + w, :]
            acc = jnp.zeros((rb, w), F32)
            for hd in range(IDX_HEADS):
                acc = acc + sgs[hd] * jnp.maximum(_nt_dot(qms[hd], ikd_t), 0.0)
            adm_t = (c0 + lax.broadcasted_iota(I32, (rb, w), 1)) < lim[rs]
            xm_scr[rs, c0:c0 + w] = jnp.where(adm_t, acc, NEG_BIG)
            if search:
                sc0 = jnp.where(adm_t, acc, 0.0)
                for l0 in range(0, w, LANES):
                    part = sc0[:, l0:l0 + LANES]
                    acc1 = acc1 + part
                    acc2 = acc2 + part * part
        if search:
            sum1_scr[rs, :] = acc1
            sum2_scr[rs, :] = acc2

    logits = []
    for c in range(A_WIDTH // LANES):
        g = (2 * c) // (A_HEADS // A_KV_HEADS)
        qb = q_ref[0, :, c * LANES:(c + 1) * LANES]
        for e in range(2):
            qm = jnp.where(halves[e], qb, jnp.zeros_like(qb))
            logits.append(_nt_dot(qm, kd_ref[0, :, g * LANES:(g + 1) * LANES]))

    if search:
        _select_topk(xm_scr, sum1_scr, sum2_scr, lim, topk, key_scr, thr_scr, lim_scr, bias_scr)
    else:
        bias_scr[...] = jnp.where(lax.broadcasted_iota(I32, (tq, s), 1) < lim, 0.0, NEG_BIG)

    ones = jnp.ones((s, LANES), BF16)
    for g in range(A_KV_HEADS):
        vd1 = jnp.concatenate([vd_ref[0, :, g * LANES:(g + 1) * LANES], ones], axis=1)
        for c in range(g * 2, g * 2 + 2):
            outs = []
            for e in range(2):
                lg = logits[2 * c + e] + bias_scr[...]
                p = jnp.exp2(lg - jnp.max(lg, axis=1, keepdims=True))
                ol = jnp.dot(p.astype(BF16), vd1, preferred_element_type=F32)
                outs.append(ol[:, :LANES] / ol[:, LANES:])
            o_ref[0, :, c * LANES:(c + 1) * LANES] = jnp.where(halves[0], outs[0], outs[1]).astype(BF16)


def _attn_call(q, iq, sg, kd, vd, ikd, prev, *, topk, qpos0, n_keys, tq, qblock, s_blk, name):
    b, t, _ = q.shape
    qmap = lambda bi: (bi, qblock, 0)
    kmap = lambda bi: (bi, 0, 0)
    return pl.pallas_call(
        functools.partial(_attn_kernel, topk=topk, tpos0=qpos0 + qblock * tq, n_keys=n_keys),
        grid=(b,),
        in_specs=[pl.BlockSpec((1, tq, A_WIDTH), qmap),
                  pl.BlockSpec((1, tq, IDX_HEADS * HEAD_DIM), qmap),
                  pl.BlockSpec((1, tq, LANES), qmap),
                  pl.BlockSpec((1, s_blk, 2 * KV_WIDTH), kmap),
                  pl.BlockSpec((1, s_blk, 2 * KV_WIDTH), kmap),
                  pl.BlockSpec((1, s_blk, LANES), kmap),
                  pl.BlockSpec(memory_space=pl.ANY)],
        out_specs=pl.BlockSpec((1, tq, A_WIDTH), qmap),
        out_shape=jax.ShapeDtypeStruct((b, t, A_WIDTH), BF16),
        input_output_aliases={6: 0},
        scratch_shapes=[pltpu.VMEM((tq, s_blk), F32), pltpu.VMEM((tq, s_blk), I32), pltpu.VMEM((tq, s_blk), F32),
                        pltpu.VMEM((tq, LANES), F32), pltpu.VMEM((tq, LANES), I32),
                        pltpu.VMEM((tq, LANES), F32), pltpu.VMEM((tq, LANES), F32)],
        compiler_params=pltpu.CompilerParams(dimension_semantics=("parallel",),
                                             vmem_limit_bytes=VMEM_LIMIT),
        name=name,
    )(q, iq, sg, kd, vd, ikd, prev)


def _attention(q, iq, sg, kd, vd, ikd, *, topk, qpos0, n_keys, tag):
    b, t, _ = q.shape
    tq = min(t, 256)
    out = jnp.zeros((b, t, A_WIDTH), BF16)
    for qblock in range(t // tq):
        last_pos = qpos0 + (qblock + 1) * tq - 1
        visible = min((last_pos // CHUNK + 1) * CHUNK, n_keys)
        s_blk = min(-(-visible // LANES) * LANES, kd.shape[1])
        out = _attn_call(q, iq, sg, kd, vd, ikd, out, topk=topk, qpos0=qpos0, n_keys=n_keys, tq=tq,
                         qblock=qblock, s_blk=s_blk, name=f"attn_{tag}_q{qblock}")
    return out


def _retention_log_decay():
    return jnp.log(1.0 - jnp.exp(jnp.linspace(math.log(1.0 / 32), math.log(1.0 / 512), REC_HEADS, dtype=F32)))


def _rec_kernel(bq_ref, bk_ref, bv_ref, blf_ref, cq_ref, ck_ref, cv_ref, s0b_ref, s0c_ref,
                hg_ref, rg_ref, tri_ref, dmat_ref, inner_ref, kscale_ref, rdec_ref, gm_ref,
                yb_ref, yc_ref, sb_out, sc_out, sb_scr, sc_scr, *, chunk, n_chunks):
    j = pl.program_id(1)
    width = B_WIDTH

    @pl.when(j == 0)
    def _():
        sb_scr[...] = s0b_ref[0]
        sc_scr[...] = s0c_ref[0]

    rowh = lax.broadcasted_iota(I32, (width, width), 0) // HEAD_DIM
    colh = lax.broadcasted_iota(I32, (width, width), 1) // HEAD_DIM
    block_diag = rowh == colh
    lane_h = lax.broadcasted_iota(I32, (chunk, width), 1) // HEAD_DIM
    causal = (lax.broadcasted_iota(I32, (chunk, chunk), 0)
              >= lax.broadcasted_iota(I32, (chunk, chunk), 1))
    gmat = gm_ref[...]
    tri = tri_ref[...]
    tn = (((0,), (0,)), ((), ()))

    def masked_heads(x):
        return [jnp.where(lane_h == hd, x, 0.0).astype(BF16) for hd in range(REC_HEADS)]

    heads = range(REC_HEADS)
    chunks = range(n_chunks)
    rows = [pl.ds(c * chunk, chunk) for c in chunks]

    bcs = []
    for c in chunks:
        lf = blf_ref[0, rows[c], :]
        a1 = lf.astype(BF16)
        r1 = lf - a1.astype(F32)
        a2 = r1.astype(BF16)
        a3 = (r1 - a2.astype(F32)).astype(BF16)
        bcs.append(jnp.dot(tri, jnp.concatenate([a1, a2, a3], axis=0), preferred_element_type=F32))

    att_b, att_c, upd_b, upd_c, qs_b, qs_c, dec_b, vb_b, vb_c = ([] for _ in range(9))
    for c in chunks:
        bc = bcs[c]
        mid = bc[chunk // 2 - 1:chunk // 2, :]
        last = bc[chunk - 1:chunk, :]
        q = bq_ref[0, rows[c], :]
        k = bk_ref[0, rows[c], :]
        vb = bv_ref[0, rows[c], :].astype(BF16)
        kt = (k * jnp.exp(mid - bc)).astype(BF16)
        att_b.append([_nt_dot(qh, kt) for qh in masked_heads(q * jnp.exp(bc - mid))])
        upd_b.append(lax.dot_general(vb, (k * jnp.exp(last - bc)).astype(BF16), tn, preferred_element_type=F32))
        qs_b.append((q * jnp.exp(bc)).astype(BF16))
        dec_b.append(jnp.exp(last))
        vb_b.append(vb)
        q = cq_ref[0, rows[c], :]
        k = ck_ref[0, rows[c], :]
        vb = cv_ref[0, rows[c], :].astype(BF16)
        kb = k.astype(BF16)
        att_c.append([_nt_dot(qh, kb) for qh in masked_heads(q)])
        upd_c.append(lax.dot_general(vb, (k * kscale_ref[...]).astype(BF16), tn, preferred_element_type=F32))
        qs_c.append((q * inner_ref[...]).astype(BF16))
        vb_c.append(vb)

    st = sb_scr[...]
    rt = sc_scr[...]
    o_b, o_c = [], []
    for c in chunks:
        o_b.append(_nt_dot(qs_b[c], st.astype(BF16)))
        st = st * dec_b[c] + jnp.where(block_diag, upd_b[c], 0.0)
        o_c.append(_nt_dot(qs_c[c], rt.astype(BF16)))
        rt = rt * rdec_ref[...] + jnp.where(block_diag, upd_c[c], 0.0)
    sb_scr[...] = st
    sc_scr[...] = rt

    prod_b = [[jnp.dot(jnp.where(causal, att_b[c][hd], 0.0).astype(BF16), vb_b[c], preferred_element_type=F32)
               for hd in heads] for c in chunks]
    prod_c = [[jnp.dot((att_c[c][hd] * dmat_ref[hd * chunk:(hd + 1) * chunk, :]).astype(BF16), vb_c[c],
                       preferred_element_type=F32) for hd in heads] for c in chunks]
    outs = []
    for c in chunks:
        for o, prod in ((o_b[c], prod_b[c]), (o_c[c], prod_c[c])):
            for hd in heads:
                o = o + jnp.where(lane_h == hd, prod[hd], 0.0)
            outs.append(o)

    means = []
    for o in outs:
        hi, lo = _split2(o * o)
        means.append(jnp.dot(jnp.concatenate([hi, lo], axis=1), gmat, preferred_element_type=F32))
    for c in chunks:
        yb_ref[0, rows[c], :] = (outs[2 * c] * lax.rsqrt(means[2 * c] + EPS) * hg_ref[...]).astype(BF16)
        yc_ref[0, rows[c], :] = (outs[2 * c + 1] * lax.rsqrt(means[2 * c + 1] + EPS) * rg_ref[...]).astype(BF16)

    @pl.when(j == pl.num_programs(1) - 1)
    def _():
        sb_out[0] = sb_scr[...]
        sc_out[0] = sc_scr[...]


def _state_to_blockdiag(s):
    b = s.shape[0]
    eye = jnp.eye(REC_HEADS, dtype=s.dtype)
    return jnp.einsum('bhkv,hg->bhvgk', s, eye).reshape(b, B_WIDTH, B_WIDTH)


def _state_from_blockdiag(sbd):
    b = sbd.shape[0]
    s5 = sbd.reshape(b, REC_HEADS, HEAD_DIM, REC_HEADS, HEAD_DIM)
    return jnp.einsum('bhvhk->bhkv', s5)


def _rec_call(bq, bk, bv, blf, cq, ck, cv, s0b, s0c, hgain, rgain, name):
    b, t, _ = bq.shape
    chunk = min(CHUNK, t)
    tb = min(t, 8 * chunk)
    n_chunks = tb // chunk
    lg = _retention_log_decay()
    n = jnp.arange(chunk, dtype=F32)
    diff = n[:, None] - n[None, :]
    dmat = jnp.where(diff >= 0, jnp.exp(jnp.where(diff >= 0, diff, 0.0)[None] * lg[:, None, None]), 0.0)
    dmat = dmat.reshape(REC_HEADS * chunk, chunk)
    per_lane = lambda a: jnp.repeat(a, HEAD_DIM, axis=-1)
    inner = per_lane(jnp.exp((n[:, None] + 1.0) * lg[None, :]))
    kscale = per_lane(jnp.exp((chunk - 1.0 - n)[:, None] * lg[None, :]))
    rdec = per_lane(jnp.exp(chunk * lg)[None, :])
    tri = jnp.tile(jnp.tril(jnp.ones((chunk, chunk), F32)), (1, 3)).astype(BF16)
    seq = lambda bi, j: (bi, j, 0)
    per_b = lambda bi, j: (bi, 0, 0)
    const = lambda bi, j: (0, 0)
    stream = pl.BlockSpec((1, tb, B_WIDTH), seq)
    state = pl.BlockSpec((1, B_WIDTH, B_WIDTH), per_b)
    return pl.pallas_call(
        functools.partial(_rec_kernel, chunk=chunk, n_chunks=n_chunks),
        grid=(b, t // tb),
        in_specs=[stream] * 7 + [state, state,
                                 pl.BlockSpec((1, B_WIDTH), const), pl.BlockSpec((1, B_WIDTH), const),
                                 pl.BlockSpec((chunk, 3 * chunk), const),
                                 pl.BlockSpec((REC_HEADS * chunk, chunk), const),
                                 pl.BlockSpec((chunk, B_WIDTH), const), pl.BlockSpec((chunk, B_WIDTH), const),
                                 pl.BlockSpec((1, B_WIDTH), const),
                                 pl.BlockSpec((2 * B_WIDTH, B_WIDTH), const)],
        out_specs=[stream, stream, state, state],
        out_shape=[jax.ShapeDtypeStruct((b, t, B_WIDTH), BF16), jax.ShapeDtypeStruct((b, t, C_WIDTH), BF16),
                   jax.ShapeDtypeStruct((b, B_WIDTH, B_WIDTH), F32),
                   jax.ShapeDtypeStruct((b, C_WIDTH, C_WIDTH), F32)],
        scratch_shapes=[pltpu.VMEM((B_WIDTH, B_WIDTH), F32), pltpu.VMEM((C_WIDTH, C_WIDTH), F32)],
        compiler_params=pltpu.CompilerParams(dimension_semantics=("parallel", "arbitrary"),
                                             vmem_limit_bytes=VMEM_LIMIT),
        name=name,
    )(bq, bk, bv, blf, cq, ck, cv, s0b, s0c,
      jnp.tile(hgain, REC_HEADS).reshape(1, B_WIDTH), jnp.tile(rgain, REC_HEADS).reshape(1, C_WIDTH),
      tri, dmat, inner, kscale, rdec, jnp.tile(_group_mean_matrix(B_WIDTH), (2, 1)))


def _out_kernel(x_ref, oa_ref, yb_ref, yc_ref, gate_ref, w_ref, y_ref):
    gate = gate_ref[...].astype(F32)
    ya = (oa_ref[...] * gate[:, :A_WIDTH]).astype(BF16)
    yb = (yb_ref[...] * gate[:, A_WIDTH:A_WIDTH + B_WIDTH]).astype(BF16)
    yc = (yc_ref[...] * gate[:, A_WIDTH + B_WIDTH:]).astype(BF16)
    acc = jnp.dot(ya, w_ref[:A_WIDTH, :], preferred_element_type=F32)
    acc = acc + jnp.dot(yb, w_ref[A_WIDTH:A_WIDTH + B_WIDTH, :], preferred_element_type=F32)
    acc = acc + jnp.dot(yc, w_ref[A_WIDTH + B_WIDTH:, :], preferred_element_type=F32)
    y_ref[...] = x_ref[...] + acc


def _out_call(x2d, oa, yb, yc, gate, w_out16, tm, name):
    n = x2d.shape[0]
    tm = min(tm, n)
    row = lambda r: (r, 0)
    return pl.pallas_call(
        _out_kernel,
        grid=(n // tm,),
        in_specs=[pl.BlockSpec((tm, D_MODEL), row), pl.BlockSpec((tm, A_WIDTH), row),
                  pl.BlockSpec((tm, B_WIDTH), row), pl.BlockSpec((tm, C_WIDTH), row),
                  pl.BlockSpec((tm, D_MODEL), row), pl.BlockSpec((D_MODEL, D_MODEL), lambda r: (0, 0))],
        out_specs=pl.BlockSpec((tm, D_MODEL), row),
        out_shape=jax.ShapeDtypeStruct((n, D_MODEL), F32),
        compiler_params=pltpu.CompilerParams(dimension_semantics=("parallel",),
                                             vmem_limit_bytes=VMEM_LIMIT),
        name=name,
    )(x2d, oa, yb, yc, gate, w_out16)


def _dup_kv(x):
    b, s = x.shape[:2]
    return jnp.repeat(x.reshape(b, s, A_KV_HEADS, 1, HEAD_DIM), 2, axis=3).reshape(b, s, 2 * KV_WIDTH).astype(BF16)


def _mixer_layer(x, pos0, past, s_hgrn, s_ret, layer, w_p, w_out16, norm_g, q_gain, k_gain,
                 lb_logits, hgain, rgain, tag):
    b, t, _ = x.shape
    n = b * t
    x2d = x.reshape(n, D_MODEL)
    pos = pos0 + jnp.arange(t)
    (q, k, v, kd, vd, iq, sg, ik, ikd, gate, bq, bk, bv, blf, cq, ck, cv) = _proj_call(
        x2d, t, pos, norm_g, w_p, q_gain, k_gain, lb_logits, layer, 512)

    three = lambda a: a.reshape(b, t, a.shape[-1])
    kd3, vd3, ikd3 = three(kd), three(vd), three(ikd)
    if past is not None:
        pk, pv, pik = past
        kd3 = jnp.concatenate([_dup_kv(pk), kd3], axis=1)
        vd3 = jnp.concatenate([_dup_kv(pv), vd3], axis=1)
        ikd3 = jnp.concatenate([jnp.concatenate([pik, pik], axis=-1).astype(BF16), ikd3], axis=1)
    n_keys = kd3.shape[1]
    s_pad = -(-n_keys // LANES) * LANES
    if s_pad != n_keys:
        padw = ((0, 0), (0, s_pad - n_keys), (0, 0))
        kd3, vd3, ikd3 = jnp.pad(kd3, padw), jnp.pad(vd3, padw), jnp.pad(ikd3, padw)
    topk = min(TOPK_MAX, n_keys // 4)
    oa = _attention(three(q), three(iq), three(sg), kd3, vd3, ikd3, topk=topk, qpos0=pos0,
                    n_keys=n_keys, tag=tag)

    yb, yc, sb, sc = _rec_call(three(bq), three(bk), three(bv), three(blf), three(cq), three(ck), three(cv),
                               _state_to_blockdiag(s_hgrn), _state_to_blockdiag(s_ret), hgain, rgain,
                               name=f"rec_{tag}")
    out = _out_call(x2d, oa.reshape(n, A_WIDTH), yb.reshape(n, B_WIDTH), yc.reshape(n, C_WIDTH), gate,
                    w_out16, 512, name=f"out_{tag}")
    return (out.reshape(b, t, D_MODEL),
            (k.reshape(b, t, A_KV_HEADS, HEAD_DIM), v.reshape(b, t, A_KV_HEADS, HEAD_DIM), ik,
             _state_from_blockdiag(sb), _state_from_blockdiag(sc)))


def kernel(x_prompt, x_sample, cache_k, cache_v, cache_idx_k, state_hgrn, state_ret, norm_g, w_in,
           q_norm_g, k_norm_g, hgrn_lb_logits, hgrn_norm_g, ret_norm_g, w_out):
    depth = w_in.shape[0]
    bp, tp = x_prompt.shape[:2]
    bs, ts = x_sample.shape[:2]
    past = cache_k.shape[2]
    zero_state = jnp.zeros((bp, REC_HEADS, HEAD_DIM, HEAD_DIM), F32)

    yp, ys = x_prompt, x_sample
    outs_p, outs_s = [], []
    for l in range(depth):
        w_p = _prep_w_in(w_in[l])
        w_o = w_out[l].astype(BF16)
        args = (l, w_p, w_o, norm_g[l], q_norm_g[l], k_norm_g[l], hgrn_lb_logits, hgrn_norm_g[l], ret_norm_g[l])
        yp, st = _mixer_layer(yp, 0, None, zero_state, zero_state, *args, tag=f"p{l}")
        outs_p.append(st)
        ys, st = _mixer_layer(ys, past, (cache_k[l], cache_v[l], cache_idx_k[l]),
                              state_hgrn[l], state_ret[l], *args, tag=f"s{l}")
        outs_s.append(st)

    def stack(outs, i, shape):
        return jnp.stack([o[i] for o in outs]).reshape(shape)

    return (yp, ys,
            stack(outs_p, 0, (depth, bp, tp, A_KV_HEADS, HEAD_DIM)),
            stack(outs_p, 1, (depth, bp, tp, A_KV_HEADS, HEAD_DIM)),
            stack(outs_p, 2, (depth, bp, tp, HEAD_DIM)),
            stack(outs_p, 3, (depth, bp, REC_HEADS, HEAD_DIM, HEAD_DIM)),
            stack(outs_p, 4, (depth, bp, REC_HEADS, HEAD_DIM, HEAD_DIM)),
            stack(outs_s, 0, (depth, bs, ts, A_KV_HEADS, HEAD_DIM)),
            stack(outs_s, 1, (depth, bs, ts, A_KV_HEADS, HEAD_DIM)),
            stack(outs_s, 2, (depth, bs, ts, HEAD_DIM)),
            stack(outs_s, 3, (depth, bs, REC_HEADS, HEAD_DIM, HEAD_DIM)),
            stack(outs_s, 4, (depth, bs, REC_HEADS, HEAD_DIM, HEAD_DIM)))
```

```python
import functools
import math

import numpy as np
import jax
import jax.numpy as jnp
from jax import lax
from jax.experimental import pallas as pl
from jax.experimental.pallas import tpu as pltpu

F32 = jnp.float32
BF16 = jnp.bfloat16
I32 = jnp.int32

D_MODEL = 1024
HEAD_DIM = 64
CHUNK = 64
A_WIDTH = 512
A_HEADS = 8
A_KV_HEADS = 2
KV_WIDTH = A_KV_HEADS * HEAD_DIM
IDX_HEADS = 8
IDX_W_SCALE = (IDX_HEADS * HEAD_DIM) ** -0.5
TOPK_MAX = 256
B_WIDTH = 256
C_WIDTH = 256
REC_HEADS = 4
ROPE_THETA = 10000.0
EPS = 1e-6
LANES = 128
VMEM_LIMIT = 48 * 1024 * 1024

_SIZES = (A_WIDTH, KV_WIDTH, KV_WIDTH, A_WIDTH, IDX_HEADS * HEAD_DIM, HEAD_DIM, IDX_HEADS,
          B_WIDTH, B_WIDTH, B_WIDTH, B_WIDTH, C_WIDTH, C_WIDTH, C_WIDTH, C_WIDTH)
_SPLIT_IDX = tuple(int(s) for s in np.cumsum(_SIZES)[:-1])

_AQ, _AK, _AV, _AG, _IQ, _IKW = 0, 512, 640, 768, 1280, 1792
_BQ, _BF, _BI, _BG, _CQ, _CK, _CV, _CG = 1920, 2176, 2432, 2688, 2944, 3200, 3456, 3712
_N_PROJ = 3968

INT_MIN = -(2 ** 31)
NEG_BIG = -1e30
POS_BIG = 1 << 20
LOG2_E = math.log2(math.e)


def _prep_w_in(w):
    (aq, ak, av, ag, iq, ik, iw, bq, bf, bi, bg, cq, ck, cv, cg) = jnp.split(w, _SPLIT_IDX, axis=1)
    pad = jnp.zeros((w.shape[0], LANES - HEAD_DIM - IDX_HEADS), w.dtype)
    return jnp.concatenate([aq, ak, av, ag, iq, ik, iw, pad, bq, bf, bi, bg, cq, ck, cv, cg],
                           axis=1).astype(BF16)


def _rope_tables(pos):
    half = HEAD_DIM // 2
    freqs = ROPE_THETA ** (-jnp.arange(half, dtype=F32) / half)
    ang = pos.astype(F32)[:, None] * freqs[None, :]
    cos, sin = jnp.cos(ang), jnp.sin(ang)
    cos64 = jnp.concatenate([cos, cos], axis=1)
    sin64 = jnp.concatenate([-sin, sin], axis=1)
    return jnp.concatenate([cos64, cos64], axis=1), jnp.concatenate([sin64, sin64], axis=1)


def _group_mean_matrix(width):
    idx = np.arange(width) // HEAD_DIM
    return jnp.asarray((idx[:, None] == idx[None, :]).astype(np.float32) / HEAD_DIM, BF16)


def _split2(x):
    hi = x.astype(BF16)
    lo = (x - hi.astype(F32)).astype(BF16)
    return hi, lo


def _nt_dot(a, b):
    return lax.dot_general(a, b, (((1,), (1,)), ((), ())), preferred_element_type=F32)


def _proj_kernel(x_ref, g_ref, w_ref, cos_ref, sin_ref, qg_ref, kg_ref, lb_ref, gm_ref, ex_ref,
                 q_out, k_out, v_out, kd_out, vd_out, iq_out, sg_out, ik_out, ikd_out, gate_out,
                 bq_out, bk_out, bv_out, blf_out, cq_out, ck_out, cv_out, *, layer):
    x = x_ref[...]
    tm = x.shape[0]
    ms = jnp.mean(x * x, axis=-1, keepdims=True)
    h = (x * lax.rsqrt(ms + EPS) * g_ref[...]).astype(BF16)
    cos = cos_ref[...]
    sin = sin_ref[...]
    lane = lax.broadcasted_iota(I32, (tm, LANES), 1)
    first_half = (lane % HEAD_DIM) < (HEAD_DIM // 2)
    low_head = lane < HEAD_DIM
    gmat = gm_ref[...]

    def proj(c0, width):
        return jnp.dot(h, w_ref[:, c0:c0 + width], preferred_element_type=F32)

    def rope(xb):
        swapped = jnp.where(first_half, pltpu.roll(xb, LANES - HEAD_DIM // 2, 1),
                            pltpu.roll(xb, HEAD_DIM // 2, 1))
        return xb * cos + swapped * sin

    def head_norm(xb, gain):
        hi, lo = _split2(xb * xb)
        msq = jnp.dot(jnp.concatenate([hi, lo], axis=1), gmat, preferred_element_type=F32)
        return xb * lax.rsqrt(msq + EPS) * gain

    def dup_heads(xb):
        other = pltpu.roll(xb, HEAD_DIM, 1)
        return jnp.where(low_head, xb, other), jnp.where(low_head, other, xb)

    ag = proj(_AG, A_WIDTH)
    bg = proj(_BG, B_WIDTH)
    cg = proj(_CG, C_WIDTH)
    bq = proj(_BQ, B_WIDTH)
    bf = proj(_BF, B_WIDTH)
    bi = proj(_BI, B_WIDTH)
    cq = proj(_CQ, C_WIDTH)
    ck = proj(_CK, C_WIDTH)
    cv = proj(_CV, C_WIDTH)
    akv = proj(_AK, 2 * KV_WIDTH)
    ikw = proj(_IKW, LANES)
    aq = proj(_AQ, A_WIDTH)
    iq = proj(_IQ, IDX_HEADS * HEAD_DIM)

    def silu(z):
        return z / (1.0 + jnp.exp(-z))

    gate_out[:, :A_WIDTH] = silu(ag).astype(BF16)
    gate_out[:, A_WIDTH:A_WIDTH + B_WIDTH] = silu(bg).astype(BF16)
    gate_out[:, A_WIDTH + B_WIDTH:] = silu(cg).astype(BF16)

    logits = lb_ref[...]
    e = jnp.exp(logits - jnp.max(logits, axis=0, keepdims=True))
    sm = e / jnp.sum(e, axis=0, keepdims=True)
    lb = jnp.zeros((1, B_WIDTH), F32)
    for j in range(1, layer + 1):
        lb = lb + sm[j:j + 1, :]
    f = lb + (1.0 - lb) / (1.0 + jnp.exp(-bf))
    bq_out[...] = bq
    bk_out[...] = (1.0 - lb) / (1.0 + jnp.exp(bf))
    bv_out[...] = bi
    blf_out[...] = jnp.log(f)

    for c in range(C_WIDTH // LANES):
        sl = slice(c * LANES, (c + 1) * LANES)
        cq_out[:, sl] = rope(cq[:, sl])
        ck_out[:, sl] = rope(ck[:, sl]) * (HEAD_DIM ** -0.5)
    cv_out[...] = cv

    v = akv[:, KV_WIDTH:]
    v_out[...] = v
    v0, v1 = dup_heads(v)
    vd_out[:, :LANES] = v0.astype(BF16)
    vd_out[:, LANES:] = v1.astype(BF16)
    ikr = rope(ikw)
    ik_out[...] = ikr[:, :HEAD_DIM]
    ikd_out[...] = dup_heads(ikr)[0].astype(BF16)
    sg_out[...] = jnp.where(ikw >= 0, 1.0, -1.0).astype(F32)

    for c in range(A_WIDTH // LANES):
        blk = rope(head_norm(aq[:, c * LANES:(c + 1) * LANES], qg_ref[...]))
        q_out[:, c * LANES:(c + 1) * LANES] = (blk * (HEAD_DIM ** -0.5 * LOG2_E)).astype(BF16)
    k = rope(head_norm(akv[:, :KV_WIDTH], kg_ref[...]))
    k_out[...] = k
    k0, k1 = dup_heads(k)
    kd_out[:, :LANES] = k0.astype(BF16)
    kd_out[:, LANES:] = k1.astype(BF16)

    whi, wlo = _split2(jnp.abs(ikw) * IDX_W_SCALE)
    wexp = jnp.dot(jnp.concatenate([whi, wlo], axis=1), ex_ref[...], preferred_element_type=F32)
    for c in range(IDX_HEADS * HEAD_DIM // LANES):
        sl = slice(c * LANES, (c + 1) * LANES)
        iq_out[:, sl] = (rope(iq[:, sl]) * wexp[:, sl]).astype(BF16)


def _proj_call(x2d, seq, pos, norm_g, w_p, q_gain, k_gain, lb_logits, layer, tm):
    n = x2d.shape[0]
    tm = min(tm, n)
    cos, sin = _rope_tables(pos)
    if seq % tm == 0:
        per_seq = seq // tm
        tab_map = lambda r: (r % per_seq, 0)
    else:
        cos = jnp.tile(cos, (n // seq, 1))
        sin = jnp.tile(sin, (n // seq, 1))
        tab_map = lambda r: (r, 0)
    expand = np.zeros((LANES, IDX_HEADS * HEAD_DIM), np.float32)
    for hd in range(IDX_HEADS):
        expand[HEAD_DIM + hd, hd * HEAD_DIM:(hd + 1) * HEAD_DIM] = 1.0
    const = lambda r: (0, 0)
    row = lambda r: (r, 0)
    widths = [(A_WIDTH, BF16), (KV_WIDTH, F32), (KV_WIDTH, F32), (2 * KV_WIDTH, BF16), (2 * KV_WIDTH, BF16),
              (IDX_HEADS * HEAD_DIM, BF16), (LANES, F32), (HEAD_DIM, F32), (LANES, BF16), (D_MODEL, BF16),
              (B_WIDTH, F32), (B_WIDTH, F32), (B_WIDTH, F32), (B_WIDTH, F32),
              (C_WIDTH, F32), (C_WIDTH, F32), (C_WIDTH, F32)]
    return pl.pallas_call(
        functools.partial(_proj_kernel, layer=layer),
        grid=(n // tm,),
        in_specs=[pl.BlockSpec((tm, D_MODEL), row),
                  pl.BlockSpec((1, D_MODEL), const),
                  pl.BlockSpec((D_MODEL, _N_PROJ), const),
                  pl.BlockSpec((tm, LANES), tab_map),
                  pl.BlockSpec((tm, LANES), tab_map),
                  pl.BlockSpec((1, LANES), const),
                  pl.BlockSpec((1, LANES), const),
                  pl.BlockSpec(lb_logits.shape, const),
                  pl.BlockSpec((2 * LANES, LANES), const),
                  pl.BlockSpec((2 * LANES, IDX_HEADS * HEAD_DIM), const)],
        out_specs=[pl.BlockSpec((tm, w), row) for w, _ in widths],
        out_shape=[jax.ShapeDtypeStruct((n, w), dt) for w, dt in widths],
        compiler_params=pltpu.CompilerParams(dimension_semantics=("parallel",),
                                             vmem_limit_bytes=VMEM_LIMIT),
        name=f"proj_l{layer}",
    )(x2d, norm_g.reshape(1, D_MODEL), w_p, cos, sin,
      jnp.tile(q_gain, 2).reshape(1, LANES), jnp.tile(k_gain, 2).reshape(1, LANES),
      lb_logits, jnp.tile(_group_mean_matrix(LANES), (2, 1)), jnp.tile(jnp.asarray(expand, BF16), (2, 1)))


SEARCH_INTERP_STEPS = 8
SEARCH_WALK_STEPS = 4
SEARCH_RANK_BIAS = 1.5
SEARCH_EDGE = 0.05
TIE_WALK_STEPS = (4, 12)


def _normal_upper_quantile(r):
    rr = jnp.minimum(r, 1.0 - r)
    t = jnp.sqrt(-2.0 * jnp.log(rr))
    z = t - ((0.010328 * t + 0.802853) * t + 2.515517) / (((0.001308 * t + 0.189269) * t + 1.432788) * t + 1.0)
    return jnp.where(r <= 0.5, z, -z)


def _select_topk(xm_scr, sum1_scr, sum2_scr, lim, topk, key_scr, thr_scr, lim_scr, bias_scr):
    tq, s = xm_scr.shape
    kf = float(topk)
    nf = lim.astype(F32)
    trivial = lim <= topk

    def count_gt(p):
        return jnp.sum(jnp.where(xm_scr[...] > p, 1.0, 0.0), axis=1, keepdims=True)

    def tally(t):
        xv = xm_scr[...]
        return (jnp.sum(jnp.where(xv > t, 1.0, 0.0), axis=1, keepdims=True),
                jnp.sum(jnp.where(xv == t, 1.0, 0.0), axis=1, keepdims=True))

    mu = jnp.sum(sum1_scr[...], axis=1, keepdims=True) / nf
    var = jnp.sum(sum2_scr[...], axis=1, keepdims=True) / nf - mu * mu
    sd = jnp.sqrt(jnp.maximum(var, 1e-30))
    target = kf + SEARCH_RANK_BIAS
    lo, hi = mu - 8.0 * sd, mu + 8.0 * sd
    clo, chi = nf, jnp.zeros_like(nf)
    p = mu + _normal_upper_quantile(jnp.clip(target / nf, 1e-6, 1.0 - 1e-6)) * sd
    for it in range(SEARCH_INTERP_STEPS):
        c = count_gt(p)
        above = c >= kf
        lo = jnp.where(above, p, lo)
        clo = jnp.where(above, c, clo)
        hi = jnp.where(above, hi, p)
        chi = jnp.where(above, chi, c)
        frac = jnp.clip((clo - target) / (clo - chi), SEARCH_EDGE, 1.0 - SEARCH_EDGE)
        p = lo + (hi - lo) * frac

    cur = lo
    thr = lo
    done = clo == kf
    for it in range(SEARCH_WALK_STEPS):
        xv = xm_scr[...]
        cur = jnp.min(jnp.where(xv > cur, xv, -NEG_BIG), axis=1, keepdims=True)
        hit = (count_gt(cur) <= kf) & jnp.logical_not(done)
        thr = jnp.where(hit, cur, thr)
        done = done | hit
    thr = jnp.where(trivial, 0.1 * NEG_BIG, thr)
    thr_scr[...] = jnp.broadcast_to(thr, (tq, LANES))

    cnt_gt, cnt_eq = tally(thr)
    sum1_scr[...] = jnp.broadcast_to(cnt_gt, (tq, LANES))
    sum2_scr[...] = jnp.broadcast_to(cnt_eq, (tq, LANES))
    proven = trivial | ((cnt_gt <= kf) & (cnt_gt + cnt_eq >= kf))

    @pl.when(jnp.min(jnp.where(proven, 1.0, 0.0)) < 0.5)
    def _():
        bits = lax.bitcast_convert_type(xm_scr[...], I32)
        key_scr[...] = bits ^ ((bits >> 31) & 0x7FFFFFFF)

        def bit_step(j, tu):
            cand = tu | lax.shift_left(jnp.int32(1), 31 - j)
            cnt = jnp.sum(jnp.where(key_scr[...] >= (cand ^ INT_MIN), 1.0, 0.0), axis=1, keepdims=True)
            return jnp.where(cnt >= kf, cand, tu)

        tkey = lax.fori_loop(0, 32, bit_step, jnp.zeros((tq, 1), I32)) ^ INT_MIN
        tval = lax.bitcast_convert_type(tkey ^ ((tkey >> 31) & 0x7FFFFFFF), F32)
        texact = jnp.where(trivial, 0.1 * NEG_BIG, tval)
        thr_scr[...] = jnp.broadcast_to(texact, (tq, LANES))
        cg, ce = tally(texact)
        sum1_scr[...] = jnp.broadcast_to(cg, (tq, LANES))
        sum2_scr[...] = jnp.broadcast_to(ce, (tq, LANES))

    thr = thr_scr[:, :1]
    cnt_gt = sum1_scr[:, :1]
    cnt_eq = sum2_scr[:, :1]
    xv = xm_scr[...]
    gt = xv > thr
    eq = xv == thr
    want_eq = kf - cnt_gt

    all_or_none = jnp.where(want_eq >= cnt_eq, POS_BIG, 0)
    lim_scr[...] = jnp.broadcast_to(all_or_none, (tq, LANES))
    partial = (want_eq > 0.0) & (want_eq < cnt_eq)
    kpos = lax.broadcasted_iota(I32, (tq, s), 1)

    @pl.when(jnp.max(jnp.where(partial, 1.0, 0.0)) > 0.5)
    def _():
        eqpos = jnp.where(eq, kpos, POS_BIG)
        eqposf = eqpos.astype(F32)

        def walk(first, last, cur, xlim):
            for it in range(first, last):
                cur = jnp.min(jnp.where(eqposf > cur, eqposf, float(POS_BIG)), axis=1, keepdims=True)
                xlim = jnp.where(partial & (want_eq == float(it)), cur.astype(I32), xlim)
            return cur, xlim

        cur, xlim = walk(0, TIE_WALK_STEPS[0], jnp.full((tq, 1), -1.0, F32), all_or_none)
        lim_scr[...] = jnp.broadcast_to(xlim, (tq, LANES))

        @pl.when(jnp.max(jnp.where(partial & (want_eq >= float(TIE_WALK_STEPS[0])), 1.0, 0.0)) > 0.5)
        def _():
            _, xlim2 = walk(TIE_WALK_STEPS[0], TIE_WALK_STEPS[1], cur, xlim)
            lim_scr[...] = jnp.broadcast_to(xlim2, (tq, LANES))

            @pl.when(jnp.max(jnp.where(partial & (want_eq >= float(TIE_WALK_STEPS[1])), 1.0, 0.0)) > 0.5)
            def _():
                nbits = max(1, int(s).bit_length())

                def pos_step(j, xl):
                    cand = xl | lax.shift_left(jnp.int32(1), nbits - 1 - j)
                    cnt = jnp.sum(jnp.where(eqpos < cand, 1.0, 0.0), axis=1, keepdims=True)
                    return jnp.where(cnt <= want_eq, cand, xl)

                lim_scr[...] = jnp.broadcast_to(lax.fori_loop(0, nbits, pos_step, jnp.zeros((tq, 1), I32)),
                                                (tq, LANES))

    sel = gt | (eq & (kpos < lim_scr[:, :1]))
    bias_scr[...] = jnp.where(sel, 0.0, NEG_BIG)


def _attn_kernel(*refs, topk, tpos0, n_keys):
    q_ref, iq_ref, sg_ref, kd_ref, vd_ref, ikd_ref, gate_ref = refs[:7]
    o_ref, xm_scr, key_scr, bias_scr, thr_scr, lim_scr, sum1_scr, sum2_scr = refs[-8:]
    tq = q_ref.shape[1]
    s = kd_ref.shape[1]
    lane = lax.broadcasted_iota(I32, (tq, LANES), 1)
    halves = (lane < HEAD_DIM, lane >= HEAD_DIM)

    tpos = tpos0 + lax.broadcasted_iota(I32, (tq, 1), 0)
    lim = jnp.minimum((tpos // CHUNK + 1) * CHUNK, n_keys)
    search = min(s, n_keys) > topk

    rb = min(tq, LANES)
    key_tile = 2 * LANES
    lane_rb = lax.broadcasted_iota(I32, (rb, LANES), 1)
    halves_rb = (lane_rb < HEAD_DIM, lane_rb >= HEAD_DIM)
    for r in range(tq // rb):
        rs = slice(r * rb, (r + 1) * rb)
        sg = sg_ref[0, rs, :]
        qms, sgs = [], []
        for hd in range(IDX_HEADS):
            qb = iq_ref[0, rs, (hd // 2) * LANES:(hd // 2 + 1) * LANES]
            qms.append(jnp.where(halves_rb[hd % 2], qb, jnp.zeros_like(qb)))
            sgs.append(sg[:, HEAD_DIM + hd:HEAD_DIM + hd + 1])
        acc1 = jnp.zeros((rb, LANES), F32)
        acc2 = jnp.zeros((rb, LANES), F32)
        for c0 in range(0, s, key_tile):
            w = min(key_tile, s - c0)
            ikd_t = ikd_ref[0, c0:c0 + w, :]
            acc = jnp.zeros((rb, w), F32)
            for hd in range(IDX_HEADS):
                acc = acc + sgs[hd] * jnp.maximum(_nt_dot(qms[hd], ikd_t), 0.0)
            adm_t = (c0 + lax.broadcasted_iota(I32, (rb, w), 1)) < lim[rs]
            xm_scr[rs, c0:c0 + w] = jnp.where(adm_t, acc, NEG_BIG)
            if search:
                sc0 = jnp.where(adm_t, acc, 0.0)
                for l0 in range(0, w, LANES):
                    part = sc0[:, l0:l0 + LANES]
                    acc1 = acc1 + part
                    acc2 = acc2 + part * part
        if search:
            sum1_scr[rs, :] = acc1
            sum2_scr[rs, :] = acc2

    logits = []
    for c in range(A_WIDTH // LANES):
        g = (2 * c) // (A_HEADS // A_KV_HEADS)
        qb = q_ref[0, :, c * LANES:(c + 1) * LANES]
        for e in range(2):
            qm = jnp.where(halves[e], qb, jnp.zeros_like(qb))
            logits.append(_nt_dot(qm, kd_ref[0, :, g * LANES:(g + 1) * LANES]))

    if search:
        _select_topk(xm_scr, sum1_scr, sum2_scr, lim, topk, key_scr, thr_scr, lim_scr, bias_scr)
    else:
        bias_scr[...] = jnp.where(lax.broadcasted_iota(I32, (tq, s), 1) < lim, 0.0, NEG_BIG)

    ones = jnp.ones((s, LANES), BF16)
    for g in range(A_KV_HEADS):
        vd1 = jnp.concatenate([vd_ref[0, :, g * LANES:(g + 1) * LANES], ones], axis=1)
        for c in range(g * 2, g * 2 + 2):
            outs = []
            for e in range(2):
                lg = logits[2 * c + e] + bias_scr[...]
                p = jnp.exp2(lg - jnp.max(lg, axis=1, keepdims=True))
                ol = jnp.dot(p.astype(BF16), vd1, preferred_element_type=F32)
                outs.append(ol[:, :LANES] / ol[:, LANES:])
            gate = gate_ref[0, :, c * LANES:(c + 1) * LANES].astype(F32)
            o_ref[0, :, c * LANES:(c + 1) * LANES] = (jnp.where(halves[0], outs[0], outs[1]) * gate).astype(BF16)


def _attn_call(q, iq, sg, kd, vd, ikd, gate, prev, *, topk, qpos0, n_keys, tq, qblock, s_blk, name):
    b, t, _ = q.shape
    qmap = lambda bi: (bi, qblock, 0)
    kmap = lambda bi: (bi, 0, 0)
    return pl.pallas_call(
        functools.partial(_attn_kernel, topk=topk, tpos0=qpos0 + qblock * tq, n_keys=n_keys),
        grid=(b,),
        in_specs=[pl.BlockSpec((1, tq, A_WIDTH), qmap),
                  pl.BlockSpec((1, tq, IDX_HEADS * HEAD_DIM), qmap),
                  pl.BlockSpec((1, tq, LANES), qmap),
                  pl.BlockSpec((1, s_blk, 2 * KV_WIDTH), kmap),
                  pl.BlockSpec((1, s_blk, 2 * KV_WIDTH), kmap),
                  pl.BlockSpec((1, s_blk, LANES), kmap),
                  pl.BlockSpec((1, tq, A_WIDTH), qmap),
                  pl.BlockSpec(memory_space=pl.ANY)],
        out_specs=pl.BlockSpec((1, tq, A_WIDTH), qmap),
        out_shape=jax.ShapeDtypeStruct((b, t, A_WIDTH), BF16),
        input_output_aliases={7: 0},
        scratch_shapes=[pltpu.VMEM((tq, s_blk), F32), pltpu.VMEM((tq, s_blk), I32), pltpu.VMEM((tq, s_blk), F32),
                        pltpu.VMEM((tq, LANES), F32), pltpu.VMEM((tq, LANES), I32),
                        pltpu.VMEM((tq, LANES), F32), pltpu.VMEM((tq, LANES), F32)],
        compiler_params=pltpu.CompilerParams(dimension_semantics=("parallel",),
                                             vmem_limit_bytes=VMEM_LIMIT),
        name=name,
    )(q, iq, sg, kd, vd, ikd, gate, prev)


def _attention(q, iq, sg, kd, vd, ikd, gate, *, topk, qpos0, n_keys, tag):
    b, t, _ = q.shape
    tq = min(t, 256)
    out = jnp.zeros((b, t, A_WIDTH), BF16)
    for qblock in range(t // tq):
        last_pos = qpos0 + (qblock + 1) * tq - 1
        visible = min((last_pos // CHUNK + 1) * CHUNK, n_keys)
        s_blk = min(-(-visible // LANES) * LANES, kd.shape[1])
        out = _attn_call(q, iq, sg, kd, vd, ikd, gate, out, topk=topk, qpos0=qpos0, n_keys=n_keys, tq=tq,
                         qblock=qblock, s_blk=s_blk, name=f"attn_{tag}_q{qblock}")
    return out


def _retention_log_decay():
    return jnp.log(1.0 - jnp.exp(jnp.linspace(math.log(1.0 / 32), math.log(1.0 / 512), REC_HEADS, dtype=F32)))


def _rec_kernel(bq_ref, bk_ref, bv_ref, blf_ref, cq_ref, ck_ref, cv_ref, gb_ref, gc_ref, s0b_ref, s0c_ref,
                hg_ref, rg_ref, tri_ref, dmat_ref, inner_ref, kscale_ref, rdec_ref, gm_ref,
                yb_ref, yc_ref, sb_out, sc_out, sb_scr, sc_scr, *, chunk, n_chunks):
    j = pl.program_id(1)
    width = B_WIDTH

    @pl.when(j == 0)
    def _():
        sb_scr[...] = s0b_ref[0]
        sc_scr[...] = s0c_ref[0]

    rowh = lax.broadcasted_iota(I32, (width, width), 0) // HEAD_DIM
    colh = lax.broadcasted_iota(I32, (width, width), 1) // HEAD_DIM
    block_diag = rowh == colh
    lane_h = lax.broadcasted_iota(I32, (chunk, width), 1) // HEAD_DIM
    causal = (lax.broadcasted_iota(I32, (chunk, chunk), 0)
              >= lax.broadcasted_iota(I32, (chunk, chunk), 1))
    gmat = gm_ref[...]
    tri = tri_ref[...]
    tn = (((0,), (0,)), ((), ()))

    def masked_heads(x):
        return [jnp.where(lane_h == hd, x, 0.0).astype(BF16) for hd in range(REC_HEADS)]

    heads = range(REC_HEADS)
    chunks = range(n_chunks)
    rows = [pl.ds(c * chunk, chunk) for c in chunks]

    bcs = []
    for c in chunks:
        lf = blf_ref[0, rows[c], :]
        a1 = lf.astype(BF16)
        r1 = lf - a1.astype(F32)
        a2 = r1.astype(BF16)
        a3 = (r1 - a2.astype(F32)).astype(BF16)
        bcs.append(jnp.dot(tri, jnp.concatenate([a1, a2, a3], axis=0), preferred_element_type=F32))

    att_b, att_c, upd_b, upd_c, qs_b, qs_c, dec_b, vb_b, vb_c = ([] for _ in range(9))
    for c in chunks:
        bc = bcs[c]
        mid = bc[chunk // 2 - 1:chunk // 2, :]
        last = bc[chunk - 1:chunk, :]
        q = bq_ref[0, rows[c], :]
        k = bk_ref[0, rows[c], :]
        vb = bv_ref[0, rows[c], :].astype(BF16)
        kt = (k * jnp.exp(mid - bc)).astype(BF16)
        att_b.append([_nt_dot(qh, kt) for qh in masked_heads(q * jnp.exp(bc - mid))])
        upd_b.append(lax.dot_general(vb, (k * jnp.exp(last - bc)).astype(BF16), tn, preferred_element_type=F32))
        qs_b.append((q * jnp.exp(bc)).astype(BF16))
        dec_b.append(jnp.exp(last))
        vb_b.append(vb)
        q = cq_ref[0, rows[c], :]
        k = ck_ref[0, rows[c], :]
        vb = cv_ref[0, rows[c], :].astype(BF16)
        kb = k.astype(BF16)
        att_c.append([_nt_dot(qh, kb) for qh in masked_heads(q)])
        upd_c.append(lax.dot_general(vb, (k * kscale_ref[...]).astype(BF16), tn, preferred_element_type=F32))
        qs_c.append((q * inner_ref[...]).astype(BF16))
        vb_c.append(vb)

    st = sb_scr[...]
    rt = sc_scr[...]
    o_b, o_c = [], []
    for c in chunks:
        o_b.append(_nt_dot(qs_b[c], st.astype(BF16)))
        st = st * dec_b[c] + jnp.where(block_diag, upd_b[c], 0.0)
        o_c.append(_nt_dot(qs_c[c], rt.astype(BF16)))
        rt = rt * rdec_ref[...] + jnp.where(block_diag, upd_c[c], 0.0)
    sb_scr[...] = st
    sc_scr[...] = rt

    prod_b = [[jnp.dot(jnp.where(causal, att_b[c][hd], 0.0).astype(BF16), vb_b[c], preferred_element_type=F32)
               for hd in heads] for c in chunks]
    prod_c = [[jnp.dot((att_c[c][hd] * dmat_ref[hd * chunk:(hd + 1) * chunk, :]).astype(BF16), vb_c[c],
                       preferred_element_type=F32) for hd in heads] for c in chunks]
    outs = []
    for c in chunks:
        for o, prod in ((o_b[c], prod_b[c]), (o_c[c], prod_c[c])):
            for hd in heads:
                o = o + jnp.where(lane_h == hd, prod[hd], 0.0)
            outs.append(o)

    means = []
    for o in outs:
        hi, lo = _split2(o * o)
        means.append(jnp.dot(jnp.concatenate([hi, lo], axis=1), gmat, preferred_element_type=F32))
    for c in chunks:
        yb = outs[2 * c] * lax.rsqrt(means[2 * c] + EPS) * hg_ref[...]
        yc = outs[2 * c + 1] * lax.rsqrt(means[2 * c + 1] + EPS) * rg_ref[...]
        yb_ref[0, rows[c], :] = (yb * gb_ref[0, rows[c], :].astype(F32)).astype(BF16)
        yc_ref[0, rows[c], :] = (yc * gc_ref[0, rows[c], :].astype(F32)).astype(BF16)

    @pl.when(j == pl.num_programs(1) - 1)
    def _():
        sb_out[0] = sb_scr[...]
        sc_out[0] = sc_scr[...]


def _state_to_blockdiag(s):
    b = s.shape[0]
    eye = jnp.eye(REC_HEADS, dtype=s.dtype)
    return jnp.einsum('bhkv,hg->bhvgk', s, eye).reshape(b, B_WIDTH, B_WIDTH)


def _state_from_blockdiag(sbd):
    b = sbd.shape[0]
    s5 = sbd.reshape(b, REC_HEADS, HEAD_DIM, REC_HEADS, HEAD_DIM)
    return jnp.einsum('bhvhk->bhkv', s5)


def _rec_call(bq, bk, bv, blf, cq, ck, cv, gate, s0b, s0c, hgain, rgain, name):
    b, t, _ = bq.shape
    chunk = min(CHUNK, t)
    tb = min(t, 8 * chunk)
    n_chunks = tb // chunk
    lg = _retention_log_decay()
    n = jnp.arange(chunk, dtype=F32)
    diff = n[:, None] - n[None, :]
    dmat = jnp.where(diff >= 0, jnp.exp(jnp.where(diff >= 0, diff, 0.0)[None] * lg[:, None, None]), 0.0)
    dmat = dmat.reshape(REC_HEADS * chunk, chunk)
    per_lane = lambda a: jnp.repeat(a, HEAD_DIM, axis=-1)
    inner = per_lane(jnp.exp((n[:, None] + 1.0) * lg[None, :]))
    kscale = per_lane(jnp.exp((chunk - 1.0 - n)[:, None] * lg[None, :]))
    rdec = per_lane(jnp.exp(chunk * lg)[None, :])
    tri = jnp.tile(jnp.tril(jnp.ones((chunk, chunk), F32)), (1, 3)).astype(BF16)
    seq = lambda bi, j: (bi, j, 0)
    per_b = lambda bi, j: (bi, 0, 0)
    const = lambda bi, j: (0, 0)
    stream = pl.BlockSpec((1, tb, B_WIDTH), seq)
    state = pl.BlockSpec((1, B_WIDTH, B_WIDTH), per_b)
    return pl.pallas_call(
        functools.partial(_rec_kernel, chunk=chunk, n_chunks=n_chunks),
        grid=(b, t // tb),
        in_specs=[stream] * 7 + [pl.BlockSpec((1, tb, B_WIDTH), lambda bi, j: (bi, j, A_WIDTH // B_WIDTH)),
                                 pl.BlockSpec((1, tb, C_WIDTH), lambda bi, j: (bi, j, (A_WIDTH + B_WIDTH) // C_WIDTH)),
                                 state, state,
                                 pl.BlockSpec((1, B_WIDTH), const), pl.BlockSpec((1, B_WIDTH), const),
                                 pl.BlockSpec((chunk, 3 * chunk), const),
                                 pl.BlockSpec((REC_HEADS * chunk, chunk), const),
                                 pl.BlockSpec((chunk, B_WIDTH), const), pl.BlockSpec((chunk, B_WIDTH), const),
                                 pl.BlockSpec((1, B_WIDTH), const),
                                 pl.BlockSpec((2 * B_WIDTH, B_WIDTH), const)],
        out_specs=[stream, stream, state, state],
        out_shape=[jax.ShapeDtypeStruct((b, t, B_WIDTH), BF16), jax.ShapeDtypeStruct((b, t, C_WIDTH), BF16),
                   jax.ShapeDtypeStruct((b, B_WIDTH, B_WIDTH), F32),
                   jax.ShapeDtypeStruct((b, C_WIDTH, C_WIDTH), F32)],
        scratch_shapes=[pltpu.VMEM((B_WIDTH, B_WIDTH), F32), pltpu.VMEM((C_WIDTH, C_WIDTH), F32)],
        compiler_params=pltpu.CompilerParams(dimension_semantics=("parallel", "arbitrary"),
                                             vmem_limit_bytes=VMEM_LIMIT),
        name=name,
    )(bq, bk, bv, blf, cq, ck, cv, gate, gate, s0b, s0c,
      jnp.tile(hgain, REC_HEADS).reshape(1, B_WIDTH), jnp.tile(rgain, REC_HEADS).reshape(1, C_WIDTH),
      tri, dmat, inner, kscale, rdec, jnp.tile(_group_mean_matrix(B_WIDTH), (2, 1)))


def _out_kernel(x_ref, ya_ref, yb_ref, yc_ref, w_ref, y_ref):
    acc = jnp.dot(ya_ref[...], w_ref[:A_WIDTH, :], preferred_element_type=F32)
    acc = acc + jnp.dot(yb_ref[...], w_ref[A_WIDTH:A_WIDTH + B_WIDTH, :], preferred_element_type=F32)
    acc = acc + jnp.dot(yc_ref[...], w_ref[A_WIDTH + B_WIDTH:, :], preferred_element_type=F32)
    y_ref[...] = x_ref[...] + acc


def _out_call(x2d, ya, yb, yc, w_out16, tm, name):
    n = x2d.shape[0]
    tm = min(tm, n)
    row = lambda r: (r, 0)
    return pl.pallas_call(
        _out_kernel,
        grid=(n // tm,),
        in_specs=[pl.BlockSpec((tm, D_MODEL), row), pl.BlockSpec((tm, A_WIDTH), row),
                  pl.BlockSpec((tm, B_WIDTH), row), pl.BlockSpec((tm, C_WIDTH), row),
                  pl.BlockSpec((D_MODEL, D_MODEL), lambda r: (0, 0))],
        out_specs=pl.BlockSpec((tm, D_MODEL), row),
        out_shape=jax.ShapeDtypeStruct((n, D_MODEL), F32),
        compiler_params=pltpu.CompilerParams(dimension_semantics=("parallel",),
                                             vmem_limit_bytes=VMEM_LIMIT),
        name=name,
    )(x2d, ya, yb, yc, w_out16)


def _with_past(past_parts, new, s_pad):
    past = jnp.concatenate([p.astype(BF16) for p in past_parts], axis=-1)
    b, n_old, width = past.shape
    tail = jnp.zeros((b, s_pad - n_old - new.shape[1], width), BF16)
    return jnp.concatenate([past, new, tail], axis=1)


def _mixer_layer(x, pos0, past, s_hgrn, s_ret, layer, w_p, w_out16, norm_g, q_gain, k_gain,
                 lb_logits, hgain, rgain, tag):
    b, t, _ = x.shape
    n = b * t
    x2d = x.reshape(n, D_MODEL)
    pos = pos0 + jnp.arange(t)
    (q, k, v, kd, vd, iq, sg, ik, ikd, gate, bq, bk, bv, blf, cq, ck, cv) = _proj_call(
        x2d, t, pos, norm_g, w_p, q_gain, k_gain, lb_logits, layer, 512)

    three = lambda a: a.reshape(b, t, a.shape[-1])
    kd3, vd3, ikd3 = three(kd), three(vd), three(ikd)
    n_keys = t if past is None else past[0].shape[1] + t
    s_pad = -(-n_keys // LANES) * LANES
    if past is not None:
        pk, pv, pik = past
        kd3 = _with_past([pk[:, :, 0], pk[:, :, 0], pk[:, :, 1], pk[:, :, 1]], kd3, s_pad)
        vd3 = _with_past([pv[:, :, 0], pv[:, :, 0], pv[:, :, 1], pv[:, :, 1]], vd3, s_pad)
        ikd3 = _with_past([pik, pik], ikd3, s_pad)
    elif s_pad != n_keys:
        padw = ((0, 0), (0, s_pad - n_keys), (0, 0))
        kd3, vd3, ikd3 = jnp.pad(kd3, padw), jnp.pad(vd3, padw), jnp.pad(ikd3, padw)
    topk = min(TOPK_MAX, n_keys // 4)
    gate3 = three(gate)
    oa = _attention(three(q), three(iq), three(sg), kd3, vd3, ikd3, gate3, topk=topk, qpos0=pos0,
                    n_keys=n_keys, tag=tag)

    yb, yc, sb, sc = _rec_call(three(bq), three(bk), three(bv), three(blf), three(cq), three(ck), three(cv), gate3,
                               _state_to_blockdiag(s_hgrn), _state_to_blockdiag(s_ret), hgain, rgain,
                               name=f"rec_{tag}")
    out = _out_call(x2d, oa.reshape(n, A_WIDTH), yb.reshape(n, B_WIDTH), yc.reshape(n, C_WIDTH),
                    w_out16, 512, name=f"out_{tag}")
    return (out.reshape(b, t, D_MODEL),
            (k.reshape(b, t, A_KV_HEADS, HEAD_DIM), v.reshape(b, t, A_KV_HEADS, HEAD_DIM), ik,
             _state_from_blockdiag(sb), _state_from_blockdiag(sc)))


def kernel(x_prompt, x_sample, cache_k, cache_v, cache_idx_k, state_hgrn, state_ret, norm_g, w_in,
           q_norm_g, k_norm_g, hgrn_lb_logits, hgrn_norm_g, ret_norm_g, w_out):
    depth = w_in.shape[0]
    bp, tp = x_prompt.shape[:2]
    bs, ts = x_sample.shape[:2]
    past = cache_k.shape[2]
    zero_state = jnp.zeros((bp, REC_HEADS, HEAD_DIM, HEAD_DIM), F32)

    yp, ys = x_prompt, x_sample
    outs_p, outs_s = [], []
    for l in range(depth):
        w_p = _prep_w_in(w_in[l])
        w_o = w_out[l].astype(BF16)
        args = (l, w_p, w_o, norm_g[l], q_norm_g[l], k_norm_g[l], hgrn_lb_logits, hgrn_norm_g[l], ret_norm_g[l])
        yp, st = _mixer_layer(yp, 0, None, zero_state, zero_state, *args, tag=f"p{l}")
        outs_p.append(st)
        ys, st = _mixer_layer(ys, past, (cache_k[l], cache_v[l], cache_idx_k[l]),
                              state_hgrn[l], state_ret[l], *args, tag=f"s{l}")
        outs_s.append(st)

    def stack(outs, i, shape):
        return jnp.stack([o[i] for o in outs]).reshape(shape)

    return (yp, ys,
            stack(outs_p, 0, (depth, bp, tp, A_KV_HEADS, HEAD_DIM)),
            stack(outs_p, 1, (depth, bp, tp, A_KV_HEADS, HEAD_DIM)),
            stack(outs_p, 2, (depth, bp, tp, HEAD_DIM)),
            stack(outs_p, 3, (depth, bp, REC_HEADS, HEAD_DIM, HEAD_DIM)),
            stack(outs_p, 4, (depth, bp, REC_HEADS, HEAD_DIM, HEAD_DIM)),
            stack(outs_s, 0, (depth, bs, ts, A_KV_HEADS, HEAD_DIM)),
            stack(outs_s, 1, (depth, bs, ts, A_KV_HEADS, HEAD_DIM)),
            stack(outs_s, 2, (depth, bs, ts, HEAD_DIM)),
            stack(outs_s, 3, (depth, bs, REC_HEADS, HEAD_DIM, HEAD_DIM)),
            stack(outs_s, 4, (depth, bs, REC_HEADS, HEAD_DIM, HEAD_DIM)))
```

```python
import functools
import math

import numpy as np
import jax
import jax.numpy as jnp
from jax import lax
from jax.experimental import pallas as pl
from jax.experimental.pallas import tpu as pltpu

F32 = jnp.float32
BF16 = jnp.bfloat16
I32 = jnp.int32

D_MODEL = 1024
HEAD_DIM = 64
CHUNK = 64
A_WIDTH = 512
A_HEADS = 8
A_KV_HEADS = 2
KV_WIDTH = A_KV_HEADS * HEAD_DIM
IDX_HEADS = 8
IDX_W_SCALE = (IDX_HEADS * HEAD_DIM) ** -0.5
TOPK_MAX = 256
B_WIDTH = 256
C_WIDTH = 256
REC_HEADS = 4
ROPE_THETA = 10000.0
EPS = 1e-6
LANES = 128
VMEM_LIMIT = 48 * 1024 * 1024

_SIZES = (A_WIDTH, KV_WIDTH, KV_WIDTH, A_WIDTH, IDX_HEADS * HEAD_DIM, HEAD_DIM, IDX_HEADS,
          B_WIDTH, B_WIDTH, B_WIDTH, B_WIDTH, C_WIDTH, C_WIDTH, C_WIDTH, C_WIDTH)
_SPLIT_IDX = tuple(int(s) for s in np.cumsum(_SIZES)[:-1])

_AQ, _AK, _AV, _AG, _IQ, _IKW = 0, 512, 640, 768, 1280, 1792
_BQ, _BF, _BI, _BG, _CQ, _CK, _CV, _CG = 1920, 2176, 2432, 2688, 2944, 3200, 3456, 3712
_N_PROJ = 3968

INT_MIN = -(2 ** 31)
NEG_BIG = -1e30
POS_BIG = 1 << 20
LOG2_E = math.log2(math.e)


def _prep_w_in(w):
    (aq, ak, av, ag, iq, ik, iw, bq, bf, bi, bg, cq, ck, cv, cg) = jnp.split(w, _SPLIT_IDX, axis=1)
    pad = jnp.zeros((w.shape[0], LANES - HEAD_DIM - IDX_HEADS), w.dtype)
    return jnp.concatenate([aq, ak, av, ag, iq, ik, iw, pad, bq, bf, bi, bg, cq, ck, cv, cg],
                           axis=1).astype(BF16)


def _rope_tables(pos):
    half = HEAD_DIM // 2
    freqs = ROPE_THETA ** (-jnp.arange(half, dtype=F32) / half)
    ang = pos.astype(F32)[:, None] * freqs[None, :]
    cos, sin = jnp.cos(ang), jnp.sin(ang)
    cos64 = jnp.concatenate([cos, cos], axis=1)
    sin64 = jnp.concatenate([-sin, sin], axis=1)
    return jnp.concatenate([cos64, cos64], axis=1), jnp.concatenate([sin64, sin64], axis=1)


def _group_mean_matrix(width):
    idx = np.arange(width) // HEAD_DIM
    return jnp.asarray((idx[:, None] == idx[None, :]).astype(np.float32) / HEAD_DIM, BF16)


def _split2(x):
    hi = x.astype(BF16)
    lo = (x - hi.astype(F32)).astype(BF16)
    return hi, lo


def _nt_dot(a, b):
    return lax.dot_general(a, b, (((1,), (1,)), ((), ())), preferred_element_type=F32)


def _proj_kernel(x_ref, g_ref, w_ref, cos_ref, sin_ref, qg_ref, kg_ref, lb_ref, gm_ref, ex_ref,
                 q_out, k_out, v_out, kd_out, vd_out, iq_out, sg_out, ik_out, ikd_out, gate_out,
                 bq_out, bk_out, bv_out, blf_out, cq_out, ck_out, cv_out, *, layer):
    x = x_ref[...]
    tm = x.shape[0]
    ms = jnp.mean(x * x, axis=-1, keepdims=True)
    h = (x * lax.rsqrt(ms + EPS) * g_ref[...]).astype(BF16)
    cos = cos_ref[...]
    sin = sin_ref[...]
    lane = lax.broadcasted_iota(I32, (tm, LANES), 1)
    first_half = (lane % HEAD_DIM) < (HEAD_DIM // 2)
    low_head = lane < HEAD_DIM
    gmat = gm_ref[...]

    def proj(c0, width):
        return jnp.dot(h, w_ref[:, c0:c0 + width], preferred_element_type=F32)

    def rope(xb):
        swapped = jnp.where(first_half, pltpu.roll(xb, LANES - HEAD_DIM // 2, 1),
                            pltpu.roll(xb, HEAD_DIM // 2, 1))
        return xb * cos + swapped * sin

    def head_norm(xb, gain):
        hi, lo = _split2(xb * xb)
        msq = jnp.dot(jnp.concatenate([hi, lo], axis=1), gmat, preferred_element_type=F32)
        return xb * lax.rsqrt(msq + EPS) * gain

    def dup_heads(xb):
        other = pltpu.roll(xb, HEAD_DIM, 1)
        return jnp.where(low_head, xb, other), jnp.where(low_head, other, xb)

    ag = proj(_AG, A_WIDTH)
    bg = proj(_BG, B_WIDTH)
    cg = proj(_CG, C_WIDTH)
    bq = proj(_BQ, B_WIDTH)
    bf = proj(_BF, B_WIDTH)
    bi = proj(_BI, B_WIDTH)
    cq = proj(_CQ, C_WIDTH)
    ck = proj(_CK, C_WIDTH)
    cv = proj(_CV, C_WIDTH)
    akv = proj(_AK, 2 * KV_WIDTH)
    ikw = proj(_IKW, LANES)
    aq = proj(_AQ, A_WIDTH)
    iq = proj(_IQ, IDX_HEADS * HEAD_DIM)

    def silu(z):
        return z / (1.0 + jnp.exp(-z))

    gate_out[:, :A_WIDTH] = silu(ag).astype(BF16)
    gate_out[:, A_WIDTH:A_WIDTH + B_WIDTH] = silu(bg).astype(BF16)
    gate_out[:, A_WIDTH + B_WIDTH:] = silu(cg).astype(BF16)

    logits = lb_ref[...]
    e = jnp.exp(logits - jnp.max(logits, axis=0, keepdims=True))
    sm = e / jnp.sum(e, axis=0, keepdims=True)
    lb = jnp.zeros((1, B_WIDTH), F32)
    for j in range(1, layer + 1):
        lb = lb + sm[j:j + 1, :]
    f = lb + (1.0 - lb) / (1.0 + jnp.exp(-bf))
    bq_out[...] = bq
    bk_out[...] = (1.0 - lb) / (1.0 + jnp.exp(bf))
    bv_out[...] = bi
    blf_out[...] = jnp.log(f)

    for c in range(C_WIDTH // LANES):
        sl = slice(c * LANES, (c + 1) * LANES)
        cq_out[:, sl] = rope(cq[:, sl])
        ck_out[:, sl] = rope(ck[:, sl]) * (HEAD_DIM ** -0.5)
    cv_out[...] = cv

    v = akv[:, KV_WIDTH:]
    v_out[...] = v
    v0, v1 = dup_heads(v)
    vd_out[:, :LANES] = v0.astype(BF16)
    vd_out[:, LANES:] = v1.astype(BF16)
    ikr = rope(ikw)
    ik_out[...] = ikr[:, :HEAD_DIM]
    ikd_out[...] = dup_heads(ikr)[0].astype(BF16)
    sg_out[...] = jnp.where(ikw >= 0, 1.0, -1.0).astype(F32)

    for c in range(A_WIDTH // LANES):
        blk = rope(head_norm(aq[:, c * LANES:(c + 1) * LANES], qg_ref[...]))
        q_out[:, c * LANES:(c + 1) * LANES] = (blk * (HEAD_DIM ** -0.5 * LOG2_E)).astype(BF16)
    k = rope(head_norm(akv[:, :KV_WIDTH], kg_ref[...]))
    k_out[...] = k
    k0, k1 = dup_heads(k)
    kd_out[:, :LANES] = k0.astype(BF16)
    kd_out[:, LANES:] = k1.astype(BF16)

    whi, wlo = _split2(jnp.abs(ikw) * IDX_W_SCALE)
    wexp = jnp.dot(jnp.concatenate([whi, wlo], axis=1), ex_ref[...], preferred_element_type=F32)
    for c in range(IDX_HEADS * HEAD_DIM // LANES):
        sl = slice(c * LANES, (c + 1) * LANES)
        iq_out[:, sl] = (rope(iq[:, sl]) * wexp[:, sl]).astype(BF16)


def _row_blocking(b, t, tm):
    if t % tm == 0:
        grid = (b, t // tm)
        rows = lambda w: pl.BlockSpec((None, tm, w), lambda bi, j: (bi, j, 0))
        per_pos = lambda w: pl.BlockSpec((tm, w), lambda bi, j: (j, 0))
        const = lambda shape: pl.BlockSpec(shape, lambda bi, j: (0, 0))
        return grid, rows, per_pos, const, False, tm
    n = b * t
    tm = min(tm, n)
    assert n % tm == 0 and tm % t == 0
    grid = (n // tm,)
    rows = lambda w: pl.BlockSpec((tm, w), lambda r: (r, 0))
    per_pos = lambda w: pl.BlockSpec((tm, w), lambda r: (0, 0))
    const = lambda shape: pl.BlockSpec(shape, lambda r: (0, 0))
    return grid, rows, per_pos, const, True, tm


def _proj_call(x, pos, norm_g, w_p, q_gain, k_gain, lb_logits, layer, tm):
    b, t, _ = x.shape
    grid, rows, per_pos, const, flat, tm = _row_blocking(b, t, tm)
    cos, sin = _rope_tables(pos)
    if flat:
        x = x.reshape(b * t, D_MODEL)
        cos = jnp.tile(cos, (tm // t, 1))
        sin = jnp.tile(sin, (tm // t, 1))
    expand = np.zeros((LANES, IDX_HEADS * HEAD_DIM), np.float32)
    for hd in range(IDX_HEADS):
        expand[HEAD_DIM + hd, hd * HEAD_DIM:(hd + 1) * HEAD_DIM] = 1.0
    widths = [(A_WIDTH, BF16), (KV_WIDTH, F32), (KV_WIDTH, F32), (2 * KV_WIDTH, BF16), (2 * KV_WIDTH, BF16),
              (IDX_HEADS * HEAD_DIM, BF16), (LANES, F32), (HEAD_DIM, F32), (LANES, BF16), (D_MODEL, BF16),
              (B_WIDTH, F32), (B_WIDTH, F32), (B_WIDTH, F32), (B_WIDTH, F32),
              (C_WIDTH, F32), (C_WIDTH, F32), (C_WIDTH, F32)]
    lead = (b * t,) if flat else (b, t)
    outs = pl.pallas_call(
        functools.partial(_proj_kernel, layer=layer),
        grid=grid,
        in_specs=[rows(D_MODEL),
                  const((1, D_MODEL)),
                  const((D_MODEL, _N_PROJ)),
                  per_pos(LANES),
                  per_pos(LANES),
                  const((1, LANES)),
                  const((1, LANES)),
                  const(lb_logits.shape),
                  const((2 * LANES, LANES)),
                  const((2 * LANES, IDX_HEADS * HEAD_DIM))],
        out_specs=[rows(w) for w, _ in widths],
        out_shape=[jax.ShapeDtypeStruct(lead + (w,), dt) for w, dt in widths],
        compiler_params=pltpu.CompilerParams(dimension_semantics=("parallel",) * len(grid),
                                             vmem_limit_bytes=VMEM_LIMIT),
        name=f"proj_l{layer}",
    )(x, norm_g.reshape(1, D_MODEL), w_p, cos, sin,
      jnp.tile(q_gain, 2).reshape(1, LANES), jnp.tile(k_gain, 2).reshape(1, LANES),
      lb_logits, jnp.tile(_group_mean_matrix(LANES), (2, 1)), jnp.tile(jnp.asarray(expand, BF16), (2, 1)))
    return [o.reshape(b, t, o.shape[-1]) for o in outs] if flat else outs


SEARCH_INTERP_STEPS = 8
SEARCH_WALK_STEPS = 4
SEARCH_RANK_BIAS = 1.5
SEARCH_EDGE = 0.05
TIE_WALK_STEPS = (4, 12)


def _normal_upper_quantile(r):
    rr = jnp.minimum(r, 1.0 - r)
    t = jnp.sqrt(-2.0 * jnp.log(rr))
    z = t - ((0.010328 * t + 0.802853) * t + 2.515517) / (((0.001308 * t + 0.189269) * t + 1.432788) * t + 1.0)
    return jnp.where(r <= 0.5, z, -z)


def _select_topk(xm_scr, sum1_scr, sum2_scr, lim, topk, key_scr, thr_scr, lim_scr, bias_scr):
    tq, s = xm_scr.shape
    kf = float(topk)
    nf = lim.astype(F32)
    trivial = lim <= topk

    def count_gt(p):
        return jnp.sum(jnp.where(xm_scr[...] > p, 1.0, 0.0), axis=1, keepdims=True)

    def tally(t):
        xv = xm_scr[...]
        return (jnp.sum(jnp.where(xv > t, 1.0, 0.0), axis=1, keepdims=True),
                jnp.sum(jnp.where(xv == t, 1.0, 0.0), axis=1, keepdims=True))

    mu = jnp.sum(sum1_scr[...], axis=1, keepdims=True) / nf
    var = jnp.sum(sum2_scr[...], axis=1, keepdims=True) / nf - mu * mu
    sd = jnp.sqrt(jnp.maximum(var, 1e-30))
    target = kf + SEARCH_RANK_BIAS
    lo, hi = mu - 8.0 * sd, mu + 8.0 * sd
    clo, chi = nf, jnp.zeros_like(nf)
    p = mu + _normal_upper_quantile(jnp.clip(target / nf, 1e-6, 1.0 - 1e-6)) * sd
    for it in range(SEARCH_INTERP_STEPS):
        c = count_gt(p)
        above = c >= kf
        lo = jnp.where(above, p, lo)
        clo = jnp.where(above, c, clo)
        hi = jnp.where(above, hi, p)
        chi = jnp.where(above, chi, c)
        frac = jnp.clip((clo - target) / (clo - chi), SEARCH_EDGE, 1.0 - SEARCH_EDGE)
        p = lo + (hi - lo) * frac

    cur = lo
    thr = lo
    done = clo == kf
    for it in range(SEARCH_WALK_STEPS):
        xv = xm_scr[...]
        cur = jnp.min(jnp.where(xv > cur, xv, -NEG_BIG), axis=1, keepdims=True)
        hit = (count_gt(cur) <= kf) & jnp.logical_not(done)
        thr = jnp.where(hit, cur, thr)
        done = done | hit
    thr = jnp.where(trivial, 0.1 * NEG_BIG, thr)
    thr_scr[...] = jnp.broadcast_to(thr, (tq, LANES))

    cnt_gt, cnt_eq = tally(thr)
    sum1_scr[...] = jnp.broadcast_to(cnt_gt, (tq, LANES))
    sum2_scr[...] = jnp.broadcast_to(cnt_eq, (tq, LANES))
    proven = trivial | ((cnt_gt <= kf) & (cnt_gt + cnt_eq >= kf))

    @pl.when(jnp.min(jnp.where(proven, 1.0, 0.0)) < 0.5)
    def _():
        bits = lax.bitcast_convert_type(xm_scr[...], I32)
        key_scr[...] = bits ^ ((bits >> 31) & 0x7FFFFFFF)

        def bit_step(j, tu):
            cand = tu | lax.shift_left(jnp.int32(1), 31 - j)
            cnt = jnp.sum(jnp.where(key_scr[...] >= (cand ^ INT_MIN), 1.0, 0.0), axis=1, keepdims=True)
            return jnp.where(cnt >= kf, cand, tu)

        tkey = lax.fori_loop(0, 32, bit_step, jnp.zeros((tq, 1), I32)) ^ INT_MIN
        tval = lax.bitcast_convert_type(tkey ^ ((tkey >> 31) & 0x7FFFFFFF), F32)
        texact = jnp.where(trivial, 0.1 * NEG_BIG, tval)
        thr_scr[...] = jnp.broadcast_to(texact, (tq, LANES))
        cg, ce = tally(texact)
        sum1_scr[...] = jnp.broadcast_to(cg, (tq, LANES))
        sum2_scr[...] = jnp.broadcast_to(ce, (tq, LANES))

    thr = thr_scr[:, :1]
    cnt_gt = sum1_scr[:, :1]
    cnt_eq = sum2_scr[:, :1]
    xv = xm_scr[...]
    gt = xv > thr
    eq = xv == thr
    want_eq = kf - cnt_gt

    all_or_none = jnp.where(want_eq >= cnt_eq, POS_BIG, 0)
    lim_scr[...] = jnp.broadcast_to(all_or_none, (tq, LANES))
    partial = (want_eq > 0.0) & (want_eq < cnt_eq)
    kpos = lax.broadcasted_iota(I32, (tq, s), 1)

    @pl.when(jnp.max(jnp.where(partial, 1.0, 0.0)) > 0.5)
    def _():
        eqpos = jnp.where(eq, kpos, POS_BIG)
        eqposf = eqpos.astype(F32)

        def walk(first, last, cur, xlim):
            for it in range(first, last):
                cur = jnp.min(jnp.where(eqposf > cur, eqposf, float(POS_BIG)), axis=1, keepdims=True)
                xlim = jnp.where(partial & (want_eq == float(it)), cur.astype(I32), xlim)
            return cur, xlim

        cur, xlim = walk(0, TIE_WALK_STEPS[0], jnp.full((tq, 1), -1.0, F32), all_or_none)
        lim_scr[...] = jnp.broadcast_to(xlim, (tq, LANES))

        @pl.when(jnp.max(jnp.where(partial & (want_eq >= float(TIE_WALK_STEPS[0])), 1.0, 0.0)) > 0.5)
        def _():
            _, xlim2 = walk(TIE_WALK_STEPS[0], TIE_WALK_STEPS[1], cur, xlim)
            lim_scr[...] = jnp.broadcast_to(xlim2, (tq, LANES))

            @pl.when(jnp.max(jnp.where(partial & (want_eq >= float(TIE_WALK_STEPS[1])), 1.0, 0.0)) > 0.5)
            def _():
                nbits = max(1, int(s).bit_length())

                def pos_step(j, xl):
                    cand = xl | lax.shift_left(jnp.int32(1), nbits - 1 - j)
                    cnt = jnp.sum(jnp.where(eqpos < cand, 1.0, 0.0), axis=1, keepdims=True)
                    return jnp.where(cnt <= want_eq, cand, xl)

                lim_scr[...] = jnp.broadcast_to(lax.fori_loop(0, nbits, pos_step, jnp.zeros((tq, 1), I32)),
                                                (tq, LANES))

    sel = gt | (eq & (kpos < lim_scr[:, :1]))
    bias_scr[...] = jnp.where(sel, 0.0, NEG_BIG)


def _attn_kernel(*refs, topk, tpos0, n_keys):
    q_ref, iq_ref, sg_ref, kd_ref, vd_ref, ikd_ref, gate_ref = refs[:7]
    o_ref, xm_scr, key_scr, bias_scr, thr_scr, lim_scr, sum1_scr, sum2_scr = refs[-8:]
    tq = q_ref.shape[1]
    s = kd_ref.shape[1]
    lane = lax.broadcasted_iota(I32, (tq, LANES), 1)
    halves = (lane < HEAD_DIM, lane >= HEAD_DIM)

    tpos = tpos0 + lax.broadcasted_iota(I32, (tq, 1), 0)
    lim = jnp.minimum((tpos // CHUNK + 1) * CHUNK, n_keys)
    search = min(s, n_keys) > topk

    rb = min(tq, LANES)
    key_tile = 2 * LANES
    lane_rb = lax.broadcasted_iota(I32, (rb, LANES), 1)
    halves_rb = (lane_rb < HEAD_DIM, lane_rb >= HEAD_DIM)
    for r in range(tq // rb):
        rs = slice(r * rb, (r + 1) * rb)
        sg = sg_ref[0, rs, :]
        qms, sgs = [], []
        for hd in range(IDX_HEADS):
            qb = iq_ref[0, rs, (hd // 2) * LANES:(hd // 2 + 1) * LANES]
            qms.append(jnp.where(halves_rb[hd % 2], qb, jnp.zeros_like(qb)))
            sgs.append(sg[:, HEAD_DIM + hd:HEAD_DIM + hd + 1])
        acc1 = jnp.zeros((rb, LANES), F32)
        acc2 = jnp.zeros((rb, LANES), F32)
        for c0 in range(0, s, key_tile):
            w = min(key_tile, s - c0)
            ikd_t = ikd_ref[0, c0:c0 + w, :]
            acc = jnp.zeros((rb, w), F32)
            for hd in range(IDX_HEADS):
                acc = acc + sgs[hd] * jnp.maximum(_nt_dot(qms[hd], ikd_t), 0.0)
            adm_t = (c0 + lax.broadcasted_iota(I32, (rb, w), 1)) < lim[rs]
            xm_scr[rs, c0:c0 + w] = jnp.where(adm_t, acc, NEG_BIG)
            if search:
                sc0 = jnp.where(adm_t, acc, 0.0)
                for l0 in range(0, w, LANES):
                    part = sc0[:, l0:l0 + LANES]
                    acc1 = acc1 + part
                    acc2 = acc2 + part * part
        if search:
            sum1_scr[rs, :] = acc1
            sum2_scr[rs, :] = acc2

    logits = []
    for c in range(A_WIDTH // LANES):
        g = (2 * c) // (A_HEADS // A_KV_HEADS)
        qb = q_ref[0, :, c * LANES:(c + 1) * LANES]
        for e in range(2):
            qm = jnp.where(halves[e], qb, jnp.zeros_like(qb))
            logits.append(_nt_dot(qm, kd_ref[0, :, g * LANES:(g + 1) * LANES]))

    if search:
        _select_topk(xm_scr, sum1_scr, sum2_scr, lim, topk, key_scr, thr_scr, lim_scr, bias_scr)
    else:
        bias_scr[...] = jnp.where(lax.broadcasted_iota(I32, (tq, s), 1) < lim, 0.0, NEG_BIG)

    ones = jnp.ones((s, LANES), BF16)
    for g in range(A_KV_HEADS):
        vd1 = jnp.concatenate([vd_ref[0, :, g * LANES:(g + 1) * LANES], ones], axis=1)
        for c in range(g * 2, g * 2 + 2):
            outs = []
            for e in range(2):
                lg = logits[2 * c + e] + bias_scr[...]
                p = jnp.exp2(lg - jnp.max(lg, axis=1, keepdims=True))
                ol = jnp.dot(p.astype(BF16), vd1, preferred_element_type=F32)
                outs.append(ol[:, :LANES] / ol[:, LANES:])
            gate = gate_ref[0, :, c * LANES:(c + 1) * LANES].astype(F32)
            o_ref[0, :, c * LANES:(c + 1) * LANES] = (jnp.where(halves[0], outs[0], outs[1]) * gate).astype(BF16)


def _attn_call(q, iq, sg, kd, vd, ikd, gate, prev, *, topk, qpos0, n_keys, tq, qblock, s_blk, name):
    b, t, _ = q.shape
    qmap = lambda bi: (bi, qblock, 0)
    kmap = lambda bi: (bi, 0, 0)
    return pl.pallas_call(
        functools.partial(_attn_kernel, topk=topk, tpos0=qpos0 + qblock * tq, n_keys=n_keys),
        grid=(b,),
        in_specs=[pl.BlockSpec((1, tq, A_WIDTH), qmap),
                  pl.BlockSpec((1, tq, IDX_HEADS * HEAD_DIM), qmap),
                  pl.BlockSpec((1, tq, LANES), qmap),
                  pl.BlockSpec((1, s_blk, 2 * KV_WIDTH), kmap),
                  pl.BlockSpec((1, s_blk, 2 * KV_WIDTH), kmap),
                  pl.BlockSpec((1, s_blk, LANES), kmap),
                  pl.BlockSpec((1, tq, A_WIDTH), qmap),
                  pl.BlockSpec(memory_space=pl.ANY)],
        out_specs=pl.BlockSpec((1, tq, A_WIDTH), qmap),
        out_shape=jax.ShapeDtypeStruct((b, t, A_WIDTH), BF16),
        input_output_aliases={7: 0},
        scratch_shapes=[pltpu.VMEM((tq, s_blk), F32), pltpu.VMEM((tq, s_blk), I32), pltpu.VMEM((tq, s_blk), F32),
                        pltpu.VMEM((tq, LANES), F32), pltpu.VMEM((tq, LANES), I32),
                        pltpu.VMEM((tq, LANES), F32), pltpu.VMEM((tq, LANES), F32)],
        compiler_params=pltpu.CompilerParams(dimension_semantics=("parallel",),
                                             vmem_limit_bytes=VMEM_LIMIT),
        name=name,
    )(q, iq, sg, kd, vd, ikd, gate, prev)


def _attention(q, iq, sg, kd, vd, ikd, gate, *, topk, qpos0, n_keys, tag):
    b, t, _ = q.shape
    tq = min(t, 256)
    out = jnp.zeros((b, t, A_WIDTH), BF16)
    for qblock in range(t // tq):
        last_pos = qpos0 + (qblock + 1) * tq - 1
        visible = min((last_pos // CHUNK + 1) * CHUNK, n_keys)
        s_blk = min(-(-visible // LANES) * LANES, kd.shape[1])
        out = _attn_call(q, iq, sg, kd, vd, ikd, gate, out, topk=topk, qpos0=qpos0, n_keys=n_keys, tq=tq,
                         qblock=qblock, s_blk=s_blk, name=f"attn_{tag}_q{qblock}")
    return out


def _retention_log_decay():
    return jnp.log(1.0 - jnp.exp(jnp.linspace(math.log(1.0 / 32), math.log(1.0 / 512), REC_HEADS, dtype=F32)))


def _rec_kernel(bq_ref, bk_ref, bv_ref, blf_ref, cq_ref, ck_ref, cv_ref, gb_ref, gc_ref, s0b_ref, s0c_ref,
                hg_ref, rg_ref, tri_ref, dmat_ref, inner_ref, kscale_ref, rdec_ref, gm_ref,
                yb_ref, yc_ref, sb_out, sc_out, sb_scr, sc_scr, *, chunk, n_chunks):
    j = pl.program_id(1)
    width = B_WIDTH

    @pl.when(j == 0)
    def _():
        sb_scr[...] = s0b_ref[0]
        sc_scr[...] = s0c_ref[0]

    rowh = lax.broadcasted_iota(I32, (width, width), 0) // HEAD_DIM
    colh = lax.broadcasted_iota(I32, (width, width), 1) // HEAD_DIM
    block_diag = rowh == colh
    lane_h = lax.broadcasted_iota(I32, (chunk, width), 1) // HEAD_DIM
    causal = (lax.broadcasted_iota(I32, (chunk, chunk), 0)
              >= lax.broadcasted_iota(I32, (chunk, chunk), 1))
    gmat = gm_ref[...]
    tri = tri_ref[...]
    tn = (((0,), (0,)), ((), ()))

    def masked_heads(x):
        return [jnp.where(lane_h == hd, x, 0.0).astype(BF16) for hd in range(REC_HEADS)]

    heads = range(REC_HEADS)
    chunks = range(n_chunks)
    rows = [pl.ds(c * chunk, chunk) for c in chunks]

    bcs = []
    for c in chunks:
        lf = blf_ref[0, rows[c], :]
        a1 = lf.astype(BF16)
        r1 = lf - a1.astype(F32)
        a2 = r1.astype(BF16)
        a3 = (r1 - a2.astype(F32)).astype(BF16)
        bcs.append(jnp.dot(tri, jnp.concatenate([a1, a2, a3], axis=0), preferred_element_type=F32))

    att_b, att_c, upd_b, upd_c, qs_b, qs_c, dec_b, vb_b, vb_c = ([] for _ in range(9))
    for c in chunks:
        bc = bcs[c]
        mid = bc[chunk // 2 - 1:chunk // 2, :]
        last = bc[chunk - 1:chunk, :]
        q = bq_ref[0, rows[c], :]
        k = bk_ref[0, rows[c], :]
        vb = bv_ref[0, rows[c], :].astype(BF16)
        kt = (k * jnp.exp(mid - bc)).astype(BF16)
        att_b.append([_nt_dot(qh, kt) for qh in masked_heads(q * jnp.exp(bc - mid))])
        upd_b.append(lax.dot_general(vb, (k * jnp.exp(last - bc)).astype(BF16), tn, preferred_element_type=F32))
        qs_b.append((q * jnp.exp(bc)).astype(BF16))
        dec_b.append(jnp.exp(last))
        vb_b.append(vb)
        q = cq_ref[0, rows[c], :]
        k = ck_ref[0, rows[c], :]
        vb = cv_ref[0, rows[c], :].astype(BF16)
        kb = k.astype(BF16)
        att_c.append([_nt_dot(qh, kb) for qh in masked_heads(q)])
        upd_c.append(lax.dot_general(vb, (k * kscale_ref[...]).astype(BF16), tn, preferred_element_type=F32))
        qs_c.append((q * inner_ref[...]).astype(BF16))
        vb_c.append(vb)

    st = sb_scr[...]
    rt = sc_scr[...]
    o_b, o_c = [], []
    for c in chunks:
        o_b.append(_nt_dot(qs_b[c], st.astype(BF16)))
        st = st * dec_b[c] + jnp.where(block_diag, upd_b[c], 0.0)
        o_c.append(_nt_dot(qs_c[c], rt.astype(BF16)))
        rt = rt * rdec_ref[...] + jnp.where(block_diag, upd_c[c], 0.0)
    sb_scr[...] = st
    sc_scr[...] = rt

    prod_b = [[jnp.dot(jnp.where(causal, att_b[c][hd], 0.0).astype(BF16), vb_b[c], preferred_element_type=F32)
               for hd in heads] for c in chunks]
    prod_c = [[jnp.dot((att_c[c][hd] * dmat_ref[hd * chunk:(hd + 1) * chunk, :]).astype(BF16), vb_c[c],
                       preferred_element_type=F32) for hd in heads] for c in chunks]
    outs = []
    for c in chunks:
        for o, prod in ((o_b[c], prod_b[c]), (o_c[c], prod_c[c])):
            for hd in heads:
                o = o + jnp.where(lane_h == hd, prod[hd], 0.0)
            outs.append(o)

    means = []
    for o in outs:
        hi, lo = _split2(o * o)
        means.append(jnp.dot(jnp.concatenate([hi, lo], axis=1), gmat, preferred_element_type=F32))
    for c in chunks:
        yb = outs[2 * c] * lax.rsqrt(means[2 * c] + EPS) * hg_ref[...]
        yc = outs[2 * c + 1] * lax.rsqrt(means[2 * c + 1] + EPS) * rg_ref[...]
        yb_ref[0, rows[c], :] = (yb * gb_ref[0, rows[c], :].astype(F32)).astype(BF16)
        yc_ref[0, rows[c], :] = (yc * gc_ref[0, rows[c], :].astype(F32)).astype(BF16)

    @pl.when(j == pl.num_programs(1) - 1)
    def _():
        sb_out[0] = sb_scr[...]
        sc_out[0] = sc_scr[...]


def _state_to_blockdiag(s):
    b = s.shape[0]
    eye = jnp.eye(REC_HEADS, dtype=s.dtype)
    return jnp.einsum('bhkv,hg->bhvgk', s, eye).reshape(b, B_WIDTH, B_WIDTH)


def _state_from_blockdiag(sbd):
    b = sbd.shape[0]
    s5 = sbd.reshape(b, REC_HEADS, HEAD_DIM, REC_HEADS, HEAD_DIM)
    return jnp.einsum('bhvhk->bhkv', s5)


def _rec_call(bq, bk, bv, blf, cq, ck, cv, gate, s0b, s0c, hgain, rgain, name):
    b, t, _ = bq.shape
    chunk = min(CHUNK, t)
    tb = min(t, 8 * chunk)
    n_chunks = tb // chunk
    lg = _retention_log_decay()
    n = jnp.arange(chunk, dtype=F32)
    diff = n[:, None] - n[None, :]
    dmat = jnp.where(diff >= 0, jnp.exp(jnp.where(diff >= 0, diff, 0.0)[None] * lg[:, None, None]), 0.0)
    dmat = dmat.reshape(REC_HEADS * chunk, chunk)
    per_lane = lambda a: jnp.repeat(a, HEAD_DIM, axis=-1)
    inner = per_lane(jnp.exp((n[:, None] + 1.0) * lg[None, :]))
    kscale = per_lane(jnp.exp((chunk - 1.0 - n)[:, None] * lg[None, :]))
    rdec = per_lane(jnp.exp(chunk * lg)[None, :])
    tri = jnp.tile(jnp.tril(jnp.ones((chunk, chunk), F32)), (1, 3)).astype(BF16)
    seq = lambda bi, j: (bi, j, 0)
    per_b = lambda bi, j: (bi, 0, 0)
    const = lambda bi, j: (0, 0)
    stream = pl.BlockSpec((1, tb, B_WIDTH), seq)
    state = pl.BlockSpec((1, B_WIDTH, B_WIDTH), per_b)
    return pl.pallas_call(
        functools.partial(_rec_kernel, chunk=chunk, n_chunks=n_chunks),
        grid=(b, t // tb),
        in_specs=[stream] * 7 + [pl.BlockSpec((1, tb, B_WIDTH), lambda bi, j: (bi, j, A_WIDTH // B_WIDTH)),
                                 pl.BlockSpec((1, tb, C_WIDTH), lambda bi, j: (bi, j, (A_WIDTH + B_WIDTH) // C_WIDTH)),
                                 state, state,
                                 pl.BlockSpec((1, B_WIDTH), const), pl.BlockSpec((1, B_WIDTH), const),
                                 pl.BlockSpec((chunk, 3 * chunk), const),
                                 pl.BlockSpec((REC_HEADS * chunk, chunk), const),
                                 pl.BlockSpec((chunk, B_WIDTH), const), pl.BlockSpec((chunk, B_WIDTH), const),
                                 pl.BlockSpec((1, B_WIDTH), const),
                                 pl.BlockSpec((2 * B_WIDTH, B_WIDTH), const)],
        out_specs=[stream, stream, state, state],
        out_shape=[jax.ShapeDtypeStruct((b, t, B_WIDTH), BF16), jax.ShapeDtypeStruct((b, t, C_WIDTH), BF16),
                   jax.ShapeDtypeStruct((b, B_WIDTH, B_WIDTH), F32),
                   jax.ShapeDtypeStruct((b, C_WIDTH, C_WIDTH), F32)],
        scratch_shapes=[pltpu.VMEM((B_WIDTH, B_WIDTH), F32), pltpu.VMEM((C_WIDTH, C_WIDTH), F32)],
        compiler_params=pltpu.CompilerParams(dimension_semantics=("parallel", "arbitrary"),
                                             vmem_limit_bytes=VMEM_LIMIT),
        name=name,
    )(bq, bk, bv, blf, cq, ck, cv, gate, gate, s0b, s0c,
      jnp.tile(hgain, REC_HEADS).reshape(1, B_WIDTH), jnp.tile(rgain, REC_HEADS).reshape(1, C_WIDTH),
      tri, dmat, inner, kscale, rdec, jnp.tile(_group_mean_matrix(B_WIDTH), (2, 1)))


def _out_kernel(x_ref, ya_ref, yb_ref, yc_ref, w_ref, y_ref):
    acc = jnp.dot(ya_ref[...], w_ref[:A_WIDTH, :], preferred_element_type=F32)
    acc = acc + jnp.dot(yb_ref[...], w_ref[A_WIDTH:A_WIDTH + B_WIDTH, :], preferred_element_type=F32)
    acc = acc + jnp.dot(yc_ref[...], w_ref[A_WIDTH + B_WIDTH:, :], preferred_element_type=F32)
    y_ref[...] = x_ref[...] + acc


def _out_call(x, ya, yb, yc, w_out16, tm, name):
    b, t, _ = x.shape
    grid, rows, _, const, flat, tm = _row_blocking(b, t, tm)
    args = (x, ya, yb, yc)
    if flat:
        args = tuple(a.reshape(b * t, a.shape[-1]) for a in args)
    out = pl.pallas_call(
        _out_kernel,
        grid=grid,
        in_specs=[rows(D_MODEL), rows(A_WIDTH), rows(B_WIDTH), rows(C_WIDTH), const((D_MODEL, D_MODEL))],
        out_specs=rows(D_MODEL),
        out_shape=jax.ShapeDtypeStruct(args[0].shape, F32),
        compiler_params=pltpu.CompilerParams(dimension_semantics=("parallel",) * len(grid),
                                             vmem_limit_bytes=VMEM_LIMIT),
        name=name,
    )(*args, w_out16)
    return out.reshape(b, t, D_MODEL) if flat else out


def _with_past(past_parts, new, s_pad):
    past = jnp.concatenate([p.astype(BF16) for p in past_parts], axis=-1)
    b, n_old, width = past.shape
    tail = jnp.zeros((b, s_pad - n_old - new.shape[1], width), BF16)
    return jnp.concatenate([past, new, tail], axis=1)


def _mixer_layer(x, pos0, past, s_hgrn, s_ret, layer, w_p, w_out16, norm_g, q_gain, k_gain,
                 lb_logits, hgain, rgain, tag):
    b, t, _ = x.shape
    pos = pos0 + jnp.arange(t)
    (q, k, v, kd3, vd3, iq, sg, ik, ikd3, gate, bq, bk, bv, blf, cq, ck, cv) = _proj_call(
        x, pos, norm_g, w_p, q_gain, k_gain, lb_logits, layer, 512)

    n_keys = t if past is None else past[0].shape[1] + t
    s_pad = -(-n_keys // LANES) * LANES
    if past is not None:
        pk, pv, pik = past
        kd3 = _with_past([pk[:, :, 0], pk[:, :, 0], pk[:, :, 1], pk[:, :, 1]], kd3, s_pad)
        vd3 = _with_past([pv[:, :, 0], pv[:, :, 0], pv[:, :, 1], pv[:, :, 1]], vd3, s_pad)
        ikd3 = _with_past([pik, pik], ikd3, s_pad)
    elif s_pad != n_keys:
        padw = ((0, 0), (0, s_pad - n_keys), (0, 0))
        kd3, vd3, ikd3 = jnp.pad(kd3, padw), jnp.pad(vd3, padw), jnp.pad(ikd3, padw)
    topk = min(TOPK_MAX, n_keys // 4)
    oa = _attention(q, iq, sg, kd3, vd3, ikd3, gate, topk=topk, qpos0=pos0, n_keys=n_keys, tag=tag)

    yb, yc, sb, sc = _rec_call(bq, bk, bv, blf, cq, ck, cv, gate, s_hgrn, s_ret, hgain, rgain, name=f"rec_{tag}")
    out = _out_call(x, oa, yb, yc, w_out16, 512, name=f"out_{tag}")
    return (out,
            (k.reshape(b, t, A_KV_HEADS, HEAD_DIM), v.reshape(b, t, A_KV_HEADS, HEAD_DIM), ik,
             _state_from_blockdiag(sb), _state_from_blockdiag(sc)))


def kernel(x_prompt, x_sample, cache_k, cache_v, cache_idx_k, state_hgrn, state_ret, norm_g, w_in,
           q_norm_g, k_norm_g, hgrn_lb_logits, hgrn_norm_g, ret_norm_g, w_out):
    depth = w_in.shape[0]
    bp, tp = x_prompt.shape[:2]
    bs, ts = x_sample.shape[:2]
    past = cache_k.shape[2]
    zero_state = jnp.zeros((bp, B_WIDTH, B_WIDTH), F32)

    yp, ys = x_prompt, x_sample
    outs_p, outs_s = [], []
    for l in range(depth):
        w_p = _prep_w_in(w_in[l])
        w_o = w_out[l].astype(BF16)
        args = (l, w_p, w_o, norm_g[l], q_norm_g[l], k_norm_g[l], hgrn_lb_logits, hgrn_norm_g[l], ret_norm_g[l])
        yp, st = _mixer_layer(yp, 0, None, zero_state, zero_state, *args, tag=f"p{l}")
        outs_p.append(st)
        ys, st = _mixer_layer(ys, past, (cache_k[l], cache_v[l], cache_idx_k[l]),
                              _state_to_blockdiag(state_hgrn[l]), _state_to_blockdiag(state_ret[l]), *args,
                              tag=f"s{l}")
        outs_s.append(st)

    def stack(outs, i, shape):
        return jnp.stack([o[i] for o in outs]).reshape(shape)

    return (yp, ys,
            stack(outs_p, 0, (depth, bp, tp, A_KV_HEADS, HEAD_DIM)),
            stack(outs_p, 1, (depth, bp, tp, A_KV_HEADS, HEAD_DIM)),
            stack(outs_p, 2, (depth, bp, tp, HEAD_DIM)),
            stack(outs_p, 3, (depth, bp, REC_HEADS, HEAD_DIM, HEAD_DIM)),
            stack(outs_p, 4, (depth, bp, REC_HEADS, HEAD_DIM, HEAD_DIM)),
            stack(outs_s, 0, (depth, bs, ts, A_KV_HEADS, HEAD_DIM)),
            stack(outs_s, 1, (depth, bs, ts, A_KV_HEADS, HEAD_DIM)),
            stack(outs_s, 2, (depth, bs, ts, HEAD_DIM)),
            stack(outs_s, 3, (depth, bs, REC_HEADS, HEAD_DIM, HEAD_DIM)),
            stack(outs_s, 4, (depth, bs, REC_HEADS, HEAD_DIM, HEAD_DIM)))
```

```python
import functools
import math

import numpy as np
import jax
import jax.numpy as jnp
from jax import lax
from jax.experimental import pallas as pl
from jax.experimental.pallas import tpu as pltpu

F32 = jnp.float32
BF16 = jnp.bfloat16
I32 = jnp.int32

D_MODEL = 1024
HEAD_DIM = 64
CHUNK = 64
A_WIDTH = 512
A_HEADS = 8
A_KV_HEADS = 2
KV_WIDTH = A_KV_HEADS * HEAD_DIM
IDX_HEADS = 8
IDX_W_SCALE = (IDX_HEADS * HEAD_DIM) ** -0.5
TOPK_MAX = 256
B_WIDTH = 256
C_WIDTH = 256
REC_HEADS = 4
ROPE_THETA = 10000.0
EPS = 1e-6
LANES = 128
VMEM_LIMIT = 48 * 1024 * 1024

_SIZES = (A_WIDTH, KV_WIDTH, KV_WIDTH, A_WIDTH, IDX_HEADS * HEAD_DIM, HEAD_DIM, IDX_HEADS,
          B_WIDTH, B_WIDTH, B_WIDTH, B_WIDTH, C_WIDTH, C_WIDTH, C_WIDTH, C_WIDTH)
_SPLIT_IDX = tuple(int(s) for s in np.cumsum(_SIZES)[:-1])

_AQ, _AK, _AV, _AG, _IQ, _IKW = 0, 512, 640, 768, 1280, 1792
_BQ, _BF, _BI, _BG, _CQ, _CK, _CV, _CG = 1920, 2176, 2432, 2688, 2944, 3200, 3456, 3712
_N_PROJ = 3968

INT_MIN = -(2 ** 31)
NEG_BIG = -1e30
POS_BIG = 1 << 20
LOG2_E = math.log2(math.e)


def _prep_w_in(w):
    (aq, ak, av, ag, iq, ik, iw, bq, bf, bi, bg, cq, ck, cv, cg) = jnp.split(w, _SPLIT_IDX, axis=1)
    pad = jnp.zeros((w.shape[0], LANES - HEAD_DIM - IDX_HEADS), w.dtype)
    return jnp.concatenate([aq, ak, av, ag, iq, ik, iw, pad, bq, bf, bi, bg, cq, ck, cv, cg],
                           axis=1).astype(BF16)


def _rope_tables(pos):
    half = HEAD_DIM // 2
    freqs = ROPE_THETA ** (-jnp.arange(half, dtype=F32) / half)
    ang = pos.astype(F32)[:, None] * freqs[None, :]
    cos, sin = jnp.cos(ang), jnp.sin(ang)
    cos64 = jnp.concatenate([cos, cos], axis=1)
    sin64 = jnp.concatenate([-sin, sin], axis=1)
    return jnp.concatenate([cos64, cos64], axis=1), jnp.concatenate([sin64, sin64], axis=1)


def _group_mean_matrix(width):
    idx = np.arange(width) // HEAD_DIM
    return jnp.asarray((idx[:, None] == idx[None, :]).astype(np.float32) / HEAD_DIM, BF16)


def _split2(x):
    hi = x.astype(BF16)
    lo = (x - hi.astype(F32)).astype(BF16)
    return hi, lo


def _nt_dot(a, b):
    return lax.dot_general(a, b, (((1,), (1,)), ((), ())), preferred_element_type=F32)


def _proj_kernel(x_ref, g_ref, w_ref, cos_ref, sin_ref, qg_ref, kg_ref, lb_ref, gm_ref, ex_ref,
                 q_out, k_out, v_out, kd_out, vd_out, iq_out, sg_out, ik_out, ikd_out, gate_out,
                 bq_out, bk_out, bv_out, blf_out, cq_out, ck_out, cv_out, *, layer):
    x = x_ref[...]
    tm = x.shape[0]
    ms = jnp.mean(x * x, axis=-1, keepdims=True)
    h = (x * lax.rsqrt(ms + EPS) * g_ref[...]).astype(BF16)
    cos = cos_ref[...]
    sin = sin_ref[...]
    lane = lax.broadcasted_iota(I32, (tm, LANES), 1)
    first_half = (lane % HEAD_DIM) < (HEAD_DIM // 2)
    low_head = lane < HEAD_DIM
    gmat = gm_ref[...]

    def proj(c0, width):
        return jnp.dot(h, w_ref[:, c0:c0 + width], preferred_element_type=F32)

    def rope(xb):
        swapped = jnp.where(first_half, pltpu.roll(xb, LANES - HEAD_DIM // 2, 1),
                            pltpu.roll(xb, HEAD_DIM // 2, 1))
        return xb * cos + swapped * sin

    def head_norm(xb, gain):
        hi, lo = _split2(xb * xb)
        msq = jnp.dot(jnp.concatenate([hi, lo], axis=1), gmat, preferred_element_type=F32)
        return xb * lax.rsqrt(msq + EPS) * gain

    def dup_heads(xb):
        other = pltpu.roll(xb, HEAD_DIM, 1)
        return jnp.where(low_head, xb, other), jnp.where(low_head, other, xb)

    ag = proj(_AG, A_WIDTH)
    bg = proj(_BG, B_WIDTH)
    cg = proj(_CG, C_WIDTH)
    bq = proj(_BQ, B_WIDTH)
    bf = proj(_BF, B_WIDTH)
    bi = proj(_BI, B_WIDTH)
    cq = proj(_CQ, C_WIDTH)
    ck = proj(_CK, C_WIDTH)
    cv = proj(_CV, C_WIDTH)
    akv = proj(_AK, 2 * KV_WIDTH)
    ikw = proj(_IKW, LANES)
    aq = proj(_AQ, A_WIDTH)
    iq = proj(_IQ, IDX_HEADS * HEAD_DIM)

    def silu(z):
        return z / (1.0 + jnp.exp(-z))

    gate_out[:, :A_WIDTH] = silu(ag).astype(BF16)
    gate_out[:, A_WIDTH:A_WIDTH + B_WIDTH] = silu(bg).astype(BF16)
    gate_out[:, A_WIDTH + B_WIDTH:] = silu(cg).astype(BF16)

    logits = lb_ref[...]
    e = jnp.exp(logits - jnp.max(logits, axis=0, keepdims=True))
    sm = e / jnp.sum(e, axis=0, keepdims=True)
    lb = jnp.zeros((1, B_WIDTH), F32)
    for j in range(1, layer + 1):
        lb = lb + sm[j:j + 1, :]
    f = lb + (1.0 - lb) / (1.0 + jnp.exp(-bf))
    bq_out[...] = bq
    bk_out[...] = (1.0 - lb) / (1.0 + jnp.exp(bf))
    bv_out[...] = bi
    blf_out[...] = jnp.log(f)

    for c in range(C_WIDTH // LANES):
        sl = slice(c * LANES, (c + 1) * LANES)
        cq_out[:, sl] = rope(cq[:, sl])
        ck_out[:, sl] = rope(ck[:, sl]) * (HEAD_DIM ** -0.5)
    cv_out[...] = cv

    v = akv[:, KV_WIDTH:]
    v_out[...] = v
    v0, v1 = dup_heads(v)
    vd_out[:, :LANES] = v0.astype(BF16)
    vd_out[:, LANES:] = v1.astype(BF16)
    ikr = rope(ikw)
    ik_out[...] = ikr[:, :HEAD_DIM]
    ikd_out[...] = dup_heads(ikr)[0].astype(BF16)
    sg_out[...] = jnp.where(ikw >= 0, 1.0, -1.0).astype(F32)

    for c in range(A_WIDTH // LANES):
        blk = rope(head_norm(aq[:, c * LANES:(c + 1) * LANES], qg_ref[...]))
        q_out[:, c * LANES:(c + 1) * LANES] = (blk * (HEAD_DIM ** -0.5 * LOG2_E)).astype(BF16)
    k = rope(head_norm(akv[:, :KV_WIDTH], kg_ref[...]))
    k_out[...] = k
    k0, k1 = dup_heads(k)
    kd_out[:, :LANES] = k0.astype(BF16)
    kd_out[:, LANES:] = k1.astype(BF16)

    whi, wlo = _split2(jnp.abs(ikw) * IDX_W_SCALE)
    wexp = jnp.dot(jnp.concatenate([whi, wlo], axis=1), ex_ref[...], preferred_element_type=F32)
    for c in range(IDX_HEADS * HEAD_DIM // LANES):
        sl = slice(c * LANES, (c + 1) * LANES)
        iq_out[:, sl] = (rope(iq[:, sl]) * wexp[:, sl]).astype(BF16)


def _row_blocking(b, t, tm):
    if t % tm == 0:
        grid = (b, t // tm)
        rows = lambda w: pl.BlockSpec((None, tm, w), lambda bi, j: (bi, j, 0))
        per_pos = lambda w: pl.BlockSpec((tm, w), lambda bi, j: (j, 0))
        const = lambda shape: pl.BlockSpec(shape, lambda bi, j: (0, 0))
        return grid, rows, per_pos, const, False, tm
    n = b * t
    tm = min(tm, n)
    assert n % tm == 0 and tm % t == 0
    grid = (n // tm,)
    rows = lambda w: pl.BlockSpec((tm, w), lambda r: (r, 0))
    per_pos = lambda w: pl.BlockSpec((tm, w), lambda r: (0, 0))
    const = lambda shape: pl.BlockSpec(shape, lambda r: (0, 0))
    return grid, rows, per_pos, const, True, tm


def _proj_call(x, pos, norm_g, w_p, q_gain, k_gain, lb_logits, layer, tm):
    b, t, _ = x.shape
    grid, rows, per_pos, const, flat, tm = _row_blocking(b, t, tm)
    cos, sin = _rope_tables(pos)
    if flat:
        x = x.reshape(b * t, D_MODEL)
        cos = jnp.tile(cos, (tm // t, 1))
        sin = jnp.tile(sin, (tm // t, 1))
    expand = np.zeros((LANES, IDX_HEADS * HEAD_DIM), np.float32)
    for hd in range(IDX_HEADS):
        expand[HEAD_DIM + hd, hd * HEAD_DIM:(hd + 1) * HEAD_DIM] = 1.0
    widths = [(A_WIDTH, BF16), (KV_WIDTH, F32), (KV_WIDTH, F32), (2 * KV_WIDTH, BF16), (2 * KV_WIDTH, BF16),
              (IDX_HEADS * HEAD_DIM, BF16), (LANES, F32), (HEAD_DIM, F32), (LANES, BF16), (D_MODEL, BF16),
              (B_WIDTH, F32), (B_WIDTH, F32), (B_WIDTH, F32), (B_WIDTH, F32),
              (C_WIDTH, F32), (C_WIDTH, F32), (C_WIDTH, F32)]
    lead = (b * t,) if flat else (b, t)
    outs = pl.pallas_call(
        functools.partial(_proj_kernel, layer=layer),
        grid=grid,
        in_specs=[rows(D_MODEL),
                  const((1, D_MODEL)),
                  const((D_MODEL, _N_PROJ)),
                  per_pos(LANES),
                  per_pos(LANES),
                  const((1, LANES)),
                  const((1, LANES)),
                  const(lb_logits.shape),
                  const((2 * LANES, LANES)),
                  const((2 * LANES, IDX_HEADS * HEAD_DIM))],
        out_specs=[rows(w) for w, _ in widths],
        out_shape=[jax.ShapeDtypeStruct(lead + (w,), dt) for w, dt in widths],
        compiler_params=pltpu.CompilerParams(dimension_semantics=("parallel",) * len(grid),
                                             vmem_limit_bytes=VMEM_LIMIT),
        name=f"proj_l{layer}",
    )(x, norm_g.reshape(1, D_MODEL), w_p, cos, sin,
      jnp.tile(q_gain, 2).reshape(1, LANES), jnp.tile(k_gain, 2).reshape(1, LANES),
      lb_logits, jnp.tile(_group_mean_matrix(LANES), (2, 1)), jnp.tile(jnp.asarray(expand, BF16), (2, 1)))
    return [o.reshape(b, t, o.shape[-1]) for o in outs] if flat else outs


SEARCH_INTERP_STEPS = 8
SEARCH_WALK_STEPS = 4
SEARCH_RANK_BIAS = 1.5
SEARCH_EDGE = 0.05
TIE_WALK_STEPS = (4, 12)


def _normal_upper_quantile(r):
    rr = jnp.minimum(r, 1.0 - r)
    t = jnp.sqrt(-2.0 * jnp.log(rr))
    z = t - ((0.010328 * t + 0.802853) * t + 2.515517) / (((0.001308 * t + 0.189269) * t + 1.432788) * t + 1.0)
    return jnp.where(r <= 0.5, z, -z)


def _select_topk(xm_scr, sum1_scr, sum2_scr, lim, topk, key_scr, thr_scr, lim_scr, bias_scr):
    tq, s = xm_scr.shape
    kf = float(topk)
    nf = lim.astype(F32)
    trivial = lim <= topk

    def count_gt(p):
        return jnp.sum(jnp.where(xm_scr[...] > p, 1.0, 0.0), axis=1, keepdims=True)

    def tally(t):
        xv = xm_scr[...]
        return (jnp.sum(jnp.where(xv > t, 1.0, 0.0), axis=1, keepdims=True),
                jnp.sum(jnp.where(xv == t, 1.0, 0.0), axis=1, keepdims=True))

    mu = jnp.sum(sum1_scr[...], axis=1, keepdims=True) / nf
    var = jnp.sum(sum2_scr[...], axis=1, keepdims=True) / nf - mu * mu
    sd = jnp.sqrt(jnp.maximum(var, 1e-30))
    target = kf + SEARCH_RANK_BIAS
    lo, hi = mu - 8.0 * sd, mu + 8.0 * sd
    clo, chi = nf, jnp.zeros_like(nf)
    p = mu + _normal_upper_quantile(jnp.clip(target / nf, 1e-6, 1.0 - 1e-6)) * sd
    for it in range(SEARCH_INTERP_STEPS):
        c = count_gt(p)
        above = c >= kf
        lo = jnp.where(above, p, lo)
        clo = jnp.where(above, c, clo)
        hi = jnp.where(above, hi, p)
        chi = jnp.where(above, chi, c)
        frac = jnp.clip((clo - target) / (clo - chi), SEARCH_EDGE, 1.0 - SEARCH_EDGE)
        p = lo + (hi - lo) * frac

    cur = lo
    thr = lo
    done = clo == kf
    for it in range(SEARCH_WALK_STEPS):
        xv = xm_scr[...]
        cur = jnp.min(jnp.where(xv > cur, xv, -NEG_BIG), axis=1, keepdims=True)
        hit = (count_gt(cur) <= kf) & jnp.logical_not(done)
        thr = jnp.where(hit, cur, thr)
        done = done | hit
    thr = jnp.where(trivial, 0.1 * NEG_BIG, thr)
    thr_scr[...] = jnp.broadcast_to(thr, (tq, LANES))

    cnt_gt, cnt_eq = tally(thr)
    sum1_scr[...] = jnp.broadcast_to(cnt_gt, (tq, LANES))
    sum2_scr[...] = jnp.broadcast_to(cnt_eq, (tq, LANES))
    proven = trivial | ((cnt_gt <= kf) & (cnt_gt + cnt_eq >= kf))

    @pl.when(jnp.min(jnp.where(proven, 1.0, 0.0)) < 0.5)
    def _():
        bits = lax.bitcast_convert_type(xm_scr[...], I32)
        key_scr[...] = bits ^ ((bits >> 31) & 0x7FFFFFFF)

        def bit_step(j, tu):
            cand = tu | lax.shift_left(jnp.int32(1), 31 - j)
            cnt = jnp.sum(jnp.where(key_scr[...] >= (cand ^ INT_MIN), 1.0, 0.0), axis=1, keepdims=True)
            return jnp.where(cnt >= kf, cand, tu)

        tkey = lax.fori_loop(0, 32, bit_step, jnp.zeros((tq, 1), I32)) ^ INT_MIN
        tval = lax.bitcast_convert_type(tkey ^ ((tkey >> 31) & 0x7FFFFFFF), F32)
        texact = jnp.where(trivial, 0.1 * NEG_BIG, tval)
        thr_scr[...] = jnp.broadcast_to(texact, (tq, LANES))
        cg, ce = tally(texact)
        sum1_scr[...] = jnp.broadcast_to(cg, (tq, LANES))
        sum2_scr[...] = jnp.broadcast_to(ce, (tq, LANES))

    thr = thr_scr[:, :1]
    cnt_gt = sum1_scr[:, :1]
    cnt_eq = sum2_scr[:, :1]
    xv = xm_scr[...]
    gt = xv > thr
    eq = xv == thr
    want_eq = kf - cnt_gt

    all_or_none = jnp.where(want_eq >= cnt_eq, POS_BIG, 0)
    lim_scr[...] = jnp.broadcast_to(all_or_none, (tq, LANES))
    partial = (want_eq > 0.0) & (want_eq < cnt_eq)
    kpos = lax.broadcasted_iota(I32, (tq, s), 1)

    @pl.when(jnp.max(jnp.where(partial, 1.0, 0.0)) > 0.5)
    def _():
        eqpos = jnp.where(eq, kpos, POS_BIG)
        eqposf = eqpos.astype(F32)

        def walk(first, last, cur, xlim):
            for it in range(first, last):
                cur = jnp.min(jnp.where(eqposf > cur, eqposf, float(POS_BIG)), axis=1, keepdims=True)
                xlim = jnp.where(partial & (want_eq == float(it)), cur.astype(I32), xlim)
            return cur, xlim

        cur, xlim = walk(0, TIE_WALK_STEPS[0], jnp.full((tq, 1), -1.0, F32), all_or_none)
        lim_scr[...] = jnp.broadcast_to(xlim, (tq, LANES))

        @pl.when(jnp.max(jnp.where(partial & (want_eq >= float(TIE_WALK_STEPS[0])), 1.0, 0.0)) > 0.5)
        def _():
            _, xlim2 = walk(TIE_WALK_STEPS[0], TIE_WALK_STEPS[1], cur, xlim)
            lim_scr[...] = jnp.broadcast_to(xlim2, (tq, LANES))

            @pl.when(jnp.max(jnp.where(partial & (want_eq >= float(TIE_WALK_STEPS[1])), 1.0, 0.0)) > 0.5)
            def _():
                nbits = max(1, int(s).bit_length())

                def pos_step(j, xl):
                    cand = xl | lax.shift_left(jnp.int32(1), nbits - 1 - j)
                    cnt = jnp.sum(jnp.where(eqpos < cand, 1.0, 0.0), axis=1, keepdims=True)
                    return jnp.where(cnt <= want_eq, cand, xl)

                lim_scr[...] = jnp.broadcast_to(lax.fori_loop(0, nbits, pos_step, jnp.zeros((tq, 1), I32)),
                                                (tq, LANES))

    sel = gt | (eq & (kpos < lim_scr[:, :1]))
    bias_scr[...] = jnp.where(sel, 0.0, NEG_BIG).astype(bias_scr.dtype)


def _attn_kernel(*refs, topk, tpos0, n_keys):
    q_ref, iq_ref, sg_ref, kd_ref, vd_ref, ikd_ref, gate_ref = refs[:7]
    o_ref, xm_scr, key_scr, bias_scr, thr_scr, lim_scr, sum1_scr, sum2_scr = refs[-8:]
    tq = q_ref.shape[1]
    s = kd_ref.shape[1]
    lane = lax.broadcasted_iota(I32, (tq, LANES), 1)
    halves = (lane < HEAD_DIM, lane >= HEAD_DIM)

    tpos = tpos0 + lax.broadcasted_iota(I32, (tq, 1), 0)
    lim = jnp.minimum((tpos // CHUNK + 1) * CHUNK, n_keys)
    search = min(s, n_keys) > topk

    rb = min(tq, LANES)
    key_tile = 2 * LANES
    lane_rb = lax.broadcasted_iota(I32, (rb, LANES), 1)
    halves_rb = (lane_rb < HEAD_DIM, lane_rb >= HEAD_DIM)
    for r in range(tq // rb):
        rs = slice(r * rb, (r + 1) * rb)
        sg = sg_ref[0, rs, :]
        qms, sgs = [], []
        for hd in range(IDX_HEADS):
            qb = iq_ref[0, rs, (hd // 2) * LANES:(hd // 2 + 1) * LANES]
            qms.append(jnp.where(halves_rb[hd % 2], qb, jnp.zeros_like(qb)))
            sgs.append(sg[:, HEAD_DIM + hd:HEAD_DIM + hd + 1])
        acc1 = jnp.zeros((rb, LANES), F32)
        acc2 = jnp.zeros((rb, LANES), F32)
        for c0 in range(0, s, key_tile):
            w = min(key_tile, s - c0)
            ikd_t = ikd_ref[0, c0:c0 + w, :]
            acc = jnp.zeros((rb, w), F32)
            for hd in range(IDX_HEADS):
                acc = acc + sgs[hd] * jnp.maximum(_nt_dot(qms[hd], ikd_t), 0.0)
            adm_t = (c0 + lax.broadcasted_iota(I32, (rb, w), 1)) < lim[rs]
            xm_scr[rs, c0:c0 + w] = jnp.where(adm_t, acc, NEG_BIG)
            if search:
                sc0 = jnp.where(adm_t, acc, 0.0)
                for l0 in range(0, w, LANES):
                    part = sc0[:, l0:l0 + LANES]
                    acc1 = acc1 + part
                    acc2 = acc2 + part * part
        if search:
            sum1_scr[rs, :] = acc1
            sum2_scr[rs, :] = acc2

    logits = []
    for c in range(A_WIDTH // LANES):
        g = (2 * c) // (A_HEADS // A_KV_HEADS)
        qb = q_ref[0, :, c * LANES:(c + 1) * LANES]
        for e in range(2):
            qm = jnp.where(halves[e], qb, jnp.zeros_like(qb))
            logits.append(_nt_dot(qm, kd_ref[0, :, g * LANES:(g + 1) * LANES]))

    if search:
        _select_topk(xm_scr, sum1_scr, sum2_scr, lim, topk, key_scr, thr_scr, lim_scr, bias_scr)
    else:
        bias_scr[...] = jnp.where(lax.broadcasted_iota(I32, (tq, s), 1) < lim, 0.0, NEG_BIG).astype(BF16)

    ones = jnp.ones((s, LANES), BF16)
    for g in range(A_KV_HEADS):
        vd1 = jnp.concatenate([vd_ref[0, :, g * LANES:(g + 1) * LANES], ones], axis=1)
        for c in range(g * 2, g * 2 + 2):
            outs = []
            for e in range(2):
                lg = logits[2 * c + e].astype(BF16) + bias_scr[...]
                p = jnp.exp2(lg - jnp.max(lg, axis=1, keepdims=True))
                ol = jnp.dot(p, vd1, preferred_element_type=F32)
                outs.append(ol[:, :LANES] / ol[:, LANES:])
            gate = gate_ref[0, :, c * LANES:(c + 1) * LANES].astype(F32)
            o_ref[0, :, c * LANES:(c + 1) * LANES] = (jnp.where(halves[0], outs[0], outs[1]) * gate).astype(BF16)


def _attn_call(q, iq, sg, kd, vd, ikd, gate, prev, *, topk, qpos0, n_keys, tq, qblock, s_blk, name):
    b, t, _ = q.shape
    qmap = lambda bi: (bi, qblock, 0)
    kmap = lambda bi: (bi, 0, 0)
    return pl.pallas_call(
        functools.partial(_attn_kernel, topk=topk, tpos0=qpos0 + qblock * tq, n_keys=n_keys),
        grid=(b,),
        in_specs=[pl.BlockSpec((1, tq, A_WIDTH), qmap),
                  pl.BlockSpec((1, tq, IDX_HEADS * HEAD_DIM), qmap),
                  pl.BlockSpec((1, tq, LANES), qmap),
                  pl.BlockSpec((1, s_blk, 2 * KV_WIDTH), kmap),
                  pl.BlockSpec((1, s_blk, 2 * KV_WIDTH), kmap),
                  pl.BlockSpec((1, s_blk, LANES), kmap),
                  pl.BlockSpec((1, tq, A_WIDTH), qmap),
                  pl.BlockSpec(memory_space=pl.ANY)],
        out_specs=pl.BlockSpec((1, tq, A_WIDTH), qmap),
        out_shape=jax.ShapeDtypeStruct((b, t, A_WIDTH), BF16),
        input_output_aliases={7: 0},
        scratch_shapes=[pltpu.VMEM((tq, s_blk), F32), pltpu.VMEM((tq, s_blk), I32), pltpu.VMEM((tq, s_blk), BF16),
                        pltpu.VMEM((tq, LANES), F32), pltpu.VMEM((tq, LANES), I32),
                        pltpu.VMEM((tq, LANES), F32), pltpu.VMEM((tq, LANES), F32)],
        compiler_params=pltpu.CompilerParams(dimension_semantics=("parallel",),
                                             vmem_limit_bytes=VMEM_LIMIT),
        name=name,
    )(q, iq, sg, kd, vd, ikd, gate, prev)


def _attention(q, iq, sg, kd, vd, ikd, gate, *, topk, qpos0, n_keys, tag):
    b, t, _ = q.shape
    tq = min(t, 256)
    out = jnp.zeros((b, t, A_WIDTH), BF16)
    for qblock in range(t // tq):
        last_pos = qpos0 + (qblock + 1) * tq - 1
        visible = min((last_pos // CHUNK + 1) * CHUNK, n_keys)
        s_blk = min(-(-visible // LANES) * LANES, kd.shape[1])
        out = _attn_call(q, iq, sg, kd, vd, ikd, gate, out, topk=topk, qpos0=qpos0, n_keys=n_keys, tq=tq,
                         qblock=qblock, s_blk=s_blk, name=f"attn_{tag}_q{qblock}")
    return out


def _retention_log_decay():
    return jnp.log(1.0 - jnp.exp(jnp.linspace(math.log(1.0 / 32), math.log(1.0 / 512), REC_HEADS, dtype=F32)))


def _rec_kernel(bq_ref, bk_ref, bv_ref, blf_ref, cq_ref, ck_ref, cv_ref, gb_ref, gc_ref, s0b_ref, s0c_ref,
                hg_ref, rg_ref, tri_ref, dmat_ref, inner_ref, kscale_ref, rdec_ref, gm_ref,
                yb_ref, yc_ref, sb_out, sc_out, sb_scr, sc_scr, *, chunk, n_chunks):
    j = pl.program_id(1)
    width = B_WIDTH

    @pl.when(j == 0)
    def _():
        sb_scr[...] = s0b_ref[0]
        sc_scr[...] = s0c_ref[0]

    rowh = lax.broadcasted_iota(I32, (width, width), 0) // HEAD_DIM
    colh = lax.broadcasted_iota(I32, (width, width), 1) // HEAD_DIM
    block_diag = rowh == colh
    lane_h = lax.broadcasted_iota(I32, (chunk, width), 1) // HEAD_DIM
    causal = (lax.broadcasted_iota(I32, (chunk, chunk), 0)
              >= lax.broadcasted_iota(I32, (chunk, chunk), 1))
    gmat = gm_ref[...]
    tri = tri_ref[...]
    tn = (((0,), (0,)), ((), ()))

    def masked_heads(x):
        return [jnp.where(lane_h == hd, x, 0.0).astype(BF16) for hd in range(REC_HEADS)]

    heads = range(REC_HEADS)
    chunks = range(n_chunks)
    rows = [pl.ds(c * chunk, chunk) for c in chunks]

    bcs = []
    for c in chunks:
        lf = blf_ref[0, rows[c], :]
        a1 = lf.astype(BF16)
        r1 = lf - a1.astype(F32)
        a2 = r1.astype(BF16)
        a3 = (r1 - a2.astype(F32)).astype(BF16)
        bcs.append(jnp.dot(tri, jnp.concatenate([a1, a2, a3], axis=0), preferred_element_type=F32))

    att_b, att_c, upd_b, upd_c, qs_b, qs_c, dec_b, vb_b, vb_c = ([] for _ in range(9))
    for c in chunks:
        bc = bcs[c]
        mid = bc[chunk // 2 - 1:chunk // 2, :]
        last = bc[chunk - 1:chunk, :]
        q = bq_ref[0, rows[c], :]
        k = bk_ref[0, rows[c], :]
        vb = bv_ref[0, rows[c], :].astype(BF16)
        kt = (k * jnp.exp(mid - bc)).astype(BF16)
        att_b.append([_nt_dot(qh, kt) for qh in masked_heads(q * jnp.exp(bc - mid))])
        upd_b.append(lax.dot_general(vb, (k * jnp.exp(last - bc)).astype(BF16), tn, preferred_element_type=F32))
        qs_b.append((q * jnp.exp(bc)).astype(BF16))
        dec_b.append(jnp.exp(last))
        vb_b.append(vb)
        q = cq_ref[0, rows[c], :]
        k = ck_ref[0, rows[c], :]
        vb = cv_ref[0, rows[c], :].astype(BF16)
        kb = k.astype(BF16)
        att_c.append([_nt_dot(qh, kb) for qh in masked_heads(q)])
        upd_c.append(lax.dot_general(vb, (k * kscale_ref[...]).astype(BF16), tn, preferred_element_type=F32))
        qs_c.append((q * inner_ref[...]).astype(BF16))
        vb_c.append(vb)

    st = sb_scr[...]
    rt = sc_scr[...]
    o_b, o_c = [], []
    for c in chunks:
        o_b.append(_nt_dot(qs_b[c], st.astype(BF16)))
        st = st * dec_b[c] + jnp.where(block_diag, upd_b[c], 0.0)
        o_c.append(_nt_dot(qs_c[c], rt.astype(BF16)))
        rt = rt * rdec_ref[...] + jnp.where(block_diag, upd_c[c], 0.0)
    sb_scr[...] = st
    sc_scr[...] = rt

    prod_b = [[jnp.dot(jnp.where(causal, att_b[c][hd], 0.0).astype(BF16), vb_b[c], preferred_element_type=F32)
               for hd in heads] for c in chunks]
    prod_c = [[jnp.dot((att_c[c][hd] * dmat_ref[hd * chunk:(hd + 1) * chunk, :]).astype(BF16), vb_c[c],
                       preferred_element_type=F32) for hd in heads] for c in chunks]
    outs = []
    for c in chunks:
        for o, prod in ((o_b[c], prod_b[c]), (o_c[c], prod_c[c])):
            for hd in heads:
                o = o + jnp.where(lane_h == hd, prod[hd], 0.0)
            outs.append(o)

    means = []
    for o in outs:
        hi, lo = _split2(o * o)
        means.append(jnp.dot(jnp.concatenate([hi, lo], axis=1), gmat, preferred_element_type=F32))
    for c in chunks:
        yb = outs[2 * c] * lax.rsqrt(means[2 * c] + EPS) * hg_ref[...]
        yc = outs[2 * c + 1] * lax.rsqrt(means[2 * c + 1] + EPS) * rg_ref[...]
        yb_ref[0, rows[c], :] = (yb * gb_ref[0, rows[c], :].astype(F32)).astype(BF16)
        yc_ref[0, rows[c], :] = (yc * gc_ref[0, rows[c], :].astype(F32)).astype(BF16)

    @pl.when(j == pl.num_programs(1) - 1)
    def _():
        sb_out[0] = sb_scr[...]
        sc_out[0] = sc_scr[...]


def _state_to_blockdiag(s):
    b = s.shape[0]
    eye = jnp.eye(REC_HEADS, dtype=s.dtype)
    return jnp.einsum('bhkv,hg->bhvgk', s, eye).reshape(b, B_WIDTH, B_WIDTH)


def _state_from_blockdiag(sbd):
    b = sbd.shape[0]
    s5 = sbd.reshape(b, REC_HEADS, HEAD_DIM, REC_HEADS, HEAD_DIM)
    return jnp.einsum('bhvhk->bhkv', s5)


def _rec_call(bq, bk, bv, blf, cq, ck, cv, gate, s0b, s0c, hgain, rgain, name):
    b, t, _ = bq.shape
    chunk = min(CHUNK, t)
    tb = min(t, 8 * chunk)
    n_chunks = tb // chunk
    lg = _retention_log_decay()
    n = jnp.arange(chunk, dtype=F32)
    diff = n[:, None] - n[None, :]
    dmat = jnp.where(diff >= 0, jnp.exp(jnp.where(diff >= 0, diff, 0.0)[None] * lg[:, None, None]), 0.0)
    dmat = dmat.reshape(REC_HEADS * chunk, chunk)
    per_lane = lambda a: jnp.repeat(a, HEAD_DIM, axis=-1)
    inner = per_lane(jnp.exp((n[:, None] + 1.0) * lg[None, :]))
    kscale = per_lane(jnp.exp((chunk - 1.0 - n)[:, None] * lg[None, :]))
    rdec = per_lane(jnp.exp(chunk * lg)[None, :])
    tri = jnp.tile(jnp.tril(jnp.ones((chunk, chunk), F32)), (1, 3)).astype(BF16)
    seq = lambda bi, j: (bi, j, 0)
    per_b = lambda bi, j: (bi, 0, 0)
    const = lambda bi, j: (0, 0)
    stream = pl.BlockSpec((1, tb, B_WIDTH), seq)
    state = pl.BlockSpec((1, B_WIDTH, B_WIDTH), per_b)
    return pl.pallas_call(
        functools.partial(_rec_kernel, chunk=chunk, n_chunks=n_chunks),
        grid=(b, t // tb),
        in_specs=[stream] * 7 + [pl.BlockSpec((1, tb, B_WIDTH), lambda bi, j: (bi, j, A_WIDTH // B_WIDTH)),
                                 pl.BlockSpec((1, tb, C_WIDTH), lambda bi, j: (bi, j, (A_WIDTH + B_WIDTH) // C_WIDTH)),
                                 state, state,
                                 pl.BlockSpec((1, B_WIDTH), const), pl.BlockSpec((1, B_WIDTH), const),
                                 pl.BlockSpec((chunk, 3 * chunk), const),
                                 pl.BlockSpec((REC_HEADS * chunk, chunk), const),
                                 pl.BlockSpec((chunk, B_WIDTH), const), pl.BlockSpec((chunk, B_WIDTH), const),
                                 pl.BlockSpec((1, B_WIDTH), const),
                                 pl.BlockSpec((2 * B_WIDTH, B_WIDTH), const)],
        out_specs=[stream, stream, state, state],
        out_shape=[jax.ShapeDtypeStruct((b, t, B_WIDTH), BF16), jax.ShapeDtypeStruct((b, t, C_WIDTH), BF16),
                   jax.ShapeDtypeStruct((b, B_WIDTH, B_WIDTH), F32),
                   jax.ShapeDtypeStruct((b, C_WIDTH, C_WIDTH), F32)],
        scratch_shapes=[pltpu.VMEM((B_WIDTH, B_WIDTH), F32), pltpu.VMEM((C_WIDTH, C_WIDTH), F32)],
        compiler_params=pltpu.CompilerParams(dimension_semantics=("parallel", "arbitrary"),
                                             vmem_limit_bytes=VMEM_LIMIT),
        name=name,
    )(bq, bk, bv, blf, cq, ck, cv, gate, gate, s0b, s0c,
      jnp.tile(hgain, REC_HEADS).reshape(1, B_WIDTH), jnp.tile(rgain, REC_HEADS).reshape(1, C_WIDTH),
      tri, dmat, inner, kscale, rdec, jnp.tile(_group_mean_matrix(B_WIDTH), (2, 1)))


def _out_kernel(x_ref, ya_ref, yb_ref, yc_ref, w_ref, y_ref):
    acc = jnp.dot(ya_ref[...], w_ref[:A_WIDTH, :], preferred_element_type=F32)
    acc = acc + jnp.dot(yb_ref[...], w_ref[A_WIDTH:A_WIDTH + B_WIDTH, :], preferred_element_type=F32)
    acc = acc + jnp.dot(yc_ref[...], w_ref[A_WIDTH + B_WIDTH:, :], preferred_element_type=F32)
    y_ref[...] = x_ref[...] + acc


def _out_call(x, ya, yb, yc, w_out16, tm, name):
    b, t, _ = x.shape
    grid, rows, _, const, flat, tm = _row_blocking(b, t, tm)
    args = (x, ya, yb, yc)
    if flat:
        args = tuple(a.reshape(b * t, a.shape[-1]) for a in args)
    out = pl.pallas_call(
        _out_kernel,
        grid=grid,
        in_specs=[rows(D_MODEL), rows(A_WIDTH), rows(B_WIDTH), rows(C_WIDTH), const((D_MODEL, D_MODEL))],
        out_specs=rows(D_MODEL),
        out_shape=jax.ShapeDtypeStruct(args[0].shape, F32),
        compiler_params=pltpu.CompilerParams(dimension_semantics=("parallel",) * len(grid),
                                             vmem_limit_bytes=VMEM_LIMIT),
        name=name,
    )(*args, w_out16)
    return out.reshape(b, t, D_MODEL) if flat else out


def _with_past(past_parts, new, s_pad):
    past = jnp.concatenate([p.astype(BF16) for p in past_parts], axis=-1)
    b, n_old, width = past.shape
    tail = jnp.zeros((b, s_pad - n_old - new.shape[1], width), BF16)
    return jnp.concatenate([past, new, tail], axis=1)


def _mixer_layer(x, pos0, past, s_hgrn, s_ret, layer, w_p, w_out16, norm_g, q_gain, k_gain,
                 lb_logits, hgain, rgain, tag):
    b, t, _ = x.shape
    pos = pos0 + jnp.arange(t)
    (q, k, v, kd3, vd3, iq, sg, ik, ikd3, gate, bq, bk, bv, blf, cq, ck, cv) = _proj_call(
        x, pos, norm_g, w_p, q_gain, k_gain, lb_logits, layer, 512)

    n_keys = t if past is None else past[0].shape[1] + t
    s_pad = -(-n_keys // LANES) * LANES
    if past is not None:
        pk, pv, pik = past
        kd3 = _with_past([pk[:, :, 0], pk[:, :, 0], pk[:, :, 1], pk[:, :, 1]], kd3, s_pad)
        vd3 = _with_past([pv[:, :, 0], pv[:, :, 0], pv[:, :, 1], pv[:, :, 1]], vd3, s_pad)
        ikd3 = _with_past([pik, pik], ikd3, s_pad)
    elif s_pad != n_keys:
        padw = ((0, 0), (0, s_pad - n_keys), (0, 0))
        kd3, vd3, ikd3 = jnp.pad(kd3, padw), jnp.pad(vd3, padw), jnp.pad(ikd3, padw)
    topk = min(TOPK_MAX, n_keys // 4)
    oa = _attention(q, iq, sg, kd3, vd3, ikd3, gate, topk=topk, qpos0=pos0, n_keys=n_keys, tag=tag)

    yb, yc, sb, sc = _rec_call(bq, bk, bv, blf, cq, ck, cv, gate, s_hgrn, s_ret, hgain, rgain, name=f"rec_{tag}")
    out = _out_call(x, oa, yb, yc, w_out16, 512, name=f"out_{tag}")
    return (out,
            (k.reshape(b, t, A_KV_HEADS, HEAD_DIM), v.reshape(b, t, A_KV_HEADS, HEAD_DIM), ik,
             _state_from_blockdiag(sb), _state_from_blockdiag(sc)))


def kernel(x_prompt, x_sample, cache_k, cache_v, cache_idx_k, state_hgrn, state_ret, norm_g, w_in,
           q_norm_g, k_norm_g, hgrn_lb_logits, hgrn_norm_g, ret_norm_g, w_out):
    depth = w_in.shape[0]
    bp, tp = x_prompt.shape[:2]
    bs, ts = x_sample.shape[:2]
    past = cache_k.shape[2]
    zero_state = jnp.zeros((bp, B_WIDTH, B_WIDTH), F32)

    yp, ys = x_prompt, x_sample
    outs_p, outs_s = [], []
    for l in range(depth):
        w_p = _prep_w_in(w_in[l])
        w_o = w_out[l].astype(BF16)
        args = (l, w_p, w_o, norm_g[l], q_norm_g[l], k_norm_g[l], hgrn_lb_logits, hgrn_norm_g[l], ret_norm_g[l])
        yp, st = _mixer_layer(yp, 0, None, zero_state, zero_state, *args, tag=f"p{l}")
        outs_p.append(st)
        ys, st = _mixer_layer(ys, past, (cache_k[l], cache_v[l], cache_idx_k[l]),
                              _state_to_blockdiag(state_hgrn[l]), _state_to_blockdiag(state_ret[l]), *args,
                              tag=f"s{l}")
        outs_s.append(st)

    def stack(outs, i, shape):
        return jnp.stack([o[i] for o in outs]).reshape(shape)

    return (yp, ys,
            stack(outs_p, 0, (depth, bp, tp, A_KV_HEADS, HEAD_DIM)),
            stack(outs_p, 1, (depth, bp, tp, A_KV_HEADS, HEAD_DIM)),
            stack(outs_p, 2, (depth, bp, tp, HEAD_DIM)),
            stack(outs_p, 3, (depth, bp, REC_HEADS, HEAD_DIM, HEAD_DIM)),
            stack(outs_p, 4, (depth, bp, REC_HEADS, HEAD_DIM, HEAD_DIM)),
            stack(outs_s, 0, (depth, bs, ts, A_KV_HEADS, HEAD_DIM)),
            stack(outs_s, 1, (depth, bs, ts, A_KV_HEADS, HEAD_DIM)),
            stack(outs_s, 2, (depth, bs, ts, HEAD_DIM)),
            stack(outs_s, 3, (depth, bs, REC_HEADS, HEAD_DIM, HEAD_DIM)),
            stack(outs_s, 4, (depth, bs, REC_HEADS, HEAD_DIM, HEAD_DIM)))
```

```python
import functools
import math

import numpy as np
import jax
import jax.numpy as jnp
from jax import lax
from jax.experimental import pallas as pl
from jax.experimental.pallas import tpu as pltpu

F32 = jnp.float32
BF16 = jnp.bfloat16
I32 = jnp.int32

D_MODEL = 1024
HEAD_DIM = 64
CHUNK = 64
A_WIDTH = 512
A_HEADS = 8
A_KV_HEADS = 2
KV_WIDTH = A_KV_HEADS * HEAD_DIM
IDX_HEADS = 8
IDX_W_SCALE = (IDX_HEADS * HEAD_DIM) ** -0.5
TOPK_MAX = 256
B_WIDTH = 256
C_WIDTH = 256
REC_HEADS = 4
ROPE_THETA = 10000.0
EPS = 1e-6
LANES = 128
VMEM_LIMIT = 48 * 1024 * 1024

_SIZES = (A_WIDTH, KV_WIDTH, KV_WIDTH, A_WIDTH, IDX_HEADS * HEAD_DIM, HEAD_DIM, IDX_HEADS,
          B_WIDTH, B_WIDTH, B_WIDTH, B_WIDTH, C_WIDTH, C_WIDTH, C_WIDTH, C_WIDTH)
_SPLIT_IDX = tuple(int(s) for s in np.cumsum(_SIZES)[:-1])

_AQ, _AK, _AV, _AG, _IQ, _IKW = 0, 512, 640, 768, 1280, 1792
_BQ, _BF, _BI, _BG, _CQ, _CK, _CV, _CG = 1920, 2176, 2432, 2688, 2944, 3200, 3456, 3712
_N_PROJ = 3968

INT_MIN = -(2 ** 31)
NEG_BIG = -1e30
POS_BIG = 1 << 20
LOG2_E = math.log2(math.e)


def _prep_w_in(w):
    (aq, ak, av, ag, iq, ik, iw, bq, bf, bi, bg, cq, ck, cv, cg) = jnp.split(w, _SPLIT_IDX, axis=1)
    pad = jnp.zeros((w.shape[0], LANES - HEAD_DIM - IDX_HEADS), w.dtype)
    return jnp.concatenate([aq, ak, av, ag, iq, ik, iw, pad, bq, bf, bi, bg, cq, ck, cv, cg],
                           axis=1).astype(BF16)


def _rope_tables(pos):
    half = HEAD_DIM // 2
    freqs = ROPE_THETA ** (-jnp.arange(half, dtype=F32) / half)
    ang = pos.astype(F32)[:, None] * freqs[None, :]
    cos, sin = jnp.cos(ang), jnp.sin(ang)
    cos64 = jnp.concatenate([cos, cos], axis=1)
    sin64 = jnp.concatenate([-sin, sin], axis=1)
    return jnp.concatenate([cos64, cos64], axis=1), jnp.concatenate([sin64, sin64], axis=1)


def _group_mean_matrix(width):
    idx = np.arange(width) // HEAD_DIM
    return jnp.asarray((idx[:, None] == idx[None, :]).astype(np.float32) / HEAD_DIM, BF16)


def _split2(x):
    hi = x.astype(BF16)
    lo = (x - hi.astype(F32)).astype(BF16)
    return hi, lo


def _nt_dot(a, b):
    return lax.dot_general(a, b, (((1,), (1,)), ((), ())), preferred_element_type=F32)


def _proj_kernel(x_ref, g_ref, w_ref, cos_ref, sin_ref, qg_ref, kg_ref, lb_ref, gm_ref, ex_ref,
                 q_out, k_out, v_out, kd_out, vd_out, iq_out, sg_out, ik_out, ikd_out, gate_out,
                 bq_out, bk_out, bv_out, blf_out, cq_out, ck_out, cv_out, *, layer):
    x = x_ref[...]
    tm = x.shape[0]
    ms = jnp.mean(x * x, axis=-1, keepdims=True)
    h = (x * lax.rsqrt(ms + EPS) * g_ref[...]).astype(BF16)
    cos = cos_ref[...]
    sin = sin_ref[...]
    lane = lax.broadcasted_iota(I32, (tm, LANES), 1)
    first_half = (lane % HEAD_DIM) < (HEAD_DIM // 2)
    low_head = lane < HEAD_DIM
    gmat = gm_ref[...]

    def proj(c0, width):
        return jnp.dot(h, w_ref[:, c0:c0 + width], preferred_element_type=F32)

    def rope(xb):
        swapped = jnp.where(first_half, pltpu.roll(xb, LANES - HEAD_DIM // 2, 1),
                            pltpu.roll(xb, HEAD_DIM // 2, 1))
        return xb * cos + swapped * sin

    def head_norm(xb, gain):
        hi, lo = _split2(xb * xb)
        msq = jnp.dot(jnp.concatenate([hi, lo], axis=1), gmat, preferred_element_type=F32)
        return xb * lax.rsqrt(msq + EPS) * gain

    def dup_heads(xb):
        other = pltpu.roll(xb, HEAD_DIM, 1)
        return jnp.where(low_head, xb, other), jnp.where(low_head, other, xb)

    ag = proj(_AG, A_WIDTH)
    bg = proj(_BG, B_WIDTH)
    cg = proj(_CG, C_WIDTH)
    bq = proj(_BQ, B_WIDTH)
    bf = proj(_BF, B_WIDTH)
    bi = proj(_BI, B_WIDTH)
    cq = proj(_CQ, C_WIDTH)
    ck = proj(_CK, C_WIDTH)
    cv = proj(_CV, C_WIDTH)
    akv = proj(_AK, 2 * KV_WIDTH)
    ikw = proj(_IKW, LANES)
    aq = proj(_AQ, A_WIDTH)
    iq = proj(_IQ, IDX_HEADS * HEAD_DIM)

    def silu(z):
        return z / (1.0 + jnp.exp(-z))

    gate_out[:, :A_WIDTH] = silu(ag).astype(BF16)
    gate_out[:, A_WIDTH:A_WIDTH + B_WIDTH] = silu(bg).astype(BF16)
    gate_out[:, A_WIDTH + B_WIDTH:] = silu(cg).astype(BF16)

    logits = lb_ref[...]
    e = jnp.exp(logits - jnp.max(logits, axis=0, keepdims=True))
    sm = e / jnp.sum(e, axis=0, keepdims=True)
    lb = jnp.zeros((1, B_WIDTH), F32)
    for j in range(1, layer + 1):
        lb = lb + sm[j:j + 1, :]
    f = lb + (1.0 - lb) / (1.0 + jnp.exp(-bf))
    bq_out[...] = bq
    bk_out[...] = (1.0 - lb) / (1.0 + jnp.exp(bf))
    bv_out[...] = bi
    blf_out[...] = jnp.log(f)

    for c in range(C_WIDTH // LANES):
        sl = slice(c * LANES, (c + 1) * LANES)
        cq_out[:, sl] = rope(cq[:, sl])
        ck_out[:, sl] = rope(ck[:, sl]) * (HEAD_DIM ** -0.5)
    cv_out[...] = cv

    v = akv[:, KV_WIDTH:]
    v_out[...] = v
    v0, v1 = dup_heads(v)
    vd_out[:, :LANES] = v0.astype(BF16)
    vd_out[:, LANES:] = v1.astype(BF16)
    ikr = rope(ikw)
    ik_out[...] = ikr[:, :HEAD_DIM]
    ikd_out[...] = dup_heads(ikr)[0].astype(BF16)
    sg_out[...] = jnp.where(ikw >= 0, 1.0, -1.0).astype(F32)

    for c in range(A_WIDTH // LANES):
        blk = rope(head_norm(aq[:, c * LANES:(c + 1) * LANES], qg_ref[...]))
        q_out[:, c * LANES:(c + 1) * LANES] = (blk * (HEAD_DIM ** -0.5 * LOG2_E)).astype(BF16)
    k = rope(head_norm(akv[:, :KV_WIDTH], kg_ref[...]))
    k_out[...] = k
    k0, k1 = dup_heads(k)
    kd_out[:, :LANES] = k0.astype(BF16)
    kd_out[:, LANES:] = k1.astype(BF16)

    whi, wlo = _split2(jnp.abs(ikw) * IDX_W_SCALE)
    wexp = jnp.dot(jnp.concatenate([whi, wlo], axis=1), ex_ref[...], preferred_element_type=F32)
    for c in range(IDX_HEADS * HEAD_DIM // LANES):
        sl = slice(c * LANES, (c + 1) * LANES)
        iq_out[:, sl] = (rope(iq[:, sl]) * wexp[:, sl]).astype(BF16)


def _row_blocking(b, t, tm):
    if t % tm == 0:
        grid = (b, t // tm)
        rows = lambda w: pl.BlockSpec((None, tm, w), lambda bi, j: (bi, j, 0))
        per_pos = lambda w: pl.BlockSpec((tm, w), lambda bi, j: (j, 0))
        const = lambda shape: pl.BlockSpec(shape, lambda bi, j: (0, 0))
        return grid, rows, per_pos, const, False, tm
    n = b * t
    tm = min(tm, n)
    assert n % tm == 0 and tm % t == 0
    grid = (n // tm,)
    rows = lambda w: pl.BlockSpec((tm, w), lambda r: (r, 0))
    per_pos = lambda w: pl.BlockSpec((tm, w), lambda r: (0, 0))
    const = lambda shape: pl.BlockSpec(shape, lambda r: (0, 0))
    return grid, rows, per_pos, const, True, tm


def _proj_call(x, pos, norm_g, w_p, q_gain, k_gain, lb_logits, layer, tm):
    b, t, _ = x.shape
    grid, rows, per_pos, const, flat, tm = _row_blocking(b, t, tm)
    cos, sin = _rope_tables(pos)
    if flat:
        x = x.reshape(b * t, D_MODEL)
        cos = jnp.tile(cos, (tm // t, 1))
        sin = jnp.tile(sin, (tm // t, 1))
    expand = np.zeros((LANES, IDX_HEADS * HEAD_DIM), np.float32)
    for hd in range(IDX_HEADS):
        expand[HEAD_DIM + hd, hd * HEAD_DIM:(hd + 1) * HEAD_DIM] = 1.0
    widths = [(A_WIDTH, BF16), (KV_WIDTH, F32), (KV_WIDTH, F32), (2 * KV_WIDTH, BF16), (2 * KV_WIDTH, BF16),
              (IDX_HEADS * HEAD_DIM, BF16), (LANES, F32), (HEAD_DIM, F32), (LANES, BF16), (D_MODEL, BF16),
              (B_WIDTH, F32), (B_WIDTH, F32), (B_WIDTH, F32), (B_WIDTH, F32),
              (C_WIDTH, F32), (C_WIDTH, F32), (C_WIDTH, F32)]
    lead = (b * t,) if flat else (b, t)
    outs = pl.pallas_call(
        functools.partial(_proj_kernel, layer=layer),
        grid=grid,
        in_specs=[rows(D_MODEL),
                  const((1, D_MODEL)),
                  const((D_MODEL, _N_PROJ)),
                  per_pos(LANES),
                  per_pos(LANES),
                  const((1, LANES)),
                  const((1, LANES)),
                  const(lb_logits.shape),
                  const((2 * LANES, LANES)),
                  const((2 * LANES, IDX_HEADS * HEAD_DIM))],
        out_specs=[rows(w) for w, _ in widths],
        out_shape=[jax.ShapeDtypeStruct(lead + (w,), dt) for w, dt in widths],
        compiler_params=pltpu.CompilerParams(dimension_semantics=("parallel",) * len(grid),
                                             vmem_limit_bytes=VMEM_LIMIT),
        name=f"proj_l{layer}",
    )(x, norm_g.reshape(1, D_MODEL), w_p, cos, sin,
      jnp.tile(q_gain, 2).reshape(1, LANES), jnp.tile(k_gain, 2).reshape(1, LANES),
      lb_logits, jnp.tile(_group_mean_matrix(LANES), (2, 1)), jnp.tile(jnp.asarray(expand, BF16), (2, 1)))
    return [o.reshape(b, t, o.shape[-1]) for o in outs] if flat else outs


SEARCH_INTERP_STEPS = 9
SEARCH_WALK_STEPS = 3
SEARCH_RANK_BIAS = 1.5
SEARCH_EDGE = 0.05
TIE_WALK_STEPS = (4, 12)


def _normal_upper_quantile(r):
    rr = jnp.minimum(r, 1.0 - r)
    t = jnp.sqrt(-2.0 * jnp.log(rr))
    z = t - ((0.010328 * t + 0.802853) * t + 2.515517) / (((0.001308 * t + 0.189269) * t + 1.432788) * t + 1.0)
    return jnp.where(r <= 0.5, z, -z)


def _select_topk(xm_scr, sum1_scr, sum2_scr, lim, topk, key_scr, thr_scr, lim_scr, bias_scr):
    tq, s = xm_scr.shape
    kf = float(topk)
    nf = lim.astype(F32)
    trivial = lim <= topk

    def count_gt(p):
        return jnp.sum(jnp.where(xm_scr[...] > p, 1.0, 0.0), axis=1, keepdims=True)

    def tally(t):
        xv = xm_scr[...]
        return (jnp.sum(jnp.where(xv > t, 1.0, 0.0), axis=1, keepdims=True),
                jnp.sum(jnp.where(xv == t, 1.0, 0.0), axis=1, keepdims=True))

    mu = jnp.sum(sum1_scr[...], axis=1, keepdims=True) / nf
    var = jnp.sum(sum2_scr[...], axis=1, keepdims=True) / nf - mu * mu
    sd = jnp.sqrt(jnp.maximum(var, 1e-30))
    target = kf + SEARCH_RANK_BIAS
    lo, hi = mu - 8.0 * sd, mu + 8.0 * sd
    clo, chi = nf, jnp.zeros_like(nf)
    p = mu + _normal_upper_quantile(jnp.clip(target / nf, 1e-6, 1.0 - 1e-6)) * sd
    lo_counted = jnp.zeros((tq, 1), jnp.bool_)
    for it in range(SEARCH_INTERP_STEPS):
        c = count_gt(p)
        above = c >= kf
        lo = jnp.where(above, p, lo)
        clo = jnp.where(above, c, clo)
        lo_counted = lo_counted | above
        hi = jnp.where(above, hi, p)
        chi = jnp.where(above, chi, c)
        frac = jnp.clip((clo - target) / (clo - chi), SEARCH_EDGE, 1.0 - SEARCH_EDGE)
        p = lo + (hi - lo) * frac

    cur = lo
    thr = lo
    cnt_prev = clo
    cnt_gt = clo
    cnt_eq = jnp.zeros_like(clo)
    done = lo_counted & (clo == kf)
    for it in range(SEARCH_WALK_STEPS):
        xv = xm_scr[...]
        cur = jnp.min(jnp.where(xv > cur, xv, -NEG_BIG), axis=1, keepdims=True)
        c = count_gt(cur)
        hit = (c <= kf) & lo_counted & jnp.logical_not(done)
        thr = jnp.where(hit, cur, thr)
        cnt_gt = jnp.where(hit, c, cnt_gt)
        cnt_eq = jnp.where(hit, cnt_prev - c, cnt_eq)
        done = done | hit
        cnt_prev = c
    thr = jnp.where(trivial, 0.1 * NEG_BIG, thr)
    cnt_gt = jnp.where(trivial, nf, cnt_gt)
    cnt_eq = jnp.where(trivial, 0.0, cnt_eq)
    thr_scr[...] = jnp.broadcast_to(thr, (tq, LANES))
    sum1_scr[...] = jnp.broadcast_to(cnt_gt, (tq, LANES))
    sum2_scr[...] = jnp.broadcast_to(cnt_eq, (tq, LANES))
    proven = trivial | done

    @pl.when(jnp.min(jnp.where(proven, 1.0, 0.0)) < 0.5)
    def _():
        bits = lax.bitcast_convert_type(xm_scr[...], I32)
        key_scr[...] = bits ^ ((bits >> 31) & 0x7FFFFFFF)

        def bit_step(j, tu):
            cand = tu | lax.shift_left(jnp.int32(1), 31 - j)
            cnt = jnp.sum(jnp.where(key_scr[...] >= (cand ^ INT_MIN), 1.0, 0.0), axis=1, keepdims=True)
            return jnp.where(cnt >= kf, cand, tu)

        tkey = lax.fori_loop(0, 32, bit_step, jnp.zeros((tq, 1), I32)) ^ INT_MIN
        tval = lax.bitcast_convert_type(tkey ^ ((tkey >> 31) & 0x7FFFFFFF), F32)
        texact = jnp.where(trivial, 0.1 * NEG_BIG, tval)
        thr_scr[...] = jnp.broadcast_to(texact, (tq, LANES))
        cg, ce = tally(texact)
        sum1_scr[...] = jnp.broadcast_to(cg, (tq, LANES))
        sum2_scr[...] = jnp.broadcast_to(ce, (tq, LANES))

    thr = thr_scr[:, :1]
    cnt_gt = sum1_scr[:, :1]
    cnt_eq = sum2_scr[:, :1]
    xv = xm_scr[...]
    gt = xv > thr
    eq = xv == thr
    want_eq = kf - cnt_gt

    all_or_none = jnp.where((want_eq >= cnt_eq) & (want_eq > 0.0), POS_BIG, 0)
    lim_scr[...] = jnp.broadcast_to(all_or_none, (tq, LANES))
    partial = (want_eq > 0.0) & (want_eq < cnt_eq)
    kpos = lax.broadcasted_iota(I32, (tq, s), 1)

    @pl.when(jnp.max(jnp.where(partial, 1.0, 0.0)) > 0.5)
    def _():
        eqpos = jnp.where(eq, kpos, POS_BIG)
        eqposf = eqpos.astype(F32)

        def walk(first, last, cur, xlim):
            for it in range(first, last):
                cur = jnp.min(jnp.where(eqposf > cur, eqposf, float(POS_BIG)), axis=1, keepdims=True)
                xlim = jnp.where(partial & (want_eq == float(it)), cur.astype(I32), xlim)
            return cur, xlim

        cur, xlim = walk(0, TIE_WALK_STEPS[0], jnp.full((tq, 1), -1.0, F32), all_or_none)
        lim_scr[...] = jnp.broadcast_to(xlim, (tq, LANES))

        @pl.when(jnp.max(jnp.where(partial & (want_eq >= float(TIE_WALK_STEPS[0])), 1.0, 0.0)) > 0.5)
        def _():
            _, xlim2 = walk(TIE_WALK_STEPS[0], TIE_WALK_STEPS[1], cur, xlim)
            lim_scr[...] = jnp.broadcast_to(xlim2, (tq, LANES))

            @pl.when(jnp.max(jnp.where(partial & (want_eq >= float(TIE_WALK_STEPS[1])), 1.0, 0.0)) > 0.5)
            def _():
                nbits = max(1, int(s).bit_length())

                def pos_step(j, xl):
                    cand = xl | lax.shift_left(jnp.int32(1), nbits - 1 - j)
                    cnt = jnp.sum(jnp.where(eqpos < cand, 1.0, 0.0), axis=1, keepdims=True)
                    return jnp.where(cnt <= want_eq, cand, xl)

                lim_scr[...] = jnp.broadcast_to(lax.fori_loop(0, nbits, pos_step, jnp.zeros((tq, 1), I32)),
                                                (tq, LANES))

    sel = gt | (eq & (kpos < lim_scr[:, :1]))
    bias_scr[...] = jnp.where(sel, 0.0, NEG_BIG).astype(bias_scr.dtype)


def _attn_kernel(*refs, topk, tpos0, n_keys):
    q_ref, iq_ref, sg_ref, kd_ref, vd_ref, ikd_ref, gate_ref = refs[:7]
    o_ref, xm_scr, key_scr, bias_scr, thr_scr, lim_scr, sum1_scr, sum2_scr = refs[-8:]
    tq = q_ref.shape[1]
    s = kd_ref.shape[1]
    lane = lax.broadcasted_iota(I32, (tq, LANES), 1)
    halves = (lane < HEAD_DIM, lane >= HEAD_DIM)

    tpos = tpos0 + lax.broadcasted_iota(I32, (tq, 1), 0)
    lim = jnp.minimum((tpos // CHUNK + 1) * CHUNK, n_keys)
    search = min(s, n_keys) > topk

    rb = min(tq, LANES)
    key_tile = 2 * LANES
    lane_rb = lax.broadcasted_iota(I32, (rb, LANES), 1)
    halves_rb = (lane_rb < HEAD_DIM, lane_rb >= HEAD_DIM)
    for r in range(tq // rb):
        rs = slice(r * rb, (r + 1) * rb)
        sg = sg_ref[0, rs, :]
        qms, sgs = [], []
        for hd in range(IDX_HEADS):
            qb = iq_ref[0, rs, (hd // 2) * LANES:(hd // 2 + 1) * LANES]
            qms.append(jnp.where(halves_rb[hd % 2], qb, jnp.zeros_like(qb)))
            sgs.append(sg[:, HEAD_DIM + hd:HEAD_DIM + hd + 1])
        acc1 = jnp.zeros((rb, LANES), F32)
        acc2 = jnp.zeros((rb, LANES), F32)
        lim_first = min(((tpos0 + r * rb) // CHUNK + 1) * CHUNK, n_keys)
        lim_last = min(((tpos0 + (r + 1) * rb - 1) // CHUNK + 1) * CHUNK, n_keys)
        for c0 in range(0, s, key_tile):
            w = min(key_tile, s - c0)
            if c0 >= lim_last:
                xm_scr[rs, c0:c0 + w] = jnp.full((rb, w), NEG_BIG, F32)
                continue
            ikd_t = ikd_ref[0, c0:c0 + w, :]
            acc = jnp.zeros((rb, w), F32)
            for hd in range(IDX_HEADS):
                acc = acc + sgs[hd] * jnp.maximum(_nt_dot(qms[hd], ikd_t), 0.0)
            if c0 + w <= lim_first:
                xm_scr[rs, c0:c0 + w] = acc
                sc0 = acc
            else:
                adm_t = (c0 + lax.broadcasted_iota(I32, (rb, w), 1)) < lim[rs]
                xm_scr[rs, c0:c0 + w] = jnp.where(adm_t, acc, NEG_BIG)
                sc0 = jnp.where(adm_t, acc, 0.0)
            if search:
                for l0 in range(0, w, LANES):
                    part = sc0[:, l0:l0 + LANES]
                    acc1 = acc1 + part
                    acc2 = acc2 + part * part
        if search:
            sum1_scr[rs, :] = acc1
            sum2_scr[rs, :] = acc2

    logits = []
    for c in range(A_WIDTH // LANES):
        g = (2 * c) // (A_HEADS // A_KV_HEADS)
        qb = q_ref[0, :, c * LANES:(c + 1) * LANES]
        for e in range(2):
            qm = jnp.where(halves[e], qb, jnp.zeros_like(qb))
            logits.append(_nt_dot(qm, kd_ref[0, :, g * LANES:(g + 1) * LANES]))

    if search:
        _select_topk(xm_scr, sum1_scr, sum2_scr, lim, topk, key_scr, thr_scr, lim_scr, bias_scr)
    else:
        bias_scr[...] = jnp.where(lax.broadcasted_iota(I32, (tq, s), 1) < lim, 0.0, NEG_BIG).astype(BF16)

    ones = jnp.ones((s, LANES), BF16)
    for g in range(A_KV_HEADS):
        vd1 = jnp.concatenate([vd_ref[0, :, g * LANES:(g + 1) * LANES], ones], axis=1)
        for c in range(g * 2, g * 2 + 2):
            outs = []
            for e in range(2):
                lg = logits[2 * c + e].astype(BF16) + bias_scr[...]
                p = jnp.exp2(lg - jnp.max(lg, axis=1, keepdims=True))
                ol = jnp.dot(p, vd1, preferred_element_type=F32)
                outs.append(ol[:, :LANES] / ol[:, LANES:])
            gate = gate_ref[0, :, c * LANES:(c + 1) * LANES].astype(F32)
            o_ref[0, :, c * LANES:(c + 1) * LANES] = (jnp.where(halves[0], outs[0], outs[1]) * gate).astype(BF16)


def _attn_call(q, iq, sg, kd, vd, ikd, gate, prev, *, topk, qpos0, n_keys, tq, qblock, s_blk, name):
    b, t, _ = q.shape
    qmap = lambda bi: (bi, qblock, 0)
    kmap = lambda bi: (bi, 0, 0)
    return pl.pallas_call(
        functools.partial(_attn_kernel, topk=topk, tpos0=qpos0 + qblock * tq, n_keys=n_keys),
        grid=(b,),
        in_specs=[pl.BlockSpec((1, tq, A_WIDTH), qmap),
                  pl.BlockSpec((1, tq, IDX_HEADS * HEAD_DIM), qmap),
                  pl.BlockSpec((1, tq, LANES), qmap),
                  pl.BlockSpec((1, s_blk, 2 * KV_WIDTH), kmap),
                  pl.BlockSpec((1, s_blk, 2 * KV_WIDTH), kmap),
                  pl.BlockSpec((1, s_blk, LANES), kmap),
                  pl.BlockSpec((1, tq, A_WIDTH), qmap),
                  pl.BlockSpec(memory_space=pl.ANY)],
        out_specs=pl.BlockSpec((1, tq, A_WIDTH), qmap),
        out_shape=jax.ShapeDtypeStruct((b, t, A_WIDTH), BF16),
        input_output_aliases={7: 0},
        scratch_shapes=[pltpu.VMEM((tq, s_blk), F32), pltpu.VMEM((tq, s_blk), I32), pltpu.VMEM((tq, s_blk), BF16),
                        pltpu.VMEM((tq, LANES), F32), pltpu.VMEM((tq, LANES), I32),
                        pltpu.VMEM((tq, LANES), F32), pltpu.VMEM((tq, LANES), F32)],
        compiler_params=pltpu.CompilerParams(dimension_semantics=("parallel",),
                                             vmem_limit_bytes=VMEM_LIMIT),
        name=name,
    )(q, iq, sg, kd, vd, ikd, gate, prev)


def _attention(q, iq, sg, kd, vd, ikd, gate, *, topk, qpos0, n_keys, tag):
    b, t, _ = q.shape
    tq = min(t, 256)
    out = jnp.zeros((b, t, A_WIDTH), BF16)
    for qblock in range(t // tq):
        last_pos = qpos0 + (qblock + 1) * tq - 1
        visible = min((last_pos // CHUNK + 1) * CHUNK, n_keys)
        s_blk = min(-(-visible // LANES) * LANES, kd.shape[1])
        out = _attn_call(q, iq, sg, kd, vd, ikd, gate, out, topk=topk, qpos0=qpos0, n_keys=n_keys, tq=tq,
                         qblock=qblock, s_blk=s_blk, name=f"attn_{tag}_q{qblock}")
    return out


def _retention_log_decay():
    return jnp.log(1.0 - jnp.exp(jnp.linspace(math.log(1.0 / 32), math.log(1.0 / 512), REC_HEADS, dtype=F32)))


def _rec_kernel(bq_ref, bk_ref, bv_ref, blf_ref, cq_ref, ck_ref, cv_ref, gb_ref, gc_ref, s0b_ref, s0c_ref,
                hg_ref, rg_ref, tri_ref, dmat_ref, inner_ref, kscale_ref, rdec_ref, gm_ref,
                yb_ref, yc_ref, sb_out, sc_out, sb_scr, sc_scr, *, chunk, n_chunks):
    j = pl.program_id(1)
    width = B_WIDTH

    @pl.when(j == 0)
    def _():
        sb_scr[...] = s0b_ref[0]
        sc_scr[...] = s0c_ref[0]

    rowh = lax.broadcasted_iota(I32, (width, width), 0) // HEAD_DIM
    colh = lax.broadcasted_iota(I32, (width, width), 1) // HEAD_DIM
    block_diag = rowh == colh
    lane_h = lax.broadcasted_iota(I32, (chunk, width), 1) // HEAD_DIM
    causal = (lax.broadcasted_iota(I32, (chunk, chunk), 0)
              >= lax.broadcasted_iota(I32, (chunk, chunk), 1))
    gmat = gm_ref[...]
    tri = tri_ref[...]
    tn = (((0,), (0,)), ((), ()))

    def masked_heads(x):
        return [jnp.where(lane_h == hd, x, 0.0).astype(BF16) for hd in range(REC_HEADS)]

    heads = range(REC_HEADS)
    chunks = range(n_chunks)
    rows = [pl.ds(c * chunk, chunk) for c in chunks]

    bcs = []
    for c in chunks:
        lf = blf_ref[0, rows[c], :]
        a1 = lf.astype(BF16)
        r1 = lf - a1.astype(F32)
        a2 = r1.astype(BF16)
        a3 = (r1 - a2.astype(F32)).astype(BF16)
        bcs.append(jnp.dot(tri, jnp.concatenate([a1, a2, a3], axis=0), preferred_element_type=F32))

    att_b, att_c, upd_b, upd_c, qs_b, qs_c, dec_b, vb_b, vb_c = ([] for _ in range(9))
    for c in chunks:
        bc = bcs[c]
        mid = bc[chunk // 2 - 1:chunk // 2, :]
        last = bc[chunk - 1:chunk, :]
        q = bq_ref[0, rows[c], :]
        k = bk_ref[0, rows[c], :]
        vb = bv_ref[0, rows[c], :].astype(BF16)
        kt = (k * jnp.exp(mid - bc)).astype(BF16)
        att_b.append([_nt_dot(qh, kt) for qh in masked_heads(q * jnp.exp(bc - mid))])
        upd_b.append(lax.dot_general(vb, (k * jnp.exp(last - bc)).astype(BF16), tn, preferred_element_type=F32))
        qs_b.append((q * jnp.exp(bc)).astype(BF16))
        dec_b.append(jnp.exp(last))
        vb_b.append(vb)
        q = cq_ref[0, rows[c], :]
        k = ck_ref[0, rows[c], :]
        vb = cv_ref[0, rows[c], :].astype(BF16)
        kb = k.astype(BF16)
        att_c.append([_nt_dot(qh, kb) for qh in masked_heads(q)])
        upd_c.append(lax.dot_general(vb, (k * kscale_ref[...]).astype(BF16), tn, preferred_element_type=F32))
        qs_c.append((q * inner_ref[...]).astype(BF16))
        vb_c.append(vb)

    st = sb_scr[...]
    rt = sc_scr[...]
    o_b, o_c = [], []
    for c in chunks:
        o_b.append(_nt_dot(qs_b[c], st.astype(BF16)))
        st = st * dec_b[c] + jnp.where(block_diag, upd_b[c], 0.0)
        o_c.append(_nt_dot(qs_c[c], rt.astype(BF16)))
        rt = rt * rdec_ref[...] + jnp.where(block_diag, upd_c[c], 0.0)
    sb_scr[...] = st
    sc_scr[...] = rt

    def blockdiag_rows(v16):
        return jnp.concatenate([jnp.where(lane_h == hd, v16, jnp.zeros_like(v16)) for hd in heads], axis=0)

    prod_b = [jnp.dot(jnp.concatenate([jnp.where(causal, att_b[c][hd], 0.0) for hd in heads], axis=1).astype(BF16),
                      blockdiag_rows(vb_b[c]), preferred_element_type=F32) for c in chunks]
    prod_c = [jnp.dot(jnp.concatenate([att_c[c][hd] * dmat_ref[hd * chunk:(hd + 1) * chunk, :] for hd in heads],
                                      axis=1).astype(BF16),
                      blockdiag_rows(vb_c[c]), preferred_element_type=F32) for c in chunks]
    outs = []
    for c in chunks:
        outs.append(o_b[c] + prod_b[c])
        outs.append(o_c[c] + prod_c[c])

    means = []
    for o in outs:
        hi, lo = _split2(o * o)
        means.append(jnp.dot(jnp.concatenate([hi, lo], axis=1), gmat, preferred_element_type=F32))
    for c in chunks:
        yb = outs[2 * c] * lax.rsqrt(means[2 * c] + EPS) * hg_ref[...]
        yc = outs[2 * c + 1] * lax.rsqrt(means[2 * c + 1] + EPS) * rg_ref[...]
        yb_ref[0, rows[c], :] = (yb * gb_ref[0, rows[c], :].astype(F32)).astype(BF16)
        yc_ref[0, rows[c], :] = (yc * gc_ref[0, rows[c], :].astype(F32)).astype(BF16)

    @pl.when(j == pl.num_programs(1) - 1)
    def _():
        sb_out[0] = sb_scr[...]
        sc_out[0] = sc_scr[...]


def _state_to_blockdiag(s):
    b = s.shape[0]
    eye = jnp.eye(REC_HEADS, dtype=s.dtype)
    return jnp.einsum('bhkv,hg->bhvgk', s, eye).reshape(b, B_WIDTH, B_WIDTH)


def _state_from_blockdiag(sbd):
    b = sbd.shape[0]
    s5 = sbd.reshape(b, REC_HEADS, HEAD_DIM, REC_HEADS, HEAD_DIM)
    return jnp.einsum('bhvhk->bhkv', s5)


def _rec_call(bq, bk, bv, blf, cq, ck, cv, gate, s0b, s0c, hgain, rgain, name):
    b, t, _ = bq.shape
    chunk = min(CHUNK, t)
    tb = min(t, 16 * chunk)
    n_chunks = tb // chunk
    lg = _retention_log_decay()
    n = jnp.arange(chunk, dtype=F32)
    diff = n[:, None] - n[None, :]
    dmat = jnp.where(diff >= 0, jnp.exp(jnp.where(diff >= 0, diff, 0.0)[None] * lg[:, None, None]), 0.0)
    dmat = dmat.reshape(REC_HEADS * chunk, chunk)
    per_lane = lambda a: jnp.repeat(a, HEAD_DIM, axis=-1)
    inner = per_lane(jnp.exp((n[:, None] + 1.0) * lg[None, :]))
    kscale = per_lane(jnp.exp((chunk - 1.0 - n)[:, None] * lg[None, :]))
    rdec = per_lane(jnp.exp(chunk * lg)[None, :])
    tri = jnp.tile(jnp.tril(jnp.ones((chunk, chunk), F32)), (1, 3)).astype(BF16)
    seq = lambda bi, j: (bi, j, 0)
    per_b = lambda bi, j: (bi, 0, 0)
    const = lambda bi, j: (0, 0)
    stream = pl.BlockSpec((1, tb, B_WIDTH), seq)
    state = pl.BlockSpec((1, B_WIDTH, B_WIDTH), per_b)
    return pl.pallas_call(
        functools.partial(_rec_kernel, chunk=chunk, n_chunks=n_chunks),
        grid=(b, t // tb),
        in_specs=[stream] * 7 + [pl.BlockSpec((1, tb, B_WIDTH), lambda bi, j: (bi, j, A_WIDTH // B_WIDTH)),
                                 pl.BlockSpec((1, tb, C_WIDTH), lambda bi, j: (bi, j, (A_WIDTH + B_WIDTH) // C_WIDTH)),
                                 state, state,
                                 pl.BlockSpec((1, B_WIDTH), const), pl.BlockSpec((1, B_WIDTH), const),
                                 pl.BlockSpec((chunk, 3 * chunk), const),
                                 pl.BlockSpec((REC_HEADS * chunk, chunk), const),
                                 pl.BlockSpec((chunk, B_WIDTH), const), pl.BlockSpec((chunk, B_WIDTH), const),
                                 pl.BlockSpec((1, B_WIDTH), const),
                                 pl.BlockSpec((2 * B_WIDTH, B_WIDTH), const)],
        out_specs=[stream, stream, state, state],
        out_shape=[jax.ShapeDtypeStruct((b, t, B_WIDTH), BF16), jax.ShapeDtypeStruct((b, t, C_WIDTH), BF16),
                   jax.ShapeDtypeStruct((b, B_WIDTH, B_WIDTH), F32),
                   jax.ShapeDtypeStruct((b, C_WIDTH, C_WIDTH), F32)],
        scratch_shapes=[pltpu.VMEM((B_WIDTH, B_WIDTH), F32), pltpu.VMEM((C_WIDTH, C_WIDTH), F32)],
        compiler_params=pltpu.CompilerParams(dimension_semantics=("parallel", "arbitrary"),
                                             vmem_limit_bytes=VMEM_LIMIT),
        name=name,
    )(bq, bk, bv, blf, cq, ck, cv, gate, gate, s0b, s0c,
      jnp.tile(hgain, REC_HEADS).reshape(1, B_WIDTH), jnp.tile(rgain, REC_HEADS).reshape(1, C_WIDTH),
      tri, dmat, inner, kscale, rdec, jnp.tile(_group_mean_matrix(B_WIDTH), (2, 1)))


def _out_kernel(x_ref, ya_ref, yb_ref, yc_ref, w_ref, y_ref):
    acc = jnp.dot(ya_ref[...], w_ref[:A_WIDTH, :], preferred_element_type=F32)
    acc = acc + jnp.dot(yb_ref[...], w_ref[A_WIDTH:A_WIDTH + B_WIDTH, :], preferred_element_type=F32)
    acc = acc + jnp.dot(yc_ref[...], w_ref[A_WIDTH + B_WIDTH:, :], preferred_element_type=F32)
    y_ref[...] = x_ref[...] + acc


def _out_call(x, ya, yb, yc, w_out16, tm, name):
    b, t, _ = x.shape
    grid, rows, _, const, flat, tm = _row_blocking(b, t, tm)
    args = (x, ya, yb, yc)
    if flat:
        args = tuple(a.reshape(b * t, a.shape[-1]) for a in args)
    out = pl.pallas_call(
        _out_kernel,
        grid=grid,
        in_specs=[rows(D_MODEL), rows(A_WIDTH), rows(B_WIDTH), rows(C_WIDTH), const((D_MODEL, D_MODEL))],
        out_specs=rows(D_MODEL),
        out_shape=jax.ShapeDtypeStruct(args[0].shape, F32),
        compiler_params=pltpu.CompilerParams(dimension_semantics=("parallel",) * len(grid),
                                             vmem_limit_bytes=VMEM_LIMIT),
        name=name,
    )(*args, w_out16)
    return out.reshape(b, t, D_MODEL) if flat else out


def _with_past(past_parts, new, s_pad):
    past = jnp.concatenate([p.astype(BF16) for p in past_parts], axis=-1)
    b, n_old, width = past.shape
    tail = jnp.zeros((b, s_pad - n_old - new.shape[1], width), BF16)
    return jnp.concatenate([past, new, tail], axis=1)


def _mixer_layer(x, pos0, past, s_hgrn, s_ret, layer, w_p, w_out16, norm_g, q_gain, k_gain,
                 lb_logits, hgain, rgain, tag):
    b, t, _ = x.shape
    pos = pos0 + jnp.arange(t)
    (q, k, v, kd3, vd3, iq, sg, ik, ikd3, gate, bq, bk, bv, blf, cq, ck, cv) = _proj_call(
        x, pos, norm_g, w_p, q_gain, k_gain, lb_logits, layer, 512)

    n_keys = t if past is None else past[0].shape[1] + t
    s_pad = -(-n_keys // LANES) * LANES
    if past is not None:
        pk, pv, pik = past
        kd3 = _with_past([pk[:, :, 0], pk[:, :, 0], pk[:, :, 1], pk[:, :, 1]], kd3, s_pad)
        vd3 = _with_past([pv[:, :, 0], pv[:, :, 0], pv[:, :, 1], pv[:, :, 1]], vd3, s_pad)
        ikd3 = _with_past([pik, pik], ikd3, s_pad)
    elif s_pad != n_keys:
        padw = ((0, 0), (0, s_pad - n_keys), (0, 0))
        kd3, vd3, ikd3 = jnp.pad(kd3, padw), jnp.pad(vd3, padw), jnp.pad(ikd3, padw)
    topk = min(TOPK_MAX, n_keys // 4)
    oa = _attention(q, iq, sg, kd3, vd3, ikd3, gate, topk=topk, qpos0=pos0, n_keys=n_keys, tag=tag)

    yb, yc, sb, sc = _rec_call(bq, bk, bv, blf, cq, ck, cv, gate, s_hgrn, s_ret, hgain, rgain, name=f"rec_{tag}")
    out = _out_call(x, oa, yb, yc, w_out16, 512, name=f"out_{tag}")
    return (out,
            (k.reshape(b, t, A_KV_HEADS, HEAD_DIM), v.reshape(b, t, A_KV_HEADS, HEAD_DIM), ik,
             _state_from_blockdiag(sb), _state_from_blockdiag(sc)))


def kernel(x_prompt, x_sample, cache_k, cache_v, cache_idx_k, state_hgrn, state_ret, norm_g, w_in,
           q_norm_g, k_norm_g, hgrn_lb_logits, hgrn_norm_g, ret_norm_g, w_out):
    depth = w_in.shape[0]
    bp, tp = x_prompt.shape[:2]
    bs, ts = x_sample.shape[:2]
    past = cache_k.shape[2]
    zero_state = jnp.zeros((bp, B_WIDTH, B_WIDTH), F32)

    yp, ys = x_prompt, x_sample
    outs_p, outs_s = [], []
    for l in range(depth):
        w_p = _prep_w_in(w_in[l])
        w_o = w_out[l].astype(BF16)
        args = (l, w_p, w_o, norm_g[l], q_norm_g[l], k_norm_g[l], hgrn_lb_logits, hgrn_norm_g[l], ret_norm_g[l])
        yp, st = _mixer_layer(yp, 0, None, zero_state, zero_state, *args, tag=f"p{l}")
        outs_p.append(st)
        ys, st = _mixer_layer(ys, past, (cache_k[l], cache_v[l], cache_idx_k[l]),
                              _state_to_blockdiag(state_hgrn[l]), _state_to_blockdiag(state_ret[l]), *args,
                              tag=f"s{l}")
        outs_s.append(st)

    def stack(outs, i, shape):
        return jnp.stack([o[i] for o in outs]).reshape(shape)

    return (yp, ys,
            stack(outs_p, 0, (depth, bp, tp, A_KV_HEADS, HEAD_DIM)),
            stack(outs_p, 1, (depth, bp, tp, A_KV_HEADS, HEAD_DIM)),
            stack(outs_p, 2, (depth, bp, tp, HEAD_DIM)),
            stack(outs_p, 3, (depth, bp, REC_HEADS, HEAD_DIM, HEAD_DIM)),
            stack(outs_p, 4, (depth, bp, REC_HEADS, HEAD_DIM, HEAD_DIM)),
            stack(outs_s, 0, (depth, bs, ts, A_KV_HEADS, HEAD_DIM)),
            stack(outs_s, 1, (depth, bs, ts, A_KV_HEADS, HEAD_DIM)),
            stack(outs_s, 2, (depth, bs, ts, HEAD_DIM)),
            stack(outs_s, 3, (depth, bs, REC_HEADS, HEAD_DIM, HEAD_DIM)),
            stack(outs_s, 4, (depth, bs, REC_HEADS, HEAD_DIM, HEAD_DIM)))
```

```python
import functools
import math

import numpy as np
import jax
import jax.numpy as jnp
from jax import lax
from jax.experimental import pallas as pl
from jax.experimental.pallas import tpu as pltpu

F32 = jnp.float32
BF16 = jnp.bfloat16
I32 = jnp.int32

D_MODEL = 1024
HEAD_DIM = 64
CHUNK = 64
A_WIDTH = 512
A_HEADS = 8
A_KV_HEADS = 2
KV_WIDTH = A_KV_HEADS * HEAD_DIM
IDX_HEADS = 8
IDX_W_SCALE = (IDX_HEADS * HEAD_DIM) ** -0.5
TOPK_MAX = 256
B_WIDTH = 256
C_WIDTH = 256
REC_HEADS = 4
ROPE_THETA = 10000.0
EPS = 1e-6
LANES = 128
VMEM_LIMIT = 48 * 1024 * 1024

_SIZES = (A_WIDTH, KV_WIDTH, KV_WIDTH, A_WIDTH, IDX_HEADS * HEAD_DIM, HEAD_DIM, IDX_HEADS,
          B_WIDTH, B_WIDTH, B_WIDTH, B_WIDTH, C_WIDTH, C_WIDTH, C_WIDTH, C_WIDTH)
_SPLIT_IDX = tuple(int(s) for s in np.cumsum(_SIZES)[:-1])

_AQ, _AK, _AV, _AG, _IQ, _IKW = 0, 512, 640, 768, 1280, 1792
_BQ, _BF, _BI, _BG, _CQ, _CK, _CV, _CG = 1920, 2176, 2432, 2688, 2944, 3200, 3456, 3712
_N_PROJ = 3968

NEG_BIG = -1e30
POS_BIG = 1 << 20
LOG2_E = math.log2(math.e)


def _prep_w_in(w):
    (aq, ak, av, ag, iq, ik, iw, bq, bf, bi, bg, cq, ck, cv, cg) = jnp.split(w, _SPLIT_IDX, axis=1)
    pad = jnp.zeros((w.shape[0], LANES - HEAD_DIM - IDX_HEADS), w.dtype)
    return jnp.concatenate([aq, ak, av, ag, iq, ik, iw, pad, bq, bf, bi, bg, cq, ck, cv, cg],
                           axis=1).astype(BF16)


def _rope_tables(pos):
    half = HEAD_DIM // 2
    freqs = ROPE_THETA ** (-jnp.arange(half, dtype=F32) / half)
    ang = pos.astype(F32)[:, None] * freqs[None, :]
    cos, sin = jnp.cos(ang), jnp.sin(ang)
    cos64 = jnp.concatenate([cos, cos], axis=1)
    sin64 = jnp.concatenate([-sin, sin], axis=1)
    return jnp.concatenate([cos64, cos64], axis=1), jnp.concatenate([sin64, sin64], axis=1)


def _group_mean_matrix(width):
    idx = np.arange(width) // HEAD_DIM
    return jnp.asarray((idx[:, None] == idx[None, :]).astype(np.float32) / HEAD_DIM, BF16)


def _split2(x):
    hi = x.astype(BF16)
    lo = (x - hi.astype(F32)).astype(BF16)
    return hi, lo


def _nt_dot(a, b):
    return lax.dot_general(a, b, (((1,), (1,)), ((), ())), preferred_element_type=F32)


def _proj_kernel(x_ref, g_ref, w_ref, cos_ref, sin_ref, qg_ref, kg_ref, lb_ref, gm_ref, ex_ref,
                 q_out, k_out, v_out, kd_out, vd_out, iq_out, sg_out, ik_out, ikd_out, gate_out,
                 bq_out, bk_out, bv_out, blf_out, cq_out, ck_out, cv_out, *, layer):
    x = x_ref[...]
    tm = x.shape[0]
    ms = jnp.mean(x * x, axis=-1, keepdims=True)
    h = (x * lax.rsqrt(ms + EPS) * g_ref[...]).astype(BF16)
    cos = cos_ref[...]
    sin = sin_ref[...]
    lane = lax.broadcasted_iota(I32, (tm, LANES), 1)
    first_half = (lane % HEAD_DIM) < (HEAD_DIM // 2)
    low_head = lane < HEAD_DIM
    gmat = gm_ref[...]

    def proj(c0, width):
        return jnp.dot(h, w_ref[:, c0:c0 + width], preferred_element_type=F32)

    def rope(xb):
        swapped = jnp.where(first_half, pltpu.roll(xb, LANES - HEAD_DIM // 2, 1),
                            pltpu.roll(xb, HEAD_DIM // 2, 1))
        return xb * cos + swapped * sin

    def head_norm(xb, gain):
        hi, lo = _split2(xb * xb)
        msq = jnp.dot(jnp.concatenate([hi, lo], axis=1), gmat, preferred_element_type=F32)
        return xb * lax.rsqrt(msq + EPS) * gain

    def dup_heads(xb):
        other = pltpu.roll(xb, HEAD_DIM, 1)
        return jnp.where(low_head, xb, other), jnp.where(low_head, other, xb)

    ag = proj(_AG, A_WIDTH)
    bg = proj(_BG, B_WIDTH)
    cg = proj(_CG, C_WIDTH)
    bq = proj(_BQ, B_WIDTH)
    bf = proj(_BF, B_WIDTH)
    bi = proj(_BI, B_WIDTH)
    cq = proj(_CQ, C_WIDTH)
    ck = proj(_CK, C_WIDTH)
    cv = proj(_CV, C_WIDTH)
    akv = proj(_AK, 2 * KV_WIDTH)
    ikw = proj(_IKW, LANES)
    aq = proj(_AQ, A_WIDTH)
    iq = proj(_IQ, IDX_HEADS * HEAD_DIM)

    def silu(z):
        return z / (1.0 + jnp.exp(-z))

    gate_out[:, :A_WIDTH] = silu(ag).astype(BF16)
    gate_out[:, A_WIDTH:A_WIDTH + B_WIDTH] = silu(bg).astype(BF16)
    gate_out[:, A_WIDTH + B_WIDTH:] = silu(cg).astype(BF16)

    logits = lb_ref[...]
    e = jnp.exp(logits - jnp.max(logits, axis=0, keepdims=True))
    sm = e / jnp.sum(e, axis=0, keepdims=True)
    lb = jnp.zeros((1, B_WIDTH), F32)
    for j in range(1, layer + 1):
        lb = lb + sm[j:j + 1, :]
    f = lb + (1.0 - lb) / (1.0 + jnp.exp(-bf))
    bq_out[...] = bq
    bk_out[...] = (1.0 - lb) / (1.0 + jnp.exp(bf))
    bv_out[...] = bi
    blf_out[...] = jnp.log(f)

    for c in range(C_WIDTH // LANES):
        sl = slice(c * LANES, (c + 1) * LANES)
        cq_out[:, sl] = rope(cq[:, sl])
        ck_out[:, sl] = rope(ck[:, sl]) * (HEAD_DIM ** -0.5)
    cv_out[...] = cv

    v = akv[:, KV_WIDTH:]
    v_out[...] = v
    v0, v1 = dup_heads(v)
    vd_out[:, :LANES] = v0.astype(BF16)
    vd_out[:, LANES:] = v1.astype(BF16)
    ikr = rope(ikw)
    ik_out[...] = ikr[:, :HEAD_DIM]
    ikd_out[...] = dup_heads(ikr)[0].astype(BF16)
    sg_out[...] = jnp.where(ikw >= 0, 1.0, -1.0).astype(F32)

    for c in range(A_WIDTH // LANES):
        blk = rope(head_norm(aq[:, c * LANES:(c + 1) * LANES], qg_ref[...]))
        q_out[:, c * LANES:(c + 1) * LANES] = (blk * (HEAD_DIM ** -0.5 * LOG2_E)).astype(BF16)
    k = rope(head_norm(akv[:, :KV_WIDTH], kg_ref[...]))
    k_out[...] = k
    k0, k1 = dup_heads(k)
    kd_out[:, :LANES] = k0.astype(BF16)
    kd_out[:, LANES:] = k1.astype(BF16)

    whi, wlo = _split2(jnp.abs(ikw) * IDX_W_SCALE)
    wexp = jnp.dot(jnp.concatenate([whi, wlo], axis=1), ex_ref[...], preferred_element_type=F32)
    for c in range(IDX_HEADS * HEAD_DIM // LANES):
        sl = slice(c * LANES, (c + 1) * LANES)
        iq_out[:, sl] = (rope(iq[:, sl]) * wexp[:, sl]).astype(BF16)


def _row_blocking(b, t, tm):
    if t % tm == 0:
        grid = (b, t // tm)
        rows = lambda w: pl.BlockSpec((None, tm, w), lambda bi, j: (bi, j, 0))
        per_pos = lambda w: pl.BlockSpec((tm, w), lambda bi, j: (j, 0))
        const = lambda shape: pl.BlockSpec(shape, lambda bi, j: (0, 0))
        return grid, rows, per_pos, const, False, tm
    n = b * t
    tm = min(tm, n)
    assert n % tm == 0 and tm % t == 0
    grid = (n // tm,)
    rows = lambda w: pl.BlockSpec((tm, w), lambda r: (r, 0))
    per_pos = lambda w: pl.BlockSpec((tm, w), lambda r: (0, 0))
    const = lambda shape: pl.BlockSpec(shape, lambda r: (0, 0))
    return grid, rows, per_pos, const, True, tm


def _proj_call(x, pos, norm_g, w_p, q_gain, k_gain, lb_logits, layer, tm):
    b, t, _ = x.shape
    grid, rows, per_pos, const, flat, tm = _row_blocking(b, t, tm)
    cos, sin = _rope_tables(pos)
    if flat:
        x = x.reshape(b * t, D_MODEL)
        cos = jnp.tile(cos, (tm // t, 1))
        sin = jnp.tile(sin, (tm // t, 1))
    expand = np.zeros((LANES, IDX_HEADS * HEAD_DIM), np.float32)
    for hd in range(IDX_HEADS):
        expand[HEAD_DIM + hd, hd * HEAD_DIM:(hd + 1) * HEAD_DIM] = 1.0
    widths = [(A_WIDTH, BF16), (KV_WIDTH, F32), (KV_WIDTH, F32), (2 * KV_WIDTH, BF16), (2 * KV_WIDTH, BF16),
              (IDX_HEADS * HEAD_DIM, BF16), (LANES, F32), (HEAD_DIM, F32), (LANES, BF16), (D_MODEL, BF16),
              (B_WIDTH, F32), (B_WIDTH, F32), (B_WIDTH, F32), (B_WIDTH, F32),
              (C_WIDTH, F32), (C_WIDTH, F32), (C_WIDTH, F32)]
    lead = (b * t,) if flat else (b, t)
    outs = pl.pallas_call(
        functools.partial(_proj_kernel, layer=layer),
        grid=grid,
        in_specs=[rows(D_MODEL),
                  const((1, D_MODEL)),
                  const((D_MODEL, _N_PROJ)),
                  per_pos(LANES),
                  per_pos(LANES),
                  const((1, LANES)),
                  const((1, LANES)),
                  const(lb_logits.shape),
                  const((2 * LANES, LANES)),
                  const((2 * LANES, IDX_HEADS * HEAD_DIM))],
        out_specs=[rows(w) for w, _ in widths],
        out_shape=[jax.ShapeDtypeStruct(lead + (w,), dt) for w, dt in widths],
        compiler_params=pltpu.CompilerParams(dimension_semantics=("parallel",) * len(grid),
                                             vmem_limit_bytes=VMEM_LIMIT),
        name=f"proj_l{layer}",
    )(x, norm_g.reshape(1, D_MODEL), w_p, cos, sin,
      jnp.tile(q_gain, 2).reshape(1, LANES), jnp.tile(k_gain, 2).reshape(1, LANES),
      lb_logits, jnp.tile(_group_mean_matrix(LANES), (2, 1)), jnp.tile(jnp.asarray(expand, BF16), (2, 1)))
    return [o.reshape(b, t, o.shape[-1]) for o in outs] if flat else outs


SEARCH_INTERP_STEPS = 9
SEARCH_WALK_STEPS = 3
SEARCH_RANK_BIAS = 1.5
SEARCH_EDGE = 0.05
TIE_WALK_STEPS = (4, 12)


def _normal_upper_quantile(r):
    rr = jnp.minimum(r, 1.0 - r)
    t = jnp.sqrt(-2.0 * jnp.log(rr))
    z = t - ((0.010328 * t + 0.802853) * t + 2.515517) / (((0.001308 * t + 0.189269) * t + 1.432788) * t + 1.0)
    return jnp.where(r <= 0.5, z, -z)


def _select_topk(xm_scr, sum1_scr, sum2_scr, lim, topk, thr_scr, lim_scr, bias_scr):
    tq, s = xm_scr.shape
    kf = float(topk)
    nf = lim.astype(F32)
    trivial = lim <= topk

    def count_gt(p):
        return jnp.sum(jnp.where(xm_scr[...] > p, 1.0, 0.0), axis=1, keepdims=True)

    mu = jnp.sum(sum1_scr[...], axis=1, keepdims=True) / nf
    var = jnp.sum(sum2_scr[...], axis=1, keepdims=True) / nf - mu * mu
    sd = jnp.sqrt(jnp.maximum(var, 1e-30))
    target = kf + SEARCH_RANK_BIAS
    lo, hi = mu - 8.0 * sd, mu + 8.0 * sd
    clo, chi = nf, jnp.zeros_like(nf)
    p = mu + _normal_upper_quantile(jnp.clip(target / nf, 1e-6, 1.0 - 1e-6)) * sd
    lo_counted = jnp.zeros((tq, 1), jnp.bool_)
    for it in range(SEARCH_INTERP_STEPS):
        c = count_gt(p)
        above = c >= kf
        lo = jnp.where(above, p, lo)
        clo = jnp.where(above, c, clo)
        lo_counted = lo_counted | above
        hi = jnp.where(above, hi, p)
        chi = jnp.where(above, chi, c)
        frac = jnp.clip((clo - target) / (clo - chi), SEARCH_EDGE, 1.0 - SEARCH_EDGE)
        p = lo + (hi - lo) * frac

    cur = lo
    thr = lo
    cnt_prev = clo
    cnt_gt = clo
    cnt_eq = jnp.zeros_like(clo)
    done = lo_counted & (clo == kf)
    for it in range(SEARCH_WALK_STEPS):
        xv = xm_scr[...]
        cur = jnp.min(jnp.where(xv > cur, xv, -NEG_BIG), axis=1, keepdims=True)
        c = count_gt(cur)
        hit = (c <= kf) & lo_counted & jnp.logical_not(done)
        thr = jnp.where(hit, cur, thr)
        cnt_gt = jnp.where(hit, c, cnt_gt)
        cnt_eq = jnp.where(hit, cnt_prev - c, cnt_eq)
        done = done | hit
        cnt_prev = c
    thr = jnp.where(trivial, 0.1 * NEG_BIG, thr)
    cnt_gt = jnp.where(trivial, nf, cnt_gt)
    cnt_eq = jnp.where(trivial, 0.0, cnt_eq)
    thr_scr[...] = jnp.broadcast_to(thr, (tq, LANES))
    sum1_scr[...] = jnp.broadcast_to(cnt_gt, (tq, LANES))
    sum2_scr[...] = jnp.broadcast_to(cnt_eq, (tq, LANES))
    proven = trivial | done

    @pl.when(jnp.min(jnp.where(proven, 1.0, 0.0)) < 0.5)
    def _():
        lo0 = jnp.where(lo_counted, lo, 0.5 * NEG_BIG)
        clo0 = jnp.where(lo_counted, clo, nf)

        def unsettled(state):
            rounds, done_f = state[0], state[-1]
            return (rounds < s) & (jnp.min(done_f) < 0.5)

        def one_round(state):
            rounds, lo_, clo_, hi_, chi_, thr_, cgt_, ceq_, done_f = state
            done_ = done_f > 0.5
            frac = jnp.clip((clo_ - target) / (clo_ - chi_), SEARCH_EDGE, 1.0 - SEARCH_EDGE)
            piv = lo_ + (hi_ - lo_) * frac
            c = count_gt(piv)
            above = c >= kf
            lo_ = jnp.where(above, piv, lo_)
            clo_ = jnp.where(above, c, clo_)
            hi_ = jnp.where(above, hi_, piv)
            chi_ = jnp.where(above, chi_, c)
            settled = (clo_ == kf) & jnp.logical_not(done_)
            thr_ = jnp.where(settled, lo_, thr_)
            cgt_ = jnp.where(settled, clo_, cgt_)
            ceq_ = jnp.where(settled, 0.0, ceq_)
            done_ = done_ | settled
            xv = xm_scr[...]
            nxt = jnp.min(jnp.where(xv > lo_, xv, -NEG_BIG), axis=1, keepdims=True)
            c = count_gt(nxt)
            hit = (c <= kf) & jnp.logical_not(done_)
            thr_ = jnp.where(hit, nxt, thr_)
            cgt_ = jnp.where(hit, c, cgt_)
            ceq_ = jnp.where(hit, clo_ - c, ceq_)
            done_ = done_ | hit
            lo_ = jnp.where(hit, lo_, nxt)
            clo_ = jnp.where(hit, clo_, c)
            return (rounds + 1, lo_, clo_, hi_, chi_, thr_, cgt_, ceq_, jnp.where(done_, 1.0, 0.0))

        hi0 = jnp.maximum(hi, lo0)
        state = (jnp.int32(0), lo0, clo0, hi0, chi, thr, cnt_gt, cnt_eq, jnp.where(proven, 1.0, 0.0))
        state = lax.while_loop(unsettled, one_round, state)
        thr_scr[...] = jnp.broadcast_to(state[5], (tq, LANES))
        sum1_scr[...] = jnp.broadcast_to(state[6], (tq, LANES))
        sum2_scr[...] = jnp.broadcast_to(state[7], (tq, LANES))

    thr = thr_scr[:, :1]
    cnt_gt = sum1_scr[:, :1]
    cnt_eq = sum2_scr[:, :1]
    xv = xm_scr[...]
    gt = xv > thr
    eq = xv == thr
    want_eq = kf - cnt_gt

    all_or_none = jnp.where((want_eq >= cnt_eq) & (want_eq > 0.0), POS_BIG, 0)
    lim_scr[...] = jnp.broadcast_to(all_or_none, (tq, LANES))
    partial = (want_eq > 0.0) & (want_eq < cnt_eq)
    kpos = lax.broadcasted_iota(I32, (tq, s), 1)

    @pl.when(jnp.max(jnp.where(partial, 1.0, 0.0)) > 0.5)
    def _():
        eqpos = jnp.where(eq, kpos, POS_BIG)
        eqposf = eqpos.astype(F32)

        def walk(first, last, cur, xlim):
            for it in range(first, last):
                cur = jnp.min(jnp.where(eqposf > cur, eqposf, float(POS_BIG)), axis=1, keepdims=True)
                xlim = jnp.where(partial & (want_eq == float(it)), cur.astype(I32), xlim)
            return cur, xlim

        cur, xlim = walk(0, TIE_WALK_STEPS[0], jnp.full((tq, 1), -1.0, F32), all_or_none)
        lim_scr[...] = jnp.broadcast_to(xlim, (tq, LANES))

        @pl.when(jnp.max(jnp.where(partial & (want_eq >= float(TIE_WALK_STEPS[0])), 1.0, 0.0)) > 0.5)
        def _():
            _, xlim2 = walk(TIE_WALK_STEPS[0], TIE_WALK_STEPS[1], cur, xlim)
            lim_scr[...] = jnp.broadcast_to(xlim2, (tq, LANES))

            @pl.when(jnp.max(jnp.where(partial & (want_eq >= float(TIE_WALK_STEPS[1])), 1.0, 0.0)) > 0.5)
            def _():
                nbits = max(1, int(s).bit_length())

                def pos_step(j, xl):
                    cand = xl | lax.shift_left(jnp.int32(1), nbits - 1 - j)
                    cnt = jnp.sum(jnp.where(eqpos < cand, 1.0, 0.0), axis=1, keepdims=True)
                    return jnp.where(cnt <= want_eq, cand, xl)

                lim_scr[...] = jnp.broadcast_to(lax.fori_loop(0, nbits, pos_step, jnp.zeros((tq, 1), I32)),
                                                (tq, LANES))

    sel = gt | (eq & (kpos < lim_scr[:, :1]))
    bias_scr[...] = jnp.where(sel, 0.0, NEG_BIG).astype(bias_scr.dtype)


def _attn_kernel(*refs, topk, tpos0, n_keys):
    q_ref, iq_ref, sg_ref, kd_ref, vd_ref, ikd_ref, gate_ref = refs[:7]
    o_ref, xm_scr, bias_scr, thr_scr, lim_scr, sum1_scr, sum2_scr = refs[-7:]
    tq = q_ref.shape[1]
    s = kd_ref.shape[1]
    lane = lax.broadcasted_iota(I32, (tq, LANES), 1)
    halves = (lane < HEAD_DIM, lane >= HEAD_DIM)

    tpos = tpos0 + lax.broadcasted_iota(I32, (tq, 1), 0)
    lim = jnp.minimum((tpos // CHUNK + 1) * CHUNK, n_keys)
    search = min(s, n_keys) > topk

    rb = min(tq, LANES)
    key_tile = 2 * LANES
    lane_rb = lax.broadcasted_iota(I32, (rb, LANES), 1)
    halves_rb = (lane_rb < HEAD_DIM, lane_rb >= HEAD_DIM)
    for r in range(tq // rb):
        rs = slice(r * rb, (r + 1) * rb)
        sg = sg_ref[0, rs, :]
        qms, sgs = [], []
        for hd in range(IDX_HEADS):
            qb = iq_ref[0, rs, (hd // 2) * LANES:(hd // 2 + 1) * LANES]
            qms.append(jnp.where(halves_rb[hd % 2], qb, jnp.zeros_like(qb)))
            sgs.append(sg[:, HEAD_DIM + hd:HEAD_DIM + hd + 1])
        acc1 = jnp.zeros((rb, LANES), F32)
        acc2 = jnp.zeros((rb, LANES), F32)
        lim_first = min(((tpos0 + r * rb) // CHUNK + 1) * CHUNK, n_keys)
        lim_last = min(((tpos0 + (r + 1) * rb - 1) // CHUNK + 1) * CHUNK, n_keys)
        for c0 in range(0, s, key_tile):
            w = min(key_tile, s - c0)
            if c0 >= lim_last:
                xm_scr[rs, c0:c0 + w] = jnp.full((rb, w), NEG_BIG, F32)
                continue
            ikd_t = ikd_ref[0, c0:c0 + w, :]
            acc = jnp.zeros((rb, w), F32)
            for hd in range(IDX_HEADS):
                acc = acc + sgs[hd] * jnp.maximum(_nt_dot(qms[hd], ikd_t), 0.0)
            if c0 + w <= lim_first:
                xm_scr[rs, c0:c0 + w] = acc
                sc0 = acc
            else:
                adm_t = (c0 + lax.broadcasted_iota(I32, (rb, w), 1)) < lim[rs]
                xm_scr[rs, c0:c0 + w] = jnp.where(adm_t, acc, NEG_BIG)
                sc0 = jnp.where(adm_t, acc, 0.0)
            if search:
                for l0 in range(0, w, LANES):
                    part = sc0[:, l0:l0 + LANES]
                    acc1 = acc1 + part
                    acc2 = acc2 + part * part
        if search:
            sum1_scr[rs, :] = acc1
            sum2_scr[rs, :] = acc2

    logits = []
    for c in range(A_WIDTH // LANES):
        g = (2 * c) // (A_HEADS // A_KV_HEADS)
        qb = q_ref[0, :, c * LANES:(c + 1) * LANES]
        for e in range(2):
            qm = jnp.where(halves[e], qb, jnp.zeros_like(qb))
            logits.append(_nt_dot(qm, kd_ref[0, :, g * LANES:(g + 1) * LANES]).astype(BF16))

    if search:
        _select_topk(xm_scr, sum1_scr, sum2_scr, lim, topk, thr_scr, lim_scr, bias_scr)
    else:
        bias_scr[...] = jnp.where(lax.broadcasted_iota(I32, (tq, s), 1) < lim, 0.0, NEG_BIG).astype(BF16)

    ones = jnp.ones((s, LANES), BF16)
    for g in range(A_KV_HEADS):
        vd1 = jnp.concatenate([vd_ref[0, :, g * LANES:(g + 1) * LANES], ones], axis=1)
        for c in range(g * 2, g * 2 + 2):
            outs = []
            for e in range(2):
                lg = logits[2 * c + e] + bias_scr[...]
                p = jnp.exp2(lg - jnp.max(lg, axis=1, keepdims=True))
                ol = jnp.dot(p, vd1, preferred_element_type=F32)
                outs.append(ol[:, :LANES] / ol[:, LANES:])
            gate = gate_ref[0, :, c * LANES:(c + 1) * LANES].astype(F32)
            o_ref[0, :, c * LANES:(c + 1) * LANES] = (jnp.where(halves[0], outs[0], outs[1]) * gate).astype(BF16)


def _attn_call(q, iq, sg, kd, vd, ikd, gate, prev, *, topk, qpos0, n_keys, tq, qblock, s_blk, name):
    b, t, _ = q.shape
    qmap = lambda bi: (bi, qblock, 0)
    kmap = lambda bi: (bi, 0, 0)
    return pl.pallas_call(
        functools.partial(_attn_kernel, topk=topk, tpos0=qpos0 + qblock * tq, n_keys=n_keys),
        grid=(b,),
        in_specs=[pl.BlockSpec((1, tq, A_WIDTH), qmap),
                  pl.BlockSpec((1, tq, IDX_HEADS * HEAD_DIM), qmap),
                  pl.BlockSpec((1, tq, LANES), qmap),
                  pl.BlockSpec((1, s_blk, 2 * KV_WIDTH), kmap),
                  pl.BlockSpec((1, s_blk, 2 * KV_WIDTH), kmap),
                  pl.BlockSpec((1, s_blk, LANES), kmap),
                  pl.BlockSpec((1, tq, A_WIDTH), qmap),
                  pl.BlockSpec(memory_space=pl.ANY)],
        out_specs=pl.BlockSpec((1, tq, A_WIDTH), qmap),
        out_shape=jax.ShapeDtypeStruct((b, t, A_WIDTH), BF16),
        input_output_aliases={7: 0},
        scratch_shapes=[pltpu.VMEM((tq, s_blk), F32), pltpu.VMEM((tq, s_blk), BF16),
                        pltpu.VMEM((tq, LANES), F32), pltpu.VMEM((tq, LANES), I32),
                        pltpu.VMEM((tq, LANES), F32), pltpu.VMEM((tq, LANES), F32)],
        compiler_params=pltpu.CompilerParams(dimension_semantics=("parallel",),
                                             vmem_limit_bytes=VMEM_LIMIT),
        name=name,
    )(q, iq, sg, kd, vd, ikd, gate, prev)


ATTN_ROWS_MAX = 256
ATTN_BLOCK_ELEMS = 512 * 1024


def _attention(q, iq, sg, kd, vd, ikd, gate, *, topk, qpos0, n_keys, tag):
    b, t, _ = q.shape

    def visible_keys(row0, rows):
        visible = min(((qpos0 + row0 + rows - 1) // CHUNK + 1) * CHUNK, n_keys)
        return min(-(-visible // LANES) * LANES, kd.shape[1])

    out = jnp.zeros((b, t, A_WIDTH), BF16)
    row0 = 0
    while row0 < t:
        tq = min(t - row0, ATTN_ROWS_MAX)
        if tq == ATTN_ROWS_MAX and (row0 % tq or tq * visible_keys(row0, tq) > ATTN_BLOCK_ELEMS):
            tq = ATTN_ROWS_MAX // 2
        out = _attn_call(q, iq, sg, kd, vd, ikd, gate, out, topk=topk, qpos0=qpos0, n_keys=n_keys, tq=tq,
                         qblock=row0 // tq, s_blk=visible_keys(row0, tq), name=f"attn_{tag}_r{row0}")
        row0 += tq
    return out


def _retention_log_decay():
    return jnp.log(1.0 - jnp.exp(jnp.linspace(math.log(1.0 / 32), math.log(1.0 / 512), REC_HEADS, dtype=F32)))


def _rec_kernel(bq_ref, bk_ref, bv_ref, blf_ref, cq_ref, ck_ref, cv_ref, gb_ref, gc_ref, s0b_ref, s0c_ref,
                hg_ref, rg_ref, tri_ref, dmat_ref, inner_ref, kscale_ref, rdec_ref, gm_ref, x_ref, ya_ref, w_ref,
                y_ref, sb_out, sc_out, sb_scr, sc_scr, ybc_scr, *, chunk, n_chunks):
    j = pl.program_id(1)
    width = B_WIDTH

    @pl.when(j == 0)
    def _():
        sb_scr[...] = s0b_ref[0]
        sc_scr[...] = s0c_ref[0]

    rowh = lax.broadcasted_iota(I32, (width, width), 0) // HEAD_DIM
    colh = lax.broadcasted_iota(I32, (width, width), 1) // HEAD_DIM
    block_diag = rowh == colh
    lane_h = lax.broadcasted_iota(I32, (chunk, width), 1) // HEAD_DIM
    causal = (lax.broadcasted_iota(I32, (chunk, chunk), 0)
              >= lax.broadcasted_iota(I32, (chunk, chunk), 1))
    gmat = gm_ref[...]
    tri = tri_ref[...]
    tn = (((0,), (0,)), ((), ()))

    def masked_heads(x):
        return [jnp.where(lane_h == hd, x, 0.0).astype(BF16) for hd in range(REC_HEADS)]

    heads = range(REC_HEADS)
    chunks = range(n_chunks)
    rows = [pl.ds(c * chunk, chunk) for c in chunks]

    bcs = []
    for c in chunks:
        lf = blf_ref[0, rows[c], :]
        a1 = lf.astype(BF16)
        r1 = lf - a1.astype(F32)
        a2 = r1.astype(BF16)
        a3 = (r1 - a2.astype(F32)).astype(BF16)
        bcs.append(jnp.dot(tri, jnp.concatenate([a1, a2, a3], axis=0), preferred_element_type=F32))

    att_b, att_c, upd_b, upd_c, qs_b, qs_c, dec_b, vb_b, vb_c = ([] for _ in range(9))
    for c in chunks:
        bc = bcs[c]
        mid = bc[chunk // 2 - 1:chunk // 2, :]
        last = bc[chunk - 1:chunk, :]
        q = bq_ref[0, rows[c], :]
        k = bk_ref[0, rows[c], :]
        vb = bv_ref[0, rows[c], :].astype(BF16)
        kt = (k * jnp.exp(mid - bc)).astype(BF16)
        att_b.append([_nt_dot(qh, kt) for qh in masked_heads(q * jnp.exp(bc - mid))])
        upd_b.append(lax.dot_general(vb, (k * jnp.exp(last - bc)).astype(BF16), tn, preferred_element_type=F32))
        qs_b.append((q * jnp.exp(bc)).astype(BF16))
        dec_b.append(jnp.exp(last))
        vb_b.append(vb)
        q = cq_ref[0, rows[c], :]
        k = ck_ref[0, rows[c], :]
        vb = cv_ref[0, rows[c], :].astype(BF16)
        kb = k.astype(BF16)
        att_c.append([_nt_dot(qh, kb) for qh in masked_heads(q)])
        upd_c.append(lax.dot_general(vb, (k * kscale_ref[...]).astype(BF16), tn, preferred_element_type=F32))
        qs_c.append((q * inner_ref[...]).astype(BF16))
        vb_c.append(vb)

    st = sb_scr[...]
    rt = sc_scr[...]
    o_b, o_c = [], []
    for c in chunks:
        o_b.append(_nt_dot(qs_b[c], st.astype(BF16)))
        st = st * dec_b[c] + jnp.where(block_diag, upd_b[c], 0.0)
        o_c.append(_nt_dot(qs_c[c], rt.astype(BF16)))
        rt = rt * rdec_ref[...] + jnp.where(block_diag, upd_c[c], 0.0)
    sb_scr[...] = st
    sc_scr[...] = rt

    def blockdiag_rows(v16):
        return jnp.concatenate([jnp.where(lane_h == hd, v16, jnp.zeros_like(v16)) for hd in heads], axis=0)

    prod_b = [jnp.dot(jnp.concatenate([jnp.where(causal, att_b[c][hd], 0.0) for hd in heads], axis=1).astype(BF16),
                      blockdiag_rows(vb_b[c]), preferred_element_type=F32) for c in chunks]
    prod_c = [jnp.dot(jnp.concatenate([att_c[c][hd] * dmat_ref[hd * chunk:(hd + 1) * chunk, :] for hd in heads],
                                      axis=1).astype(BF16),
                      blockdiag_rows(vb_c[c]), preferred_element_type=F32) for c in chunks]
    outs = []
    for c in chunks:
        outs.append(o_b[c] + prod_b[c])
        outs.append(o_c[c] + prod_c[c])

    means = []
    for o in outs:
        hi, lo = _split2(o * o)
        means.append(jnp.dot(jnp.concatenate([hi, lo], axis=1), gmat, preferred_element_type=F32))
    for c in chunks:
        yb = outs[2 * c] * lax.rsqrt(means[2 * c] + EPS) * hg_ref[...]
        yc = outs[2 * c + 1] * lax.rsqrt(means[2 * c + 1] + EPS) * rg_ref[...]
        ybc_scr[rows[c], :B_WIDTH] = (yb * gb_ref[0, rows[c], :].astype(F32)).astype(BF16)
        ybc_scr[rows[c], B_WIDTH:] = (yc * gc_ref[0, rows[c], :].astype(F32)).astype(BF16)

    y_ref[0] = (x_ref[0]
                + jnp.dot(ya_ref[0], w_ref[:A_WIDTH, :], preferred_element_type=F32)
                + jnp.dot(ybc_scr[...], w_ref[A_WIDTH:, :], preferred_element_type=F32))

    @pl.when(j == pl.num_programs(1) - 1)
    def _():
        sb_out[0] = sb_scr[...]
        sc_out[0] = sc_scr[...]


def _state_to_blockdiag(s):
    b = s.shape[0]
    eye = jnp.eye(REC_HEADS, dtype=s.dtype)
    return jnp.einsum('bhkv,hg->bhvgk', s, eye).reshape(b, B_WIDTH, B_WIDTH)


def _state_from_blockdiag(sbd):
    b = sbd.shape[0]
    s5 = sbd.reshape(b, REC_HEADS, HEAD_DIM, REC_HEADS, HEAD_DIM)
    return jnp.einsum('bhvhk->bhkv', s5)


def _rec_out_call(bq, bk, bv, blf, cq, ck, cv, gate, s0b, s0c, hgain, rgain, x, ya, w_out16, name):
    b, t, _ = bq.shape
    chunk = min(CHUNK, t)
    tb = min(t, 8 * chunk)
    n_chunks = tb // chunk
    lg = _retention_log_decay()
    n = jnp.arange(chunk, dtype=F32)
    diff = n[:, None] - n[None, :]
    dmat = jnp.where(diff >= 0, jnp.exp(jnp.where(diff >= 0, diff, 0.0)[None] * lg[:, None, None]), 0.0)
    dmat = dmat.reshape(REC_HEADS * chunk, chunk)
    per_lane = lambda a: jnp.repeat(a, HEAD_DIM, axis=-1)
    inner = per_lane(jnp.exp((n[:, None] + 1.0) * lg[None, :]))
    kscale = per_lane(jnp.exp((chunk - 1.0 - n)[:, None] * lg[None, :]))
    rdec = per_lane(jnp.exp(chunk * lg)[None, :])
    tri = jnp.tile(jnp.tril(jnp.ones((chunk, chunk), F32)), (1, 3)).astype(BF16)
    seq = lambda bi, j: (bi, j, 0)
    per_b = lambda bi, j: (bi, 0, 0)
    const = lambda bi, j: (0, 0)
    stream = pl.BlockSpec((1, tb, B_WIDTH), seq)
    state = pl.BlockSpec((1, B_WIDTH, B_WIDTH), per_b)
    return pl.pallas_call(
        functools.partial(_rec_kernel, chunk=chunk, n_chunks=n_chunks),
        grid=(b, t // tb),
        in_specs=[stream] * 7 + [pl.BlockSpec((1, tb, B_WIDTH), lambda bi, j: (bi, j, A_WIDTH // B_WIDTH)),
                                 pl.BlockSpec((1, tb, C_WIDTH), lambda bi, j: (bi, j, (A_WIDTH + B_WIDTH) // C_WIDTH)),
                                 state, state,
                                 pl.BlockSpec((1, B_WIDTH), const), pl.BlockSpec((1, B_WIDTH), const),
                                 pl.BlockSpec((chunk, 3 * chunk), const),
                                 pl.BlockSpec((REC_HEADS * chunk, chunk), const),
                                 pl.BlockSpec((chunk, B_WIDTH), const), pl.BlockSpec((chunk, B_WIDTH), const),
                                 pl.BlockSpec((1, B_WIDTH), const),
                                 pl.BlockSpec((2 * B_WIDTH, B_WIDTH), const),
                                 pl.BlockSpec((1, tb, D_MODEL), seq),
                                 pl.BlockSpec((1, tb, A_WIDTH), seq),
                                 pl.BlockSpec((D_MODEL, D_MODEL), const)],
        out_specs=[pl.BlockSpec((1, tb, D_MODEL), seq), state, state],
        out_shape=[jax.ShapeDtypeStruct((b, t, D_MODEL), F32),
                   jax.ShapeDtypeStruct((b, B_WIDTH, B_WIDTH), F32),
                   jax.ShapeDtypeStruct((b, C_WIDTH, C_WIDTH), F32)],
        scratch_shapes=[pltpu.VMEM((B_WIDTH, B_WIDTH), F32), pltpu.VMEM((C_WIDTH, C_WIDTH), F32),
                        pltpu.VMEM((tb, B_WIDTH + C_WIDTH), BF16)],
        compiler_params=pltpu.CompilerParams(dimension_semantics=("parallel", "arbitrary"),
                                             vmem_limit_bytes=VMEM_LIMIT),
        name=name,
    )(bq, bk, bv, blf, cq, ck, cv, gate, gate, s0b, s0c,
      jnp.tile(hgain, REC_HEADS).reshape(1, B_WIDTH), jnp.tile(rgain, REC_HEADS).reshape(1, C_WIDTH),
      tri, dmat, inner, kscale, rdec, jnp.tile(_group_mean_matrix(B_WIDTH), (2, 1)), x, ya, w_out16)


def _with_past(past_parts, new, s_pad):
    past = jnp.concatenate([p.astype(BF16) for p in past_parts], axis=-1)
    b, n_old, width = past.shape
    tail = jnp.zeros((b, s_pad - n_old - new.shape[1], width), BF16)
    return jnp.concatenate([past, new, tail], axis=1)


def _mixer_layer(x, pos0, past, s_hgrn, s_ret, layer, w_p, w_out16, norm_g, q_gain, k_gain,
                 lb_logits, hgain, rgain, tag):
    b, t, _ = x.shape
    pos = pos0 + jnp.arange(t)
    (q, k, v, kd3, vd3, iq, sg, ik, ikd3, gate, bq, bk, bv, blf, cq, ck, cv) = _proj_call(
        x, pos, norm_g, w_p, q_gain, k_gain, lb_logits, layer, 512)

    n_keys = t if past is None else past[0].shape[1] + t
    s_pad = -(-n_keys // LANES) * LANES
    if past is not None:
        pk, pv, pik = past
        kd3 = _with_past([pk[:, :, 0], pk[:, :, 0], pk[:, :, 1], pk[:, :, 1]], kd3, s_pad)
        vd3 = _with_past([pv[:, :, 0], pv[:, :, 0], pv[:, :, 1], pv[:, :, 1]], vd3, s_pad)
        ikd3 = _with_past([pik, pik], ikd3, s_pad)
    elif s_pad != n_keys:
        padw = ((0, 0), (0, s_pad - n_keys), (0, 0))
        kd3, vd3, ikd3 = jnp.pad(kd3, padw), jnp.pad(vd3, padw), jnp.pad(ikd3, padw)
    topk = min(TOPK_MAX, n_keys // 4)
    oa = _attention(q, iq, sg, kd3, vd3, ikd3, gate, topk=topk, qpos0=pos0, n_keys=n_keys, tag=tag)

    out, sb, sc = _rec_out_call(bq, bk, bv, blf, cq, ck, cv, gate, s_hgrn, s_ret, hgain, rgain, x, oa, w_out16,
                                name=f"rec_{tag}")
    return (out,
            (k.reshape(b, t, A_KV_HEADS, HEAD_DIM), v.reshape(b, t, A_KV_HEADS, HEAD_DIM), ik,
             _state_from_blockdiag(sb), _state_from_blockdiag(sc)))


def kernel(x_prompt, x_sample, cache_k, cache_v, cache_idx_k, state_hgrn, state_ret, norm_g, w_in,
           q_norm_g, k_norm_g, hgrn_lb_logits, hgrn_norm_g, ret_norm_g, w_out):
    depth = w_in.shape[0]
    bp, tp = x_prompt.shape[:2]
    bs, ts = x_sample.shape[:2]
    past = cache_k.shape[2]
    zero_state = jnp.zeros((bp, B_WIDTH, B_WIDTH), F32)

    yp, ys = x_prompt, x_sample
    outs_p, outs_s = [], []
    for l in range(depth):
        w_p = _prep_w_in(w_in[l])
        w_o = w_out[l].astype(BF16)
        args = (l, w_p, w_o, norm_g[l], q_norm_g[l], k_norm_g[l], hgrn_lb_logits, hgrn_norm_g[l], ret_norm_g[l])
        yp, st = _mixer_layer(yp, 0, None, zero_state, zero_state, *args, tag=f"p{l}")
        outs_p.append(st)
        ys, st = _mixer_layer(ys, past, (cache_k[l], cache_v[l], cache_idx_k[l]),
                              _state_to_blockdiag(state_hgrn[l]), _state_to_blockdiag(state_ret[l]), *args,
                              tag=f"s{l}")
        outs_s.append(st)

    def stack(outs, i, shape):
        return jnp.stack([o[i] for o in outs]).reshape(shape)

    return (yp, ys,
            stack(outs_p, 0, (depth, bp, tp, A_KV_HEADS, HEAD_DIM)),
            stack(outs_p, 1, (depth, bp, tp, A_KV_HEADS, HEAD_DIM)),
            stack(outs_p, 2, (depth, bp, tp, HEAD_DIM)),
            stack(outs_p, 3, (depth, bp, REC_HEADS, HEAD_DIM, HEAD_DIM)),
            stack(outs_p, 4, (depth, bp, REC_HEADS, HEAD_DIM, HEAD_DIM)),
            stack(outs_s, 0, (depth, bs, ts, A_KV_HEADS, HEAD_DIM)),
            stack(outs_s, 1, (depth, bs, ts, A_KV_HEADS, HEAD_DIM)),
            stack(outs_s, 2, (depth, bs, ts, HEAD_DIM)),
            stack(outs_s, 3, (depth, bs, REC_HEADS, HEAD_DIM, HEAD_DIM)),
            stack(outs_s, 4, (depth, bs, REC_HEADS, HEAD_DIM, HEAD_DIM)))
```

```python
import functools
import math

import numpy as np
import jax
import jax.numpy as jnp
from jax import lax
from jax.experimental import pallas as pl
from jax.experimental.pallas import tpu as pltpu

F32 = jnp.float32
BF16 = jnp.bfloat16
I32 = jnp.int32

D_MODEL = 1024
HEAD_DIM = 64
CHUNK = 64
A_WIDTH = 512
A_HEADS = 8
A_KV_HEADS = 2
KV_WIDTH = A_KV_HEADS * HEAD_DIM
IDX_HEADS = 8
IDX_W_SCALE = (IDX_HEADS * HEAD_DIM) ** -0.5
TOPK_MAX = 256
B_WIDTH = 256
C_WIDTH = 256
REC_HEADS = 4
ROPE_THETA = 10000.0
EPS = 1e-6
LANES = 128
VMEM_LIMIT = 48 * 1024 * 1024
PROJ_ROWS = 512
ATTN_ROWS = 256
REC_CHUNKS_PER_STEP = 8

_SIZES = (A_WIDTH, KV_WIDTH, KV_WIDTH, A_WIDTH, IDX_HEADS * HEAD_DIM, HEAD_DIM, IDX_HEADS,
          B_WIDTH, B_WIDTH, B_WIDTH, B_WIDTH, C_WIDTH, C_WIDTH, C_WIDTH, C_WIDTH)
_SPLIT_IDX = tuple(int(s) for s in np.cumsum(_SIZES)[:-1])

_AQ, _AK, _AV, _AG, _IQ, _IKW = 0, 512, 640, 768, 1280, 1792
_BQ, _BF, _BI, _BG, _CQ, _CK, _CV, _CG = 1920, 2176, 2432, 2688, 2944, 3200, 3456, 3712
_N_PROJ = 3968

NEG_BIG = -1e30
POS_BIG = 1 << 20
LOG2_E = math.log2(math.e)


def _prep_w_in(w):
    (aq, ak, av, ag, iq, ik, iw, bq, bf, bi, bg, cq, ck, cv, cg) = jnp.split(w, _SPLIT_IDX, axis=1)
    pad = jnp.zeros((w.shape[0], LANES - HEAD_DIM - IDX_HEADS), w.dtype)
    return jnp.concatenate([aq, ak, av, ag, iq, ik, iw, pad, bq, bf, bi, bg, cq, ck, cv, cg],
                           axis=1).astype(BF16)


def _rope_tables(pos):
    half = HEAD_DIM // 2
    freqs = ROPE_THETA ** (-jnp.arange(half, dtype=F32) / half)
    ang = pos.astype(F32)[:, None] * freqs[None, :]
    cos, sin = jnp.cos(ang), jnp.sin(ang)
    cos64 = jnp.concatenate([cos, cos], axis=1)
    sin64 = jnp.concatenate([-sin, sin], axis=1)
    return jnp.concatenate([cos64, cos64], axis=1), jnp.concatenate([sin64, sin64], axis=1)


def _group_mean_matrix(width):
    idx = np.arange(width) // HEAD_DIM
    return jnp.asarray((idx[:, None] == idx[None, :]).astype(np.float32) / HEAD_DIM, BF16)


def _split2(x):
    hi = x.astype(BF16)
    lo = (x - hi.astype(F32)).astype(BF16)
    return hi, lo


def _nt_dot(a, b):
    return lax.dot_general(a, b, (((1,), (1,)), ((), ())), preferred_element_type=F32)


def _proj_kernel(x_ref, g_ref, w_ref, cos_ref, sin_ref, qg_ref, kg_ref, lb_ref, gm_ref, ex_ref,
                 q_out, k_out, v_out, kd_out, vd_out, iq_out, sg_out, ik_out, ikd_out, gate_out,
                 bq_out, bk_out, bv_out, blf_out, cq_out, ck_out, cv_out, *, layer):
    x = x_ref[...]
    tm = x.shape[0]
    ms = jnp.mean(x * x, axis=-1, keepdims=True)
    h = (x * lax.rsqrt(ms + EPS) * g_ref[...]).astype(BF16)
    cos = cos_ref[...]
    sin = sin_ref[...]
    lane = lax.broadcasted_iota(I32, (tm, LANES), 1)
    first_half = (lane % HEAD_DIM) < (HEAD_DIM // 2)
    low_head = lane < HEAD_DIM
    gmat = gm_ref[...]

    def proj(c0, width):
        return jnp.dot(h, w_ref[:, c0:c0 + width], preferred_element_type=F32)

    def rope(xb):
        swapped = jnp.where(first_half, pltpu.roll(xb, LANES - HEAD_DIM // 2, 1),
                            pltpu.roll(xb, HEAD_DIM // 2, 1))
        return xb * cos + swapped * sin

    def head_norm(xb, gain):
        hi, lo = _split2(xb * xb)
        msq = jnp.dot(jnp.concatenate([hi, lo], axis=1), gmat, preferred_element_type=F32)
        return xb * lax.rsqrt(msq + EPS) * gain

    def dup_heads(xb):
        other = pltpu.roll(xb, HEAD_DIM, 1)
        return jnp.where(low_head, xb, other), jnp.where(low_head, other, xb)

    ag = proj(_AG, A_WIDTH)
    bg = proj(_BG, B_WIDTH)
    cg = proj(_CG, C_WIDTH)
    bq = proj(_BQ, B_WIDTH)
    bf = proj(_BF, B_WIDTH)
    bi = proj(_BI, B_WIDTH)
    cq = proj(_CQ, C_WIDTH)
    ck = proj(_CK, C_WIDTH)
    cv = proj(_CV, C_WIDTH)
    akv = proj(_AK, 2 * KV_WIDTH)
    ikw = proj(_IKW, LANES)
    aq = proj(_AQ, A_WIDTH)
    iq = proj(_IQ, IDX_HEADS * HEAD_DIM)

    def silu(z):
        return z / (1.0 + jnp.exp(-z))

    gate_out[:, :A_WIDTH] = silu(ag).astype(BF16)
    gate_out[:, A_WIDTH:A_WIDTH + B_WIDTH] = silu(bg).astype(BF16)
    gate_out[:, A_WIDTH + B_WIDTH:] = silu(cg).astype(BF16)

    logits = lb_ref[...]
    e = jnp.exp(logits - jnp.max(logits, axis=0, keepdims=True))
    sm = e / jnp.sum(e, axis=0, keepdims=True)
    lb = jnp.zeros((1, B_WIDTH), F32)
    for j in range(1, layer + 1):
        lb = lb + sm[j:j + 1, :]
    f = lb + (1.0 - lb) / (1.0 + jnp.exp(-bf))
    bq_out[...] = bq
    bk_out[...] = (1.0 - lb) / (1.0 + jnp.exp(bf))
    bv_out[...] = bi
    blf_out[...] = jnp.log(f)

    for c in range(C_WIDTH // LANES):
        sl = slice(c * LANES, (c + 1) * LANES)
        cq_out[:, sl] = rope(cq[:, sl])
        ck_out[:, sl] = rope(ck[:, sl]) * (HEAD_DIM ** -0.5)
    cv_out[...] = cv

    v = akv[:, KV_WIDTH:]
    v_out[...] = v
    v0, v1 = dup_heads(v)
    vd_out[:, :LANES] = v0.astype(BF16)
    vd_out[:, LANES:] = v1.astype(BF16)
    ikr = rope(ikw)
    ik_out[...] = ikr[:, :HEAD_DIM]
    ikd_out[...] = dup_heads(ikr)[0].astype(BF16)
    sg_out[...] = jnp.where(ikw >= 0, 1.0, -1.0).astype(F32)

    for c in range(A_WIDTH // LANES):
        blk = rope(head_norm(aq[:, c * LANES:(c + 1) * LANES], qg_ref[...]))
        q_out[:, c * LANES:(c + 1) * LANES] = (blk * (HEAD_DIM ** -0.5 * LOG2_E)).astype(BF16)
    k = rope(head_norm(akv[:, :KV_WIDTH], kg_ref[...]))
    k_out[...] = k
    k0, k1 = dup_heads(k)
    kd_out[:, :LANES] = k0.astype(BF16)
    kd_out[:, LANES:] = k1.astype(BF16)

    whi, wlo = _split2(jnp.abs(ikw) * IDX_W_SCALE)
    wexp = jnp.dot(jnp.concatenate([whi, wlo], axis=1), ex_ref[...], preferred_element_type=F32)
    for c in range(IDX_HEADS * HEAD_DIM // LANES):
        sl = slice(c * LANES, (c + 1) * LANES)
        iq_out[:, sl] = (rope(iq[:, sl]) * wexp[:, sl]).astype(BF16)


def _row_blocking(b, t, tm):
    if t % tm == 0:
        grid = (b, t // tm)
        rows = lambda w: pl.BlockSpec((None, tm, w), lambda bi, j: (bi, j, 0))
        per_pos = lambda w: pl.BlockSpec((tm, w), lambda bi, j: (j, 0))
        const = lambda shape: pl.BlockSpec(shape, lambda bi, j: (0, 0))
        return grid, rows, per_pos, const, False, tm
    n = b * t
    tm = min(tm, n)
    assert n % tm == 0 and tm % t == 0
    grid = (n // tm,)
    rows = lambda w: pl.BlockSpec((tm, w), lambda r: (r, 0))
    per_pos = lambda w: pl.BlockSpec((tm, w), lambda r: (0, 0))
    const = lambda shape: pl.BlockSpec(shape, lambda r: (0, 0))
    return grid, rows, per_pos, const, True, tm


def _proj_call(x, pos, norm_g, w_p, q_gain, k_gain, lb_logits, layer, tm):
    b, t, _ = x.shape
    grid, rows, per_pos, const, flat, tm = _row_blocking(b, t, tm)
    cos, sin = _rope_tables(pos)
    if flat:
        x = x.reshape(b * t, D_MODEL)
        cos = jnp.tile(cos, (tm // t, 1))
        sin = jnp.tile(sin, (tm // t, 1))
    expand = np.zeros((LANES, IDX_HEADS * HEAD_DIM), np.float32)
    for hd in range(IDX_HEADS):
        expand[HEAD_DIM + hd, hd * HEAD_DIM:(hd + 1) * HEAD_DIM] = 1.0
    widths = [(A_WIDTH, BF16), (KV_WIDTH, F32), (KV_WIDTH, F32), (2 * KV_WIDTH, BF16), (2 * KV_WIDTH, BF16),
              (IDX_HEADS * HEAD_DIM, BF16), (LANES, F32), (HEAD_DIM, F32), (LANES, BF16), (D_MODEL, BF16),
              (B_WIDTH, F32), (B_WIDTH, F32), (B_WIDTH, F32), (B_WIDTH, F32),
              (C_WIDTH, F32), (C_WIDTH, F32), (C_WIDTH, F32)]
    lead = (b * t,) if flat else (b, t)
    outs = pl.pallas_call(
        functools.partial(_proj_kernel, layer=layer),
        grid=grid,
        in_specs=[rows(D_MODEL),
                  const((1, D_MODEL)),
                  const((D_MODEL, _N_PROJ)),
                  per_pos(LANES),
                  per_pos(LANES),
                  const((1, LANES)),
                  const((1, LANES)),
                  const(lb_logits.shape),
                  const((2 * LANES, LANES)),
                  const((2 * LANES, IDX_HEADS * HEAD_DIM))],
        out_specs=[rows(w) for w, _ in widths],
        out_shape=[jax.ShapeDtypeStruct(lead + (w,), dt) for w, dt in widths],
        compiler_params=pltpu.CompilerParams(dimension_semantics=("parallel",) * len(grid),
                                             vmem_limit_bytes=VMEM_LIMIT),
        name=f"proj_l{layer}",
    )(x, norm_g.reshape(1, D_MODEL), w_p, cos, sin,
      jnp.tile(q_gain, 2).reshape(1, LANES), jnp.tile(k_gain, 2).reshape(1, LANES),
      lb_logits, jnp.tile(_group_mean_matrix(LANES), (2, 1)), jnp.tile(jnp.asarray(expand, BF16), (2, 1)))
    return [o.reshape(b, t, o.shape[-1]) for o in outs] if flat else outs


SEARCH_INTERP_STEPS = 9
SEARCH_WALK_STEPS = 3
SEARCH_RANK_BIAS = 1.5
SEARCH_EDGE = 0.05
SEARCH_SPAN_SD = 8.0
TIE_WALK_STEPS = (4, 12)


def _normal_upper_quantile(r):
    rr = jnp.minimum(r, 1.0 - r)
    t = jnp.sqrt(-2.0 * jnp.log(rr))
    z = t - ((0.010328 * t + 0.802853) * t + 2.515517) / (((0.001308 * t + 0.189269) * t + 1.432788) * t + 1.0)
    return jnp.where(r <= 0.5, z, -z)


def _select_topk(xm_scr, sum1_scr, sum2_scr, lim, topk, thr_scr, lim_scr, bias_scr):
    tq, s = xm_scr.shape
    kf = float(topk)
    nf = lim.astype(F32)
    trivial = lim <= topk

    def count_gt(p):
        return jnp.sum(jnp.where(xm_scr[...] > p, 1.0, 0.0), axis=1, keepdims=True)

    mu = jnp.sum(sum1_scr[...], axis=1, keepdims=True) / nf
    var = jnp.sum(sum2_scr[...], axis=1, keepdims=True) / nf - mu * mu
    sd = jnp.sqrt(jnp.maximum(var, 1e-30))
    target = kf + SEARCH_RANK_BIAS
    lo, hi = mu - SEARCH_SPAN_SD * sd, mu + SEARCH_SPAN_SD * sd
    clo, chi = nf, jnp.zeros_like(nf)
    p = mu + _normal_upper_quantile(jnp.clip(target / nf, 1e-6, 1.0 - 1e-6)) * sd
    lo_counted = jnp.zeros((tq, 1), jnp.bool_)
    for it in range(SEARCH_INTERP_STEPS):
        c = count_gt(p)
        above = c >= kf
        lo = jnp.where(above, p, lo)
        clo = jnp.where(above, c, clo)
        lo_counted = lo_counted | above
        hi = jnp.where(above, hi, p)
        chi = jnp.where(above, chi, c)
        frac = jnp.clip((clo - target) / (clo - chi), SEARCH_EDGE, 1.0 - SEARCH_EDGE)
        p = lo + (hi - lo) * frac

    cur = lo
    thr = lo
    cnt_prev = clo
    cnt_gt = clo
    cnt_eq = jnp.zeros_like(clo)
    done = lo_counted & (clo == kf)
    for it in range(SEARCH_WALK_STEPS):
        xv = xm_scr[...]
        cur = jnp.min(jnp.where(xv > cur, xv, -NEG_BIG), axis=1, keepdims=True)
        c = count_gt(cur)
        hit = (c <= kf) & lo_counted & jnp.logical_not(done)
        thr = jnp.where(hit, cur, thr)
        cnt_gt = jnp.where(hit, c, cnt_gt)
        cnt_eq = jnp.where(hit, cnt_prev - c, cnt_eq)
        done = done | hit
        cnt_prev = c
    thr = jnp.where(trivial, 0.1 * NEG_BIG, thr)
    cnt_gt = jnp.where(trivial, nf, cnt_gt)
    cnt_eq = jnp.where(trivial, 0.0, cnt_eq)
    thr_scr[...] = jnp.broadcast_to(thr, (tq, LANES))
    sum1_scr[...] = jnp.broadcast_to(cnt_gt, (tq, LANES))
    sum2_scr[...] = jnp.broadcast_to(cnt_eq, (tq, LANES))
    proven = trivial | done

    @pl.when(jnp.min(jnp.where(proven, 1.0, 0.0)) < 0.5)
    def _():
        lo0 = jnp.where(lo_counted, lo, 0.5 * NEG_BIG)
        clo0 = jnp.where(lo_counted, clo, nf)

        def unsettled(state):
            rounds, done_f = state[0], state[-1]
            return (rounds < s) & (jnp.min(done_f) < 0.5)

        def one_round(state):
            rounds, lo_, clo_, hi_, chi_, thr_, cgt_, ceq_, done_f = state
            done_ = done_f > 0.5
            frac = jnp.clip((clo_ - target) / (clo_ - chi_), SEARCH_EDGE, 1.0 - SEARCH_EDGE)
            piv = lo_ + (hi_ - lo_) * frac
            c = count_gt(piv)
            above = c >= kf
            lo_ = jnp.where(above, piv, lo_)
            clo_ = jnp.where(above, c, clo_)
            hi_ = jnp.where(above, hi_, piv)
            chi_ = jnp.where(above, chi_, c)
            settled = (clo_ == kf) & jnp.logical_not(done_)
            thr_ = jnp.where(settled, lo_, thr_)
            cgt_ = jnp.where(settled, clo_, cgt_)
            ceq_ = jnp.where(settled, 0.0, ceq_)
            done_ = done_ | settled
            xv = xm_scr[...]
            nxt = jnp.min(jnp.where(xv > lo_, xv, -NEG_BIG), axis=1, keepdims=True)
            c = count_gt(nxt)
            hit = (c <= kf) & jnp.logical_not(done_)
            thr_ = jnp.where(hit, nxt, thr_)
            cgt_ = jnp.where(hit, c, cgt_)
            ceq_ = jnp.where(hit, clo_ - c, ceq_)
            done_ = done_ | hit
            lo_ = jnp.where(hit, lo_, nxt)
            clo_ = jnp.where(hit, clo_, c)
            return (rounds + 1, lo_, clo_, hi_, chi_, thr_, cgt_, ceq_, jnp.where(done_, 1.0, 0.0))

        hi0 = jnp.maximum(hi, lo0)
        state = (jnp.int32(0), lo0, clo0, hi0, chi, thr, cnt_gt, cnt_eq, jnp.where(proven, 1.0, 0.0))
        state = lax.while_loop(unsettled, one_round, state)
        thr_scr[...] = jnp.broadcast_to(state[5], (tq, LANES))
        sum1_scr[...] = jnp.broadcast_to(state[6], (tq, LANES))
        sum2_scr[...] = jnp.broadcast_to(state[7], (tq, LANES))

    thr = thr_scr[:, :1]
    cnt_gt = sum1_scr[:, :1]
    cnt_eq = sum2_scr[:, :1]
    xv = xm_scr[...]
    gt = xv > thr
    eq = xv == thr
    want_eq = kf - cnt_gt

    all_or_none = jnp.where((want_eq >= cnt_eq) & (want_eq > 0.0), POS_BIG, 0)
    lim_scr[...] = jnp.broadcast_to(all_or_none, (tq, LANES))
    partial = (want_eq > 0.0) & (want_eq < cnt_eq)
    kpos = lax.broadcasted_iota(I32, (tq, s), 1)

    @pl.when(jnp.max(jnp.where(partial, 1.0, 0.0)) > 0.5)
    def _():
        eqpos = jnp.where(eq, kpos, POS_BIG)
        eqposf = eqpos.astype(F32)

        def walk(first, last, cur, xlim):
            for it in range(first, last):
                cur = jnp.min(jnp.where(eqposf > cur, eqposf, float(POS_BIG)), axis=1, keepdims=True)
                xlim = jnp.where(partial & (want_eq == float(it)), cur.astype(I32), xlim)
            return cur, xlim

        cur, xlim = walk(0, TIE_WALK_STEPS[0], jnp.full((tq, 1), -1.0, F32), all_or_none)
        lim_scr[...] = jnp.broadcast_to(xlim, (tq, LANES))

        @pl.when(jnp.max(jnp.where(partial & (want_eq >= float(TIE_WALK_STEPS[0])), 1.0, 0.0)) > 0.5)
        def _():
            _, xlim2 = walk(TIE_WALK_STEPS[0], TIE_WALK_STEPS[1], cur, xlim)
            lim_scr[...] = jnp.broadcast_to(xlim2, (tq, LANES))

            @pl.when(jnp.max(jnp.where(partial & (want_eq >= float(TIE_WALK_STEPS[1])), 1.0, 0.0)) > 0.5)
            def _():
                nbits = max(1, int(s).bit_length())

                def pos_step(j, xl):
                    cand = xl | lax.shift_left(jnp.int32(1), nbits - 1 - j)
                    cnt = jnp.sum(jnp.where(eqpos < cand, 1.0, 0.0), axis=1, keepdims=True)
                    return jnp.where(cnt <= want_eq, cand, xl)

                lim_scr[...] = jnp.broadcast_to(lax.fori_loop(0, nbits, pos_step, jnp.zeros((tq, 1), I32)),
                                                (tq, LANES))

    sel = gt | (eq & (kpos < lim_scr[:, :1]))
    bias_scr[...] = jnp.where(sel, 0.0, NEG_BIG).astype(bias_scr.dtype)


def _attn_kernel(*refs, topk, tpos0, n_keys):
    q_ref, iq_ref, sg_ref, kd_ref, vd_ref, ikd_ref, gate_ref = refs[:7]
    o_ref, xm_scr, bias_scr, thr_scr, lim_scr, sum1_scr, sum2_scr = refs[-7:]
    tq = q_ref.shape[1]
    s = kd_ref.shape[1]
    lane = lax.broadcasted_iota(I32, (tq, LANES), 1)
    halves = (lane < HEAD_DIM, lane >= HEAD_DIM)

    tpos = tpos0 + lax.broadcasted_iota(I32, (tq, 1), 0)
    lim = jnp.minimum((tpos // CHUNK + 1) * CHUNK, n_keys)
    search = min(s, n_keys) > topk

    rb = min(tq, LANES)
    key_tile = 2 * LANES
    lane_rb = lax.broadcasted_iota(I32, (rb, LANES), 1)
    halves_rb = (lane_rb < HEAD_DIM, lane_rb >= HEAD_DIM)
    for r in range(tq // rb):
        rs = slice(r * rb, (r + 1) * rb)
        sg = sg_ref[0, rs, :]
        qms, sgs = [], []
        for hd in range(IDX_HEADS):
            qb = iq_ref[0, rs, (hd // 2) * LANES:(hd // 2 + 1) * LANES]
            qms.append(jnp.where(halves_rb[hd % 2], qb, jnp.zeros_like(qb)))
            sgs.append(sg[:, HEAD_DIM + hd:HEAD_DIM + hd + 1])
        acc1 = jnp.zeros((rb, LANES), F32)
        acc2 = jnp.zeros((rb, LANES), F32)
        lim_first = min(((tpos0 + r * rb) // CHUNK + 1) * CHUNK, n_keys)
        lim_last = min(((tpos0 + (r + 1) * rb - 1) // CHUNK + 1) * CHUNK, n_keys)
        for c0 in range(0, s, key_tile):
            w = min(key_tile, s - c0)
            if c0 >= lim_last:
                xm_scr[rs, c0:c0 + w] = jnp.full((rb, w), NEG_BIG, F32)
                continue
            ikd_t = ikd_ref[0, c0:c0 + w, :]
            acc = jnp.zeros((rb, w), F32)
            for hd in range(IDX_HEADS):
                acc = acc + sgs[hd] * jnp.maximum(_nt_dot(qms[hd], ikd_t), 0.0)
            if c0 + w <= lim_first:
                xm_scr[rs, c0:c0 + w] = acc
                sc0 = acc
            else:
                adm_t = (c0 + lax.broadcasted_iota(I32, (rb, w), 1)) < lim[rs]
                xm_scr[rs, c0:c0 + w] = jnp.where(adm_t, acc, NEG_BIG)
                sc0 = jnp.where(adm_t, acc, 0.0)
            if search:
                for l0 in range(0, w, LANES):
                    part = sc0[:, l0:l0 + LANES]
                    acc1 = acc1 + part
                    acc2 = acc2 + part * part
        if search:
            sum1_scr[rs, :] = acc1
            sum2_scr[rs, :] = acc2

    logits = []
    for c in range(A_WIDTH // LANES):
        g = (2 * c) // (A_HEADS // A_KV_HEADS)
        qb = q_ref[0, :, c * LANES:(c + 1) * LANES]
        for e in range(2):
            qm = jnp.where(halves[e], qb, jnp.zeros_like(qb))
            logits.append(_nt_dot(qm, kd_ref[0, :, g * LANES:(g + 1) * LANES]))

    if search:
        _select_topk(xm_scr, sum1_scr, sum2_scr, lim, topk, thr_scr, lim_scr, bias_scr)
    else:
        bias_scr[...] = jnp.where(lax.broadcasted_iota(I32, (tq, s), 1) < lim, 0.0, NEG_BIG).astype(BF16)

    ones = jnp.ones((s, LANES), BF16)
    for g in range(A_KV_HEADS):
        vd1 = jnp.concatenate([vd_ref[0, :, g * LANES:(g + 1) * LANES], ones], axis=1)
        for c in range(g * 2, g * 2 + 2):
            outs = []
            for e in range(2):
                lg = logits[2 * c + e].astype(BF16) + bias_scr[...]
                p = jnp.exp2(lg - jnp.max(lg, axis=1, keepdims=True))
                ol = jnp.dot(p, vd1, preferred_element_type=F32)
                outs.append(ol[:, :LANES] / ol[:, LANES:])
            gate = gate_ref[0, :, c * LANES:(c + 1) * LANES].astype(F32)
            o_ref[0, :, c * LANES:(c + 1) * LANES] = (jnp.where(halves[0], outs[0], outs[1]) * gate).astype(BF16)


def _attn_call(q, iq, sg, kd, vd, ikd, gate, prev, *, topk, qpos0, n_keys, tq, qblock, s_blk, name):
    b, t, _ = q.shape
    qmap = lambda bi: (bi, qblock, 0)
    kmap = lambda bi: (bi, 0, 0)
    return pl.pallas_call(
        functools.partial(_attn_kernel, topk=topk, tpos0=qpos0 + qblock * tq, n_keys=n_keys),
        grid=(b,),
        in_specs=[pl.BlockSpec((1, tq, A_WIDTH), qmap),
                  pl.BlockSpec((1, tq, IDX_HEADS * HEAD_DIM), qmap),
                  pl.BlockSpec((1, tq, LANES), qmap),
                  pl.BlockSpec((1, s_blk, 2 * KV_WIDTH), kmap),
                  pl.BlockSpec((1, s_blk, 2 * KV_WIDTH), kmap),
                  pl.BlockSpec((1, s_blk, LANES), kmap),
                  pl.BlockSpec((1, tq, A_WIDTH), qmap),
                  pl.BlockSpec(memory_space=pl.ANY)],
        out_specs=pl.BlockSpec((1, tq, A_WIDTH), qmap),
        out_shape=jax.ShapeDtypeStruct((b, t, A_WIDTH), BF16),
        input_output_aliases={7: 0},
        scratch_shapes=[pltpu.VMEM((tq, s_blk), F32), pltpu.VMEM((tq, s_blk), BF16),
                        pltpu.VMEM((tq, LANES), F32), pltpu.VMEM((tq, LANES), I32),
                        pltpu.VMEM((tq, LANES), F32), pltpu.VMEM((tq, LANES), F32)],
        compiler_params=pltpu.CompilerParams(dimension_semantics=("parallel",),
                                             vmem_limit_bytes=VMEM_LIMIT),
        name=name,
    )(q, iq, sg, kd, vd, ikd, gate, prev)


def _attention(q, iq, sg, kd, vd, ikd, gate, *, topk, qpos0, n_keys, tag):
    b, t, _ = q.shape
    tq = min(t, ATTN_ROWS)
    out = jnp.zeros((b, t, A_WIDTH), BF16)
    for qblock in range(t // tq):
        last_pos = qpos0 + (qblock + 1) * tq - 1
        visible = min((last_pos // CHUNK + 1) * CHUNK, n_keys)
        s_blk = min(-(-visible // LANES) * LANES, kd.shape[1])
        out = _attn_call(q, iq, sg, kd, vd, ikd, gate, out, topk=topk, qpos0=qpos0, n_keys=n_keys, tq=tq,
                         qblock=qblock, s_blk=s_blk, name=f"attn_{tag}_q{qblock}")
    return out


def _retention_log_decay():
    return jnp.log(1.0 - jnp.exp(jnp.linspace(math.log(1.0 / 32), math.log(1.0 / 512), REC_HEADS, dtype=F32)))


def _rec_kernel(bq_ref, bk_ref, bv_ref, blf_ref, cq_ref, ck_ref, cv_ref, gb_ref, gc_ref, s0b_ref, s0c_ref,
                hg_ref, rg_ref, tri_ref, dmat_ref, inner_ref, kscale_ref, rdec_ref, gm_ref, x_ref, ya_ref, w_ref,
                y_ref, sb_out, sc_out, sb_scr, sc_scr, ybc_scr, *, chunk, n_chunks):
    j = pl.program_id(1)
    width = B_WIDTH

    @pl.when(j == 0)
    def _():
        sb_scr[...] = s0b_ref[0]
        sc_scr[...] = s0c_ref[0]

    rowh = lax.broadcasted_iota(I32, (width, width), 0) // HEAD_DIM
    colh = lax.broadcasted_iota(I32, (width, width), 1) // HEAD_DIM
    block_diag = rowh == colh
    lane_h = lax.broadcasted_iota(I32, (chunk, width), 1) // HEAD_DIM
    causal = (lax.broadcasted_iota(I32, (chunk, chunk), 0)
              >= lax.broadcasted_iota(I32, (chunk, chunk), 1))
    gmat = gm_ref[...]
    tri = tri_ref[...]
    tn = (((0,), (0,)), ((), ()))

    def masked_heads(x):
        return [jnp.where(lane_h == hd, x, 0.0).astype(BF16) for hd in range(REC_HEADS)]

    heads = range(REC_HEADS)
    chunks = range(n_chunks)
    rows = [pl.ds(c * chunk, chunk) for c in chunks]

    bcs = []
    for c in chunks:
        lf = blf_ref[0, rows[c], :]
        a1 = lf.astype(BF16)
        r1 = lf - a1.astype(F32)
        a2 = r1.astype(BF16)
        a3 = (r1 - a2.astype(F32)).astype(BF16)
        bcs.append(jnp.dot(tri, jnp.concatenate([a1, a2, a3], axis=0), preferred_element_type=F32))

    att_b, att_c, upd_b, upd_c, qs_b, qs_c, dec_b, vb_b, vb_c = ([] for _ in range(9))
    for c in chunks:
        bc = bcs[c]
        mid = bc[chunk // 2 - 1:chunk // 2, :]
        last = bc[chunk - 1:chunk, :]
        q = bq_ref[0, rows[c], :]
        k = bk_ref[0, rows[c], :]
        vb = bv_ref[0, rows[c], :].astype(BF16)
        kt = (k * jnp.exp(mid - bc)).astype(BF16)
        att_b.append([_nt_dot(qh, kt) for qh in masked_heads(q * jnp.exp(bc - mid))])
        upd_b.append(lax.dot_general(vb, (k * jnp.exp(last - bc)).astype(BF16), tn, preferred_element_type=F32))
        qs_b.append((q * jnp.exp(bc)).astype(BF16))
        dec_b.append(jnp.exp(last))
        vb_b.append(vb)
        q = cq_ref[0, rows[c], :]
        k = ck_ref[0, rows[c], :]
        vb = cv_ref[0, rows[c], :].astype(BF16)
        kb = k.astype(BF16)
        att_c.append([_nt_dot(qh, kb) for qh in masked_heads(q)])
        upd_c.append(lax.dot_general(vb, (k * kscale_ref[...]).astype(BF16), tn, preferred_element_type=F32))
        qs_c.append((q * inner_ref[...]).astype(BF16))
        vb_c.append(vb)

    st = sb_scr[...]
    rt = sc_scr[...]
    o_b, o_c = [], []
    for c in chunks:
        o_b.append(_nt_dot(qs_b[c], st.astype(BF16)))
        st = st * dec_b[c] + jnp.where(block_diag, upd_b[c], 0.0)
        o_c.append(_nt_dot(qs_c[c], rt.astype(BF16)))
        rt = rt * rdec_ref[...] + jnp.where(block_diag, upd_c[c], 0.0)
    sb_scr[...] = st
    sc_scr[...] = rt

    def blockdiag_rows(v16):
        return jnp.concatenate([jnp.where(lane_h == hd, v16, jnp.zeros_like(v16)) for hd in heads], axis=0)

    prod_b = [jnp.dot(jnp.concatenate([jnp.where(causal, att_b[c][hd], 0.0) for hd in heads], axis=1).astype(BF16),
                      blockdiag_rows(vb_b[c]), preferred_element_type=F32) for c in chunks]
    prod_c = [jnp.dot(jnp.concatenate([att_c[c][hd] * dmat_ref[hd * chunk:(hd + 1) * chunk, :] for hd in heads],
                                      axis=1).astype(BF16),
                      blockdiag_rows(vb_c[c]), preferred_element_type=F32) for c in chunks]
    outs = []
    for c in chunks:
        outs.append(o_b[c] + prod_b[c])
        outs.append(o_c[c] + prod_c[c])

    means = []
    for o in outs:
        hi, lo = _split2(o * o)
        means.append(jnp.dot(jnp.concatenate([hi, lo], axis=1), gmat, preferred_element_type=F32))
    for c in chunks:
        yb = outs[2 * c] * lax.rsqrt(means[2 * c] + EPS) * hg_ref[...]
        yc = outs[2 * c + 1] * lax.rsqrt(means[2 * c + 1] + EPS) * rg_ref[...]
        ybc_scr[rows[c], :B_WIDTH] = (yb * gb_ref[0, rows[c], :].astype(F32)).astype(BF16)
        ybc_scr[rows[c], B_WIDTH:] = (yc * gc_ref[0, rows[c], :].astype(F32)).astype(BF16)

    y_ref[0] = (x_ref[0]
                + jnp.dot(ya_ref[0], w_ref[:A_WIDTH, :], preferred_element_type=F32)
                + jnp.dot(ybc_scr[...], w_ref[A_WIDTH:, :], preferred_element_type=F32))

    @pl.when(j == pl.num_programs(1) - 1)
    def _():
        sb_out[0] = sb_scr[...]
        sc_out[0] = sc_scr[...]


def _state_to_blockdiag(s):
    b = s.shape[0]
    eye = jnp.eye(REC_HEADS, dtype=s.dtype)
    return jnp.einsum('bhkv,hg->bhvgk', s, eye).reshape(b, B_WIDTH, B_WIDTH)


def _state_from_blockdiag(sbd):
    b = sbd.shape[0]
    s5 = sbd.reshape(b, REC_HEADS, HEAD_DIM, REC_HEADS, HEAD_DIM)
    return jnp.einsum('bhvhk->bhkv', s5)


def _rec_out_call(bq, bk, bv, blf, cq, ck, cv, gate, s0b, s0c, hgain, rgain, x, ya, w_out16, name):
    b, t, _ = bq.shape
    chunk = min(CHUNK, t)
    tb = min(t, REC_CHUNKS_PER_STEP * chunk)
    n_chunks = tb // chunk
    lg = _retention_log_decay()
    n = jnp.arange(chunk, dtype=F32)
    diff = n[:, None] - n[None, :]
    dmat = jnp.where(diff >= 0, jnp.exp(jnp.where(diff >= 0, diff, 0.0)[None] * lg[:, None, None]), 0.0)
    dmat = dmat.reshape(REC_HEADS * chunk, chunk)
    per_lane = lambda a: jnp.repeat(a, HEAD_DIM, axis=-1)
    inner = per_lane(jnp.exp((n[:, None] + 1.0) * lg[None, :]))
    kscale = per_lane(jnp.exp((chunk - 1.0 - n)[:, None] * lg[None, :]))
    rdec = per_lane(jnp.exp(chunk * lg)[None, :])
    tri = jnp.tile(jnp.tril(jnp.ones((chunk, chunk), F32)), (1, 3)).astype(BF16)
    seq = lambda bi, j: (bi, j, 0)
    per_b = lambda bi, j: (bi, 0, 0)
    const = lambda bi, j: (0, 0)
    stream = pl.BlockSpec((1, tb, B_WIDTH), seq)
    state = pl.BlockSpec((1, B_WIDTH, B_WIDTH), per_b)
    return pl.pallas_call(
        functools.partial(_rec_kernel, chunk=chunk, n_chunks=n_chunks),
        grid=(b, t // tb),
        in_specs=[stream] * 7 + [pl.BlockSpec((1, tb, B_WIDTH), lambda bi, j: (bi, j, A_WIDTH // B_WIDTH)),
                                 pl.BlockSpec((1, tb, C_WIDTH), lambda bi, j: (bi, j, (A_WIDTH + B_WIDTH) // C_WIDTH)),
                                 state, state,
                                 pl.BlockSpec((1, B_WIDTH), const), pl.BlockSpec((1, B_WIDTH), const),
                                 pl.BlockSpec((chunk, 3 * chunk), const),
                                 pl.BlockSpec((REC_HEADS * chunk, chunk), const),
                                 pl.BlockSpec((chunk, B_WIDTH), const), pl.BlockSpec((chunk, B_WIDTH), const),
                                 pl.BlockSpec((1, B_WIDTH), const),
                                 pl.BlockSpec((2 * B_WIDTH, B_WIDTH), const),
                                 pl.BlockSpec((1, tb, D_MODEL), seq),
                                 pl.BlockSpec((1, tb, A_WIDTH), seq),
                                 pl.BlockSpec((D_MODEL, D_MODEL), const)],
        out_specs=[pl.BlockSpec((1, tb, D_MODEL), seq), state, state],
        out_shape=[jax.ShapeDtypeStruct((b, t, D_MODEL), F32),
                   jax.ShapeDtypeStruct((b, B_WIDTH, B_WIDTH), F32),
                   jax.ShapeDtypeStruct((b, C_WIDTH, C_WIDTH), F32)],
        scratch_shapes=[pltpu.VMEM((B_WIDTH, B_WIDTH), F32), pltpu.VMEM((C_WIDTH, C_WIDTH), F32),
                        pltpu.VMEM((tb, B_WIDTH + C_WIDTH), BF16)],
        compiler_params=pltpu.CompilerParams(dimension_semantics=("parallel", "arbitrary"),
                                             vmem_limit_bytes=VMEM_LIMIT),
        name=name,
    )(bq, bk, bv, blf, cq, ck, cv, gate, gate, s0b, s0c,
      jnp.tile(hgain, REC_HEADS).reshape(1, B_WIDTH), jnp.tile(rgain, REC_HEADS).reshape(1, C_WIDTH),
      tri, dmat, inner, kscale, rdec, jnp.tile(_group_mean_matrix(B_WIDTH), (2, 1)), x, ya, w_out16)


def _with_past(past_parts, new, s_pad):
    past = jnp.concatenate([p.astype(BF16) for p in past_parts], axis=-1)
    b, n_old, width = past.shape
    tail = jnp.zeros((b, s_pad - n_old - new.shape[1], width), BF16)
    return jnp.concatenate([past, new, tail], axis=1)


def _mixer_layer(x, pos0, past, s_hgrn, s_ret, layer, w_p, w_out16, norm_g, q_gain, k_gain,
                 lb_logits, hgain, rgain, tag):
    b, t, _ = x.shape
    pos = pos0 + jnp.arange(t)
    (q, k, v, kd3, vd3, iq, sg, ik, ikd3, gate, bq, bk, bv, blf, cq, ck, cv) = _proj_call(
        x, pos, norm_g, w_p, q_gain, k_gain, lb_logits, layer, PROJ_ROWS)

    n_keys = t if past is None else past[0].shape[1] + t
    s_pad = -(-n_keys // LANES) * LANES
    if past is not None:
        pk, pv, pik = past
        kd3 = _with_past([pk[:, :, 0], pk[:, :, 0], pk[:, :, 1], pk[:, :, 1]], kd3, s_pad)
        vd3 = _with_past([pv[:, :, 0], pv[:, :, 0], pv[:, :, 1], pv[:, :, 1]], vd3, s_pad)
        ikd3 = _with_past([pik, pik], ikd3, s_pad)
    elif s_pad != n_keys:
        padw = ((0, 0), (0, s_pad - n_keys), (0, 0))
        kd3, vd3, ikd3 = jnp.pad(kd3, padw), jnp.pad(vd3, padw), jnp.pad(ikd3, padw)
    topk = min(TOPK_MAX, n_keys // 4)
    oa = _attention(q, iq, sg, kd3, vd3, ikd3, gate, topk=topk, qpos0=pos0, n_keys=n_keys, tag=tag)

    out, sb, sc = _rec_out_call(bq, bk, bv, blf, cq, ck, cv, gate, s_hgrn, s_ret, hgain, rgain, x, oa, w_out16,
                                name=f"rec_{tag}")
    return (out,
            (k.reshape(b, t, A_KV_HEADS, HEAD_DIM), v.reshape(b, t, A_KV_HEADS, HEAD_DIM), ik,
             _state_from_blockdiag(sb), _state_from_blockdiag(sc)))


def kernel(x_prompt, x_sample, cache_k, cache_v, cache_idx_k, state_hgrn, state_ret, norm_g, w_in,
           q_norm_g, k_norm_g, hgrn_lb_logits, hgrn_norm_g, ret_norm_g, w_out):
    depth = w_in.shape[0]
    bp, tp = x_prompt.shape[:2]
    bs, ts = x_sample.shape[:2]
    past = cache_k.shape[2]
    zero_state = jnp.zeros((bp, B_WIDTH, B_WIDTH), F32)

    yp, ys = x_prompt, x_sample
    outs_p, outs_s = [], []
    for l in range(depth):
        w_p = _prep_w_in(w_in[l])
        w_o = w_out[l].astype(BF16)
        args = (l, w_p, w_o, norm_g[l], q_norm_g[l], k_norm_g[l], hgrn_lb_logits, hgrn_norm_g[l], ret_norm_g[l])
        yp, st = _mixer_layer(yp, 0, None, zero_state, zero_state, *args, tag=f"p{l}")
        outs_p.append(st)
        ys, st = _mixer_layer(ys, past, (cache_k[l], cache_v[l], cache_idx_k[l]),
                              _state_to_blockdiag(state_hgrn[l]), _state_to_blockdiag(state_ret[l]), *args,
                              tag=f"s{l}")
        outs_s.append(st)

    def stack(outs, i, shape):
        return jnp.stack([o[i] for o in outs]).reshape(shape)

    return (yp, ys,
            stack(outs_p, 0, (depth, bp, tp, A_KV_HEADS, HEAD_DIM)),
            stack(outs_p, 1, (depth, bp, tp, A_KV_HEADS, HEAD_DIM)),
            stack(outs_p, 2, (depth, bp, tp, HEAD_DIM)),
            stack(outs_p, 3, (depth, bp, REC_HEADS, HEAD_DIM, HEAD_DIM)),
            stack(outs_p, 4, (depth, bp, REC_HEADS, HEAD_DIM, HEAD_DIM)),
            stack(outs_s, 0, (depth, bs, ts, A_KV_HEADS, HEAD_DIM)),
            stack(outs_s, 1, (depth, bs, ts, A_KV_HEADS, HEAD_DIM)),
            stack(outs_s, 2, (depth, bs, ts, HEAD_DIM)),
            stack(outs_s, 3, (depth, bs, REC_HEADS, HEAD_DIM, HEAD_DIM)),
            stack(outs_s, 4, (depth, bs, REC_HEADS, HEAD_DIM, HEAD_DIM)))
```

```python
import functools
import math

import numpy as np
import jax
import jax.numpy as jnp
from jax import lax
from jax.experimental import pallas as pl
from jax.experimental.pallas import tpu as pltpu

F32 = jnp.float32
BF16 = jnp.bfloat16
I32 = jnp.int32

D_MODEL = 1024
HEAD_DIM = 64
CHUNK = 64
A_WIDTH = 512
A_HEADS = 8
A_KV_HEADS = 2
KV_WIDTH = A_KV_HEADS * HEAD_DIM
IDX_HEADS = 8
IDX_W_SCALE = (IDX_HEADS * HEAD_DIM) ** -0.5
TOPK_MAX = 256
B_WIDTH = 256
C_WIDTH = 256
REC_HEADS = 4
ROPE_THETA = 10000.0
EPS = 1e-6
LANES = 128
VMEM_LIMIT = 48 * 1024 * 1024
PROJ_ROWS = 512
ATTN_ROWS = 256
REC_CHUNKS_PER_STEP = 8

_SIZES = (A_WIDTH, KV_WIDTH, KV_WIDTH, A_WIDTH, IDX_HEADS * HEAD_DIM, HEAD_DIM, IDX_HEADS,
          B_WIDTH, B_WIDTH, B_WIDTH, B_WIDTH, C_WIDTH, C_WIDTH, C_WIDTH, C_WIDTH)
_SPLIT_IDX = tuple(int(s) for s in np.cumsum(_SIZES)[:-1])

_AQ, _AK, _AV, _AG, _IQ, _IKW = 0, 512, 640, 768, 1280, 1792
_BQ, _BF, _BI, _BG, _CQ, _CK, _CV, _CG = 1920, 2176, 2432, 2688, 2944, 3200, 3456, 3712
_N_PROJ = 3968

NEG_BIG = -1e30
POS_BIG = 1 << 20
LOG2_E = math.log2(math.e)


def _prep_w_in(w):
    (aq, ak, av, ag, iq, ik, iw, bq, bf, bi, bg, cq, ck, cv, cg) = jnp.split(w, _SPLIT_IDX, axis=1)
    pad = jnp.zeros((w.shape[0], LANES - HEAD_DIM - IDX_HEADS), w.dtype)
    return jnp.concatenate([aq, ak, av, ag, iq, ik, iw, pad, bq, bf, bi, bg, cq, ck, cv, cg],
                           axis=1).astype(BF16)


def _rope_tables(pos):
    half = HEAD_DIM // 2
    freqs = ROPE_THETA ** (-jnp.arange(half, dtype=F32) / half)
    ang = pos.astype(F32)[:, None] * freqs[None, :]
    cos, sin = jnp.cos(ang), jnp.sin(ang)
    cos64 = jnp.concatenate([cos, cos], axis=1)
    sin64 = jnp.concatenate([-sin, sin], axis=1)
    return jnp.concatenate([cos64, cos64], axis=1), jnp.concatenate([sin64, sin64], axis=1)


def _group_mean_matrix(width):
    idx = np.arange(width) // HEAD_DIM
    return jnp.asarray((idx[:, None] == idx[None, :]).astype(np.float32) / HEAD_DIM, BF16)


def _split2(x):
    hi = x.astype(BF16)
    lo = (x - hi.astype(F32)).astype(BF16)
    return hi, lo


def _nt_dot(a, b):
    return lax.dot_general(a, b, (((1,), (1,)), ((), ())), preferred_element_type=F32)


def _proj_kernel(x_ref, g_ref, w_ref, cos_ref, sin_ref, qg_ref, kg_ref, lb_ref, gm_ref, ex_ref,
                 q_out, k_out, v_out, kd_out, vd_out, iq_out, sg_out, ik_out, ikd_out, gate_out,
                 bq_out, bk_out, bv_out, blf_out, cq_out, ck_out, cv_out, *, layer):
    x = x_ref[...]
    tm = x.shape[0]
    ms = jnp.mean(x * x, axis=-1, keepdims=True)
    h = (x * lax.rsqrt(ms + EPS) * g_ref[...]).astype(BF16)
    cos = cos_ref[...]
    sin = sin_ref[...]
    lane = lax.broadcasted_iota(I32, (tm, LANES), 1)
    first_half = (lane % HEAD_DIM) < (HEAD_DIM // 2)
    low_head = lane < HEAD_DIM
    gmat = gm_ref[...]

    def proj(c0, width):
        return jnp.dot(h, w_ref[:, c0:c0 + width], preferred_element_type=F32)

    def rope(xb):
        swapped = jnp.where(first_half, pltpu.roll(xb, LANES - HEAD_DIM // 2, 1),
                            pltpu.roll(xb, HEAD_DIM // 2, 1))
        return xb * cos + swapped * sin

    def head_norm(xb, gain):
        hi, lo = _split2(xb * xb)
        msq = jnp.dot(jnp.concatenate([hi, lo], axis=1), gmat, preferred_element_type=F32)
        return xb * lax.rsqrt(msq + EPS) * gain

    def dup_heads(xb):
        other = pltpu.roll(xb, HEAD_DIM, 1)
        return jnp.where(low_head, xb, other), jnp.where(low_head, other, xb)

    ag = proj(_AG, A_WIDTH)
    bg = proj(_BG, B_WIDTH)
    cg = proj(_CG, C_WIDTH)
    bq = proj(_BQ, B_WIDTH)
    bf = proj(_BF, B_WIDTH)
    bi = proj(_BI, B_WIDTH)
    cq = proj(_CQ, C_WIDTH)
    ck = proj(_CK, C_WIDTH)
    cv = proj(_CV, C_WIDTH)
    akv = proj(_AK, 2 * KV_WIDTH)
    ikw = proj(_IKW, LANES)
    aq = proj(_AQ, A_WIDTH)
    iq = proj(_IQ, IDX_HEADS * HEAD_DIM)

    def silu(z):
        return z / (1.0 + jnp.exp(-z))

    gate_out[:, :A_WIDTH] = silu(ag).astype(BF16)
    gate_out[:, A_WIDTH:A_WIDTH + B_WIDTH] = silu(bg).astype(BF16)
    gate_out[:, A_WIDTH + B_WIDTH:] = silu(cg).astype(BF16)

    logits = lb_ref[...]
    e = jnp.exp(logits - jnp.max(logits, axis=0, keepdims=True))
    sm = e / jnp.sum(e, axis=0, keepdims=True)
    lb = jnp.zeros((1, B_WIDTH), F32)
    for j in range(1, layer + 1):
        lb = lb + sm[j:j + 1, :]
    f = lb + (1.0 - lb) / (1.0 + jnp.exp(-bf))
    bq_out[...] = bq.astype(BF16)
    bk_out[...] = ((1.0 - lb) / (1.0 + jnp.exp(bf))).astype(BF16)
    bv_out[...] = bi.astype(BF16)
    blf_out[...] = jnp.log(f)

    for c in range(C_WIDTH // LANES):
        sl = slice(c * LANES, (c + 1) * LANES)
        cq_out[:, sl] = rope(cq[:, sl]).astype(BF16)
        ck_out[:, sl] = (rope(ck[:, sl]) * (HEAD_DIM ** -0.5)).astype(BF16)
    cv_out[...] = cv.astype(BF16)

    v = akv[:, KV_WIDTH:]
    v_out[...] = v
    v0, v1 = dup_heads(v)
    vd_out[:, :LANES] = v0.astype(BF16)
    vd_out[:, LANES:] = v1.astype(BF16)
    ikr = rope(ikw)
    ik_out[...] = ikr[:, :HEAD_DIM]
    ikd_out[...] = dup_heads(ikr)[0].astype(BF16)
    sg_out[...] = jnp.where(ikw >= 0, 1.0, -1.0).astype(F32)

    for c in range(A_WIDTH // LANES):
        blk = rope(head_norm(aq[:, c * LANES:(c + 1) * LANES], qg_ref[...]))
        q_out[:, c * LANES:(c + 1) * LANES] = (blk * (HEAD_DIM ** -0.5 * LOG2_E)).astype(BF16)
    k = rope(head_norm(akv[:, :KV_WIDTH], kg_ref[...]))
    k_out[...] = k
    k0, k1 = dup_heads(k)
    kd_out[:, :LANES] = k0.astype(BF16)
    kd_out[:, LANES:] = k1.astype(BF16)

    whi, wlo = _split2(jnp.abs(ikw) * IDX_W_SCALE)
    wexp = jnp.dot(jnp.concatenate([whi, wlo], axis=1), ex_ref[...], preferred_element_type=F32)
    for c in range(IDX_HEADS * HEAD_DIM // LANES):
        sl = slice(c * LANES, (c + 1) * LANES)
        iq_out[:, sl] = (rope(iq[:, sl]) * wexp[:, sl]).astype(BF16)


def _row_blocking(b, t, tm):
    if t % tm == 0:
        grid = (b, t // tm)
        rows = lambda w: pl.BlockSpec((None, tm, w), lambda bi, j: (bi, j, 0))
        per_pos = lambda w: pl.BlockSpec((tm, w), lambda bi, j: (j, 0))
        const = lambda shape: pl.BlockSpec(shape, lambda bi, j: (0, 0))
        return grid, rows, per_pos, const, False, tm
    n = b * t
    tm = min(tm, n)
    assert n % tm == 0 and tm % t == 0
    grid = (n // tm,)
    rows = lambda w: pl.BlockSpec((tm, w), lambda r: (r, 0))
    per_pos = lambda w: pl.BlockSpec((tm, w), lambda r: (0, 0))
    const = lambda shape: pl.BlockSpec(shape, lambda r: (0, 0))
    return grid, rows, per_pos, const, True, tm


def _proj_call(x, pos, norm_g, w_p, q_gain, k_gain, lb_logits, layer, tm):
    b, t, _ = x.shape
    grid, rows, per_pos, const, flat, tm = _row_blocking(b, t, tm)
    cos, sin = _rope_tables(pos)
    if flat:
        x = x.reshape(b * t, D_MODEL)
        cos = jnp.tile(cos, (tm // t, 1))
        sin = jnp.tile(sin, (tm // t, 1))
    expand = np.zeros((LANES, IDX_HEADS * HEAD_DIM), np.float32)
    for hd in range(IDX_HEADS):
        expand[HEAD_DIM + hd, hd * HEAD_DIM:(hd + 1) * HEAD_DIM] = 1.0
    widths = [(A_WIDTH, BF16), (KV_WIDTH, F32), (KV_WIDTH, F32), (2 * KV_WIDTH, BF16), (2 * KV_WIDTH, BF16),
              (IDX_HEADS * HEAD_DIM, BF16), (LANES, F32), (HEAD_DIM, F32), (LANES, BF16), (D_MODEL, BF16),
              (B_WIDTH, BF16), (B_WIDTH, BF16), (B_WIDTH, BF16), (B_WIDTH, F32),
              (C_WIDTH, BF16), (C_WIDTH, BF16), (C_WIDTH, BF16)]
    lead = (b * t,) if flat else (b, t)
    outs = pl.pallas_call(
        functools.partial(_proj_kernel, layer=layer),
        grid=grid,
        in_specs=[rows(D_MODEL),
                  const((1, D_MODEL)),
                  const((D_MODEL, _N_PROJ)),
                  per_pos(LANES),
                  per_pos(LANES),
                  const((1, LANES)),
                  const((1, LANES)),
                  const(lb_logits.shape),
                  const((2 * LANES, LANES)),
                  const((2 * LANES, IDX_HEADS * HEAD_DIM))],
        out_specs=[rows(w) for w, _ in widths],
        out_shape=[jax.ShapeDtypeStruct(lead + (w,), dt) for w, dt in widths],
        compiler_params=pltpu.CompilerParams(dimension_semantics=("parallel",) * len(grid),
                                             vmem_limit_bytes=VMEM_LIMIT),
        name=f"proj_l{layer}",
    )(x, norm_g.reshape(1, D_MODEL), w_p, cos, sin,
      jnp.tile(q_gain, 2).reshape(1, LANES), jnp.tile(k_gain, 2).reshape(1, LANES),
      lb_logits, jnp.tile(_group_mean_matrix(LANES), (2, 1)), jnp.tile(jnp.asarray(expand, BF16), (2, 1)))
    return [o.reshape(b, t, o.shape[-1]) for o in outs] if flat else outs


SEARCH_INTERP_STEPS = 8
SEARCH_WALK_STEPS = 3
SEARCH_RANK_BIAS = 1.5
SEARCH_EDGE = 0.05
SEARCH_SPAN_SD = 8.0
TIE_WALK_STEPS = (4, 12)


def _normal_upper_quantile(r):
    rr = jnp.minimum(r, 1.0 - r)
    t = jnp.sqrt(-2.0 * jnp.log(rr))
    z = t - ((0.010328 * t + 0.802853) * t + 2.515517) / (((0.001308 * t + 0.189269) * t + 1.432788) * t + 1.0)
    return jnp.where(r <= 0.5, z, -z)


def _select_topk(xm_scr, sum1_scr, sum2_scr, lim, topk, thr_scr, lim_scr, bias_scr):
    tq, s = xm_scr.shape
    kf = float(topk)
    nf = lim.astype(F32)
    trivial = lim <= topk

    def count_gt(p):
        return jnp.sum(jnp.where(xm_scr[...] > p, 1.0, 0.0), axis=1, keepdims=True)

    mu = jnp.sum(sum1_scr[...], axis=1, keepdims=True) / nf
    var = jnp.sum(sum2_scr[...], axis=1, keepdims=True) / nf - mu * mu
    sd = jnp.sqrt(jnp.maximum(var, 1e-30))
    target = kf + SEARCH_RANK_BIAS
    lo, hi = mu - SEARCH_SPAN_SD * sd, mu + SEARCH_SPAN_SD * sd
    clo, chi = nf, jnp.zeros_like(nf)
    p = mu + _normal_upper_quantile(jnp.clip(target / nf, 1e-6, 1.0 - 1e-6)) * sd
    lo_counted = jnp.zeros((tq, 1), jnp.bool_)
    for it in range(SEARCH_INTERP_STEPS):
        c = count_gt(p)
        above = c >= kf
        lo = jnp.where(above, p, lo)
        clo = jnp.where(above, c, clo)
        lo_counted = lo_counted | above
        hi = jnp.where(above, hi, p)
        chi = jnp.where(above, chi, c)
        frac = jnp.clip((clo - target) / (clo - chi), SEARCH_EDGE, 1.0 - SEARCH_EDGE)
        p = lo + (hi - lo) * frac

    cur = lo
    thr = lo
    cnt_prev = clo
    cnt_gt = clo
    cnt_eq = jnp.zeros_like(clo)
    done = lo_counted & (clo == kf)
    for it in range(SEARCH_WALK_STEPS):
        xv = xm_scr[...]
        cur = jnp.min(jnp.where(xv > cur, xv, -NEG_BIG), axis=1, keepdims=True)
        c = count_gt(cur)
        hit = (c <= kf) & lo_counted & jnp.logical_not(done)
        thr = jnp.where(hit, cur, thr)
        cnt_gt = jnp.where(hit, c, cnt_gt)
        cnt_eq = jnp.where(hit, cnt_prev - c, cnt_eq)
        done = done | hit
        cnt_prev = c
    thr = jnp.where(trivial, 0.1 * NEG_BIG, thr)
    cnt_gt = jnp.where(trivial, nf, cnt_gt)
    cnt_eq = jnp.where(trivial, 0.0, cnt_eq)
    thr_scr[...] = jnp.broadcast_to(thr, (tq, LANES))
    sum1_scr[...] = jnp.broadcast_to(cnt_gt, (tq, LANES))
    sum2_scr[...] = jnp.broadcast_to(cnt_eq, (tq, LANES))
    proven = trivial | done

    @pl.when(jnp.min(jnp.where(proven, 1.0, 0.0)) < 0.5)
    def _():
        lo0 = jnp.where(lo_counted, lo, 0.5 * NEG_BIG)
        clo0 = jnp.where(lo_counted, clo, nf)

        def unsettled(state):
            rounds, done_f = state[0], state[-1]
            return (rounds < s) & (jnp.min(done_f) < 0.5)

        def one_round(state):
            rounds, lo_, clo_, hi_, chi_, thr_, cgt_, ceq_, done_f = state
            done_ = done_f > 0.5
            frac = jnp.clip((clo_ - target) / (clo_ - chi_), SEARCH_EDGE, 1.0 - SEARCH_EDGE)
            piv = lo_ + (hi_ - lo_) * frac
            c = count_gt(piv)
            above = c >= kf
            lo_ = jnp.where(above, piv, lo_)
            clo_ = jnp.where(above, c, clo_)
            hi_ = jnp.where(above, hi_, piv)
            chi_ = jnp.where(above, chi_, c)
            settled = (clo_ == kf) & jnp.logical_not(done_)
            thr_ = jnp.where(settled, lo_, thr_)
            cgt_ = jnp.where(settled, clo_, cgt_)
            ceq_ = jnp.where(settled, 0.0, ceq_)
            done_ = done_ | settled
            xv = xm_scr[...]
            nxt = jnp.min(jnp.where(xv > lo_, xv, -NEG_BIG), axis=1, keepdims=True)
            c = count_gt(nxt)
            hit = (c <= kf) & jnp.logical_not(done_)
            thr_ = jnp.where(hit, nxt, thr_)
            cgt_ = jnp.where(hit, c, cgt_)
            ceq_ = jnp.where(hit, clo_ - c, ceq_)
            done_ = done_ | hit
            lo_ = jnp.where(hit, lo_, nxt)
            clo_ = jnp.where(hit, clo_, c)
            return (rounds + 1, lo_, clo_, hi_, chi_, thr_, cgt_, ceq_, jnp.where(done_, 1.0, 0.0))

        hi0 = jnp.maximum(hi, lo0)
        state = (jnp.int32(0), lo0, clo0, hi0, chi, thr, cnt_gt, cnt_eq, jnp.where(proven, 1.0, 0.0))
        state = lax.while_loop(unsettled, one_round, state)
        thr_scr[...] = jnp.broadcast_to(state[5], (tq, LANES))
        sum1_scr[...] = jnp.broadcast_to(state[6], (tq, LANES))
        sum2_scr[...] = jnp.broadcast_to(state[7], (tq, LANES))

    thr = thr_scr[:, :1]
    cnt_gt = sum1_scr[:, :1]
    cnt_eq = sum2_scr[:, :1]
    xv = xm_scr[...]
    gt = xv > thr
    eq = xv == thr
    want_eq = kf - cnt_gt

    all_or_none = jnp.where((want_eq >= cnt_eq) & (want_eq > 0.0), POS_BIG, 0)
    lim_scr[...] = jnp.broadcast_to(all_or_none, (tq, LANES))
    partial = (want_eq > 0.0) & (want_eq < cnt_eq)
    kpos = lax.broadcasted_iota(I32, (tq, s), 1)

    @pl.when(jnp.max(jnp.where(partial, 1.0, 0.0)) > 0.5)
    def _():
        eqpos = jnp.where(eq, kpos, POS_BIG)
        eqposf = eqpos.astype(F32)

        def walk(first, last, cur, xlim):
            for it in range(first, last):
                cur = jnp.min(jnp.where(eqposf > cur, eqposf, float(POS_BIG)), axis=1, keepdims=True)
                xlim = jnp.where(partial & (want_eq == float(it)), cur.astype(I32), xlim)
            return cur, xlim

        cur, xlim = walk(0, TIE_WALK_STEPS[0], jnp.full((tq, 1), -1.0, F32), all_or_none)
        lim_scr[...] = jnp.broadcast_to(xlim, (tq, LANES))

        @pl.when(jnp.max(jnp.where(partial & (want_eq >= float(TIE_WALK_STEPS[0])), 1.0, 0.0)) > 0.5)
        def _():
            _, xlim2 = walk(TIE_WALK_STEPS[0], TIE_WALK_STEPS[1], cur, xlim)
            lim_scr[...] = jnp.broadcast_to(xlim2, (tq, LANES))

            @pl.when(jnp.max(jnp.where(partial & (want_eq >= float(TIE_WALK_STEPS[1])), 1.0, 0.0)) > 0.5)
            def _():
                nbits = max(1, int(s).bit_length())

                def pos_step(j, xl):
                    cand = xl | lax.shift_left(jnp.int32(1), nbits - 1 - j)
                    cnt = jnp.sum(jnp.where(eqpos < cand, 1.0, 0.0), axis=1, keepdims=True)
                    return jnp.where(cnt <= want_eq, cand, xl)

                lim_scr[...] = jnp.broadcast_to(lax.fori_loop(0, nbits, pos_step, jnp.zeros((tq, 1), I32)),
                                                (tq, LANES))

    sel = gt | (eq & (kpos < lim_scr[:, :1]))
    bias_scr[...] = jnp.where(sel, 0.0, NEG_BIG).astype(bias_scr.dtype)


def _attn_kernel(*refs, topk, tpos0, n_keys):
    q_ref, iq_ref, sg_ref, kd_ref, vd_ref, ikd_ref, gate_ref = refs[:7]
    o_ref, xm_scr, bias_scr, thr_scr, lim_scr, sum1_scr, sum2_scr = refs[-7:]
    tq = q_ref.shape[1]
    s = kd_ref.shape[1]
    lane = lax.broadcasted_iota(I32, (tq, LANES), 1)
    halves = (lane < HEAD_DIM, lane >= HEAD_DIM)

    tpos = tpos0 + lax.broadcasted_iota(I32, (tq, 1), 0)
    lim = jnp.minimum((tpos // CHUNK + 1) * CHUNK, n_keys)
    search = min(s, n_keys) > topk

    rb = min(tq, LANES)
    key_tile = 2 * LANES
    lane_rb = lax.broadcasted_iota(I32, (rb, LANES), 1)
    halves_rb = (lane_rb < HEAD_DIM, lane_rb >= HEAD_DIM)
    for r in range(tq // rb):
        rs = slice(r * rb, (r + 1) * rb)
        sg = sg_ref[0, rs, :]
        qms, sgs = [], []
        for hd in range(IDX_HEADS):
            qb = iq_ref[0, rs, (hd // 2) * LANES:(hd // 2 + 1) * LANES]
            qms.append(jnp.where(halves_rb[hd % 2], qb, jnp.zeros_like(qb)))
            sgs.append(sg[:, HEAD_DIM + hd:HEAD_DIM + hd + 1])
        acc1 = jnp.zeros((rb, LANES), F32)
        acc2 = jnp.zeros((rb, LANES), F32)
        lim_first = min(((tpos0 + r * rb) // CHUNK + 1) * CHUNK, n_keys)
        lim_last = min(((tpos0 + (r + 1) * rb - 1) // CHUNK + 1) * CHUNK, n_keys)
        for c0 in range(0, s, key_tile):
            w = min(key_tile, s - c0)
            if c0 >= lim_last:
                xm_scr[rs, c0:c0 + w] = jnp.full((rb, w), NEG_BIG, F32)
                continue
            ikd_t = ikd_ref[0, c0:c0 + w, :]
            acc = jnp.zeros((rb, w), F32)
            for hd in range(IDX_HEADS):
                acc = acc + sgs[hd] * jnp.maximum(_nt_dot(qms[hd], ikd_t), 0.0)
            if c0 + w <= lim_first:
                xm_scr[rs, c0:c0 + w] = acc
                sc0 = acc
            else:
                adm_t = (c0 + lax.broadcasted_iota(I32, (rb, w), 1)) < lim[rs]
                xm_scr[rs, c0:c0 + w] = jnp.where(adm_t, acc, NEG_BIG)
                sc0 = jnp.where(adm_t, acc, 0.0)
            if search:
                for l0 in range(0, w, LANES):
                    part = sc0[:, l0:l0 + LANES]
                    acc1 = acc1 + part
                    acc2 = acc2 + part * part
        if search:
            sum1_scr[rs, :] = acc1
            sum2_scr[rs, :] = acc2

    logits = []
    for c in range(A_WIDTH // LANES):
        g = (2 * c) // (A_HEADS // A_KV_HEADS)
        qb = q_ref[0, :, c * LANES:(c + 1) * LANES]
        for e in range(2):
            qm = jnp.where(halves[e], qb, jnp.zeros_like(qb))
            logits.append(_nt_dot(qm, kd_ref[0, :, g * LANES:(g + 1) * LANES]))

    if search:
        _select_topk(xm_scr, sum1_scr, sum2_scr, lim, topk, thr_scr, lim_scr, bias_scr)
    else:
        bias_scr[...] = jnp.where(lax.broadcasted_iota(I32, (tq, s), 1) < lim, 0.0, NEG_BIG).astype(BF16)

    ones = jnp.ones((s, LANES), BF16)
    for g in range(A_KV_HEADS):
        vd1 = jnp.concatenate([vd_ref[0, :, g * LANES:(g + 1) * LANES], ones], axis=1)
        for c in range(g * 2, g * 2 + 2):
            outs = []
            for e in range(2):
                lg = logits[2 * c + e].astype(BF16) + bias_scr[...]
                p = jnp.exp2(lg - jnp.max(lg, axis=1, keepdims=True))
                ol = jnp.dot(p, vd1, preferred_element_type=F32)
                outs.append(ol[:, :LANES] / ol[:, LANES:])
            gate = gate_ref[0, :, c * LANES:(c + 1) * LANES].astype(F32)
            o_ref[0, :, c * LANES:(c + 1) * LANES] = (jnp.where(halves[0], outs[0], outs[1]) * gate).astype(BF16)


def _attn_call(q, iq, sg, kd, vd, ikd, gate, prev, *, topk, qpos0, n_keys, tq, qblock, s_blk, name):
    b, t, _ = q.shape
    qmap = lambda bi: (bi, qblock, 0)
    kmap = lambda bi: (bi, 0, 0)
    return pl.pallas_call(
        functools.partial(_attn_kernel, topk=topk, tpos0=qpos0 + qblock * tq, n_keys=n_keys),
        grid=(b,),
        in_specs=[pl.BlockSpec((1, tq, A_WIDTH), qmap),
                  pl.BlockSpec((1, tq, IDX_HEADS * HEAD_DIM), qmap),
                  pl.BlockSpec((1, tq, LANES), qmap),
                  pl.BlockSpec((1, s_blk, 2 * KV_WIDTH), kmap),
                  pl.BlockSpec((1, s_blk, 2 * KV_WIDTH), kmap),
                  pl.BlockSpec((1, s_blk, LANES), kmap),
                  pl.BlockSpec((1, tq, A_WIDTH), qmap),
                  pl.BlockSpec(memory_space=pl.ANY)],
        out_specs=pl.BlockSpec((1, tq, A_WIDTH), qmap),
        out_shape=jax.ShapeDtypeStruct((b, t, A_WIDTH), BF16),
        input_output_aliases={7: 0},
        scratch_shapes=[pltpu.VMEM((tq, s_blk), F32), pltpu.VMEM((tq, s_blk), BF16),
                        pltpu.VMEM((tq, LANES), F32), pltpu.VMEM((tq, LANES), I32),
                        pltpu.VMEM((tq, LANES), F32), pltpu.VMEM((tq, LANES), F32)],
        compiler_params=pltpu.CompilerParams(dimension_semantics=("parallel",),
                                             vmem_limit_bytes=VMEM_LIMIT),
        name=name,
    )(q, iq, sg, kd, vd, ikd, gate, prev)


def _attention(q, iq, sg, kd, vd, ikd, gate, *, topk, qpos0, n_keys, tag):
    b, t, _ = q.shape
    tq = min(t, ATTN_ROWS)
    out = jnp.zeros((b, t, A_WIDTH), BF16)
    for qblock in range(t // tq):
        last_pos = qpos0 + (qblock + 1) * tq - 1
        visible = min((last_pos // CHUNK + 1) * CHUNK, n_keys)
        s_blk = min(-(-visible // LANES) * LANES, kd.shape[1])
        out = _attn_call(q, iq, sg, kd, vd, ikd, gate, out, topk=topk, qpos0=qpos0, n_keys=n_keys, tq=tq,
                         qblock=qblock, s_blk=s_blk, name=f"attn_{tag}_q{qblock}")
    return out


def _retention_log_decay():
    return jnp.log(1.0 - jnp.exp(jnp.linspace(math.log(1.0 / 32), math.log(1.0 / 512), REC_HEADS, dtype=F32)))


def _rec_kernel(bq_ref, bk_ref, bv_ref, blf_ref, cq_ref, ck_ref, cv_ref, gb_ref, gc_ref, s0b_ref, s0c_ref,
                hg_ref, rg_ref, tri_ref, dmat_ref, inner_ref, kscale_ref, rdec_ref, gm_ref, x_ref, ya_ref, w_ref,
                y_ref, sb_out, sc_out, sb_scr, sc_scr, ybc_scr, *, chunk, n_chunks):
    j = pl.program_id(1)
    width = B_WIDTH

    @pl.when(j == 0)
    def _():
        sb_scr[...] = s0b_ref[0]
        sc_scr[...] = s0c_ref[0]

    rowh = lax.broadcasted_iota(I32, (width, width), 0) // HEAD_DIM
    colh = lax.broadcasted_iota(I32, (width, width), 1) // HEAD_DIM
    block_diag = rowh == colh
    lane_h = lax.broadcasted_iota(I32, (chunk, width), 1) // HEAD_DIM
    causal = (lax.broadcasted_iota(I32, (chunk, chunk), 0)
              >= lax.broadcasted_iota(I32, (chunk, chunk), 1))
    gmat = gm_ref[...]
    tri = tri_ref[...]
    tn = (((0,), (0,)), ((), ()))

    def masked_heads(x):
        return [jnp.where(lane_h == hd, x, 0.0).astype(BF16) for hd in range(REC_HEADS)]

    heads = range(REC_HEADS)
    chunks = range(n_chunks)
    rows = [pl.ds(c * chunk, chunk) for c in chunks]

    bcs = []
    for c in chunks:
        lf = blf_ref[0, rows[c], :]
        a1 = lf.astype(BF16)
        r1 = lf - a1.astype(F32)
        a2 = r1.astype(BF16)
        a3 = (r1 - a2.astype(F32)).astype(BF16)
        bcs.append(jnp.dot(tri, jnp.concatenate([a1, a2, a3], axis=0), preferred_element_type=F32))

    att_b, att_c, upd_b, upd_c, qs_b, qs_c, dec_b, vb_b, vb_c = ([] for _ in range(9))
    for c in chunks:
        bc = bcs[c]
        mid = bc[chunk // 2 - 1:chunk // 2, :]
        last = bc[chunk - 1:chunk, :]
        q = bq_ref[0, rows[c], :]
        k = bk_ref[0, rows[c], :]
        vb = bv_ref[0, rows[c], :].astype(BF16)
        kt = (k * jnp.exp(mid - bc)).astype(BF16)
        att_b.append([_nt_dot(qh, kt) for qh in masked_heads(q * jnp.exp(bc - mid))])
        upd_b.append(lax.dot_general(vb, (k * jnp.exp(last - bc)).astype(BF16), tn, preferred_element_type=F32))
        qs_b.append((q * jnp.exp(bc)).astype(BF16))
        dec_b.append(jnp.exp(last))
        vb_b.append(vb)
        q = cq_ref[0, rows[c], :]
        k = ck_ref[0, rows[c], :]
        vb = cv_ref[0, rows[c], :].astype(BF16)
        kb = k.astype(BF16)
        att_c.append([_nt_dot(qh, kb) for qh in masked_heads(q)])
        upd_c.append(lax.dot_general(vb, (k * kscale_ref[...]).astype(BF16), tn, preferred_element_type=F32))
        qs_c.append((q * inner_ref[...]).astype(BF16))
        vb_c.append(vb)

    st = sb_scr[...]
    rt = sc_scr[...]
    o_b, o_c = [], []
    for c in chunks:
        o_b.append(_nt_dot(qs_b[c], st.astype(BF16)))
        st = st * dec_b[c] + jnp.where(block_diag, upd_b[c], 0.0)
        o_c.append(_nt_dot(qs_c[c], rt.astype(BF16)))
        rt = rt * rdec_ref[...] + jnp.where(block_diag, upd_c[c], 0.0)
    sb_scr[...] = st
    sc_scr[...] = rt

    def blockdiag_rows(v16):
        return jnp.concatenate([jnp.where(lane_h == hd, v16, jnp.zeros_like(v16)) for hd in heads], axis=0)

    prod_b = [jnp.dot(jnp.concatenate([jnp.where(causal, att_b[c][hd], 0.0) for hd in heads], axis=1).astype(BF16),
                      blockdiag_rows(vb_b[c]), preferred_element_type=F32) for c in chunks]
    prod_c = [jnp.dot(jnp.concatenate([att_c[c][hd] * dmat_ref[hd * chunk:(hd + 1) * chunk, :] for hd in heads],
                                      axis=1).astype(BF16),
                      blockdiag_rows(vb_c[c]), preferred_element_type=F32) for c in chunks]
    outs = []
    for c in chunks:
        outs.append(o_b[c] + prod_b[c])
        outs.append(o_c[c] + prod_c[c])

    means = []
    for o in outs:
        hi, lo = _split2(o * o)
        means.append(jnp.dot(jnp.concatenate([hi, lo], axis=1), gmat, preferred_element_type=F32))
    for c in chunks:
        yb = outs[2 * c] * lax.rsqrt(means[2 * c] + EPS) * hg_ref[...]
        yc = outs[2 * c + 1] * lax.rsqrt(means[2 * c + 1] + EPS) * rg_ref[...]
        ybc_scr[rows[c], :B_WIDTH] = (yb * gb_ref[0, rows[c], :].astype(F32)).astype(BF16)
        ybc_scr[rows[c], B_WIDTH:] = (yc * gc_ref[0, rows[c], :].astype(F32)).astype(BF16)

    y_ref[0] = (x_ref[0]
                + jnp.dot(ya_ref[0], w_ref[:A_WIDTH, :], preferred_element_type=F32)
                + jnp.dot(ybc_scr[...], w_ref[A_WIDTH:, :], preferred_element_type=F32))

    @pl.when(j == pl.num_programs(1) - 1)
    def _():
        sb_out[0] = sb_scr[...]
        sc_out[0] = sc_scr[...]


def _state_to_blockdiag(s):
    b = s.shape[0]
    eye = jnp.eye(REC_HEADS, dtype=s.dtype)
    return jnp.einsum('bhkv,hg->bhvgk', s, eye).reshape(b, B_WIDTH, B_WIDTH)


def _state_from_blockdiag(sbd):
    b = sbd.shape[0]
    s5 = sbd.reshape(b, REC_HEADS, HEAD_DIM, REC_HEADS, HEAD_DIM)
    return jnp.einsum('bhvhk->bhkv', s5)


def _rec_out_call(bq, bk, bv, blf, cq, ck, cv, gate, s0b, s0c, hgain, rgain, x, ya, w_out16, name):
    b, t, _ = bq.shape
    chunk = min(CHUNK, t)
    tb = min(t, REC_CHUNKS_PER_STEP * chunk)
    n_chunks = tb // chunk
    lg = _retention_log_decay()
    n = jnp.arange(chunk, dtype=F32)
    diff = n[:, None] - n[None, :]
    dmat = jnp.where(diff >= 0, jnp.exp(jnp.where(diff >= 0, diff, 0.0)[None] * lg[:, None, None]), 0.0)
    dmat = dmat.reshape(REC_HEADS * chunk, chunk)
    per_lane = lambda a: jnp.repeat(a, HEAD_DIM, axis=-1)
    inner = per_lane(jnp.exp((n[:, None] + 1.0) * lg[None, :]))
    kscale = per_lane(jnp.exp((chunk - 1.0 - n)[:, None] * lg[None, :]))
    rdec = per_lane(jnp.exp(chunk * lg)[None, :])
    tri = jnp.tile(jnp.tril(jnp.ones((chunk, chunk), F32)), (1, 3)).astype(BF16)
    seq = lambda bi, j: (bi, j, 0)
    per_b = lambda bi, j: (bi, 0, 0)
    const = lambda bi, j: (0, 0)
    stream = pl.BlockSpec((1, tb, B_WIDTH), seq)
    state = pl.BlockSpec((1, B_WIDTH, B_WIDTH), per_b)
    return pl.pallas_call(
        functools.partial(_rec_kernel, chunk=chunk, n_chunks=n_chunks),
        grid=(b, t // tb),
        in_specs=[stream] * 7 + [pl.BlockSpec((1, tb, B_WIDTH), lambda bi, j: (bi, j, A_WIDTH // B_WIDTH)),
                                 pl.BlockSpec((1, tb, C_WIDTH), lambda bi, j: (bi, j, (A_WIDTH + B_WIDTH) // C_WIDTH)),
                                 state, state,
                                 pl.BlockSpec((1, B_WIDTH), const), pl.BlockSpec((1, B_WIDTH), const),
                                 pl.BlockSpec((chunk, 3 * chunk), const),
                                 pl.BlockSpec((REC_HEADS * chunk, chunk), const),
                                 pl.BlockSpec((chunk, B_WIDTH), const), pl.BlockSpec((chunk, B_WIDTH), const),
                                 pl.BlockSpec((1, B_WIDTH), const),
                                 pl.BlockSpec((2 * B_WIDTH, B_WIDTH), const),
                                 pl.BlockSpec((1, tb, D_MODEL), seq),
                                 pl.BlockSpec((1, tb, A_WIDTH), seq),
                                 pl.BlockSpec((D_MODEL, D_MODEL), const)],
        out_specs=[pl.BlockSpec((1, tb, D_MODEL), seq), state, state],
        out_shape=[jax.ShapeDtypeStruct((b, t, D_MODEL), F32),
                   jax.ShapeDtypeStruct((b, B_WIDTH, B_WIDTH), F32),
                   jax.ShapeDtypeStruct((b, C_WIDTH, C_WIDTH), F32)],
        scratch_shapes=[pltpu.VMEM((B_WIDTH, B_WIDTH), F32), pltpu.VMEM((C_WIDTH, C_WIDTH), F32),
                        pltpu.VMEM((tb, B_WIDTH + C_WIDTH), BF16)],
        compiler_params=pltpu.CompilerParams(dimension_semantics=("parallel", "arbitrary"),
                                             vmem_limit_bytes=VMEM_LIMIT),
        name=name,
    )(bq, bk, bv, blf, cq, ck, cv, gate, gate, s0b, s0c,
      jnp.tile(hgain, REC_HEADS).reshape(1, B_WIDTH), jnp.tile(rgain, REC_HEADS).reshape(1, C_WIDTH),
      tri, dmat, inner, kscale, rdec, jnp.tile(_group_mean_matrix(B_WIDTH), (2, 1)), x, ya, w_out16)


def _with_past(past_parts, new, s_pad):
    past = jnp.concatenate([p.astype(BF16) for p in past_parts], axis=-1)
    b, n_old, width = past.shape
    tail = jnp.zeros((b, s_pad - n_old - new.shape[1], width), BF16)
    return jnp.concatenate([past, new, tail], axis=1)


def _mixer_layer(x, pos0, past, s_hgrn, s_ret, layer, w_p, w_out16, norm_g, q_gain, k_gain,
                 lb_logits, hgain, rgain, tag):
    b, t, _ = x.shape
    pos = pos0 + jnp.arange(t)
    (q, k, v, kd3, vd3, iq, sg, ik, ikd3, gate, bq, bk, bv, blf, cq, ck, cv) = _proj_call(
        x, pos, norm_g, w_p, q_gain, k_gain, lb_logits, layer, PROJ_ROWS)

    n_keys = t if past is None else past[0].shape[1] + t
    s_pad = -(-n_keys // LANES) * LANES
    if past is not None:
        pk, pv, pik = past
        kd3 = _with_past([pk[:, :, 0], pk[:, :, 0], pk[:, :, 1], pk[:, :, 1]], kd3, s_pad)
        vd3 = _with_past([pv[:, :, 0], pv[:, :, 0], pv[:, :, 1], pv[:, :, 1]], vd3, s_pad)
        ikd3 = _with_past([pik, pik], ikd3, s_pad)
    elif s_pad != n_keys:
        padw = ((0, 0), (0, s_pad - n_keys), (0, 0))
        kd3, vd3, ikd3 = jnp.pad(kd3, padw), jnp.pad(vd3, padw), jnp.pad(ikd3, padw)
    topk = min(TOPK_MAX, n_keys // 4)
    oa = _attention(q, iq, sg, kd3, vd3, ikd3, gate, topk=topk, qpos0=pos0, n_keys=n_keys, tag=tag)

    out, sb, sc = _rec_out_call(bq, bk, bv, blf, cq, ck, cv, gate, s_hgrn, s_ret, hgain, rgain, x, oa, w_out16,
                                name=f"rec_{tag}")
    return (out,
            (k.reshape(b, t, A_KV_HEADS, HEAD_DIM), v.reshape(b, t, A_KV_HEADS, HEAD_DIM), ik,
             _state_from_blockdiag(sb), _state_from_blockdiag(sc)))


def kernel(x_prompt, x_sample, cache_k, cache_v, cache_idx_k, state_hgrn, state_ret, norm_g, w_in,
           q_norm_g, k_norm_g, hgrn_lb_logits, hgrn_norm_g, ret_norm_g, w_out):
    depth = w_in.shape[0]
    bp, tp = x_prompt.shape[:2]
    bs, ts = x_sample.shape[:2]
    past = cache_k.shape[2]
    zero_state = jnp.zeros((bp, B_WIDTH, B_WIDTH), F32)

    yp, ys = x_prompt, x_sample
    outs_p, outs_s = [], []
    for l in range(depth):
        w_p = _prep_w_in(w_in[l])
        w_o = w_out[l].astype(BF16)
        args = (l, w_p, w_o, norm_g[l], q_norm_g[l], k_norm_g[l], hgrn_lb_logits, hgrn_norm_g[l], ret_norm_g[l])
        yp, st = _mixer_layer(yp, 0, None, zero_state, zero_state, *args, tag=f"p{l}")
        outs_p.append(st)
        ys, st = _mixer_layer(ys, past, (cache_k[l], cache_v[l], cache_idx_k[l]),
                              _state_to_blockdiag(state_hgrn[l]), _state_to_blockdiag(state_ret[l]), *args,
                              tag=f"s{l}")
        outs_s.append(st)

    def stack(outs, i, shape):
        return jnp.stack([o[i] for o in outs]).reshape(shape)

    return (yp, ys,
            stack(outs_p, 0, (depth, bp, tp, A_KV_HEADS, HEAD_DIM)),
            stack(outs_p, 1, (depth, bp, tp, A_KV_HEADS, HEAD_DIM)),
            stack(outs_p, 2, (depth, bp, tp, HEAD_DIM)),
            stack(outs_p, 3, (depth, bp, REC_HEADS, HEAD_DIM, HEAD_DIM)),
            stack(outs_p, 4, (depth, bp, REC_HEADS, HEAD_DIM, HEAD_DIM)),
            stack(outs_s, 0, (depth, bs, ts, A_KV_HEADS, HEAD_DIM)),
            stack(outs_s, 1, (depth, bs, ts, A_KV_HEADS, HEAD_DIM)),
            stack(outs_s, 2, (depth, bs, ts, HEAD_DIM)),
            stack(outs_s, 3, (depth, bs, REC_HEADS, HEAD_DIM, HEAD_DIM)),
            stack(outs_s, 4, (depth, bs, REC_HEADS, HEAD_DIM, HEAD_DIM)))
```

```python
import functools
import math

import numpy as np
import jax
import jax.numpy as jnp
from jax import lax
from jax.experimental import pallas as pl
from jax.experimental.pallas import tpu as pltpu

F32 = jnp.float32
BF16 = jnp.bfloat16
I32 = jnp.int32

D_MODEL = 1024
HEAD_DIM = 64
CHUNK = 64
A_WIDTH = 512
A_HEADS = 8
A_KV_HEADS = 2
KV_WIDTH = A_KV_HEADS * HEAD_DIM
IDX_HEADS = 8
IDX_W_SCALE = (IDX_HEADS * HEAD_DIM) ** -0.5
TOPK_MAX = 256
B_WIDTH = 256
C_WIDTH = 256
REC_HEADS = 4
ROPE_THETA = 10000.0
EPS = 1e-6
LANES = 128
VMEM_LIMIT = 48 * 1024 * 1024
PROJ_ROWS = 512
ATTN_ROWS = 256
REC_CHUNKS_PER_STEP = 16

_SIZES = (A_WIDTH, KV_WIDTH, KV_WIDTH, A_WIDTH, IDX_HEADS * HEAD_DIM, HEAD_DIM, IDX_HEADS,
          B_WIDTH, B_WIDTH, B_WIDTH, B_WIDTH, C_WIDTH, C_WIDTH, C_WIDTH, C_WIDTH)
_SPLIT_IDX = tuple(int(s) for s in np.cumsum(_SIZES)[:-1])

_AQ, _AK, _AV, _AG, _IQ, _IKW = 0, 512, 640, 768, 1280, 1792
_BQ, _BF, _BI, _BG, _CQ, _CK, _CV, _CG = 1920, 2176, 2432, 2688, 2944, 3200, 3456, 3712
_N_PROJ = 3968

NEG_BIG = -1e30
POS_BIG = 1 << 20
LOG2_E = math.log2(math.e)


def _prep_w_in(w):
    (aq, ak, av, ag, iq, ik, iw, bq, bf, bi, bg, cq, ck, cv, cg) = jnp.split(w, _SPLIT_IDX, axis=1)
    pad = jnp.zeros((w.shape[0], LANES - HEAD_DIM - IDX_HEADS), w.dtype)
    return jnp.concatenate([aq, ak, av, ag, iq, ik, iw, pad, bq, bf, bi, bg, cq, ck, cv, cg],
                           axis=1).astype(BF16)


def _rope_tables(pos):
    half = HEAD_DIM // 2
    freqs = ROPE_THETA ** (-jnp.arange(half, dtype=F32) / half)
    ang = pos.astype(F32)[:, None] * freqs[None, :]
    cos, sin = jnp.cos(ang), jnp.sin(ang)
    cos64 = jnp.concatenate([cos, cos], axis=1)
    sin64 = jnp.concatenate([-sin, sin], axis=1)
    return jnp.concatenate([cos64, cos64], axis=1), jnp.concatenate([sin64, sin64], axis=1)


def _group_mean_matrix(width):
    idx = np.arange(width) // HEAD_DIM
    return jnp.asarray((idx[:, None] == idx[None, :]).astype(np.float32) / HEAD_DIM, BF16)


def _split2(x):
    hi = x.astype(BF16)
    lo = (x - hi.astype(F32)).astype(BF16)
    return hi, lo


def _nt_dot(a, b):
    return lax.dot_general(a, b, (((1,), (1,)), ((), ())), preferred_element_type=F32)


def _proj_kernel(x_ref, g_ref, w_ref, cos_ref, sin_ref, qg_ref, kg_ref, lb_ref, gm_ref, ex_ref,
                 q_out, k_out, v_out, kd_out, vd_out, iq_out, sg_out, ik_out, ikd_out, gate_out,
                 bq_out, bk_out, bv_out, blf_out, cq_out, ck_out, cv_out, *, layer):
    x = x_ref[...]
    tm = x.shape[0]
    ms = jnp.mean(x * x, axis=-1, keepdims=True)
    h = (x * lax.rsqrt(ms + EPS) * g_ref[...]).astype(BF16)
    cos = cos_ref[...]
    sin = sin_ref[...]
    lane = lax.broadcasted_iota(I32, (tm, LANES), 1)
    first_half = (lane % HEAD_DIM) < (HEAD_DIM // 2)
    low_head = lane < HEAD_DIM
    gmat = gm_ref[...]

    def proj(c0, width):
        return jnp.dot(h, w_ref[:, c0:c0 + width], preferred_element_type=F32)

    def rope(xb):
        swapped = jnp.where(first_half, pltpu.roll(xb, LANES - HEAD_DIM // 2, 1),
                            pltpu.roll(xb, HEAD_DIM // 2, 1))
        return xb * cos + swapped * sin

    def head_norm(xb, gain):
        hi, lo = _split2(xb * xb)
        msq = jnp.dot(jnp.concatenate([hi, lo], axis=1), gmat, preferred_element_type=F32)
        return xb * lax.rsqrt(msq + EPS) * gain

    def dup_heads(xb):
        other = pltpu.roll(xb, HEAD_DIM, 1)
        return jnp.where(low_head, xb, other), jnp.where(low_head, other, xb)

    ag = proj(_AG, A_WIDTH)
    bg = proj(_BG, B_WIDTH)
    cg = proj(_CG, C_WIDTH)
    bq = proj(_BQ, B_WIDTH)
    bf = proj(_BF, B_WIDTH)
    bi = proj(_BI, B_WIDTH)
    cq = proj(_CQ, C_WIDTH)
    ck = proj(_CK, C_WIDTH)
    cv = proj(_CV, C_WIDTH)
    akv = proj(_AK, 2 * KV_WIDTH)
    ikw = proj(_IKW, LANES)
    aq = proj(_AQ, A_WIDTH)
    iq = proj(_IQ, IDX_HEADS * HEAD_DIM)

    def silu(z):
        return z / (1.0 + jnp.exp(-z))

    gate_out[:, :A_WIDTH] = silu(ag).astype(BF16)
    gate_out[:, A_WIDTH:A_WIDTH + B_WIDTH] = silu(bg).astype(BF16)
    gate_out[:, A_WIDTH + B_WIDTH:] = silu(cg).astype(BF16)

    logits = lb_ref[...]
    e = jnp.exp(logits - jnp.max(logits, axis=0, keepdims=True))
    sm = e / jnp.sum(e, axis=0, keepdims=True)
    lb = jnp.zeros((1, B_WIDTH), F32)
    for j in range(1, layer + 1):
        lb = lb + sm[j:j + 1, :]
    f = lb + (1.0 - lb) / (1.0 + jnp.exp(-bf))
    bq_out[...] = bq.astype(BF16)
    bk_out[...] = ((1.0 - lb) / (1.0 + jnp.exp(bf))).astype(BF16)
    bv_out[...] = bi.astype(BF16)
    blf_out[...] = jnp.log(f)

    for c in range(C_WIDTH // LANES):
        sl = slice(c * LANES, (c + 1) * LANES)
        cq_out[:, sl] = rope(cq[:, sl]).astype(BF16)
        ck_out[:, sl] = (rope(ck[:, sl]) * (HEAD_DIM ** -0.5)).astype(BF16)
    cv_out[...] = cv.astype(BF16)

    v = akv[:, KV_WIDTH:]
    v_out[...] = v
    v0, v1 = dup_heads(v)
    vd_out[:, :LANES] = v0.astype(BF16)
    vd_out[:, LANES:] = v1.astype(BF16)
    ikr = rope(ikw)
    ik_out[...] = ikr[:, :HEAD_DIM]
    ikd_out[...] = dup_heads(ikr)[0].astype(BF16)
    sg_out[...] = jnp.where(ikw >= 0, 1.0, -1.0).astype(F32)

    for c in range(A_WIDTH // LANES):
        blk = rope(head_norm(aq[:, c * LANES:(c + 1) * LANES], qg_ref[...]))
        q_out[:, c * LANES:(c + 1) * LANES] = (blk * (HEAD_DIM ** -0.5 * LOG2_E)).astype(BF16)
    k = rope(head_norm(akv[:, :KV_WIDTH], kg_ref[...]))
    k_out[...] = k
    k0, k1 = dup_heads(k)
    kd_out[:, :LANES] = k0.astype(BF16)
    kd_out[:, LANES:] = k1.astype(BF16)

    whi, wlo = _split2(jnp.abs(ikw) * IDX_W_SCALE)
    wexp = jnp.dot(jnp.concatenate([whi, wlo], axis=1), ex_ref[...], preferred_element_type=F32)
    for c in range(IDX_HEADS * HEAD_DIM // LANES):
        sl = slice(c * LANES, (c + 1) * LANES)
        iq_out[:, sl] = (rope(iq[:, sl]) * wexp[:, sl]).astype(BF16)


def _row_blocking(b, t, tm):
    if t % tm == 0:
        grid = (b, t // tm)
        rows = lambda w: pl.BlockSpec((None, tm, w), lambda bi, j: (bi, j, 0))
        per_pos = lambda w: pl.BlockSpec((tm, w), lambda bi, j: (j, 0))
        const = lambda shape: pl.BlockSpec(shape, lambda bi, j: (0, 0))
        return grid, rows, per_pos, const, False, tm
    n = b * t
    tm = min(tm, n)
    assert n % tm == 0 and tm % t == 0
    grid = (n // tm,)
    rows = lambda w: pl.BlockSpec((tm, w), lambda r: (r, 0))
    per_pos = lambda w: pl.BlockSpec((tm, w), lambda r: (0, 0))
    const = lambda shape: pl.BlockSpec(shape, lambda r: (0, 0))
    return grid, rows, per_pos, const, True, tm


def _proj_call(x, pos, norm_g, w_p, q_gain, k_gain, lb_logits, layer, tm):
    b, t, _ = x.shape
    grid, rows, per_pos, const, flat, tm = _row_blocking(b, t, tm)
    cos, sin = _rope_tables(pos)
    if flat:
        x = x.reshape(b * t, D_MODEL)
        cos = jnp.tile(cos, (tm // t, 1))
        sin = jnp.tile(sin, (tm // t, 1))
    expand = np.zeros((LANES, IDX_HEADS * HEAD_DIM), np.float32)
    for hd in range(IDX_HEADS):
        expand[HEAD_DIM + hd, hd * HEAD_DIM:(hd + 1) * HEAD_DIM] = 1.0
    widths = [(A_WIDTH, BF16), (KV_WIDTH, F32), (KV_WIDTH, F32), (2 * KV_WIDTH, BF16), (2 * KV_WIDTH, BF16),
              (IDX_HEADS * HEAD_DIM, BF16), (LANES, F32), (HEAD_DIM, F32), (LANES, BF16), (D_MODEL, BF16),
              (B_WIDTH, BF16), (B_WIDTH, BF16), (B_WIDTH, BF16), (B_WIDTH, F32),
              (C_WIDTH, BF16), (C_WIDTH, BF16), (C_WIDTH, BF16)]
    lead = (b * t,) if flat else (b, t)
    outs = pl.pallas_call(
        functools.partial(_proj_kernel, layer=layer),
        grid=grid,
        in_specs=[rows(D_MODEL),
                  const((1, D_MODEL)),
                  const((D_MODEL, _N_PROJ)),
                  per_pos(LANES),
                  per_pos(LANES),
                  const((1, LANES)),
                  const((1, LANES)),
                  const(lb_logits.shape),
                  const((2 * LANES, LANES)),
                  const((2 * LANES, IDX_HEADS * HEAD_DIM))],
        out_specs=[rows(w) for w, _ in widths],
        out_shape=[jax.ShapeDtypeStruct(lead + (w,), dt) for w, dt in widths],
        compiler_params=pltpu.CompilerParams(dimension_semantics=("parallel",) * len(grid),
                                             vmem_limit_bytes=VMEM_LIMIT),
        name=f"proj_l{layer}",
    )(x, norm_g.reshape(1, D_MODEL), w_p, cos, sin,
      jnp.tile(q_gain, 2).reshape(1, LANES), jnp.tile(k_gain, 2).reshape(1, LANES),
      lb_logits, jnp.tile(_group_mean_matrix(LANES), (2, 1)), jnp.tile(jnp.asarray(expand, BF16), (2, 1)))
    return [o.reshape(b, t, o.shape[-1]) for o in outs] if flat else outs


SEARCH_INTERP_STEPS = 8
SEARCH_WALK_STEPS = 3
SEARCH_RANK_BIAS = 1.5
SEARCH_EDGE = 0.05
SEARCH_SPAN_SD = 8.0
TIE_WALK_STEPS = (4, 12)


def _normal_upper_quantile(r):
    rr = jnp.minimum(r, 1.0 - r)
    t = jnp.sqrt(-2.0 * jnp.log(rr))
    z = t - ((0.010328 * t + 0.802853) * t + 2.515517) / (((0.001308 * t + 0.189269) * t + 1.432788) * t + 1.0)
    return jnp.where(r <= 0.5, z, -z)


def _select_topk(xm_scr, sum1_scr, sum2_scr, lim, topk, thr_scr, lim_scr, bias_scr):
    tq, s = xm_scr.shape
    kf = float(topk)
    nf = lim.astype(F32)
    trivial = lim <= topk

    def count_gt(p):
        return jnp.sum(jnp.where(xm_scr[...] > p, 1.0, 0.0), axis=1, keepdims=True)

    mu = jnp.sum(sum1_scr[...], axis=1, keepdims=True) / nf
    var = jnp.sum(sum2_scr[...], axis=1, keepdims=True) / nf - mu * mu
    sd = jnp.sqrt(jnp.maximum(var, 1e-30))
    target = kf + SEARCH_RANK_BIAS
    lo, hi = mu - SEARCH_SPAN_SD * sd, mu + SEARCH_SPAN_SD * sd
    clo, chi = nf, jnp.zeros_like(nf)
    p = mu + _normal_upper_quantile(jnp.clip(target / nf, 1e-6, 1.0 - 1e-6)) * sd
    lo_counted = jnp.zeros((tq, 1), jnp.bool_)
    for it in range(SEARCH_INTERP_STEPS):
        c = count_gt(p)
        above = c >= kf
        lo = jnp.where(above, p, lo)
        clo = jnp.where(above, c, clo)
        lo_counted = lo_counted | above
        hi = jnp.where(above, hi, p)
        chi = jnp.where(above, chi, c)
        frac = jnp.clip((clo - target) / (clo - chi), SEARCH_EDGE, 1.0 - SEARCH_EDGE)
        p = lo + (hi - lo) * frac

    cur = lo
    thr = lo
    cnt_prev = clo
    cnt_gt = clo
    cnt_eq = jnp.zeros_like(clo)
    done = lo_counted & (clo == kf)
    for it in range(SEARCH_WALK_STEPS):
        xv = xm_scr[...]
        cur = jnp.min(jnp.where(xv > cur, xv, -NEG_BIG), axis=1, keepdims=True)
        c = count_gt(cur)
        hit = (c <= kf) & lo_counted & jnp.logical_not(done)
        thr = jnp.where(hit, cur, thr)
        cnt_gt = jnp.where(hit, c, cnt_gt)
        cnt_eq = jnp.where(hit, cnt_prev - c, cnt_eq)
        done = done | hit
        cnt_prev = c
    thr = jnp.where(trivial, 0.1 * NEG_BIG, thr)
    cnt_gt = jnp.where(trivial, nf, cnt_gt)
    cnt_eq = jnp.where(trivial, 0.0, cnt_eq)
    thr_scr[...] = jnp.broadcast_to(thr, (tq, LANES))
    sum1_scr[...] = jnp.broadcast_to(cnt_gt, (tq, LANES))
    sum2_scr[...] = jnp.broadcast_to(cnt_eq, (tq, LANES))
    proven = trivial | done

    @pl.when(jnp.min(jnp.where(proven, 1.0, 0.0)) < 0.5)
    def _():
        lo0 = jnp.where(lo_counted, lo, 0.5 * NEG_BIG)
        clo0 = jnp.where(lo_counted, clo, nf)

        def unsettled(state):
            rounds, done_f = state[0], state[-1]
            return (rounds < s) & (jnp.min(done_f) < 0.5)

        def one_round(state):
            rounds, lo_, clo_, hi_, chi_, thr_, cgt_, ceq_, done_f = state
            done_ = done_f > 0.5
            frac = jnp.clip((clo_ - target) / (clo_ - chi_), SEARCH_EDGE, 1.0 - SEARCH_EDGE)
            piv = lo_ + (hi_ - lo_) * frac
            c = count_gt(piv)
            above = c >= kf
            lo_ = jnp.where(above, piv, lo_)
            clo_ = jnp.where(above, c, clo_)
            hi_ = jnp.where(above, hi_, piv)
            chi_ = jnp.where(above, chi_, c)
            settled = (clo_ == kf) & jnp.logical_not(done_)
            thr_ = jnp.where(settled, lo_, thr_)
            cgt_ = jnp.where(settled, clo_, cgt_)
            ceq_ = jnp.where(settled, 0.0, ceq_)
            done_ = done_ | settled
            xv = xm_scr[...]
            nxt = jnp.min(jnp.where(xv > lo_, xv, -NEG_BIG), axis=1, keepdims=True)
            c = count_gt(nxt)
            hit = (c <= kf) & jnp.logical_not(done_)
            thr_ = jnp.where(hit, nxt, thr_)
            cgt_ = jnp.where(hit, c, cgt_)
            ceq_ = jnp.where(hit, clo_ - c, ceq_)
            done_ = done_ | hit
            lo_ = jnp.where(hit, lo_, nxt)
            clo_ = jnp.where(hit, clo_, c)
            return (rounds + 1, lo_, clo_, hi_, chi_, thr_, cgt_, ceq_, jnp.where(done_, 1.0, 0.0))

        hi0 = jnp.maximum(hi, lo0)
        state = (jnp.int32(0), lo0, clo0, hi0, chi, thr, cnt_gt, cnt_eq, jnp.where(proven, 1.0, 0.0))
        state = lax.while_loop(unsettled, one_round, state)
        thr_scr[...] = jnp.broadcast_to(state[5], (tq, LANES))
        sum1_scr[...] = jnp.broadcast_to(state[6], (tq, LANES))
        sum2_scr[...] = jnp.broadcast_to(state[7], (tq, LANES))

    thr = thr_scr[:, :1]
    cnt_gt = sum1_scr[:, :1]
    cnt_eq = sum2_scr[:, :1]
    xv = xm_scr[...]
    gt = xv > thr
    eq = xv == thr
    want_eq = kf - cnt_gt

    all_or_none = jnp.where((want_eq >= cnt_eq) & (want_eq > 0.0), POS_BIG, 0)
    lim_scr[...] = jnp.broadcast_to(all_or_none, (tq, LANES))
    partial = (want_eq > 0.0) & (want_eq < cnt_eq)
    kpos = lax.broadcasted_iota(I32, (tq, s), 1)

    @pl.when(jnp.max(jnp.where(partial, 1.0, 0.0)) > 0.5)
    def _():
        eqpos = jnp.where(eq, kpos, POS_BIG)
        eqposf = eqpos.astype(F32)

        def walk(first, last, cur, xlim):
            for it in range(first, last):
                cur = jnp.min(jnp.where(eqposf > cur, eqposf, float(POS_BIG)), axis=1, keepdims=True)
                xlim = jnp.where(partial & (want_eq == float(it)), cur.astype(I32), xlim)
            return cur, xlim

        cur, xlim = walk(0, TIE_WALK_STEPS[0], jnp.full((tq, 1), -1.0, F32), all_or_none)
        lim_scr[...] = jnp.broadcast_to(xlim, (tq, LANES))

        @pl.when(jnp.max(jnp.where(partial & (want_eq >= float(TIE_WALK_STEPS[0])), 1.0, 0.0)) > 0.5)
        def _():
            _, xlim2 = walk(TIE_WALK_STEPS[0], TIE_WALK_STEPS[1], cur, xlim)
            lim_scr[...] = jnp.broadcast_to(xlim2, (tq, LANES))

            @pl.when(jnp.max(jnp.where(partial & (want_eq >= float(TIE_WALK_STEPS[1])), 1.0, 0.0)) > 0.5)
            def _():
                nbits = max(1, int(s).bit_length())

                def pos_step(j, xl):
                    cand = xl | lax.shift_left(jnp.int32(1), nbits - 1 - j)
                    cnt = jnp.sum(jnp.where(eqpos < cand, 1.0, 0.0), axis=1, keepdims=True)
                    return jnp.where(cnt <= want_eq, cand, xl)

                lim_scr[...] = jnp.broadcast_to(lax.fori_loop(0, nbits, pos_step, jnp.zeros((tq, 1), I32)),
                                                (tq, LANES))

    sel = gt | (eq & (kpos < lim_scr[:, :1]))
    bias_scr[...] = jnp.where(sel, 0.0, NEG_BIG).astype(bias_scr.dtype)


def _attn_kernel(*refs, topk, tpos0, n_keys):
    q_ref, iq_ref, sg_ref, kd_ref, vd_ref, ikd_ref, gate_ref = refs[:7]
    o_ref, xm_scr, bias_scr, thr_scr, lim_scr, sum1_scr, sum2_scr = refs[-7:]
    tq = q_ref.shape[1]
    s = kd_ref.shape[1]
    lane = lax.broadcasted_iota(I32, (tq, LANES), 1)
    halves = (lane < HEAD_DIM, lane >= HEAD_DIM)

    tpos = tpos0 + lax.broadcasted_iota(I32, (tq, 1), 0)
    lim = jnp.minimum((tpos // CHUNK + 1) * CHUNK, n_keys)
    search = min(s, n_keys) > topk

    rb = min(tq, LANES)
    key_tile = 2 * LANES
    lane_rb = lax.broadcasted_iota(I32, (rb, LANES), 1)
    halves_rb = (lane_rb < HEAD_DIM, lane_rb >= HEAD_DIM)
    for r in range(tq // rb):
        rs = slice(r * rb, (r + 1) * rb)
        sg = sg_ref[0, rs, :]
        qms, sgs = [], []
        for hd in range(IDX_HEADS):
            qb = iq_ref[0, rs, (hd // 2) * LANES:(hd // 2 + 1) * LANES]
            qms.append(jnp.where(halves_rb[hd % 2], qb, jnp.zeros_like(qb)))
            sgs.append(sg[:, HEAD_DIM + hd:HEAD_DIM + hd + 1])
        acc1 = jnp.zeros((rb, LANES), F32)
        acc2 = jnp.zeros((rb, LANES), F32)
        lim_first = min(((tpos0 + r * rb) // CHUNK + 1) * CHUNK, n_keys)
        lim_last = min(((tpos0 + (r + 1) * rb - 1) // CHUNK + 1) * CHUNK, n_keys)
        for c0 in range(0, s, key_tile):
            w = min(key_tile, s - c0)
            if c0 >= lim_last:
                xm_scr[rs, c0:c0 + w] = jnp.full((rb, w), NEG_BIG, F32)
                continue
            ikd_t = ikd_ref[0, c0:c0 + w, :]
            acc = jnp.zeros((rb, w), F32)
            for hd in range(IDX_HEADS):
                acc = acc + sgs[hd] * jnp.maximum(_nt_dot(qms[hd], ikd_t), 0.0)
            if c0 + w <= lim_first:
                xm_scr[rs, c0:c0 + w] = acc
                sc0 = acc
            else:
                adm_t = (c0 + lax.broadcasted_iota(I32, (rb, w), 1)) < lim[rs]
                xm_scr[rs, c0:c0 + w] = jnp.where(adm_t, acc, NEG_BIG)
                sc0 = jnp.where(adm_t, acc, 0.0)
            if search:
                for l0 in range(0, w, LANES):
                    part = sc0[:, l0:l0 + LANES]
                    acc1 = acc1 + part
                    acc2 = acc2 + part * part
        if search:
            sum1_scr[rs, :] = acc1
            sum2_scr[rs, :] = acc2

    logits = []
    for c in range(A_WIDTH // LANES):
        g = (2 * c) // (A_HEADS // A_KV_HEADS)
        qb = q_ref[0, :, c * LANES:(c + 1) * LANES]
        for e in range(2):
            qm = jnp.where(halves[e], qb, jnp.zeros_like(qb))
            logits.append(_nt_dot(qm, kd_ref[0, :, g * LANES:(g + 1) * LANES]))

    if search:
        _select_topk(xm_scr, sum1_scr, sum2_scr, lim, topk, thr_scr, lim_scr, bias_scr)
    else:
        bias_scr[...] = jnp.where(lax.broadcasted_iota(I32, (tq, s), 1) < lim, 0.0, NEG_BIG).astype(BF16)

    ones = jnp.ones((s, LANES), BF16)
    for g in range(A_KV_HEADS):
        vd1 = jnp.concatenate([vd_ref[0, :, g * LANES:(g + 1) * LANES], ones], axis=1)
        for c in range(g * 2, g * 2 + 2):
            outs = []
            for e in range(2):
                lg = logits[2 * c + e].astype(BF16) + bias_scr[...]
                p = jnp.exp2(lg - jnp.max(lg, axis=1, keepdims=True))
                ol = jnp.dot(p, vd1, preferred_element_type=F32)
                outs.append(ol[:, :LANES] / ol[:, LANES:])
            gate = gate_ref[0, :, c * LANES:(c + 1) * LANES].astype(F32)
            o_ref[0, :, c * LANES:(c + 1) * LANES] = (jnp.where(halves[0], outs[0], outs[1]) * gate).astype(BF16)


def _attn_call(q, iq, sg, kd, vd, ikd, gate, prev, *, topk, qpos0, n_keys, tq, qblock, s_blk, name):
    b, t, _ = q.shape
    qmap = lambda bi: (bi, qblock, 0)
    kmap = lambda bi: (bi, 0, 0)
    return pl.pallas_call(
        functools.partial(_attn_kernel, topk=topk, tpos0=qpos0 + qblock * tq, n_keys=n_keys),
        grid=(b,),
        in_specs=[pl.BlockSpec((1, tq, A_WIDTH), qmap),
                  pl.BlockSpec((1, tq, IDX_HEADS * HEAD_DIM), qmap),
                  pl.BlockSpec((1, tq, LANES), qmap),
                  pl.BlockSpec((1, s_blk, 2 * KV_WIDTH), kmap),
                  pl.BlockSpec((1, s_blk, 2 * KV_WIDTH), kmap),
                  pl.BlockSpec((1, s_blk, LANES), kmap),
                  pl.BlockSpec((1, tq, A_WIDTH), qmap),
                  pl.BlockSpec(memory_space=pl.ANY)],
        out_specs=pl.BlockSpec((1, tq, A_WIDTH), qmap),
        out_shape=jax.ShapeDtypeStruct((b, t, A_WIDTH), BF16),
        input_output_aliases={7: 0},
        scratch_shapes=[pltpu.VMEM((tq, s_blk), F32), pltpu.VMEM((tq, s_blk), BF16),
                        pltpu.VMEM((tq, LANES), F32), pltpu.VMEM((tq, LANES), I32),
                        pltpu.VMEM((tq, LANES), F32), pltpu.VMEM((tq, LANES), F32)],
        compiler_params=pltpu.CompilerParams(dimension_semantics=("parallel",),
                                             vmem_limit_bytes=VMEM_LIMIT),
        name=name,
    )(q, iq, sg, kd, vd, ikd, gate, prev)


def _attention(q, iq, sg, kd, vd, ikd, gate, *, topk, qpos0, n_keys, tag):
    b, t, _ = q.shape
    tq = min(t, ATTN_ROWS)
    out = jnp.zeros((b, t, A_WIDTH), BF16)
    for qblock in range(t // tq):
        last_pos = qpos0 + (qblock + 1) * tq - 1
        visible = min((last_pos // CHUNK + 1) * CHUNK, n_keys)
        s_blk = min(-(-visible // LANES) * LANES, kd.shape[1])
        out = _attn_call(q, iq, sg, kd, vd, ikd, gate, out, topk=topk, qpos0=qpos0, n_keys=n_keys, tq=tq,
                         qblock=qblock, s_blk=s_blk, name=f"attn_{tag}_q{qblock}")
    return out


def _retention_log_decay():
    return jnp.log(1.0 - jnp.exp(jnp.linspace(math.log(1.0 / 32), math.log(1.0 / 512), REC_HEADS, dtype=F32)))


def _rec_kernel(bq_ref, bk_ref, bv_ref, blf_ref, cq_ref, ck_ref, cv_ref, gb_ref, gc_ref, s0b_ref, s0c_ref,
                hg_ref, rg_ref, tri_ref, dmat_ref, inner_ref, kscale_ref, rdec_ref, gm_ref, x_ref, ya_ref, w_ref,
                y_ref, sb_out, sc_out, sb_scr, sc_scr, ybc_scr, *, chunk, n_chunks):
    j = pl.program_id(1)
    width = B_WIDTH

    @pl.when(j == 0)
    def _():
        sb_scr[...] = s0b_ref[0]
        sc_scr[...] = s0c_ref[0]

    rowh = lax.broadcasted_iota(I32, (width, width), 0) // HEAD_DIM
    colh = lax.broadcasted_iota(I32, (width, width), 1) // HEAD_DIM
    block_diag = rowh == colh
    lane_h = lax.broadcasted_iota(I32, (chunk, width), 1) // HEAD_DIM
    causal = (lax.broadcasted_iota(I32, (chunk, chunk), 0)
              >= lax.broadcasted_iota(I32, (chunk, chunk), 1))
    gmat = gm_ref[...]
    tri = tri_ref[...]
    tn = (((0,), (0,)), ((), ()))

    def masked_heads(x):
        return [jnp.where(lane_h == hd, x, 0.0).astype(BF16) for hd in range(REC_HEADS)]

    heads = range(REC_HEADS)
    chunks = range(n_chunks)
    rows = [pl.ds(c * chunk, chunk) for c in chunks]

    bcs = []
    for c in chunks:
        lf = blf_ref[0, rows[c], :]
        a1 = lf.astype(BF16)
        r1 = lf - a1.astype(F32)
        a2 = r1.astype(BF16)
        a3 = (r1 - a2.astype(F32)).astype(BF16)
        bcs.append(jnp.dot(tri, jnp.concatenate([a1, a2, a3], axis=0), preferred_element_type=F32))

    att_b, att_c, upd_b, upd_c, qs_b, qs_c, dec_b, vb_b, vb_c = ([] for _ in range(9))
    for c in chunks:
        bc = bcs[c]
        mid = bc[chunk // 2 - 1:chunk // 2, :]
        last = bc[chunk - 1:chunk, :]
        q = bq_ref[0, rows[c], :]
        k = bk_ref[0, rows[c], :]
        vb = bv_ref[0, rows[c], :].astype(BF16)
        kt = (k * jnp.exp(mid - bc)).astype(BF16)
        att_b.append([_nt_dot(qh, kt) for qh in masked_heads(q * jnp.exp(bc - mid))])
        upd_b.append(lax.dot_general(vb, (k * jnp.exp(last - bc)).astype(BF16), tn, preferred_element_type=F32))
        qs_b.append((q * jnp.exp(bc)).astype(BF16))
        dec_b.append(jnp.exp(last))
        vb_b.append(vb)
        q = cq_ref[0, rows[c], :]
        k = ck_ref[0, rows[c], :]
        vb = cv_ref[0, rows[c], :].astype(BF16)
        kb = k.astype(BF16)
        att_c.append([_nt_dot(qh, kb) for qh in masked_heads(q)])
        upd_c.append(lax.dot_general(vb, (k * kscale_ref[...]).astype(BF16), tn, preferred_element_type=F32))
        qs_c.append((q * inner_ref[...]).astype(BF16))
        vb_c.append(vb)

    st = sb_scr[...]
    rt = sc_scr[...]
    o_b, o_c = [], []
    for c in chunks:
        o_b.append(_nt_dot(qs_b[c], st.astype(BF16)))
        st = st * dec_b[c] + jnp.where(block_diag, upd_b[c], 0.0)
        o_c.append(_nt_dot(qs_c[c], rt.astype(BF16)))
        rt = rt * rdec_ref[...] + jnp.where(block_diag, upd_c[c], 0.0)
    sb_scr[...] = st
    sc_scr[...] = rt

    def blockdiag_rows(v16):
        return jnp.concatenate([jnp.where(lane_h == hd, v16, jnp.zeros_like(v16)) for hd in heads], axis=0)

    prod_b = [jnp.dot(jnp.concatenate([jnp.where(causal, att_b[c][hd], 0.0) for hd in heads], axis=1).astype(BF16),
                      blockdiag_rows(vb_b[c]), preferred_element_type=F32) for c in chunks]
    prod_c = [jnp.dot(jnp.concatenate([att_c[c][hd] * dmat_ref[hd * chunk:(hd + 1) * chunk, :] for hd in heads],
                                      axis=1).astype(BF16),
                      blockdiag_rows(vb_c[c]), preferred_element_type=F32) for c in chunks]
    outs = []
    for c in chunks:
        outs.append(o_b[c] + prod_b[c])
        outs.append(o_c[c] + prod_c[c])

    means = []
    for o in outs:
        hi, lo = _split2(o * o)
        means.append(jnp.dot(jnp.concatenate([hi, lo], axis=1), gmat, preferred_element_type=F32))
    for c in chunks:
        yb = outs[2 * c] * lax.rsqrt(means[2 * c] + EPS) * hg_ref[...]
        yc = outs[2 * c + 1] * lax.rsqrt(means[2 * c + 1] + EPS) * rg_ref[...]
        ybc_scr[rows[c], :B_WIDTH] = (yb * gb_ref[0, rows[c], :].astype(F32)).astype(BF16)
        ybc_scr[rows[c], B_WIDTH:] = (yc * gc_ref[0, rows[c], :].astype(F32)).astype(BF16)

    y_ref[0] = (x_ref[0]
                + jnp.dot(ya_ref[0], w_ref[:A_WIDTH, :], preferred_element_type=F32)
                + jnp.dot(ybc_scr[...], w_ref[A_WIDTH:, :], preferred_element_type=F32))

    @pl.when(j == pl.num_programs(1) - 1)
    def _():
        sb_out[0] = sb_scr[...]
        sc_out[0] = sc_scr[...]


def _state_to_blockdiag(s):
    b = s.shape[0]
    eye = jnp.eye(REC_HEADS, dtype=s.dtype)
    return jnp.einsum('bhkv,hg->bhvgk', s, eye).reshape(b, B_WIDTH, B_WIDTH)


def _state_from_blockdiag(sbd):
    b = sbd.shape[0]
    s5 = sbd.reshape(b, REC_HEADS, HEAD_DIM, REC_HEADS, HEAD_DIM)
    return jnp.einsum('bhvhk->bhkv', s5)


def _rec_out_call(bq, bk, bv, blf, cq, ck, cv, gate, s0b, s0c, hgain, rgain, x, ya, w_out16, name):
    b, t, _ = bq.shape
    chunk = min(CHUNK, t)
    tb = min(t, REC_CHUNKS_PER_STEP * chunk)
    n_chunks = tb // chunk
    lg = _retention_log_decay()
    n = jnp.arange(chunk, dtype=F32)
    diff = n[:, None] - n[None, :]
    dmat = jnp.where(diff >= 0, jnp.exp(jnp.where(diff >= 0, diff, 0.0)[None] * lg[:, None, None]), 0.0)
    dmat = dmat.reshape(REC_HEADS * chunk, chunk)
    per_lane = lambda a: jnp.repeat(a, HEAD_DIM, axis=-1)
    inner = per_lane(jnp.exp((n[:, None] + 1.0) * lg[None, :]))
    kscale = per_lane(jnp.exp((chunk - 1.0 - n)[:, None] * lg[None, :]))
    rdec = per_lane(jnp.exp(chunk * lg)[None, :])
    tri = jnp.tile(jnp.tril(jnp.ones((chunk, chunk), F32)), (1, 3)).astype(BF16)
    seq = lambda bi, j: (bi, j, 0)
    per_b = lambda bi, j: (bi, 0, 0)
    const = lambda bi, j: (0, 0)
    stream = pl.BlockSpec((1, tb, B_WIDTH), seq)
    state = pl.BlockSpec((1, B_WIDTH, B_WIDTH), per_b)
    return pl.pallas_call(
        functools.partial(_rec_kernel, chunk=chunk, n_chunks=n_chunks),
        grid=(b, t // tb),
        in_specs=[stream] * 7 + [pl.BlockSpec((1, tb, B_WIDTH), lambda bi, j: (bi, j, A_WIDTH // B_WIDTH)),
                                 pl.BlockSpec((1, tb, C_WIDTH), lambda bi, j: (bi, j, (A_WIDTH + B_WIDTH) // C_WIDTH)),
                                 state, state,
                                 pl.BlockSpec((1, B_WIDTH), const), pl.BlockSpec((1, B_WIDTH), const),
                                 pl.BlockSpec((chunk, 3 * chunk), const),
                                 pl.BlockSpec((REC_HEADS * chunk, chunk), const),
                                 pl.BlockSpec((chunk, B_WIDTH), const), pl.BlockSpec((chunk, B_WIDTH), const),
                                 pl.BlockSpec((1, B_WIDTH), const),
                                 pl.BlockSpec((2 * B_WIDTH, B_WIDTH), const),
                                 pl.BlockSpec((1, tb, D_MODEL), seq),
                                 pl.BlockSpec((1, tb, A_WIDTH), seq),
                                 pl.BlockSpec((D_MODEL, D_MODEL), const)],
        out_specs=[pl.BlockSpec((1, tb, D_MODEL), seq), state, state],
        out_shape=[jax.ShapeDtypeStruct((b, t, D_MODEL), F32),
                   jax.ShapeDtypeStruct((b, B_WIDTH, B_WIDTH), F32),
                   jax.ShapeDtypeStruct((b, C_WIDTH, C_WIDTH), F32)],
        scratch_shapes=[pltpu.VMEM((B_WIDTH, B_WIDTH), F32), pltpu.VMEM((C_WIDTH, C_WIDTH), F32),
                        pltpu.VMEM((tb, B_WIDTH + C_WIDTH), BF16)],
        compiler_params=pltpu.CompilerParams(dimension_semantics=("parallel", "arbitrary"),
                                             vmem_limit_bytes=VMEM_LIMIT),
        name=name,
    )(bq, bk, bv, blf, cq, ck, cv, gate, gate, s0b, s0c,
      jnp.tile(hgain, REC_HEADS).reshape(1, B_WIDTH), jnp.tile(rgain, REC_HEADS).reshape(1, C_WIDTH),
      tri, dmat, inner, kscale, rdec, jnp.tile(_group_mean_matrix(B_WIDTH), (2, 1)), x, ya, w_out16)


def _with_past(past_parts, new, s_pad):
    past = jnp.concatenate([p.astype(BF16) for p in past_parts], axis=-1)
    b, n_old, width = past.shape
    tail = jnp.zeros((b, s_pad - n_old - new.shape[1], width), BF16)
    return jnp.concatenate([past, new, tail], axis=1)


def _mixer_layer(x, pos0, past, s_hgrn, s_ret, layer, w_p, w_out16, norm_g, q_gain, k_gain,
                 lb_logits, hgain, rgain, tag):
    b, t, _ = x.shape
    pos = pos0 + jnp.arange(t)
    (q, k, v, kd3, vd3, iq, sg, ik, ikd3, gate, bq, bk, bv, blf, cq, ck, cv) = _proj_call(
        x, pos, norm_g, w_p, q_gain, k_gain, lb_logits, layer, PROJ_ROWS)

    n_keys = t if past is None else past[0].shape[1] + t
    s_pad = -(-n_keys // LANES) * LANES
    if past is not None:
        pk, pv, pik = past
        kd3 = _with_past([pk[:, :, 0], pk[:, :, 0], pk[:, :, 1], pk[:, :, 1]], kd3, s_pad)
        vd3 = _with_past([pv[:, :, 0], pv[:, :, 0], pv[:, :, 1], pv[:, :, 1]], vd3, s_pad)
        ikd3 = _with_past([pik, pik], ikd3, s_pad)
    elif s_pad != n_keys:
        padw = ((0, 0), (0, s_pad - n_keys), (0, 0))
        kd3, vd3, ikd3 = jnp.pad(kd3, padw), jnp.pad(vd3, padw), jnp.pad(ikd3, padw)
    topk = min(TOPK_MAX, n_keys // 4)
    oa = _attention(q, iq, sg, kd3, vd3, ikd3, gate, topk=topk, qpos0=pos0, n_keys=n_keys, tag=tag)

    out, sb, sc = _rec_out_call(bq, bk, bv, blf, cq, ck, cv, gate, s_hgrn, s_ret, hgain, rgain, x, oa, w_out16,
                                name=f"rec_{tag}")
    return (out,
            (k.reshape(b, t, A_KV_HEADS, HEAD_DIM), v.reshape(b, t, A_KV_HEADS, HEAD_DIM), ik,
             _state_from_blockdiag(sb), _state_from_blockdiag(sc)))


def kernel(x_prompt, x_sample, cache_k, cache_v, cache_idx_k, state_hgrn, state_ret, norm_g, w_in,
           q_norm_g, k_norm_g, hgrn_lb_logits, hgrn_norm_g, ret_norm_g, w_out):
    depth = w_in.shape[0]
    bp, tp = x_prompt.shape[:2]
    bs, ts = x_sample.shape[:2]
    past = cache_k.shape[2]
    zero_state = jnp.zeros((bp, B_WIDTH, B_WIDTH), F32)

    yp, ys = x_prompt, x_sample
    outs_p, outs_s = [], []
    for l in range(depth):
        w_p = _prep_w_in(w_in[l])
        w_o = w_out[l].astype(BF16)
        args = (l, w_p, w_o, norm_g[l], q_norm_g[l], k_norm_g[l], hgrn_lb_logits, hgrn_norm_g[l], ret_norm_g[l])
        yp, st = _mixer_layer(yp, 0, None, zero_state, zero_state, *args, tag=f"p{l}")
        outs_p.append(st)
        ys, st = _mixer_layer(ys, past, (cache_k[l], cache_v[l], cache_idx_k[l]),
                              _state_to_blockdiag(state_hgrn[l]), _state_to_blockdiag(state_ret[l]), *args,
                              tag=f"s{l}")
        outs_s.append(st)

    def stack(outs, i, shape):
        return jnp.stack([o[i] for o in outs]).reshape(shape)

    return (yp, ys,
            stack(outs_p, 0, (depth, bp, tp, A_KV_HEADS, HEAD_DIM)),
            stack(outs_p, 1, (depth, bp, tp, A_KV_HEADS, HEAD_DIM)),
            stack(outs_p, 2, (depth, bp, tp, HEAD_DIM)),
            stack(outs_p, 3, (depth, bp, REC_HEADS, HEAD_DIM, HEAD_DIM)),
            stack(outs_p, 4, (depth, bp, REC_HEADS, HEAD_DIM, HEAD_DIM)),
            stack(outs_s, 0, (depth, bs, ts, A_KV_HEADS, HEAD_DIM)),
            stack(outs_s, 1, (depth, bs, ts, A_KV_HEADS, HEAD_DIM)),
            stack(outs_s, 2, (depth, bs, ts, HEAD_DIM)),
            stack(outs_s, 3, (depth, bs, REC_HEADS, HEAD_DIM, HEAD_DIM)),
            stack(outs_s, 4, (depth, bs, REC_HEADS, HEAD_DIM, HEAD_DIM)))
```

```python
import functools
import math

import numpy as np
import jax
import jax.numpy as jnp
from jax import lax
from jax.experimental import pallas as pl
from jax.experimental.pallas import tpu as pltpu

F32 = jnp.float32
BF16 = jnp.bfloat16
I32 = jnp.int32

D_MODEL = 1024
HEAD_DIM = 64
CHUNK = 64
A_WIDTH = 512
A_HEADS = 8
A_KV_HEADS = 2
KV_WIDTH = A_KV_HEADS * HEAD_DIM
IDX_HEADS = 8
IDX_W_SCALE = (IDX_HEADS * HEAD_DIM) ** -0.5
TOPK_MAX = 256
B_WIDTH = 256
C_WIDTH = 256
REC_HEADS = 4
ROPE_THETA = 10000.0
EPS = 1e-6
LANES = 128
VMEM_LIMIT = 48 * 1024 * 1024
PROJ_ROWS = 512
ATTN_ROWS = 256
REC_CHUNKS_PER_STEP = 16

_SIZES = (A_WIDTH, KV_WIDTH, KV_WIDTH, A_WIDTH, IDX_HEADS * HEAD_DIM, HEAD_DIM, IDX_HEADS,
          B_WIDTH, B_WIDTH, B_WIDTH, B_WIDTH, C_WIDTH, C_WIDTH, C_WIDTH, C_WIDTH)
_SPLIT_IDX = tuple(int(s) for s in np.cumsum(_SIZES)[:-1])

_AQ, _AK, _AV, _AG, _IQ, _IKW = 0, 512, 640, 768, 1280, 1792
_BQ, _BF, _BI, _BG, _CQ, _CK, _CV, _CG = 1920, 2176, 2432, 2688, 2944, 3200, 3456, 3712
_N_PROJ = 3968

NEG_BIG = -1e30
POS_BIG = 1 << 20
LOG2_E = math.log2(math.e)


def _prep_w_in(w):
    (aq, ak, av, ag, iq, ik, iw, bq, bf, bi, bg, cq, ck, cv, cg) = jnp.split(w, _SPLIT_IDX, axis=1)
    pad = jnp.zeros((w.shape[0], LANES - HEAD_DIM - IDX_HEADS), w.dtype)
    return jnp.concatenate([aq, ak, av, ag, iq, ik, iw, pad, bq, bf, bi, bg, cq, ck, cv, cg],
                           axis=1).astype(BF16)


def _rope_tables(pos):
    half = HEAD_DIM // 2
    freqs = ROPE_THETA ** (-jnp.arange(half, dtype=F32) / half)
    ang = pos.astype(F32)[:, None] * freqs[None, :]
    cos, sin = jnp.cos(ang), jnp.sin(ang)
    cos64 = jnp.concatenate([cos, cos], axis=1)
    sin64 = jnp.concatenate([-sin, sin], axis=1)
    return jnp.concatenate([cos64, cos64], axis=1), jnp.concatenate([sin64, sin64], axis=1)


def _group_mean_matrix(width):
    idx = np.arange(width) // HEAD_DIM
    return jnp.asarray((idx[:, None] == idx[None, :]).astype(np.float32) / HEAD_DIM, BF16)


def _split2(x):
    hi = x.astype(BF16)
    lo = (x - hi.astype(F32)).astype(BF16)
    return hi, lo


def _nt_dot(a, b):
    return lax.dot_general(a, b, (((1,), (1,)), ((), ())), preferred_element_type=F32)


def _proj_kernel(x_ref, g_ref, w_ref, cos_ref, sin_ref, qg_ref, kg_ref, lb_ref, gm_ref, ex_ref,
                 q_out, k_out, v_out, kd_out, vd_out, iq_out, sg_out, ik_out, ikd_out, gate_out,
                 bq_out, bk_out, bv_out, blf_out, cq_out, ck_out, cv_out, *, layer):
    x = x_ref[...]
    tm = x.shape[0]
    ms = jnp.mean(x * x, axis=-1, keepdims=True)
    h = (x * lax.rsqrt(ms + EPS) * g_ref[...]).astype(BF16)
    cos = cos_ref[...]
    sin = sin_ref[...]
    lane = lax.broadcasted_iota(I32, (tm, LANES), 1)
    first_half = (lane % HEAD_DIM) < (HEAD_DIM // 2)
    low_head = lane < HEAD_DIM
    gmat = gm_ref[...]

    def proj(c0, width):
        return jnp.dot(h, w_ref[:, c0:c0 + width], preferred_element_type=F32)

    def rope(xb):
        swapped = jnp.where(first_half, pltpu.roll(xb, LANES - HEAD_DIM // 2, 1),
                            pltpu.roll(xb, HEAD_DIM // 2, 1))
        return xb * cos + swapped * sin

    def head_norm(xb, gain):
        hi, lo = _split2(xb * xb)
        msq = jnp.dot(jnp.concatenate([hi, lo], axis=1), gmat, preferred_element_type=F32)
        return xb * lax.rsqrt(msq + EPS) * gain

    def dup_heads(xb):
        other = pltpu.roll(xb, HEAD_DIM, 1)
        return jnp.where(low_head, xb, other), jnp.where(low_head, other, xb)

    ag = proj(_AG, A_WIDTH)
    bg = proj(_BG, B_WIDTH)
    cg = proj(_CG, C_WIDTH)
    bq = proj(_BQ, B_WIDTH)
    bf = proj(_BF, B_WIDTH)
    bi = proj(_BI, B_WIDTH)
    cq = proj(_CQ, C_WIDTH)
    ck = proj(_CK, C_WIDTH)
    cv = proj(_CV, C_WIDTH)
    akv = proj(_AK, 2 * KV_WIDTH)
    ikw = proj(_IKW, LANES)
    aq = proj(_AQ, A_WIDTH)
    iq = proj(_IQ, IDX_HEADS * HEAD_DIM)

    def silu(z):
        return z / (1.0 + jnp.exp(-z))

    gate_out[:, :A_WIDTH] = silu(ag).astype(BF16)
    gate_out[:, A_WIDTH:A_WIDTH + B_WIDTH] = silu(bg).astype(BF16)
    gate_out[:, A_WIDTH + B_WIDTH:] = silu(cg).astype(BF16)

    logits = lb_ref[...]
    e = jnp.exp(logits - jnp.max(logits, axis=0, keepdims=True))
    sm = e / jnp.sum(e, axis=0, keepdims=True)
    lb = jnp.zeros((1, B_WIDTH), F32)
    for j in range(1, layer + 1):
        lb = lb + sm[j:j + 1, :]
    f = lb + (1.0 - lb) / (1.0 + jnp.exp(-bf))
    bq_out[...] = bq.astype(BF16)
    bk_out[...] = ((1.0 - lb) / (1.0 + jnp.exp(bf))).astype(BF16)
    bv_out[...] = bi.astype(BF16)
    blf_out[...] = jnp.log(f)

    for c in range(C_WIDTH // LANES):
        sl = slice(c * LANES, (c + 1) * LANES)
        cq_out[:, sl] = rope(cq[:, sl]).astype(BF16)
        ck_out[:, sl] = (rope(ck[:, sl]) * (HEAD_DIM ** -0.5)).astype(BF16)
    cv_out[...] = cv.astype(BF16)

    v = akv[:, KV_WIDTH:]
    v_out[...] = v
    v0, v1 = dup_heads(v)
    vd_out[:, :LANES] = v0.astype(BF16)
    vd_out[:, LANES:] = v1.astype(BF16)
    ikr = rope(ikw)
    ik_out[...] = ikr[:, :HEAD_DIM]
    ikd_out[...] = dup_heads(ikr)[0].astype(BF16)
    sg_out[...] = jnp.where(ikw >= 0, 1.0, -1.0).astype(F32)

    for c in range(A_WIDTH // LANES):
        blk = rope(head_norm(aq[:, c * LANES:(c + 1) * LANES], qg_ref[...]))
        q_out[:, c * LANES:(c + 1) * LANES] = (blk * (HEAD_DIM ** -0.5 * LOG2_E)).astype(BF16)
    k = rope(head_norm(akv[:, :KV_WIDTH], kg_ref[...]))
    k_out[...] = k
    k0, k1 = dup_heads(k)
    kd_out[:, :LANES] = k0.astype(BF16)
    kd_out[:, LANES:] = k1.astype(BF16)

    whi, wlo = _split2(jnp.abs(ikw) * IDX_W_SCALE)
    wexp = jnp.dot(jnp.concatenate([whi, wlo], axis=1), ex_ref[...], preferred_element_type=F32)
    for c in range(IDX_HEADS * HEAD_DIM // LANES):
        sl = slice(c * LANES, (c + 1) * LANES)
        iq_out[:, sl] = (rope(iq[:, sl]) * wexp[:, sl]).astype(BF16)


def _row_blocking(b, t, tm):
    if t % tm == 0:
        grid = (b, t // tm)
        rows = lambda w: pl.BlockSpec((None, tm, w), lambda bi, j: (bi, j, 0))
        per_pos = lambda w: pl.BlockSpec((tm, w), lambda bi, j: (j, 0))
        const = lambda shape: pl.BlockSpec(shape, lambda bi, j: (0, 0))
        return grid, rows, per_pos, const, False, tm
    n = b * t
    tm = min(tm, n)
    assert n % tm == 0 and tm % t == 0
    grid = (n // tm,)
    rows = lambda w: pl.BlockSpec((tm, w), lambda r: (r, 0))
    per_pos = lambda w: pl.BlockSpec((tm, w), lambda r: (0, 0))
    const = lambda shape: pl.BlockSpec(shape, lambda r: (0, 0))
    return grid, rows, per_pos, const, True, tm


def _proj_call(x, pos, norm_g, w_p, q_gain, k_gain, lb_logits, layer, tm):
    b, t, _ = x.shape
    grid, rows, per_pos, const, flat, tm = _row_blocking(b, t, tm)
    cos, sin = _rope_tables(pos)
    if flat:
        x = x.reshape(b * t, D_MODEL)
        cos = jnp.tile(cos, (tm // t, 1))
        sin = jnp.tile(sin, (tm // t, 1))
    expand = np.zeros((LANES, IDX_HEADS * HEAD_DIM), np.float32)
    for hd in range(IDX_HEADS):
        expand[HEAD_DIM + hd, hd * HEAD_DIM:(hd + 1) * HEAD_DIM] = 1.0
    widths = [(A_WIDTH, BF16), (KV_WIDTH, F32), (KV_WIDTH, F32), (2 * KV_WIDTH, BF16), (2 * KV_WIDTH, BF16),
              (IDX_HEADS * HEAD_DIM, BF16), (LANES, F32), (HEAD_DIM, F32), (LANES, BF16), (D_MODEL, BF16),
              (B_WIDTH, BF16), (B_WIDTH, BF16), (B_WIDTH, BF16), (B_WIDTH, F32),
              (C_WIDTH, BF16), (C_WIDTH, BF16), (C_WIDTH, BF16)]
    lead = (b * t,) if flat else (b, t)
    outs = pl.pallas_call(
        functools.partial(_proj_kernel, layer=layer),
        grid=grid,
        in_specs=[rows(D_MODEL),
                  const((1, D_MODEL)),
                  const((D_MODEL, _N_PROJ)),
                  per_pos(LANES),
                  per_pos(LANES),
                  const((1, LANES)),
                  const((1, LANES)),
                  const(lb_logits.shape),
                  const((2 * LANES, LANES)),
                  const((2 * LANES, IDX_HEADS * HEAD_DIM))],
        out_specs=[rows(w) for w, _ in widths],
        out_shape=[jax.ShapeDtypeStruct(lead + (w,), dt) for w, dt in widths],
        compiler_params=pltpu.CompilerParams(dimension_semantics=("parallel",) * len(grid),
                                             vmem_limit_bytes=VMEM_LIMIT),
        name=f"proj_l{layer}",
    )(x, norm_g.reshape(1, D_MODEL), w_p, cos, sin,
      jnp.tile(q_gain, 2).reshape(1, LANES), jnp.tile(k_gain, 2).reshape(1, LANES),
      lb_logits, jnp.tile(_group_mean_matrix(LANES), (2, 1)), jnp.tile(jnp.asarray(expand, BF16), (2, 1)))
    return [o.reshape(b, t, o.shape[-1]) for o in outs] if flat else outs


SEARCH_INTERP_STEPS = 9
SEARCH_WALK_STEPS = 2
SEARCH_RANK_BIAS = 1.0
SEARCH_EDGE = 0.1
SEARCH_SPAN_SD = 8.0
TIE_WALK_STEPS = (4, 12)


def _normal_upper_quantile(r):
    rr = jnp.minimum(r, 1.0 - r)
    t = jnp.sqrt(-2.0 * jnp.log(rr))
    z = t - ((0.010328 * t + 0.802853) * t + 2.515517) / (((0.001308 * t + 0.189269) * t + 1.432788) * t + 1.0)
    return jnp.where(r <= 0.5, z, -z)


def _select_topk(xm_scr, sum1_scr, sum2_scr, lim, topk, thr_scr, lim_scr, bias_scr):
    tq, s = xm_scr.shape
    kf = float(topk)
    nf = lim.astype(F32)
    trivial = lim <= topk

    def count_gt(p):
        return jnp.sum(jnp.where(xm_scr[...] > p, 1.0, 0.0), axis=1, keepdims=True)

    mu = jnp.sum(sum1_scr[...], axis=1, keepdims=True) / nf
    var = jnp.sum(sum2_scr[...], axis=1, keepdims=True) / nf - mu * mu
    sd = jnp.sqrt(jnp.maximum(var, 1e-30))
    target = kf + SEARCH_RANK_BIAS
    lo, hi = mu - SEARCH_SPAN_SD * sd, mu + SEARCH_SPAN_SD * sd
    clo, chi = nf, jnp.zeros_like(nf)
    p = mu + _normal_upper_quantile(jnp.clip(target / nf, 1e-6, 1.0 - 1e-6)) * sd
    lo_counted = jnp.zeros((tq, 1), jnp.bool_)
    for it in range(SEARCH_INTERP_STEPS):
        c = count_gt(p)
        above = c >= kf
        lo = jnp.where(above, p, lo)
        clo = jnp.where(above, c, clo)
        lo_counted = lo_counted | above
        hi = jnp.where(above, hi, p)
        chi = jnp.where(above, chi, c)
        frac = jnp.clip((clo - target) / (clo - chi), SEARCH_EDGE, 1.0 - SEARCH_EDGE)
        p = lo + (hi - lo) * frac

    cur = lo
    thr = lo
    cnt_prev = clo
    cnt_gt = clo
    cnt_eq = jnp.zeros_like(clo)
    done = lo_counted & (clo == kf)
    for it in range(SEARCH_WALK_STEPS):
        xv = xm_scr[...]
        cur = jnp.min(jnp.where(xv > cur, xv, -NEG_BIG), axis=1, keepdims=True)
        c = count_gt(cur)
        hit = (c <= kf) & lo_counted & jnp.logical_not(done)
        thr = jnp.where(hit, cur, thr)
        cnt_gt = jnp.where(hit, c, cnt_gt)
        cnt_eq = jnp.where(hit, cnt_prev - c, cnt_eq)
        done = done | hit
        cnt_prev = c
    thr = jnp.where(trivial, 0.1 * NEG_BIG, thr)
    cnt_gt = jnp.where(trivial, nf, cnt_gt)
    cnt_eq = jnp.where(trivial, 0.0, cnt_eq)
    thr_scr[...] = jnp.broadcast_to(thr, (tq, LANES))
    sum1_scr[...] = jnp.broadcast_to(cnt_gt, (tq, LANES))
    sum2_scr[...] = jnp.broadcast_to(cnt_eq, (tq, LANES))
    proven = trivial | done

    @pl.when(jnp.min(jnp.where(proven, 1.0, 0.0)) < 0.5)
    def _():
        lo0 = jnp.where(lo_counted, lo, 0.5 * NEG_BIG)
        clo0 = jnp.where(lo_counted, clo, nf)

        def unsettled(state):
            rounds, done_f = state[0], state[-1]
            return (rounds < s) & (jnp.min(done_f) < 0.5)

        def one_round(state):
            rounds, lo_, clo_, hi_, chi_, thr_, cgt_, ceq_, done_f = state
            done_ = done_f > 0.5
            frac = jnp.clip((clo_ - target) / (clo_ - chi_), SEARCH_EDGE, 1.0 - SEARCH_EDGE)
            piv = lo_ + (hi_ - lo_) * frac
            c = count_gt(piv)
            above = c >= kf
            lo_ = jnp.where(above, piv, lo_)
            clo_ = jnp.where(above, c, clo_)
            hi_ = jnp.where(above, hi_, piv)
            chi_ = jnp.where(above, chi_, c)
            settled = (clo_ == kf) & jnp.logical_not(done_)
            thr_ = jnp.where(settled, lo_, thr_)
            cgt_ = jnp.where(settled, clo_, cgt_)
            ceq_ = jnp.where(settled, 0.0, ceq_)
            done_ = done_ | settled
            xv = xm_scr[...]
            nxt = jnp.min(jnp.where(xv > lo_, xv, -NEG_BIG), axis=1, keepdims=True)
            c = count_gt(nxt)
            hit = (c <= kf) & jnp.logical_not(done_)
            thr_ = jnp.where(hit, nxt, thr_)
            cgt_ = jnp.where(hit, c, cgt_)
            ceq_ = jnp.where(hit, clo_ - c, ceq_)
            done_ = done_ | hit
            lo_ = jnp.where(hit, lo_, nxt)
            clo_ = jnp.where(hit, clo_, c)
            return (rounds + 1, lo_, clo_, hi_, chi_, thr_, cgt_, ceq_, jnp.where(done_, 1.0, 0.0))

        hi0 = jnp.maximum(hi, lo0)
        state = (jnp.int32(0), lo0, clo0, hi0, chi, thr, cnt_gt, cnt_eq, jnp.where(proven, 1.0, 0.0))
        state = lax.while_loop(unsettled, one_round, state)
        thr_scr[...] = jnp.broadcast_to(state[5], (tq, LANES))
        sum1_scr[...] = jnp.broadcast_to(state[6], (tq, LANES))
        sum2_scr[...] = jnp.broadcast_to(state[7], (tq, LANES))

    thr = thr_scr[:, :1]
    cnt_gt = sum1_scr[:, :1]
    cnt_eq = sum2_scr[:, :1]
    xv = xm_scr[...]
    gt = xv > thr
    eq = xv == thr
    want_eq = kf - cnt_gt

    all_or_none = jnp.where((want_eq >= cnt_eq) & (want_eq > 0.0), POS_BIG, 0)
    lim_scr[...] = jnp.broadcast_to(all_or_none, (tq, LANES))
    partial = (want_eq > 0.0) & (want_eq < cnt_eq)
    kpos = lax.broadcasted_iota(I32, (tq, s), 1)

    @pl.when(jnp.max(jnp.where(partial, 1.0, 0.0)) > 0.5)
    def _():
        eqpos = jnp.where(eq, kpos, POS_BIG)
        eqposf = eqpos.astype(F32)

        def walk(first, last, cur, xlim):
            for it in range(first, last):
                cur = jnp.min(jnp.where(eqposf > cur, eqposf, float(POS_BIG)), axis=1, keepdims=True)
                xlim = jnp.where(partial & (want_eq == float(it)), cur.astype(I32), xlim)
            return cur, xlim

        cur, xlim = walk(0, TIE_WALK_STEPS[0], jnp.full((tq, 1), -1.0, F32), all_or_none)
        lim_scr[...] = jnp.broadcast_to(xlim, (tq, LANES))

        @pl.when(jnp.max(jnp.where(partial & (want_eq >= float(TIE_WALK_STEPS[0])), 1.0, 0.0)) > 0.5)
        def _():
            _, xlim2 = walk(TIE_WALK_STEPS[0], TIE_WALK_STEPS[1], cur, xlim)
            lim_scr[...] = jnp.broadcast_to(xlim2, (tq, LANES))

            @pl.when(jnp.max(jnp.where(partial & (want_eq >= float(TIE_WALK_STEPS[1])), 1.0, 0.0)) > 0.5)
            def _():
                nbits = max(1, int(s).bit_length())

                def pos_step(j, xl):
                    cand = xl | lax.shift_left(jnp.int32(1), nbits - 1 - j)
                    cnt = jnp.sum(jnp.where(eqpos < cand, 1.0, 0.0), axis=1, keepdims=True)
                    return jnp.where(cnt <= want_eq, cand, xl)

                lim_scr[...] = jnp.broadcast_to(lax.fori_loop(0, nbits, pos_step, jnp.zeros((tq, 1), I32)),
                                                (tq, LANES))

    sel = gt | (eq & (kpos < lim_scr[:, :1]))
    bias_scr[...] = jnp.where(sel, 0.0, NEG_BIG).astype(bias_scr.dtype)


def _attn_kernel(*refs, topk, tpos0, n_keys):
    q_ref, iq_ref, sg_ref, kd_ref, vd_ref, ikd_ref, gate_ref = refs[:7]
    o_ref, xm_scr, bias_scr, thr_scr, lim_scr, sum1_scr, sum2_scr = refs[-7:]
    tq = q_ref.shape[1]
    s = kd_ref.shape[1]
    lane = lax.broadcasted_iota(I32, (tq, LANES), 1)
    halves = (lane < HEAD_DIM, lane >= HEAD_DIM)

    tpos = tpos0 + lax.broadcasted_iota(I32, (tq, 1), 0)
    lim = jnp.minimum((tpos // CHUNK + 1) * CHUNK, n_keys)
    search = min(s, n_keys) > topk

    rb = min(tq, LANES)
    key_tile = 2 * LANES
    lane_rb = lax.broadcasted_iota(I32, (rb, LANES), 1)
    halves_rb = (lane_rb < HEAD_DIM, lane_rb >= HEAD_DIM)
    for r in range(tq // rb):
        rs = slice(r * rb, (r + 1) * rb)
        sg = sg_ref[0, rs, :]
        qms, sgs = [], []
        for hd in range(IDX_HEADS):
            qb = iq_ref[0, rs, (hd // 2) * LANES:(hd // 2 + 1) * LANES]
            qms.append(jnp.where(halves_rb[hd % 2], qb, jnp.zeros_like(qb)))
            sgs.append(sg[:, HEAD_DIM + hd:HEAD_DIM + hd + 1])
        acc1 = jnp.zeros((rb, LANES), F32)
        acc2 = jnp.zeros((rb, LANES), F32)
        lim_first = min(((tpos0 + r * rb) // CHUNK + 1) * CHUNK, n_keys)
        lim_last = min(((tpos0 + (r + 1) * rb - 1) // CHUNK + 1) * CHUNK, n_keys)
        for c0 in range(0, s, key_tile):
            w = min(key_tile, s - c0)
            if c0 >= lim_last:
                xm_scr[rs, c0:c0 + w] = jnp.full((rb, w), NEG_BIG, F32)
                continue
            ikd_t = ikd_ref[0, c0:c0 + w, :]
            acc = jnp.zeros((rb, w), F32)
            for hd in range(IDX_HEADS):
                acc = acc + sgs[hd] * jnp.maximum(_nt_dot(qms[hd], ikd_t), 0.0)
            if c0 + w <= lim_first:
                xm_scr[rs, c0:c0 + w] = acc
                sc0 = acc
            else:
                adm_t = (c0 + lax.broadcasted_iota(I32, (rb, w), 1)) < lim[rs]
                xm_scr[rs, c0:c0 + w] = jnp.where(adm_t, acc, NEG_BIG)
                sc0 = jnp.where(adm_t, acc, 0.0)
            if search:
                for l0 in range(0, w, LANES):
                    part = sc0[:, l0:l0 + LANES]
                    acc1 = acc1 + part
                    acc2 = acc2 + part * part
        if search:
            sum1_scr[rs, :] = acc1
            sum2_scr[rs, :] = acc2

    logits = []
    for c in range(A_WIDTH // LANES):
        g = (2 * c) // (A_HEADS // A_KV_HEADS)
        qb = q_ref[0, :, c * LANES:(c + 1) * LANES]
        for e in range(2):
            qm = jnp.where(halves[e], qb, jnp.zeros_like(qb))
            logits.append(_nt_dot(qm, kd_ref[0, :, g * LANES:(g + 1) * LANES]))

    if search:
        _select_topk(xm_scr, sum1_scr, sum2_scr, lim, topk, thr_scr, lim_scr, bias_scr)
    else:
        bias_scr[...] = jnp.where(lax.broadcasted_iota(I32, (tq, s), 1) < lim, 0.0, NEG_BIG).astype(BF16)

    ones = jnp.ones((s, LANES), BF16)
    for g in range(A_KV_HEADS):
        vd1 = jnp.concatenate([vd_ref[0, :, g * LANES:(g + 1) * LANES], ones], axis=1)
        for c in range(g * 2, g * 2 + 2):
            outs = []
            for e in range(2):
                lg = logits[2 * c + e].astype(BF16) + bias_scr[...]
                p = jnp.exp2(lg - jnp.max(lg, axis=1, keepdims=True))
                ol = jnp.dot(p, vd1, preferred_element_type=F32)
                outs.append(ol[:, :LANES] / ol[:, LANES:])
            gate = gate_ref[0, :, c * LANES:(c + 1) * LANES].astype(F32)
            o_ref[0, :, c * LANES:(c + 1) * LANES] = (jnp.where(halves[0], outs[0], outs[1]) * gate).astype(BF16)


def _attn_call(q, iq, sg, kd, vd, ikd, gate, prev, *, topk, qpos0, n_keys, tq, qblock, s_blk, name):
    b, t, _ = q.shape
    qmap = lambda bi: (bi, qblock, 0)
    kmap = lambda bi: (bi, 0, 0)
    return pl.pallas_call(
        functools.partial(_attn_kernel, topk=topk, tpos0=qpos0 + qblock * tq, n_keys=n_keys),
        grid=(b,),
        in_specs=[pl.BlockSpec((1, tq, A_WIDTH), qmap),
                  pl.BlockSpec((1, tq, IDX_HEADS * HEAD_DIM), qmap),
                  pl.BlockSpec((1, tq, LANES), qmap),
                  pl.BlockSpec((1, s_blk, 2 * KV_WIDTH), kmap),
                  pl.BlockSpec((1, s_blk, 2 * KV_WIDTH), kmap),
                  pl.BlockSpec((1, s_blk, LANES), kmap),
                  pl.BlockSpec((1, tq, A_WIDTH), qmap),
                  pl.BlockSpec(memory_space=pl.ANY)],
        out_specs=pl.BlockSpec((1, tq, A_WIDTH), qmap),
        out_shape=jax.ShapeDtypeStruct((b, t, A_WIDTH), BF16),
        input_output_aliases={7: 0},
        scratch_shapes=[pltpu.VMEM((tq, s_blk), F32), pltpu.VMEM((tq, s_blk), BF16),
                        pltpu.VMEM((tq, LANES), F32), pltpu.VMEM((tq, LANES), I32),
                        pltpu.VMEM((tq, LANES), F32), pltpu.VMEM((tq, LANES), F32)],
        compiler_params=pltpu.CompilerParams(dimension_semantics=("parallel",),
                                             vmem_limit_bytes=VMEM_LIMIT),
        name=name,
    )(q, iq, sg, kd, vd, ikd, gate, prev)


def _attention(q, iq, sg, kd, vd, ikd, gate, *, topk, qpos0, n_keys, tag):
    b, t, _ = q.shape
    tq = min(t, ATTN_ROWS)
    out = jnp.zeros((b, t, A_WIDTH), BF16)
    for qblock in range(t // tq):
        last_pos = qpos0 + (qblock + 1) * tq - 1
        visible = min((last_pos // CHUNK + 1) * CHUNK, n_keys)
        s_blk = min(-(-visible // LANES) * LANES, kd.shape[1])
        out = _attn_call(q, iq, sg, kd, vd, ikd, gate, out, topk=topk, qpos0=qpos0, n_keys=n_keys, tq=tq,
                         qblock=qblock, s_blk=s_blk, name=f"attn_{tag}_q{qblock}")
    return out


def _retention_log_decay():
    return jnp.log(1.0 - jnp.exp(jnp.linspace(math.log(1.0 / 32), math.log(1.0 / 512), REC_HEADS, dtype=F32)))


def _rec_kernel(bq_ref, bk_ref, bv_ref, blf_ref, cq_ref, ck_ref, cv_ref, gb_ref, gc_ref, s0b_ref, s0c_ref,
                hg_ref, rg_ref, tri_ref, dmat_ref, inner_ref, kscale_ref, rdec_ref, gm_ref, x_ref, ya_ref, w_ref,
                y_ref, sb_out, sc_out, sb_scr, sc_scr, ybc_scr, *, chunk, n_chunks):
    j = pl.program_id(1)
    width = B_WIDTH

    @pl.when(j == 0)
    def _():
        sb_scr[...] = s0b_ref[0]
        sc_scr[...] = s0c_ref[0]

    rowh = lax.broadcasted_iota(I32, (width, width), 0) // HEAD_DIM
    colh = lax.broadcasted_iota(I32, (width, width), 1) // HEAD_DIM
    block_diag = rowh == colh
    lane_h = lax.broadcasted_iota(I32, (chunk, width), 1) // HEAD_DIM
    causal = (lax.broadcasted_iota(I32, (chunk, chunk), 0)
              >= lax.broadcasted_iota(I32, (chunk, chunk), 1))
    gmat = gm_ref[...]
    tri = tri_ref[...]
    tn = (((0,), (0,)), ((), ()))

    def masked_heads(x):
        return [jnp.where(lane_h == hd, x, 0.0).astype(BF16) for hd in range(REC_HEADS)]

    heads = range(REC_HEADS)
    chunks = range(n_chunks)
    rows = [pl.ds(c * chunk, chunk) for c in chunks]

    bcs = []
    for c in chunks:
        lf = blf_ref[0, rows[c], :]
        a1 = lf.astype(BF16)
        r1 = lf - a1.astype(F32)
        a2 = r1.astype(BF16)
        a3 = (r1 - a2.astype(F32)).astype(BF16)
        bcs.append(jnp.dot(tri, jnp.concatenate([a1, a2, a3], axis=0), preferred_element_type=F32))

    att_b, att_c, upd_b, upd_c, qs_b, qs_c, dec_b, vb_b, vb_c = ([] for _ in range(9))
    for c in chunks:
        bc = bcs[c]
        mid = bc[chunk // 2 - 1:chunk // 2, :]
        last = bc[chunk - 1:chunk, :]
        q = bq_ref[0, rows[c], :]
        k = bk_ref[0, rows[c], :]
        vb = bv_ref[0, rows[c], :].astype(BF16)
        kt = (k * jnp.exp(mid - bc)).astype(BF16)
        att_b.append([_nt_dot(qh, kt) for qh in masked_heads(q * jnp.exp(bc - mid))])
        upd_b.append(lax.dot_general(vb, (k * jnp.exp(last - bc)).astype(BF16), tn, preferred_element_type=F32))
        qs_b.append((q * jnp.exp(bc)).astype(BF16))
        dec_b.append(jnp.exp(last))
        vb_b.append(vb)
        q = cq_ref[0, rows[c], :]
        k = ck_ref[0, rows[c], :]
        vb = cv_ref[0, rows[c], :].astype(BF16)
        kb = k.astype(BF16)
        att_c.append([_nt_dot(qh, kb) for qh in masked_heads(q)])
        upd_c.append(lax.dot_general(vb, (k * kscale_ref[...]).astype(BF16), tn, preferred_element_type=F32))
        qs_c.append((q * inner_ref[...]).astype(BF16))
        vb_c.append(vb)

    st = sb_scr[...]
    rt = sc_scr[...]
    o_b, o_c = [], []
    for c in chunks:
        o_b.append(_nt_dot(qs_b[c], st.astype(BF16)))
        st = st * dec_b[c] + jnp.where(block_diag, upd_b[c], 0.0)
        o_c.append(_nt_dot(qs_c[c], rt.astype(BF16)))
        rt = rt * rdec_ref[...] + jnp.where(block_diag, upd_c[c], 0.0)
    sb_scr[...] = st
    sc_scr[...] = rt

    def blockdiag_rows(v16):
        return jnp.concatenate([jnp.where(lane_h == hd, v16, jnp.zeros_like(v16)) for hd in heads], axis=0)

    prod_b = [jnp.dot(jnp.concatenate([jnp.where(causal, att_b[c][hd], 0.0) for hd in heads], axis=1).astype(BF16),
                      blockdiag_rows(vb_b[c]), preferred_element_type=F32) for c in chunks]
    prod_c = [jnp.dot(jnp.concatenate([att_c[c][hd] * dmat_ref[hd * chunk:(hd + 1) * chunk, :] for hd in heads],
                                      axis=1).astype(BF16),
                      blockdiag_rows(vb_c[c]), preferred_element_type=F32) for c in chunks]
    outs = []
    for c in chunks:
        outs.append(o_b[c] + prod_b[c])
        outs.append(o_c[c] + prod_c[c])

    means = []
    for o in outs:
        hi, lo = _split2(o * o)
        means.append(jnp.dot(jnp.concatenate([hi, lo], axis=1), gmat, preferred_element_type=F32))
    for c in chunks:
        yb = outs[2 * c] * lax.rsqrt(means[2 * c] + EPS) * hg_ref[...]
        yc = outs[2 * c + 1] * lax.rsqrt(means[2 * c + 1] + EPS) * rg_ref[...]
        ybc_scr[rows[c], :B_WIDTH] = (yb * gb_ref[0, rows[c], :].astype(F32)).astype(BF16)
        ybc_scr[rows[c], B_WIDTH:] = (yc * gc_ref[0, rows[c], :].astype(F32)).astype(BF16)

    y_ref[0] = (x_ref[0]
                + jnp.dot(ya_ref[0], w_ref[:A_WIDTH, :], preferred_element_type=F32)
                + jnp.dot(ybc_scr[...], w_ref[A_WIDTH:, :], preferred_element_type=F32))

    @pl.when(j == pl.num_programs(1) - 1)
    def _():
        sb_out[0] = sb_scr[...]
        sc_out[0] = sc_scr[...]


def _state_to_blockdiag(s):
    b = s.shape[0]
    eye = jnp.eye(REC_HEADS, dtype=s.dtype)
    return jnp.einsum('bhkv,hg->bhvgk', s, eye).reshape(b, B_WIDTH, B_WIDTH)


def _state_from_blockdiag(sbd):
    b = sbd.shape[0]
    s5 = sbd.reshape(b, REC_HEADS, HEAD_DIM, REC_HEADS, HEAD_DIM)
    return jnp.einsum('bhvhk->bhkv', s5)


def _rec_out_call(bq, bk, bv, blf, cq, ck, cv, gate, s0b, s0c, hgain, rgain, x, ya, w_out16, name):
    b, t, _ = bq.shape
    chunk = min(CHUNK, t)
    tb = min(t, REC_CHUNKS_PER_STEP * chunk)
    n_chunks = tb // chunk
    lg = _retention_log_decay()
    n = jnp.arange(chunk, dtype=F32)
    diff = n[:, None] - n[None, :]
    dmat = jnp.where(diff >= 0, jnp.exp(jnp.where(diff >= 0, diff, 0.0)[None] * lg[:, None, None]), 0.0)
    dmat = dmat.reshape(REC_HEADS * chunk, chunk)
    per_lane = lambda a: jnp.repeat(a, HEAD_DIM, axis=-1)
    inner = per_lane(jnp.exp((n[:, None] + 1.0) * lg[None, :]))
    kscale = per_lane(jnp.exp((chunk - 1.0 - n)[:, None] * lg[None, :]))
    rdec = per_lane(jnp.exp(chunk * lg)[None, :])
    tri = jnp.tile(jnp.tril(jnp.ones((chunk, chunk), F32)), (1, 3)).astype(BF16)
    seq = lambda bi, j: (bi, j, 0)
    per_b = lambda bi, j: (bi, 0, 0)
    const = lambda bi, j: (0, 0)
    stream = pl.BlockSpec((1, tb, B_WIDTH), seq)
    state = pl.BlockSpec((1, B_WIDTH, B_WIDTH), per_b)
    return pl.pallas_call(
        functools.partial(_rec_kernel, chunk=chunk, n_chunks=n_chunks),
        grid=(b, t // tb),
        in_specs=[stream] * 7 + [pl.BlockSpec((1, tb, B_WIDTH), lambda bi, j: (bi, j, A_WIDTH // B_WIDTH)),
                                 pl.BlockSpec((1, tb, C_WIDTH), lambda bi, j: (bi, j, (A_WIDTH + B_WIDTH) // C_WIDTH)),
                                 state, state,
                                 pl.BlockSpec((1, B_WIDTH), const), pl.BlockSpec((1, B_WIDTH), const),
                                 pl.BlockSpec((chunk, 3 * chunk), const),
                                 pl.BlockSpec((REC_HEADS * chunk, chunk), const),
                                 pl.BlockSpec((chunk, B_WIDTH), const), pl.BlockSpec((chunk, B_WIDTH), const),
                                 pl.BlockSpec((1, B_WIDTH), const),
                                 pl.BlockSpec((2 * B_WIDTH, B_WIDTH), const),
                                 pl.BlockSpec((1, tb, D_MODEL), seq),
                                 pl.BlockSpec((1, tb, A_WIDTH), seq),
                                 pl.BlockSpec((D_MODEL, D_MODEL), const)],
        out_specs=[pl.BlockSpec((1, tb, D_MODEL), seq), state, state],
        out_shape=[jax.ShapeDtypeStruct((b, t, D_MODEL), F32),
                   jax.ShapeDtypeStruct((b, B_WIDTH, B_WIDTH), F32),
                   jax.ShapeDtypeStruct((b, C_WIDTH, C_WIDTH), F32)],
        scratch_shapes=[pltpu.VMEM((B_WIDTH, B_WIDTH), F32), pltpu.VMEM((C_WIDTH, C_WIDTH), F32),
                        pltpu.VMEM((tb, B_WIDTH + C_WIDTH), BF16)],
        compiler_params=pltpu.CompilerParams(dimension_semantics=("parallel", "arbitrary"),
                                             vmem_limit_bytes=VMEM_LIMIT),
        name=name,
    )(bq, bk, bv, blf, cq, ck, cv, gate, gate, s0b, s0c,
      jnp.tile(hgain, REC_HEADS).reshape(1, B_WIDTH), jnp.tile(rgain, REC_HEADS).reshape(1, C_WIDTH),
      tri, dmat, inner, kscale, rdec, jnp.tile(_group_mean_matrix(B_WIDTH), (2, 1)), x, ya, w_out16)


def _with_past(past_parts, new, s_pad):
    past = jnp.concatenate([p.astype(BF16) for p in past_parts], axis=-1)
    b, n_old, width = past.shape
    tail = jnp.zeros((b, s_pad - n_old - new.shape[1], width), BF16)
    return jnp.concatenate([past, new, tail], axis=1)


def _mixer_layer(x, pos0, past, s_hgrn, s_ret, layer, w_p, w_out16, norm_g, q_gain, k_gain,
                 lb_logits, hgain, rgain, tag):
    b, t, _ = x.shape
    pos = pos0 + jnp.arange(t)
    (q, k, v, kd3, vd3, iq, sg, ik, ikd3, gate, bq, bk, bv, blf, cq, ck, cv) = _proj_call(
        x, pos, norm_g, w_p, q_gain, k_gain, lb_logits, layer, PROJ_ROWS)

    n_keys = t if past is None else past[0].shape[1] + t
    s_pad = -(-n_keys // LANES) * LANES
    if past is not None:
        pk, pv, pik = past
        kd3 = _with_past([pk[:, :, 0], pk[:, :, 0], pk[:, :, 1], pk[:, :, 1]], kd3, s_pad)
        vd3 = _with_past([pv[:, :, 0], pv[:, :, 0], pv[:, :, 1], pv[:, :, 1]], vd3, s_pad)
        ikd3 = _with_past([pik, pik], ikd3, s_pad)
    elif s_pad != n_keys:
        padw = ((0, 0), (0, s_pad - n_keys), (0, 0))
        kd3, vd3, ikd3 = jnp.pad(kd3, padw), jnp.pad(vd3, padw), jnp.pad(ikd3, padw)
    topk = min(TOPK_MAX, n_keys // 4)
    oa = _attention(q, iq, sg, kd3, vd3, ikd3, gate, topk=topk, qpos0=pos0, n_keys=n_keys, tag=tag)

    out, sb, sc = _rec_out_call(bq, bk, bv, blf, cq, ck, cv, gate, s_hgrn, s_ret, hgain, rgain, x, oa, w_out16,
                                name=f"rec_{tag}")
    return (out,
            (k.reshape(b, t, A_KV_HEADS, HEAD_DIM), v.reshape(b, t, A_KV_HEADS, HEAD_DIM), ik,
             _state_from_blockdiag(sb), _state_from_blockdiag(sc)))


def kernel(x_prompt, x_sample, cache_k, cache_v, cache_idx_k, state_hgrn, state_ret, norm_g, w_in,
           q_norm_g, k_norm_g, hgrn_lb_logits, hgrn_norm_g, ret_norm_g, w_out):
    depth = w_in.shape[0]
    bp, tp = x_prompt.shape[:2]
    bs, ts = x_sample.shape[:2]
    past = cache_k.shape[2]
    zero_state = jnp.zeros((bp, B_WIDTH, B_WIDTH), F32)

    yp, ys = x_prompt, x_sample
    outs_p, outs_s = [], []
    for l in range(depth):
        w_p = _prep_w_in(w_in[l])
        w_o = w_out[l].astype(BF16)
        args = (l, w_p, w_o, norm_g[l], q_norm_g[l], k_norm_g[l], hgrn_lb_logits, hgrn_norm_g[l], ret_norm_g[l])
        yp, st = _mixer_layer(yp, 0, None, zero_state, zero_state, *args, tag=f"p{l}")
        outs_p.append(st)
        ys, st = _mixer_layer(ys, past, (cache_k[l], cache_v[l], cache_idx_k[l]),
                              _state_to_blockdiag(state_hgrn[l]), _state_to_blockdiag(state_ret[l]), *args,
                              tag=f"s{l}")
        outs_s.append(st)

    def stack(outs, i, shape):
        return jnp.stack([o[i] for o in outs]).reshape(shape)

    return (yp, ys,
            stack(outs_p, 0, (depth, bp, tp, A_KV_HEADS, HEAD_DIM)),
            stack(outs_p, 1, (depth, bp, tp, A_KV_HEADS, HEAD_DIM)),
            stack(outs_p, 2, (depth, bp, tp, HEAD_DIM)),
            stack(outs_p, 3, (depth, bp, REC_HEADS, HEAD_DIM, HEAD_DIM)),
            stack(outs_p, 4, (depth, bp, REC_HEADS, HEAD_DIM, HEAD_DIM)),
            stack(outs_s, 0, (depth, bs, ts, A_KV_HEADS, HEAD_DIM)),
            stack(outs_s, 1, (depth, bs, ts, A_KV_HEADS, HEAD_DIM)),
            stack(outs_s, 2, (depth, bs, ts, HEAD_DIM)),
            stack(outs_s, 3, (depth, bs, REC_HEADS, HEAD_DIM, HEAD_DIM)),
            stack(outs_s, 4, (depth, bs, REC_HEADS, HEAD_DIM, HEAD_DIM)))
```
